```python
import math
import jax, jax.numpy as jnp
from jax import lax
import numpy as np

D_MODEL = 1024
BATCH = 2
SEQ = 8192
DEPTH = 1

ATTN_HEADS = 8
ATTN_HEAD_DIM = D_MODEL // ATTN_HEADS
ATTN_WIDTH = ATTN_HEADS * ATTN_HEAD_DIM
MOBA_BLOCK = 256
MOBA_TOPK = 3
MOBA_Q_CHUNK = 32
REL_BUCKETS = 32
REL_MAX_DIST = 1024
SSM_INNER = 2 * D_MODEL
SSM_HEAD_DIM = 64
SSM_HEADS = SSM_INNER // SSM_HEAD_DIM
SSM_GROUPS = 4
SSM_HEADS_PER_GROUP = SSM_HEADS // SSM_GROUPS
SSM_STATE = 128
SSM_CONV = 4
SSM_CHUNK = 256
SSM_CONV_DIM = SSM_INNER + 2 * SSM_GROUPS * SSM_STATE
MEM_LEN = 256
MEM_HEADS = 4
MEM_HEAD_DIM = D_MODEL // MEM_HEADS
MEM_WIDTH = MEM_HEADS * MEM_HEAD_DIM
N_BRANCHES = 3
FFN_HIDDEN = ((8 * D_MODEL // 3 + 255) // 256) * 256
FFN_CONV = 3
NORM_EPS = 1e-6
IN_SIZES = (ATTN_WIDTH, ATTN_WIDTH, ATTN_WIDTH, SSM_INNER, SSM_CONV_DIM, SSM_HEADS, MEM_WIDTH, N_BRANCHES * D_MODEL)
IN_WIDTH = ATTN_WIDTH * 3 + SSM_INNER + SSM_CONV_DIM + SSM_HEADS + MEM_WIDTH + N_BRANCHES * D_MODEL

kernel_name = 'hybrid_moba_ssd_memory_gated_block'


def rms_norm(x, g):
    xf = x.astype(jnp.float32)
    y = xf * lax.rsqrt(jnp.mean(xf * xf, axis=-1, keepdims=True) + NORM_EPS)
    return (y * g.astype(jnp.float32)).astype(x.dtype)


def gated_group_rms_norm(y, z, g):
    v = y.astype(jnp.float32) * jax.nn.silu(z.astype(jnp.float32))
    shp = v.shape
    v = v.reshape(shp[:-1] + (SSM_GROUPS, shp[-1] // SSM_GROUPS))
    v = v * lax.rsqrt(jnp.mean(v * v, axis=-1, keepdims=True) + NORM_EPS)
    return (v.reshape(shp) * g.astype(jnp.float32)).astype(z.dtype)


def causal_dwconv(x, w, b):
    k_w, c = w.shape
    y = lax.conv_general_dilated(x, w[:, None, :].astype(x.dtype), window_strides=(1,),
                                 padding=[(k_w - 1, 0)], dimension_numbers=('NWC', 'WIO', 'NWC'),
                                 feature_group_count=c)
    return y + b.astype(x.dtype)


def t5_bucket(dist):
    n = jnp.maximum(dist, 0)
    max_exact = REL_BUCKETS // 2
    nf = jnp.maximum(n, max_exact).astype(jnp.float32)
    large = max_exact + (jnp.log(nf / max_exact) / math.log(REL_MAX_DIST / max_exact)
                         * (REL_BUCKETS - max_exact)).astype(jnp.int32)
    large = jnp.minimum(large, REL_BUCKETS - 1)
    return jnp.where(n < max_exact, n, large)


def moba_attention(q, k, v, rel_bias):
    bsz, s_len, n_h, d_h = q.shape
    nb = -(-s_len // MOBA_BLOCK)
    s_pad = nb * MOBA_BLOCK
    pad = s_pad - s_len
    q, k, v = [jnp.pad(t, ((0, 0), (0, pad), (0, 0), (0, 0))).transpose(0, 2, 1, 3) for t in (q, k, v)]
    kb = k.reshape(bsz, n_h, nb, MOBA_BLOCK, d_h)
    vb = v.reshape(bsz, n_h, nb, MOBA_BLOCK, d_h)
    k_mean = jnp.mean(kb.astype(jnp.float32), axis=3)
    q_blk = jnp.arange(s_pad) // MOBA_BLOCK
    gate = jnp.einsum('bhsd,bhnd->bhsn', q.astype(jnp.float32), k_mean)
    fully_past = jnp.arange(nb)[None, :] < q_blk[:, None]
    gate = jnp.where(fully_past, gate, -jnp.inf)
    topk = min(MOBA_TOPK, nb)
    _, sel = lax.top_k(gate, topk)
    scale = d_h ** -0.5
    bias_tab = rel_bias.T
    b_idx = jnp.arange(bsz)[:, None, None]
    h_idx = jnp.arange(n_h)[None, :, None]
    j = jnp.arange(MOBA_BLOCK)

    def one_chunk(ci):
        start = ci * MOBA_Q_CHUNK
        q_c = lax.dynamic_slice_in_dim(q, start, MOBA_Q_CHUNK, axis=2).astype(jnp.float32) * scale
        sel_c = lax.dynamic_slice_in_dim(sel, start, MOBA_Q_CHUNK, axis=2)
        qpos = start + jnp.arange(MOBA_Q_CHUNK)
        own = start // MOBA_BLOCK
        valid = jnp.arange(topk) < own
        sel_flat = sel_c.reshape(bsz, n_h, MOBA_Q_CHUNK * topk)
        k_sel = kb[b_idx, h_idx, sel_flat].reshape(bsz, n_h, MOBA_Q_CHUNK, topk, MOBA_BLOCK, d_h)
        v_sel = vb[b_idx, h_idx, sel_flat].reshape(bsz, n_h, MOBA_Q_CHUNK, topk, MOBA_BLOCK, d_h)
        k_own = lax.dynamic_index_in_dim(kb, own, axis=2, keepdims=False)
        v_own = lax.dynamic_index_in_dim(vb, own, axis=2, keepdims=False)
        dist_sel = qpos[:, None, None] - (sel_c[..., None] * MOBA_BLOCK + j)
        bias_sel = bias_tab[h_idx[..., None, None], t5_bucket(dist_sel)]
        s_sel = jnp.einsum('bhqd,bhqkjd->bhqkj', q_c, k_sel.astype(jnp.float32)) + bias_sel
        s_sel = jnp.where(valid[:, None], s_sel, -jnp.inf)
        dist_own = qpos[:, None] - (own * MOBA_BLOCK + j)[None, :]
        s_own = jnp.einsum('bhqd,bhjd->bhqj', q_c, k_own.astype(jnp.float32)) + bias_tab[:, t5_bucket(dist_own)]
        s_own = jnp.where(dist_own >= 0, s_own, -jnp.inf)
        s_all = jnp.concatenate([s_sel.reshape(bsz, n_h, MOBA_Q_CHUNK, topk * MOBA_BLOCK), s_own], axis=-1)
        p = jax.nn.softmax(s_all, axis=-1)
        p_sel = p[..., :topk * MOBA_BLOCK].reshape(bsz, n_h, MOBA_Q_CHUNK, topk, MOBA_BLOCK)
        p_own = p[..., topk * MOBA_BLOCK:]
        o = (jnp.einsum('bhqkj,bhqkjd->bhqd', p_sel, v_sel.astype(jnp.float32))
             + jnp.einsum('bhqj,bhjd->bhqd', p_own, v_own.astype(jnp.float32)))
        return o.astype(v.dtype)

    out = lax.map(one_chunk, jnp.arange(s_pad // MOBA_Q_CHUNK))
    out = out.transpose(1, 0, 3, 2, 4).reshape(bsz, s_pad, n_h * d_h)
    return out[:, :s_len]


def ssd_chunked_scan(xh, dt, a_neg, bm, cm, d_skip):
    bsz, s_len = xh.shape[:2]
    L = SSM_CHUNK
    nc = -(-s_len // L)
    pad = nc * L - s_len
    xf = xh.astype(jnp.float32)

    def chunks(t, tail):
        t = jnp.pad(t, [(0, 0), (0, pad)] + [(0, 0)] * (t.ndim - 2))
        return jnp.moveaxis(t.reshape((bsz, nc, L) + tail), 1, 0)

    x_c = chunks(xf * dt[..., None], (SSM_GROUPS, SSM_HEADS_PER_GROUP, SSM_HEAD_DIM))
    a_c = chunks(dt * a_neg, (SSM_GROUPS, SSM_HEADS_PER_GROUP))
    b_c = chunks(bm.astype(jnp.float32), (SSM_GROUPS, SSM_STATE))
    c_c = chunks(cm.astype(jnp.float32), (SSM_GROUPS, SSM_STATE))
    causal = jnp.tril(jnp.ones((L, L), dtype=bool))

    def step(state, inp):
        xc, ac, bc, cc = inp
        acs = jnp.cumsum(ac, axis=1)
        seg = acs[:, :, None] - acs[:, None, :]
        decay = jnp.exp(jnp.where(causal[None, :, :, None, None], seg, -jnp.inf))
        cb = jnp.einsum('blgn,bsgn->blsg', cc, bc)
        y_diag = jnp.einsum('blsg,blsgh,bsghp->blghp', cb, decay, xc)
        y_off = jnp.einsum('blgn,bghpn->blghp', cc, state) * jnp.exp(acs)[..., None]
        decay_end = jnp.exp(acs[:, -1:] - acs)
        new_state = (state * jnp.exp(acs[:, -1])[..., None, None]
                     + jnp.einsum('bsgn,bsgh,bsghp->bghpn', bc, decay_end, xc))
        return new_state, y_diag + y_off

    state0 = jnp.zeros((bsz, SSM_GROUPS, SSM_HEADS_PER_GROUP, SSM_HEAD_DIM, SSM_STATE), jnp.float32)
    _, ys = lax.scan(step, state0, (x_c, a_c, b_c, c_c))
    y = jnp.moveaxis(ys, 0, 1).reshape(bsz, nc * L, SSM_HEADS, SSM_HEAD_DIM)[:, :s_len]
    return y + d_skip.astype(jnp.float32)[:, None] * xf


def memory_cross_attention(qm, mem, mem_norm_g, w_mem_kv):
    bsz, s_len = qm.shape[:2]
    kv = rms_norm(mem, mem_norm_g) @ w_mem_kv
    km, vm = jnp.split(kv, 2, axis=-1)
    km = km.reshape(bsz, -1, MEM_HEADS, MEM_HEAD_DIM).astype(jnp.float32)
    vm = vm.reshape(bsz, -1, MEM_HEADS, MEM_HEAD_DIM).astype(jnp.float32)
    s = jnp.einsum('bshd,bmhd->bhsm', qm.astype(jnp.float32) * MEM_HEAD_DIM ** -0.5, km)
    p = jax.nn.softmax(s, axis=-1)
    o = jnp.einsum('bhsm,bmhd->bshd', p, vm)
    return o.reshape(bsz, s_len, MEM_WIDTH).astype(qm.dtype)


def hybrid_layer(h, mem, rel_bias, mix_norm_g, w_in, b_gate, ssm_conv_w, ssm_conv_b, ssm_dt_bias,
                 ssm_A_log, ssm_D, ssm_norm_g, mem_norm_g, w_mem_kv, w_br_attn, w_br_ssm, w_br_mem,
                 w_out, ffn_norm_g, w_ffn_up, ffn_conv_w, ffn_conv_b, w_ffn_down):
    bsz, s_len, _ = h.shape
    u = rms_norm(h, mix_norm_g)
    proj = u @ w_in
    split_points = []
    acc = 0
    for sz in IN_SIZES[:-1]:
        acc += sz
        split_points.append(acc)
    q, k, v, z, xbc, dt_raw, qm, gate_logits = jnp.split(proj, split_points, axis=-1)

    hs = (bsz, s_len, ATTN_HEADS, ATTN_HEAD_DIM)
    o_attn = moba_attention(q.reshape(hs), k.reshape(hs), v.reshape(hs), rel_bias)

    xbc = jax.nn.silu(causal_dwconv(xbc, ssm_conv_w, ssm_conv_b))
    xs, bm, cm = jnp.split(xbc, [SSM_INNER, SSM_INNER + SSM_GROUPS * SSM_STATE], axis=-1)
    dt = jax.nn.softplus(dt_raw.astype(jnp.float32) + ssm_dt_bias.astype(jnp.float32))
    a_neg = -jnp.exp(ssm_A_log.astype(jnp.float32))
    y = ssd_chunked_scan(xs.reshape(bsz, s_len, SSM_HEADS, SSM_HEAD_DIM), dt, a_neg,
                         bm.reshape(bsz, s_len, SSM_GROUPS, SSM_STATE),
                         cm.reshape(bsz, s_len, SSM_GROUPS, SSM_STATE), ssm_D)
    o_ssm = gated_group_rms_norm(y.reshape(bsz, s_len, SSM_INNER), z, ssm_norm_g)

    o_mem = memory_cross_attention(qm.reshape(bsz, s_len, MEM_HEADS, MEM_HEAD_DIM), mem, mem_norm_g, w_mem_kv)

    gates = jax.nn.sigmoid((gate_logits + b_gate).reshape(bsz, s_len, N_BRANCHES, D_MODEL))
    merged = (gates[:, :, 0] * (o_attn @ w_br_attn) + gates[:, :, 1] * (o_ssm @ w_br_ssm)
              + gates[:, :, 2] * (o_mem @ w_br_mem))
    h = h + merged @ w_out

    hid = causal_dwconv(rms_norm(h, ffn_norm_g) @ w_ffn_up, ffn_conv_w, ffn_conv_b)
    g_ff, up = jnp.split(hid, 2, axis=-1)
    return h + (jax.nn.silu(g_ff) * up) @ w_ffn_down


def setup_inputs(seed: int = 0) -> dict:
    key = jax.random.key(seed)
    ks = jax.random.split(key, 24)
    f32 = jnp.float32

    def nrm(k, shape, scale):
        return jax.random.normal(k, shape, f32) * scale

    L = DEPTH
    D = D_MODEL
    F2 = 2 * FFN_HIDDEN
    dt0 = jnp.exp(jax.random.uniform(ks[8], (L, SSM_HEADS), f32, math.log(1e-3), math.log(1e-1)))
    return {
        'x': nrm(ks[0], (BATCH, SEQ, D), 1.0),
        'mem': nrm(ks[1], (BATCH, MEM_LEN, D), 1.0),
        'rel_bias': nrm(ks[2], (REL_BUCKETS, ATTN_HEADS), 0.3),
        'mix_norm_g': 1.0 + nrm(ks[3], (L, D), 0.05),
        'w_in': nrm(ks[4], (L, D, IN_WIDTH), D ** -0.5),
        'b_gate': nrm(ks[5], (L, N_BRANCHES * D), 0.1),
        'ssm_conv_w': nrm(ks[6], (L, SSM_CONV, SSM_CONV_DIM), SSM_CONV ** -0.5),
        'ssm_conv_b': nrm(ks[7], (L, SSM_CONV_DIM), 0.02),
        'ssm_dt_bias': dt0 + jnp.log(-jnp.expm1(-dt0)),
        'ssm_A_log': jnp.log(jax.random.uniform(ks[9], (L, SSM_HEADS), f32, 1.0, 16.0)),
        'ssm_D': 1.0 + nrm(ks[10], (L, SSM_HEADS), 0.1),
        'ssm_norm_g': 1.0 + nrm(ks[11], (L, SSM_INNER), 0.05),
        'mem_norm_g': 1.0 + nrm(ks[12], (L, D), 0.05),
        'w_mem_kv': nrm(ks[13], (L, D, 2 * MEM_WIDTH), D ** -0.5),
        'w_br_attn': nrm(ks[14], (L, ATTN_WIDTH, D), ATTN_WIDTH ** -0.5),
        'w_br_ssm': nrm(ks[15], (L, SSM_INNER, D), SSM_INNER ** -0.5),
        'w_br_mem': nrm(ks[16], (L, MEM_WIDTH, D), MEM_WIDTH ** -0.5),
        'w_out': nrm(ks[17], (L, D, D), D ** -0.5),
        'ffn_norm_g': 1.0 + nrm(ks[18], (L, D), 0.05),
        'w_ffn_up': nrm(ks[19], (L, D, F2), D ** -0.5),
        'ffn_conv_w': nrm(ks[20], (L, FFN_CONV, F2), FFN_CONV ** -0.5),
        'ffn_conv_b': nrm(ks[21], (L, F2), 0.02),
        'w_ffn_down': nrm(ks[22], (L, FFN_HIDDEN, D), FFN_HIDDEN ** -0.5),
        'final_norm_g': 1.0 + nrm(ks[23], (D,), 0.05),
    }


def reference(x, mem, rel_bias, mix_norm_g, w_in, b_gate, ssm_conv_w, ssm_conv_b, ssm_dt_bias, ssm_A_log,
              ssm_D, ssm_norm_g, mem_norm_g, w_mem_kv, w_br_attn, w_br_ssm, w_br_mem, w_out, ffn_norm_g,
              w_ffn_up, ffn_conv_w, ffn_conv_b, w_ffn_down, final_norm_g):
    h = x
    for l in range(DEPTH):
        h = hybrid_layer(h, mem, rel_bias, mix_norm_g[l], w_in[l], b_gate[l], ssm_conv_w[l], ssm_conv_b[l],
                         ssm_dt_bias[l], ssm_A_log[l], ssm_D[l], ssm_norm_g[l], mem_norm_g[l], w_mem_kv[l],
                         w_br_attn[l], w_br_ssm[l], w_br_mem[l], w_out[l], ffn_norm_g[l], w_ffn_up[l],
                         ffn_conv_w[l], ffn_conv_b[l], w_ffn_down[l])
    return rms_norm(h, final_norm_g)
```

```python
import functools
import math

import numpy as np
import jax
import jax.numpy as jnp
from jax import lax
from jax.experimental import pallas as pl
from jax.experimental.pallas import tpu as pltpu

F32 = jnp.float32
BF16 = jnp.bfloat16

D_MODEL = 1024
ATTN_HEADS = 8
ATTN_HEAD_DIM = 128
MOBA_BLOCK = 256
MOBA_TOPK = 3
REL_BUCKETS = 32
REL_MAX_DIST = 1024
SSM_INNER = 2048
SSM_HEAD_DIM = 64
SSM_HEADS = 32
SSM_GROUPS = 4
SSM_STATE = 128
SSM_CONV = 4
SSM_CHUNK = 256
SSM_CONV_DIM = 3072
MEM_HEADS = 4
MEM_HEAD_DIM = 256
FFN_HIDDEN = 2816
FFN_CONV = 3
NORM_EPS = 1e-6

LANES = 128
SUBLANES = 8
VMEM_LIMIT = 56 * 1024 * 1024

COL_XBC = 0
COL_GATE = 3072
COL_Z = 6144
COL_Q = 8192
COL_K = 9216
COL_V = 10240
COL_QM = 11264
PROJ_WIDTH = 12288

MOBA_NEAR = -(-(REL_MAX_DIST + MOBA_BLOCK - 1) // MOBA_BLOCK)
NEG_BIG = -1e30


def _cparams(sem):
    return pltpu.CompilerParams(dimension_semantics=sem, vmem_limit_bytes=VMEM_LIMIT)


def _sigmoid(x):
    return 1.0 / (1.0 + jnp.exp(-x))


def _silu(x):
    return x * _sigmoid(x)


def _split_bf16(x, parts):
    out = []
    r = x
    for _ in range(parts):
        hi = r.astype(BF16)
        out.append(hi)
        r = r - hi.astype(F32)
    return out


def _inproj_kernel(x_ref, g_ref, w_ref, wdt_ref, proj_ref, dt_ref, kmean_ref, u_ref, *, k_tile):
    j = pl.program_id(1)

    @pl.when(j == 0)
    def _():
        x = x_ref[...]
        ms = jnp.mean(x * x, axis=-1, keepdims=True)
        u = (x * lax.rsqrt(ms + NORM_EPS) * g_ref[...]).astype(BF16)
        u_ref[...] = u
        dt_ref[...] = jnp.dot(u, wdt_ref[...], preferred_element_type=F32)

    acc = jnp.dot(u_ref[...], w_ref[...], preferred_element_type=F32)
    proj_ref[...] = acc.astype(BF16)

    @pl.when(j == k_tile)
    def _():
        blk = MOBA_BLOCK
        for r in range(acc.shape[0] // blk):
            kmean_ref[0, r:r + 1, :] = jnp.sum(acc[r * blk:(r + 1) * blk], axis=0, keepdims=True) * (1.0 / blk)


def _in_proj(x2, g, w_cat, w_dt, tm, tn):
    t = x2.shape[0]
    kw = ATTN_HEADS * ATTN_HEAD_DIM
    assert tn == kw and COL_K % tn == 0 and tm % MOBA_BLOCK == 0
    return pl.pallas_call(
        functools.partial(_inproj_kernel, k_tile=COL_K // tn),
        grid=(t // tm, PROJ_WIDTH // tn),
        in_specs=[
            pl.BlockSpec((tm, D_MODEL), lambda i, j: (i, 0)),
            pl.BlockSpec((1, D_MODEL), lambda i, j: (0, 0)),
            pl.BlockSpec((D_MODEL, tn), lambda i, j: (0, j)),
            pl.BlockSpec((D_MODEL, LANES), lambda i, j: (0, 0)),
        ],
        out_specs=[
            pl.BlockSpec((tm, tn), lambda i, j: (i, j)),
            pl.BlockSpec((tm, LANES), lambda i, j: (i, 0)),
            pl.BlockSpec((1, tm // MOBA_BLOCK, kw), lambda i, j: (i, 0, 0)),
        ],
        out_shape=[
            jax.ShapeDtypeStruct((t, PROJ_WIDTH), BF16),
            jax.ShapeDtypeStruct((t, LANES), F32),
            jax.ShapeDtypeStruct((t // tm, tm // MOBA_BLOCK, kw), F32),
        ],
        scratch_shapes=[pltpu.VMEM((tm, D_MODEL), BF16)],
        compiler_params=_cparams(("arbitrary", "arbitrary")),
        name="in_proj",
    )(x2, g, w_cat, w_dt)


def _t5_bucket(dist):
    n = jnp.maximum(dist, 0)
    max_exact = REL_BUCKETS // 2
    nf = jnp.maximum(n, max_exact).astype(F32)
    large = max_exact + (jnp.log(nf * (1.0 / max_exact)) / math.log(REL_MAX_DIST / max_exact)
                         * (REL_BUCKETS - max_exact)).astype(jnp.int32)
    large = jnp.minimum(large, REL_BUCKETS - 1)
    return jnp.where(n < max_exact, n, large)


def _moba_kernel(rel_ref, q_ref, k_ref, v_ref, km_ref, o_ref, kmean_ref, bias_ref, m_ref, l_ref, acc_ref, *, nb):
    h = pl.program_id(0)
    b = pl.program_id(1)
    i = pl.program_id(2)
    blk = MOBA_BLOCK
    scale = ATTN_HEAD_DIM ** -0.5

    @pl.when((b == 0) & (i == 0))
    def _():
        r = lax.broadcasted_iota(jnp.int32, (blk, blk), 0)
        c = lax.broadcasted_iota(jnp.int32, (blk, blk), 1)
        for d in range(MOBA_NEAR):
            bucket = _t5_bucket(d * blk + r - c)
            tile = jnp.zeros((blk, blk), F32)
            for bk in range(REL_BUCKETS):
                tile = jnp.where(bucket == bk, rel_ref[h, bk], tile)
            bias_ref[d] = tile

    @pl.when(i == 0)
    def _():
        kmean_ref[...] = jnp.zeros_like(kmean_ref)
        kmean_ref[0:nb, :] = km_ref[0]

    q = q_ref[0]

    nt = (((1,), (1,)), ((), ()))
    gate = lax.dot_general(q, kmean_ref[...].astype(BF16), nt, preferred_element_type=F32)
    lane = lax.broadcasted_iota(jnp.int32, (blk, LANES), 1).astype(F32)
    i_f = i.astype(F32)
    g = jnp.where(lane < i_f, gate, -jnp.inf)
    sel = jnp.zeros((blk, LANES), F32)
    for t in range(MOBA_TOPK):
        mx = jnp.max(g, axis=1, keepdims=True)
        idx = jnp.min(jnp.where(g == mx, lane, float(LANES)), axis=1, keepdims=True)
        hit = lane == idx
        sel = jnp.maximum(sel, jnp.where(hit, jnp.where(i > t, 1.0, 0.0), 0.0))
        g = jnp.where(hit, -jnp.inf, g)

    r = lax.broadcasted_iota(jnp.int32, (blk, blk), 0)
    c = lax.broadcasted_iota(jnp.int32, (blk, blk), 1)
    own = pl.multiple_of(i * blk, blk)
    k_own = k_ref[0, pl.ds(own, blk), :]
    v_own = v_ref[0, pl.ds(own, blk), :]
    s = lax.dot_general(q, k_own, nt, preferred_element_type=F32) * scale + bias_ref[0]
    s = jnp.where(r >= c, s, NEG_BIG)
    m0 = jnp.max(s, axis=1, keepdims=True)
    p = jnp.exp(s - m0)
    m_ref[...] = m0
    l_ref[...] = jnp.sum(p, axis=1, keepdims=True)
    acc_ref[...] = jnp.dot(p.astype(BF16), v_own, preferred_element_type=F32)

    def past_block(j, bias):
        start = pl.multiple_of(j * blk, blk)
        kj = k_ref[0, pl.ds(start, blk), :]
        vj = v_ref[0, pl.ds(start, blk), :]
        picked = jnp.sum(jnp.where(lane == j.astype(F32), sel, 0.0), axis=1, keepdims=True)
        sj = lax.dot_general(q, kj, nt, preferred_element_type=F32) * scale + bias
        sj = jnp.where(picked > 0.5, sj, NEG_BIG)
        m_old = m_ref[...]
        m_new = jnp.maximum(m_old, jnp.max(sj, axis=1, keepdims=True))
        alpha = jnp.exp(m_old - m_new)
        pj = jnp.exp(sj - m_new)
        l_ref[...] = alpha * l_ref[...] + jnp.sum(pj, axis=1, keepdims=True)
        acc_ref[...] = alpha * acc_ref[...] + jnp.dot(pj.astype(BF16), vj, preferred_element_type=F32)
        m_ref[...] = m_new

    n_far = jnp.maximum(i - (MOBA_NEAR - 1), 0)
    far_bias = rel_ref[h, REL_BUCKETS - 1]

    def far_body(j, carry):
        past_block(j, far_bias)
        return carry

    def near_body(j, carry):
        past_block(j, bias_ref[i - j])
        return carry

    lax.fori_loop(0, n_far, far_body, 0)
    lax.fori_loop(n_far, i, near_body, 0)

    o_ref[0] = (acc_ref[...] / l_ref[...]).astype(o_ref.dtype)


def _moba(proj3, kmean3, rel_t):
    bsz, s_len, _ = proj3.shape
    blk = MOBA_BLOCK
    nb = s_len // blk
    assert nb * blk == s_len and nb <= LANES
    cb = LANES
    return pl.pallas_call(
        functools.partial(_moba_kernel, nb=nb),
        grid=(ATTN_HEADS, bsz, nb),
        in_specs=[
            pl.BlockSpec(memory_space=pltpu.SMEM),
            pl.BlockSpec((1, blk, cb), lambda h, b, i: (b, i, COL_Q // cb + h)),
            pl.BlockSpec((1, s_len, cb), lambda h, b, i: (b, 0, COL_K // cb + h)),
            pl.BlockSpec((1, s_len, cb), lambda h, b, i: (b, 0, COL_V // cb + h)),
            pl.BlockSpec((1, nb, cb), lambda h, b, i: (b, 0, h)),
        ],
        out_specs=pl.BlockSpec((1, blk, cb), lambda h, b, i: (b, i, h)),
        out_shape=jax.ShapeDtypeStruct((bsz, s_len, ATTN_HEADS * ATTN_HEAD_DIM), BF16),
        scratch_shapes=[
            pltpu.VMEM((LANES, ATTN_HEAD_DIM), F32),
            pltpu.VMEM((MOBA_NEAR, blk, blk), F32),
            pltpu.VMEM((blk, 1), F32),
            pltpu.VMEM((blk, 1), F32),
            pltpu.VMEM((blk, ATTN_HEAD_DIM), F32),
        ],
        compiler_params=_cparams(("arbitrary", "arbitrary", "arbitrary")),
        name="moba",
    )(rel_t, proj3, proj3, proj3, kmean3)


def _ssd_kernel(xbc_ref, z_ref, dt_ref, cw_ref, cbias_ref, dtb_ref, alog_ref, dskip_ref, ng_ref,
                tril_ref, e_ref, o_ref, xpad_ref, state_ref):
    L = SSM_CHUNK
    G = SSM_GROUPS
    N = SSM_STATE
    GW = SSM_INNER // G
    pad = SUBLANES
    nt = (((1,), (1,)), ((), ()))

    @pl.when(pl.program_id(1) == 0)
    def _():
        xpad_ref[0:pad, :] = jnp.zeros((pad, SSM_CONV_DIM), F32)
        state_ref[...] = jnp.zeros_like(state_ref)

    xpad_ref[pad:pad + L, :] = xbc_ref[0].astype(F32)
    conv = cbias_ref[...]
    for k in range(SSM_CONV):
        off = pad - (SSM_CONV - 1) + k
        conv = conv + xpad_ref[off:off + L, :] * cw_ref[k:k + 1, :]
    xpad_ref[0:pad, :] = xpad_ref[L:L + pad, :]
    xa = _silu(conv)
    xs = xa[:, :SSM_INNER]
    bm = xa[:, SSM_INNER:SSM_INNER + G * N]
    cm = xa[:, SSM_INNER + G * N:].astype(BF16)

    dtr = dt_ref[0] + dtb_ref[...]
    dt = jnp.maximum(dtr, 0.0) + jnp.log1p(jnp.exp(-jnp.abs(dtr)))
    a = dt * (-jnp.exp(alog_ref[...]))
    a_cat = jnp.concatenate(_split_bf16(a, 3), axis=0)
    acs = jnp.dot(tril_ref[...], a_cat, preferred_element_type=F32)
    acs_t = acs.T

    def expand(v):
        v_cat = jnp.concatenate(_split_bf16(v, 2), axis=1)
        return jnp.dot(v_cat, e_ref[...], preferred_element_type=F32)

    dt_x = expand(dt)
    acs_x = expand(acs)
    xc = xs * dt_x
    xc_b = xc.astype(BF16)
    last = acs_x[L - 1:L, :]
    ea_x = jnp.exp(acs_x)
    w_end = (jnp.exp(last - acs_x) * xc).astype(BF16)
    ea_last = jnp.exp(last)

    row = lax.broadcasted_iota(jnp.int32, (L, L), 0)
    col = lax.broadcasted_iota(jnp.int32, (L, L), 1)
    causal = row >= col
    lane = lax.broadcasted_iota(jnp.int32, (L, LANES), 1)
    low_half = lane < SSM_HEAD_DIM

    y_parts = []
    for g in range(G):
        bg = bm[:, g * N:(g + 1) * N]
        cg = cm[:, g * N:(g + 1) * N]
        cb = lax.dot_general(cg, bg.astype(BF16), nt, preferred_element_type=F32)
        st = state_ref[g]
        y_off = jnp.dot(cg, st.astype(BF16), preferred_element_type=F32) * ea_x[:, g * GW:(g + 1) * GW]
        for pr in range(GW // LANES):
            c0 = g * GW + pr * LANES
            x_pair = xc_b[:, c0:c0 + LANES]
            zero = jnp.zeros_like(x_pair)
            y_pair = y_off[:, pr * LANES:(pr + 1) * LANES]
            for half in range(2):
                hd = c0 // SSM_HEAD_DIM + half
                seg = acs[:, hd:hd + 1] - acs_t[hd:hd + 1, :]
                decay = jnp.exp(jnp.where(causal, seg, -jnp.inf))
                mmat = (cb * decay).astype(BF16)
                x_half = jnp.where(low_half, x_pair, zero) if half == 0 else jnp.where(low_half, zero, x_pair)
                y_pair = y_pair + jnp.dot(mmat, x_half, preferred_element_type=F32)
            y_parts.append(y_pair)
        bg_t = bg.T.astype(BF16)
        state_ref[g] = (st * ea_last[:, g * GW:(g + 1) * GW]
                        + jnp.dot(bg_t, w_end[:, g * GW:(g + 1) * GW], preferred_element_type=F32))

    y = jnp.concatenate(y_parts, axis=1) + dskip_ref[...] * xs
    v = y * _silu(z_ref[0].astype(F32))
    outs = []
    for g in range(G):
        vg = v[:, g * GW:(g + 1) * GW]
        ms = jnp.mean(vg * vg, axis=-1, keepdims=True)
        outs.append(vg * lax.rsqrt(ms + NORM_EPS))
    o_ref[0] = (jnp.concatenate(outs, axis=1) * ng_ref[...]).astype(o_ref.dtype)


def _ssd(proj3, dt3, conv_w, conv_b, dt_bias, a_log, d_skip_x, norm_g, tril3, e2):
    bsz, s_len, _ = proj3.shape
    L = SSM_CHUNK
    nc = s_len // L
    assert nc * L == s_len
    const = lambda b, c: (0, 0)
    return pl.pallas_call(
        _ssd_kernel,
        grid=(bsz, nc),
        in_specs=[
            pl.BlockSpec((1, L, SSM_CONV_DIM), lambda b, c: (b, c, COL_XBC // SSM_CONV_DIM)),
            pl.BlockSpec((1, L, SSM_INNER), lambda b, c: (b, c, COL_Z // SSM_INNER)),
            pl.BlockSpec((1, L, LANES), lambda b, c: (b, c, 0)),
            pl.BlockSpec((SSM_CONV, SSM_CONV_DIM), const),
            pl.BlockSpec((1, SSM_CONV_DIM), const),
            pl.BlockSpec((1, LANES), const),
            pl.BlockSpec((1, LANES), const),
            pl.BlockSpec((1, SSM_INNER), const),
            pl.BlockSpec((1, SSM_INNER), const),
            pl.BlockSpec((L, 3 * L), const),
            pl.BlockSpec((2 * LANES, SSM_INNER), const),
        ],
        out_specs=pl.BlockSpec((1, L, SSM_INNER), lambda b, c: (b, c, 0)),
        out_shape=jax.ShapeDtypeStruct((bsz, s_len, SSM_INNER), BF16),
        scratch_shapes=[
            pltpu.VMEM((L + 2 * SUBLANES, SSM_CONV_DIM), F32),
            pltpu.VMEM((SSM_GROUPS, SSM_STATE, SSM_INNER // SSM_GROUPS), F32),
        ],
        compiler_params=_cparams(("arbitrary", "arbitrary")),
        name="ssd",
    )(proj3, proj3, dt3, conv_w, conv_b, dt_bias, a_log, d_skip_x, norm_g, tril3, e2)


def _memkv_kernel(mem_ref, g_ref, w_ref, kv_ref):
    x = mem_ref[...]
    ms = jnp.mean(x * x, axis=-1, keepdims=True)
    u = (x * lax.rsqrt(ms + NORM_EPS) * g_ref[...]).astype(BF16)
    kv_ref[...] = jnp.dot(u, w_ref[...], preferred_element_type=F32).astype(BF16)


def _mem_kv(mem2, g, w_kv):
    rows = mem2.shape[0]
    width = w_kv.shape[1]
    return pl.pallas_call(
        _memkv_kernel,
        grid=(1,),
        in_specs=[
            pl.BlockSpec((rows, D_MODEL), lambda i: (0, 0)),
            pl.BlockSpec((1, D_MODEL), lambda i: (0, 0)),
            pl.BlockSpec((D_MODEL, width), lambda i: (0, 0)),
        ],
        out_specs=pl.BlockSpec((rows, width), lambda i: (0, 0)),
        out_shape=jax.ShapeDtypeStruct((rows, width), BF16),
        compiler_params=_cparams(("arbitrary",)),
        name="mem_kv",
    )(mem2, g, w_kv)


def _merge_kernel(x_ref, oa_ref, os_ref, qm_ref, gl_ref, kv_ref, bg_ref, wa_ref, ws_ref, wm_ref, wo_ref, h_ref):
    nt = (((1,), (1,)), ((), ()))
    hd = MEM_HEAD_DIM
    width = MEM_HEADS * hd
    scale = hd ** -0.5
    qm = qm_ref[0]
    kv = kv_ref[0]
    outs = []
    for hh in range(MEM_HEADS):
        q = qm[:, hh * hd:(hh + 1) * hd]
        km = kv[:, hh * hd:(hh + 1) * hd]
        vm = kv[:, width + hh * hd:width + (hh + 1) * hd]
        s = lax.dot_general(q, km, nt, preferred_element_type=F32) * scale
        p = jnp.exp(s - jnp.max(s, axis=1, keepdims=True))
        o = jnp.dot(p.astype(BF16), vm, preferred_element_type=F32)
        outs.append(o / jnp.sum(p, axis=1, keepdims=True))
    o_mem = jnp.concatenate(outs, axis=1).astype(BF16)

    gates = _sigmoid(gl_ref[0].astype(F32) + bg_ref[...])
    merged = (gates[:, :D_MODEL] * jnp.dot(oa_ref[0], wa_ref[...], preferred_element_type=F32)
              + gates[:, D_MODEL:2 * D_MODEL] * jnp.dot(os_ref[0], ws_ref[...], preferred_element_type=F32)
              + gates[:, 2 * D_MODEL:] * jnp.dot(o_mem, wm_ref[...], preferred_element_type=F32))
    h_ref[0] = x_ref[0] + jnp.dot(merged.astype(BF16), wo_ref[...], preferred_element_type=F32)


def _resident(shape):
    return pl.BlockSpec(shape, lambda *_: (0,) * len(shape), pipeline_mode=pl.Buffered(1))


def _merge(x, o_attn, o_ssm, proj3, kv3, b_gate, wa, ws, wm, wo, tm):
    bsz, s_len, _ = x.shape
    mem_len = kv3.shape[1]
    return pl.pallas_call(
        _merge_kernel,
        grid=(bsz, s_len // tm),
        in_specs=[
            pl.BlockSpec((1, tm, D_MODEL), lambda b, i: (b, i, 0)),
            pl.BlockSpec((1, tm, D_MODEL), lambda b, i: (b, i, 0)),
            pl.BlockSpec((1, tm, SSM_INNER), lambda b, i: (b, i, 0)),
            pl.BlockSpec((1, tm, D_MODEL), lambda b, i: (b, i, COL_QM // D_MODEL)),
            pl.BlockSpec((1, tm, 3 * D_MODEL), lambda b, i: (b, i, COL_GATE // (3 * D_MODEL))),
            pl.BlockSpec((1, mem_len, 2 * D_MODEL), lambda b, i: (b, 0, 0)),
            _resident((1, 3 * D_MODEL)),
            _resident((D_MODEL, D_MODEL)),
            _resident((SSM_INNER, D_MODEL)),
            _resident((D_MODEL, D_MODEL)),
            _resident((D_MODEL, D_MODEL)),
        ],
        out_specs=pl.BlockSpec((1, tm, D_MODEL), lambda b, i: (b, i, 0)),
        out_shape=jax.ShapeDtypeStruct((bsz, s_len, D_MODEL), F32),
        compiler_params=_cparams(("arbitrary", "arbitrary")),
        name="merge",
    )(x, o_attn, o_ssm, proj3, proj3, kv3, b_gate, wa, ws, wm, wo)


def _ffn_kernel(h_ref, ng_ref, wup_ref, cw_ref, cb_ref, wdn_ref, fg_ref, o_ref, hid_ref, *, tm):
    pad = SUBLANES

    @pl.when(pl.program_id(1) == 0)
    def _():
        hid_ref[0:pad, :] = jnp.zeros((pad, 2 * FFN_HIDDEN), F32)

    h = h_ref[0]
    ms = jnp.mean(h * h, axis=-1, keepdims=True)
    u = (h * lax.rsqrt(ms + NORM_EPS) * ng_ref[...]).astype(BF16)
    hid_ref[pad:pad + tm, :] = jnp.dot(u, wup_ref[...], preferred_element_type=F32)
    conv = cb_ref[...]
    for k in range(FFN_CONV):
        off = pad - (FFN_CONV - 1) + k
        conv = conv + hid_ref[off:off + tm, :] * cw_ref[k:k + 1, :]
    hid_ref[0:pad, :] = hid_ref[tm:tm + pad, :]
    act = (_silu(conv[:, :FFN_HIDDEN]) * conv[:, FFN_HIDDEN:]).astype(BF16)
    y = h + jnp.dot(act, wdn_ref[...], preferred_element_type=F32)
    ms2 = jnp.mean(y * y, axis=-1, keepdims=True)
    o_ref[0] = y * lax.rsqrt(ms2 + NORM_EPS) * fg_ref[...]


def _ffn(h, norm_g, w_up, conv_w, conv_b, w_down, final_g, tm):
    bsz, s_len, _ = h.shape
    f2 = 2 * FFN_HIDDEN
    return pl.pallas_call(
        functools.partial(_ffn_kernel, tm=tm),
        grid=(bsz, s_len // tm),
        in_specs=[
            pl.BlockSpec((1, tm, D_MODEL), lambda b, i: (b, i, 0)),
            _resident((1, D_MODEL)),
            _resident((D_MODEL, f2)),
            _resident((FFN_CONV, f2)),
            _resident((1, f2)),
            _resident((FFN_HIDDEN, D_MODEL)),
            _resident((1, D_MODEL)),
        ],
        out_specs=pl.BlockSpec((1, tm, D_MODEL), lambda b, i: (b, i, 0)),
        out_shape=jax.ShapeDtypeStruct((bsz, s_len, D_MODEL), F32),
        scratch_shapes=[pltpu.VMEM((tm + 2 * SUBLANES, f2), F32)],
        compiler_params=_cparams(("arbitrary", "arbitrary")),
        name="ffn",
    )(h, norm_g, w_up, conv_w, conv_b, w_down, final_g)


def _ssd_constants():
    L = SSM_CHUNK
    tril = np.tril(np.ones((L, L), np.float32))
    tril3 = np.concatenate([tril, tril, tril], axis=1)
    e = np.zeros((LANES, SSM_INNER), np.float32)
    for hd in range(SSM_HEADS):
        e[hd, hd * SSM_HEAD_DIM:(hd + 1) * SSM_HEAD_DIM] = 1.0
    e2 = np.concatenate([e, e], axis=0)
    return jnp.asarray(tril3, BF16), jnp.asarray(e2, BF16)


def _pad_lanes(v):
    return jnp.pad(v, ((0, 0), (0, LANES - v.shape[1])))


def _layer(h, mem, rel_bias, mix_norm_g, w_in, b_gate, ssm_conv_w, ssm_conv_b, ssm_dt_bias, ssm_A_log, ssm_D,
           ssm_norm_g, mem_norm_g, w_mem_kv, w_br_attn, w_br_ssm, w_br_mem, w_out, ffn_norm_g, w_ffn_up,
           ffn_conv_w, ffn_conv_b, w_ffn_down, final_g):
    bsz, s_len, _ = h.shape
    w_cat = jnp.concatenate([w_in[:, 5120:8192], w_in[:, 9248:12320], w_in[:, 3072:5120], w_in[:, 0:3072],
                             w_in[:, 8224:9248]], axis=1).astype(BF16)
    w_dt = _pad_lanes(w_in[:, 8192:8224]).astype(BF16)

    proj, dt_raw, kmean = _in_proj(h.reshape(bsz * s_len, D_MODEL), mix_norm_g[None, :], w_cat, w_dt,
                                   tm=1024, tn=1024)
    proj3 = proj.reshape(bsz, s_len, PROJ_WIDTH)
    dt3 = dt_raw.reshape(bsz, s_len, LANES)
    kmean3 = kmean.reshape(bsz, s_len // MOBA_BLOCK, ATTN_HEADS * ATTN_HEAD_DIM)

    o_attn = _moba(proj3, kmean3, rel_bias.T)

    tril3, e2 = _ssd_constants()
    o_ssm = _ssd(proj3, dt3, ssm_conv_w, ssm_conv_b[None, :], _pad_lanes(ssm_dt_bias[None, :]),
                 _pad_lanes(ssm_A_log[None, :]), jnp.repeat(ssm_D, SSM_HEAD_DIM)[None, :], ssm_norm_g[None, :],
                 tril3, e2)

    mem_len = mem.shape[1]
    kv = _mem_kv(mem.reshape(bsz * mem_len, D_MODEL), mem_norm_g[None, :], w_mem_kv.astype(BF16))
    kv3 = kv.reshape(bsz, mem_len, 2 * D_MODEL)

    h1 = _merge(h, o_attn, o_ssm, proj3, kv3, b_gate[None, :], w_br_attn.astype(BF16), w_br_ssm.astype(BF16),
                w_br_mem.astype(BF16), w_out.astype(BF16), tm=512)

    return _ffn(h1, ffn_norm_g[None, :], w_ffn_up.astype(BF16), ffn_conv_w, ffn_conv_b[None, :],
                w_ffn_down.astype(BF16), final_g[None, :], tm=256)


def kernel(x, mem, rel_bias, mix_norm_g, w_in, b_gate, ssm_conv_w, ssm_conv_b, ssm_dt_bias, ssm_A_log, ssm_D,
           ssm_norm_g, mem_norm_g, w_mem_kv, w_br_attn, w_br_ssm, w_br_mem, w_out, ffn_norm_g, w_ffn_up,
           ffn_conv_w, ffn_conv_b, w_ffn_down, final_norm_g):
    assert w_in.shape[0] == 1, "single-layer trunk"
    return _layer(x, mem, rel_bias, mix_norm_g[0], w_in[0], b_gate[0], ssm_conv_w[0], ssm_conv_b[0],
                  ssm_dt_bias[0], ssm_A_log[0], ssm_D[0], ssm_norm_g[0], mem_norm_g[0], w_mem_kv[0],
                  w_br_attn[0], w_br_ssm[0], w_br_mem[0], w_out[0], ffn_norm_g[0], w_ffn_up[0],
                  ffn_conv_w[0], ffn_conv_b[0], w_ffn_down[0], final_norm_g)
```

```python
import functools
import math

import numpy as np
import jax
import jax.numpy as jnp
from jax import lax
from jax.experimental import pallas as pl
from jax.experimental.pallas import tpu as pltpu

F32 = jnp.float32
BF16 = jnp.bfloat16

D_MODEL = 1024
ATTN_HEADS = 8
ATTN_HEAD_DIM = 128
MOBA_BLOCK = 256
MOBA_TOPK = 3
REL_BUCKETS = 32
REL_MAX_DIST = 1024
SSM_INNER = 2048
SSM_HEAD_DIM = 64
SSM_HEADS = 32
SSM_GROUPS = 4
SSM_STATE = 128
SSM_CONV = 4
SSM_CHUNK = 256
SSM_CONV_DIM = 3072
MEM_HEADS = 4
MEM_HEAD_DIM = 256
FFN_HIDDEN = 2816
FFN_CONV = 3
NORM_EPS = 1e-6

LANES = 128
SUBLANES = 8
VMEM_LIMIT = 56 * 1024 * 1024

COL_XBC = 0
COL_GATE = 3072
COL_Z = 6144
COL_Q = 8192
COL_K = 9216
COL_V = 10240
COL_QM = 11264
PROJ_WIDTH = 12288

MOBA_NEAR = -(-(REL_MAX_DIST + MOBA_BLOCK - 1) // MOBA_BLOCK)
NEG_BIG = -1e30
LOG2E = math.log2(math.e)
MOBA_GROUP = 4
assert MOBA_GROUP <= MOBA_NEAR + 1


def _cparams(sem):
    return pltpu.CompilerParams(dimension_semantics=sem, vmem_limit_bytes=VMEM_LIMIT)


def _sigmoid(x):
    return 1.0 / (1.0 + jnp.exp(-x))


def _silu(x):
    return x * _sigmoid(x)


def _split_bf16(x, parts):
    out = []
    r = x
    for _ in range(parts):
        hi = r.astype(BF16)
        out.append(hi)
        r = r - hi.astype(F32)
    return out


def _inproj_kernel(x_ref, g_ref, w_ref, wdt_ref, proj_ref, dt_ref, kmean_ref, u_ref, *, k_tile):
    j = pl.program_id(1)

    @pl.when(j == 0)
    def _():
        x = x_ref[...]
        ms = jnp.mean(x * x, axis=-1, keepdims=True)
        u = (x * lax.rsqrt(ms + NORM_EPS) * g_ref[...]).astype(BF16)
        u_ref[...] = u
        dt_ref[...] = jnp.dot(u, wdt_ref[...], preferred_element_type=F32)

    acc = jnp.dot(u_ref[...], w_ref[...], preferred_element_type=F32)
    proj_ref[...] = acc.astype(BF16)

    @pl.when(j == k_tile)
    def _():
        blk = MOBA_BLOCK
        for r in range(acc.shape[0] // blk):
            kmean_ref[0, r:r + 1, :] = jnp.sum(acc[r * blk:(r + 1) * blk], axis=0, keepdims=True) * (1.0 / blk)


def _in_proj(x2, g, w_cat, w_dt, tm, tn):
    t = x2.shape[0]
    kw = ATTN_HEADS * ATTN_HEAD_DIM
    assert tn == kw and COL_K % tn == 0 and tm % MOBA_BLOCK == 0
    return pl.pallas_call(
        functools.partial(_inproj_kernel, k_tile=COL_K // tn),
        grid=(t // tm, PROJ_WIDTH // tn),
        in_specs=[
            pl.BlockSpec((tm, D_MODEL), lambda i, j: (i, 0)),
            pl.BlockSpec((1, D_MODEL), lambda i, j: (0, 0)),
            pl.BlockSpec((D_MODEL, tn), lambda i, j: (0, j)),
            pl.BlockSpec((D_MODEL, LANES), lambda i, j: (0, 0)),
        ],
        out_specs=[
            pl.BlockSpec((tm, tn), lambda i, j: (i, j)),
            pl.BlockSpec((tm, LANES), lambda i, j: (i, 0)),
            pl.BlockSpec((1, tm // MOBA_BLOCK, kw), lambda i, j: (i, 0, 0)),
        ],
        out_shape=[
            jax.ShapeDtypeStruct((t, PROJ_WIDTH), BF16),
            jax.ShapeDtypeStruct((t, LANES), F32),
            jax.ShapeDtypeStruct((t // tm, tm // MOBA_BLOCK, kw), F32),
        ],
        scratch_shapes=[pltpu.VMEM((tm, D_MODEL), BF16)],
        compiler_params=_cparams(("arbitrary", "arbitrary")),
        name="in_proj",
    )(x2, g, w_cat, w_dt)


def _t5_bucket(dist):
    n = jnp.maximum(dist, 0)
    max_exact = REL_BUCKETS // 2
    nf = jnp.maximum(n, max_exact).astype(F32)
    large = max_exact + (jnp.log(nf * (1.0 / max_exact)) / math.log(REL_MAX_DIST / max_exact)
                         * (REL_BUCKETS - max_exact)).astype(jnp.int32)
    large = jnp.minimum(large, REL_BUCKETS - 1)
    return jnp.where(n < max_exact, n, large)


def _moba_kernel(rel_ref, q_ref, k_ref, v_ref, km_ref, o_ref, kmean_ref, bias_ref, vt_ref, neg_ref, s_ref, m_ref,
                 l_ref, acc_ref, *, nb):
    h = pl.program_id(0)
    b = pl.program_id(1)
    i = pl.program_id(2)
    blk = MOBA_BLOCK
    scale = ATTN_HEAD_DIM ** -0.5


    @pl.when((b == 0) & (i == 0))
    def _():
        key = lax.broadcasted_iota(jnp.int32, (blk, blk), 0)
        qry = lax.broadcasted_iota(jnp.int32, (blk, blk), 1)
        for d in range(MOBA_NEAR):
            bucket = _t5_bucket(d * blk + qry - key)
            tile = jnp.zeros((blk, blk), F32)
            for bk in range(REL_BUCKETS):
                tile = jnp.where(bucket == bk, rel_ref[h, bk], tile)
            bias_ref[d] = tile * LOG2E

    @pl.when(i == 0)
    def _():
        kmean_ref[...] = jnp.zeros_like(kmean_ref)
        kmean_ref[0:nb, :] = km_ref[0]
        for jb in range(nb):
            vt_ref[jb] = v_ref[0, jb * blk:(jb + 1) * blk, :].astype(F32).T.astype(BF16)

    q_t = q_ref[0].astype(F32).T
    qs_t = (q_t * (scale * LOG2E)).astype(BF16)

    gate = jnp.dot(kmean_ref[...].astype(BF16), q_t.astype(BF16), preferred_element_type=F32)
    row = lax.broadcasted_iota(jnp.int32, (LANES, blk), 0).astype(F32)
    g = jnp.where(row < i.astype(F32), gate, -jnp.inf)
    sel = jnp.zeros((LANES, blk), F32)
    for t in range(MOBA_TOPK):
        mx = jnp.max(g, axis=0, keepdims=True)
        idx = jnp.min(jnp.where(g == mx, row, float(LANES)), axis=0, keepdims=True)
        hit = row == idx
        sel = jnp.maximum(sel, jnp.where(hit, jnp.where(i > t, 1.0, 0.0), 0.0))
        g = jnp.where(hit, -jnp.inf, g)
    neg_ref[...] = jnp.where(sel > 0.5, 0.0, NEG_BIG)

    def far_scores(j0):
        start = pl.multiple_of(j0 * blk, blk)
        return jnp.dot(k_ref[0, pl.ds(start, MOBA_GROUP * blk), :], qs_t, preferred_element_type=F32)

    s_ref[0] = far_scores(0)

    def softmax_group(scores, offsets, v_idx, m_old):
        m_new = m_old
        for sj, off in zip(scores, offsets):
            m_new = jnp.maximum(m_new, jnp.max(sj, axis=0, keepdims=True) + off)
        l_sum = jnp.zeros((1, blk), F32)
        pv = jnp.zeros((ATTN_HEAD_DIM, blk), F32)
        for sj, off, vj in zip(scores, offsets, v_idx):
            pj = jnp.exp2(sj + (off - m_new))
            l_sum = l_sum + jnp.sum(pj, axis=0, keepdims=True)
            pv = pv + jnp.dot(vt_ref[vj], pj.astype(BF16), preferred_element_type=F32)
        return m_new, l_sum, pv

    key = lax.broadcasted_iota(jnp.int32, (blk, blk), 0)
    qry = lax.broadcasted_iota(jnp.int32, (blk, blk), 1)
    scores, offsets, v_idx = [], [], []
    for d in range(MOBA_NEAR):
        jd = jnp.maximum(i - d, 0)
        start = pl.multiple_of(jd * blk, blk)
        sd = jnp.dot(k_ref[0, pl.ds(start, blk), :], qs_t, preferred_element_type=F32) + bias_ref[d]
        if d == 0:
            scores.append(jnp.where(qry >= key, sd, NEG_BIG))
            offsets.append(jnp.zeros((1, blk), F32))
        else:
            scores.append(sd)
            offsets.append(neg_ref[pl.ds(jnp.where(i >= d, i - d, LANES - 1), 1), :])
        v_idx.append(jd)
    m0, l0, pv0 = softmax_group(scores, offsets, v_idx, jnp.full((1, blk), NEG_BIG, F32))
    m_ref[...] = m0
    l_ref[...] = l0
    acc_ref[...] = pv0

    n_far = jnp.maximum(i - (MOBA_NEAR - 1), 0)
    far_bias = rel_ref[h, REL_BUCKETS - 1] * LOG2E

    n_groups = (n_far + MOBA_GROUP - 1) // MOBA_GROUP

    def far_group(gi, slot):
        j0 = gi * MOBA_GROUP
        s_ref[1 - slot] = far_scores(jnp.minimum(j0 + MOBA_GROUP, nb - MOBA_GROUP))
        scores, offsets, v_idx = [], [], []
        for u in range(MOBA_GROUP):
            ju = j0 + u
            scores.append(s_ref[slot, u * blk:(u + 1) * blk, :])
            offsets.append(neg_ref[pl.ds(jnp.where(ju < n_far, ju, LANES - 1), 1), :] + far_bias)
            v_idx.append(ju)
        m_old = m_ref[...]
        m_new, l_sum, pv = softmax_group(scores, offsets, v_idx, m_old)
        alpha = jnp.exp2(m_old - m_new)
        l_ref[...] = alpha * l_ref[...] + l_sum
        acc_ref[...] = alpha * acc_ref[...] + pv
        m_ref[...] = m_new

    def far_body(t, carry):
        far_group(2 * t, 0)

        @pl.when(2 * t + 1 < n_groups)
        def _():
            far_group(2 * t + 1, 1)

        return carry

    lax.fori_loop(0, (n_groups + 1) // 2, far_body, 0)

    o_ref[0] = (acc_ref[...] / l_ref[...]).T.astype(o_ref.dtype)


def _moba(proj3, kmean3, rel_t):
    bsz, s_len, _ = proj3.shape
    blk = MOBA_BLOCK
    nb = s_len // blk
    assert nb * blk == s_len and nb < LANES
    cb = LANES
    return pl.pallas_call(
        functools.partial(_moba_kernel, nb=nb),
        grid=(ATTN_HEADS, bsz, nb),
        in_specs=[
            pl.BlockSpec(memory_space=pltpu.SMEM),
            pl.BlockSpec((1, blk, cb), lambda h, b, i: (b, i, COL_Q // cb + h)),
            pl.BlockSpec((1, s_len, cb), lambda h, b, i: (b, 0, COL_K // cb + h)),
            pl.BlockSpec((1, s_len, cb), lambda h, b, i: (b, 0, COL_V // cb + h)),
            pl.BlockSpec((1, nb, cb), lambda h, b, i: (b, 0, h)),
        ],
        out_specs=pl.BlockSpec((1, blk, cb), lambda h, b, i: (b, i, h)),
        out_shape=jax.ShapeDtypeStruct((bsz, s_len, ATTN_HEADS * ATTN_HEAD_DIM), BF16),
        scratch_shapes=[
            pltpu.VMEM((LANES, ATTN_HEAD_DIM), F32),
            pltpu.VMEM((MOBA_NEAR, blk, blk), F32),
            pltpu.VMEM((nb, ATTN_HEAD_DIM, blk), BF16),
            pltpu.VMEM((LANES, blk), F32),
            pltpu.VMEM((2, MOBA_GROUP * blk, blk), F32),
            pltpu.VMEM((1, blk), F32),
            pltpu.VMEM((1, blk), F32),
            pltpu.VMEM((ATTN_HEAD_DIM, blk), F32),
        ],
        compiler_params=_cparams(("arbitrary", "arbitrary", "arbitrary")),
        name="moba",
    )(rel_t, proj3, proj3, proj3, kmean3)


def _ssd_kernel(xbc_ref, z_ref, dt_ref, cw_ref, cbias_ref, dtb_ref, alog_ref, dskip_ref, ng_ref,
                tril_ref, e_ref, o_ref, xpad_ref, state_ref):
    L = SSM_CHUNK
    G = SSM_GROUPS
    N = SSM_STATE
    GW = SSM_INNER // G
    pad = SUBLANES
    nt = (((1,), (1,)), ((), ()))

    @pl.when(pl.program_id(1) == 0)
    def _():
        xpad_ref[0:pad, :] = jnp.zeros((pad, SSM_CONV_DIM), F32)
        state_ref[...] = jnp.zeros_like(state_ref)

    xpad_ref[pad:pad + L, :] = xbc_ref[0].astype(F32)
    conv = cbias_ref[...]
    for k in range(SSM_CONV):
        off = pad - (SSM_CONV - 1) + k
        conv = conv + xpad_ref[off:off + L, :] * cw_ref[k:k + 1, :]
    xpad_ref[0:pad, :] = xpad_ref[L:L + pad, :]
    xa = _silu(conv)
    xs = xa[:, :SSM_INNER]
    bm = xa[:, SSM_INNER:SSM_INNER + G * N]
    cm = xa[:, SSM_INNER + G * N:].astype(BF16)

    dtr = dt_ref[0] + dtb_ref[...]
    dt = jnp.maximum(dtr, 0.0) + jnp.log1p(jnp.exp(-jnp.abs(dtr)))
    a = dt * (-jnp.exp(alog_ref[...]))
    a_cat = jnp.concatenate(_split_bf16(a, 3), axis=0)
    acs = jnp.dot(tril_ref[...], a_cat, preferred_element_type=F32)
    acs_t = acs.T

    def expand(v):
        v_cat = jnp.concatenate(_split_bf16(v, 2), axis=1)
        return jnp.dot(v_cat, e_ref[...], preferred_element_type=F32)

    dt_x = expand(dt)
    acs_x = expand(acs)
    xc = xs * dt_x
    xc_b = xc.astype(BF16)
    last = acs_x[L - 1:L, :]
    ea_x = jnp.exp(acs_x)
    w_end = (jnp.exp(last - acs_x) * xc).astype(BF16)
    ea_last = jnp.exp(last)

    row = lax.broadcasted_iota(jnp.int32, (L, L), 0)
    col = lax.broadcasted_iota(jnp.int32, (L, L), 1)
    causal = row >= col
    lane = lax.broadcasted_iota(jnp.int32, (L, LANES), 1)
    low_half = lane < SSM_HEAD_DIM

    y_parts = []
    for g in range(G):
        bg = bm[:, g * N:(g + 1) * N]
        cg = cm[:, g * N:(g + 1) * N]
        cb = lax.dot_general(cg, bg.astype(BF16), nt, preferred_element_type=F32)
        st = state_ref[g]
        y_off = jnp.dot(cg, st.astype(BF16), preferred_element_type=F32) * ea_x[:, g * GW:(g + 1) * GW]
        for pr in range(GW // LANES):
            c0 = g * GW + pr * LANES
            x_pair = xc_b[:, c0:c0 + LANES]
            zero = jnp.zeros_like(x_pair)
            y_pair = y_off[:, pr * LANES:(pr + 1) * LANES]
            for half in range(2):
                hd = c0 // SSM_HEAD_DIM + half
                seg = acs[:, hd:hd + 1] - acs_t[hd:hd + 1, :]
                decay = jnp.exp(jnp.where(causal, seg, -jnp.inf))
                mmat = (cb * decay).astype(BF16)
                x_half = jnp.where(low_half, x_pair, zero) if half == 0 else jnp.where(low_half, zero, x_pair)
                y_pair = y_pair + jnp.dot(mmat, x_half, preferred_element_type=F32)
            y_parts.append(y_pair)
        bg_t = bg.T.astype(BF16)
        state_ref[g] = (st * ea_last[:, g * GW:(g + 1) * GW]
                        + jnp.dot(bg_t, w_end[:, g * GW:(g + 1) * GW], preferred_element_type=F32))

    y = jnp.concatenate(y_parts, axis=1) + dskip_ref[...] * xs
    v = y * _silu(z_ref[0].astype(F32))
    outs = []
    for g in range(G):
        vg = v[:, g * GW:(g + 1) * GW]
        ms = jnp.mean(vg * vg, axis=-1, keepdims=True)
        outs.append(vg * lax.rsqrt(ms + NORM_EPS))
    o_ref[0] = (jnp.concatenate(outs, axis=1) * ng_ref[...]).astype(o_ref.dtype)


def _ssd(proj3, dt3, conv_w, conv_b, dt_bias, a_log, d_skip_x, norm_g, tril3, e2):
    bsz, s_len, _ = proj3.shape
    L = SSM_CHUNK
    nc = s_len // L
    assert nc * L == s_len
    const = lambda b, c: (0, 0)
    return pl.pallas_call(
        _ssd_kernel,
        grid=(bsz, nc),
        in_specs=[
            pl.BlockSpec((1, L, SSM_CONV_DIM), lambda b, c: (b, c, COL_XBC // SSM_CONV_DIM)),
            pl.BlockSpec((1, L, SSM_INNER), lambda b, c: (b, c, COL_Z // SSM_INNER)),
            pl.BlockSpec((1, L, LANES), lambda b, c: (b, c, 0)),
            pl.BlockSpec((SSM_CONV, SSM_CONV_DIM), const),
            pl.BlockSpec((1, SSM_CONV_DIM), const),
            pl.BlockSpec((1, LANES), const),
            pl.BlockSpec((1, LANES), const),
            pl.BlockSpec((1, SSM_INNER), const),
            pl.BlockSpec((1, SSM_INNER), const),
            pl.BlockSpec((L, 3 * L), const),
            pl.BlockSpec((2 * LANES, SSM_INNER), const),
        ],
        out_specs=pl.BlockSpec((1, L, SSM_INNER), lambda b, c: (b, c, 0)),
        out_shape=jax.ShapeDtypeStruct((bsz, s_len, SSM_INNER), BF16),
        scratch_shapes=[
            pltpu.VMEM((L + 2 * SUBLANES, SSM_CONV_DIM), F32),
            pltpu.VMEM((SSM_GROUPS, SSM_STATE, SSM_INNER // SSM_GROUPS), F32),
        ],
        compiler_params=_cparams(("arbitrary", "arbitrary")),
        name="ssd",
    )(proj3, proj3, dt3, conv_w, conv_b, dt_bias, a_log, d_skip_x, norm_g, tril3, e2)


def _memkv_kernel(mem_ref, g_ref, w_ref, kv_ref):
    x = mem_ref[...]
    ms = jnp.mean(x * x, axis=-1, keepdims=True)
    u = (x * lax.rsqrt(ms + NORM_EPS) * g_ref[...]).astype(BF16)
    kv_ref[...] = jnp.dot(u, w_ref[...], preferred_element_type=F32).astype(BF16)


def _mem_kv(mem2, g, w_kv):
    rows = mem2.shape[0]
    width = w_kv.shape[1]
    return pl.pallas_call(
        _memkv_kernel,
        grid=(1,),
        in_specs=[
            pl.BlockSpec((rows, D_MODEL), lambda i: (0, 0)),
            pl.BlockSpec((1, D_MODEL), lambda i: (0, 0)),
            pl.BlockSpec((D_MODEL, width), lambda i: (0, 0)),
        ],
        out_specs=pl.BlockSpec((rows, width), lambda i: (0, 0)),
        out_shape=jax.ShapeDtypeStruct((rows, width), BF16),
        compiler_params=_cparams(("arbitrary",)),
        name="mem_kv",
    )(mem2, g, w_kv)


def _merge_kernel(x_ref, oa_ref, os_ref, qm_ref, gl_ref, kv_ref, bg_ref, wa_ref, ws_ref, wm_ref, wo_ref, h_ref):
    nt = (((1,), (1,)), ((), ()))
    hd = MEM_HEAD_DIM
    width = MEM_HEADS * hd
    scale = hd ** -0.5
    qm = qm_ref[0]
    kv = kv_ref[0]
    outs = []
    for hh in range(MEM_HEADS):
        q = qm[:, hh * hd:(hh + 1) * hd]
        km = kv[:, hh * hd:(hh + 1) * hd]
        vm = kv[:, width + hh * hd:width + (hh + 1) * hd]
        s = lax.dot_general(q, km, nt, preferred_element_type=F32) * scale
        p = jnp.exp(s - jnp.max(s, axis=1, keepdims=True))
        o = jnp.dot(p.astype(BF16), vm, preferred_element_type=F32)
        outs.append(o / jnp.sum(p, axis=1, keepdims=True))
    o_mem = jnp.concatenate(outs, axis=1).astype(BF16)

    gates = _sigmoid(gl_ref[0].astype(F32) + bg_ref[...])
    merged = (gates[:, :D_MODEL] * jnp.dot(oa_ref[0], wa_ref[...], preferred_element_type=F32)
              + gates[:, D_MODEL:2 * D_MODEL] * jnp.dot(os_ref[0], ws_ref[...], preferred_element_type=F32)
              + gates[:, 2 * D_MODEL:] * jnp.dot(o_mem, wm_ref[...], preferred_element_type=F32))
    h_ref[0] = x_ref[0] + jnp.dot(merged.astype(BF16), wo_ref[...], preferred_element_type=F32)


def _resident(shape):
    return pl.BlockSpec(shape, lambda *_: (0,) * len(shape), pipeline_mode=pl.Buffered(1))


def _merge(x, o_attn, o_ssm, proj3, kv3, b_gate, wa, ws, wm, wo, tm):
    bsz, s_len, _ = x.shape
    mem_len = kv3.shape[1]
    return pl.pallas_call(
        _merge_kernel,
        grid=(bsz, s_len // tm),
        in_specs=[
            pl.BlockSpec((1, tm, D_MODEL), lambda b, i: (b, i, 0)),
            pl.BlockSpec((1, tm, D_MODEL), lambda b, i: (b, i, 0)),
            pl.BlockSpec((1, tm, SSM_INNER), lambda b, i: (b, i, 0)),
            pl.BlockSpec((1, tm, D_MODEL), lambda b, i: (b, i, COL_QM // D_MODEL)),
            pl.BlockSpec((1, tm, 3 * D_MODEL), lambda b, i: (b, i, COL_GATE // (3 * D_MODEL))),
            pl.BlockSpec((1, mem_len, 2 * D_MODEL), lambda b, i: (b, 0, 0)),
            _resident((1, 3 * D_MODEL)),
            _resident((D_MODEL, D_MODEL)),
            _resident((SSM_INNER, D_MODEL)),
            _resident((D_MODEL, D_MODEL)),
            _resident((D_MODEL, D_MODEL)),
        ],
        out_specs=pl.BlockSpec((1, tm, D_MODEL), lambda b, i: (b, i, 0)),
        out_shape=jax.ShapeDtypeStruct((bsz, s_len, D_MODEL), F32),
        compiler_params=_cparams(("arbitrary", "arbitrary")),
        name="merge",
    )(x, o_attn, o_ssm, proj3, proj3, kv3, b_gate, wa, ws, wm, wo)


def _ffn_kernel(h_ref, ng_ref, wup_ref, cw_ref, cb_ref, wdn_ref, fg_ref, o_ref, hid_ref, *, tm):
    pad = SUBLANES

    @pl.when(pl.program_id(1) == 0)
    def _():
        hid_ref[0:pad, :] = jnp.zeros((pad, 2 * FFN_HIDDEN), F32)

    h = h_ref[0]
    ms = jnp.mean(h * h, axis=-1, keepdims=True)
    u = (h * lax.rsqrt(ms + NORM_EPS) * ng_ref[...]).astype(BF16)
    hid_ref[pad:pad + tm, :] = jnp.dot(u, wup_ref[...], preferred_element_type=F32)
    conv = cb_ref[...]
    for k in range(FFN_CONV):
        off = pad - (FFN_CONV - 1) + k
        conv = conv + hid_ref[off:off + tm, :] * cw_ref[k:k + 1, :]
    hid_ref[0:pad, :] = hid_ref[tm:tm + pad, :]
    act = (_silu(conv[:, :FFN_HIDDEN]) * conv[:, FFN_HIDDEN:]).astype(BF16)
    y = h + jnp.dot(act, wdn_ref[...], preferred_element_type=F32)
    ms2 = jnp.mean(y * y, axis=-1, keepdims=True)
    o_ref[0] = y * lax.rsqrt(ms2 + NORM_EPS) * fg_ref[...]


def _ffn(h, norm_g, w_up, conv_w, conv_b, w_down, final_g, tm):
    bsz, s_len, _ = h.shape
    f2 = 2 * FFN_HIDDEN
    return pl.pallas_call(
        functools.partial(_ffn_kernel, tm=tm),
        grid=(bsz, s_len // tm),
        in_specs=[
            pl.BlockSpec((1, tm, D_MODEL), lambda b, i: (b, i, 0)),
            _resident((1, D_MODEL)),
            _resident((D_MODEL, f2)),
            _resident((FFN_CONV, f2)),
            _resident((1, f2)),
            _resident((FFN_HIDDEN, D_MODEL)),
            _resident((1, D_MODEL)),
        ],
        out_specs=pl.BlockSpec((1, tm, D_MODEL), lambda b, i: (b, i, 0)),
        out_shape=jax.ShapeDtypeStruct((bsz, s_len, D_MODEL), F32),
        scratch_shapes=[pltpu.VMEM((tm + 2 * SUBLANES, f2), F32)],
        compiler_params=_cparams(("arbitrary", "arbitrary")),
        name="ffn",
    )(h, norm_g, w_up, conv_w, conv_b, w_down, final_g)


def _ssd_constants():
    L = SSM_CHUNK
    tril = np.tril(np.ones((L, L), np.float32))
    tril3 = np.concatenate([tril, tril, tril], axis=1)
    e = np.zeros((LANES, SSM_INNER), np.float32)
    for hd in range(SSM_HEADS):
        e[hd, hd * SSM_HEAD_DIM:(hd + 1) * SSM_HEAD_DIM] = 1.0
    e2 = np.concatenate([e, e], axis=0)
    return jnp.asarray(tril3, BF16), jnp.asarray(e2, BF16)


def _pad_lanes(v):
    return jnp.pad(v, ((0, 0), (0, LANES - v.shape[1])))


def _layer(h, mem, rel_bias, mix_norm_g, w_in, b_gate, ssm_conv_w, ssm_conv_b, ssm_dt_bias, ssm_A_log, ssm_D,
           ssm_norm_g, mem_norm_g, w_mem_kv, w_br_attn, w_br_ssm, w_br_mem, w_out, ffn_norm_g, w_ffn_up,
           ffn_conv_w, ffn_conv_b, w_ffn_down, final_g):
    bsz, s_len, _ = h.shape
    w_cat = jnp.concatenate([w_in[:, 5120:8192], w_in[:, 9248:12320], w_in[:, 3072:5120], w_in[:, 0:3072],
                             w_in[:, 8224:9248]], axis=1).astype(BF16)
    w_dt = _pad_lanes(w_in[:, 8192:8224]).astype(BF16)

    proj, dt_raw, kmean = _in_proj(h.reshape(bsz * s_len, D_MODEL), mix_norm_g[None, :], w_cat, w_dt,
                                   tm=1024, tn=1024)
    proj3 = proj.reshape(bsz, s_len, PROJ_WIDTH)
    dt3 = dt_raw.reshape(bsz, s_len, LANES)
    kmean3 = kmean.reshape(bsz, s_len // MOBA_BLOCK, ATTN_HEADS * ATTN_HEAD_DIM)

    o_attn = _moba(proj3, kmean3, rel_bias.T)

    tril3, e2 = _ssd_constants()
    o_ssm = _ssd(proj3, dt3, ssm_conv_w, ssm_conv_b[None, :], _pad_lanes(ssm_dt_bias[None, :]),
                 _pad_lanes(ssm_A_log[None, :]), jnp.repeat(ssm_D, SSM_HEAD_DIM)[None, :], ssm_norm_g[None, :],
                 tril3, e2)

    mem_len = mem.shape[1]
    kv = _mem_kv(mem.reshape(bsz * mem_len, D_MODEL), mem_norm_g[None, :], w_mem_kv.astype(BF16))
    kv3 = kv.reshape(bsz, mem_len, 2 * D_MODEL)

    h1 = _merge(h, o_attn, o_ssm, proj3, kv3, b_gate[None, :], w_br_attn.astype(BF16), w_br_ssm.astype(BF16),
                w_br_mem.astype(BF16), w_out.astype(BF16), tm=512)

    return _ffn(h1, ffn_norm_g[None, :], w_ffn_up.astype(BF16), ffn_conv_w, ffn_conv_b[None, :],
                w_ffn_down.astype(BF16), final_g[None, :], tm=256)


def kernel(x, mem, rel_bias, mix_norm_g, w_in, b_gate, ssm_conv_w, ssm_conv_b, ssm_dt_bias, ssm_A_log, ssm_D,
           ssm_norm_g, mem_norm_g, w_mem_kv, w_br_attn, w_br_ssm, w_br_mem, w_out, ffn_norm_g, w_ffn_up,
           ffn_conv_w, ffn_conv_b, w_ffn_down, final_norm_g):
    assert w_in.shape[0] == 1, "single-layer trunk"
    return _layer(x, mem, rel_bias, mix_norm_g[0], w_in[0], b_gate[0], ssm_conv_w[0], ssm_conv_b[0],
                  ssm_dt_bias[0], ssm_A_log[0], ssm_D[0], ssm_norm_g[0], mem_norm_g[0], w_mem_kv[0],
                  w_br_attn[0], w_br_ssm[0], w_br_mem[0], w_out[0], ffn_norm_g[0], w_ffn_up[0],
                  ffn_conv_w[0], ffn_conv_b[0], w_ffn_down[0], final_norm_g)
```

```python
import functools
import math

import numpy as np
import jax
import jax.numpy as jnp
from jax import lax
from jax.experimental import pallas as pl
from jax.experimental.pallas import tpu as pltpu

F32 = jnp.float32
BF16 = jnp.bfloat16

D_MODEL = 1024
ATTN_HEADS = 8
ATTN_HEAD_DIM = 128
MOBA_BLOCK = 256
MOBA_TOPK = 3
REL_BUCKETS = 32
REL_MAX_DIST = 1024
SSM_INNER = 2048
SSM_HEAD_DIM = 64
SSM_HEADS = 32
SSM_GROUPS = 4
SSM_STATE = 128
SSM_CONV = 4
SSM_CHUNK = 256
SSM_CONV_DIM = 3072
MEM_HEADS = 4
MEM_HEAD_DIM = 256
FFN_HIDDEN = 2816
FFN_CONV = 3
NORM_EPS = 1e-6

LANES = 128
SUBLANES = 8
VMEM_LIMIT = 56 * 1024 * 1024

COL_XBC = 0
COL_GATE = 3072
COL_Z = 6144
COL_Q = 8192
COL_K = 9216
COL_V = 10240
COL_QM = 11264
PROJ_WIDTH = 12288

MOBA_NEAR = -(-(REL_MAX_DIST + MOBA_BLOCK - 1) // MOBA_BLOCK)
NEG_BIG = -1e30
LOG2E = math.log2(math.e)
MOBA_QBLOCKS = 2
MOBA_GROUP = 4
assert MOBA_GROUP <= MOBA_NEAR + 1


def _cparams(sem):
    return pltpu.CompilerParams(dimension_semantics=sem, vmem_limit_bytes=VMEM_LIMIT)


def _sigmoid(x):
    return 1.0 / (1.0 + jnp.exp(-x))


def _silu(x):
    return x * _sigmoid(x)


def _split_bf16(x, parts):
    out = []
    r = x
    for _ in range(parts):
        hi = r.astype(BF16)
        out.append(hi)
        r = r - hi.astype(F32)
    return out


def _inproj_kernel(x_ref, g_ref, w_ref, wdt_ref, proj_ref, dt_ref, kmean_ref, u_ref, *, k_tile):
    j = pl.program_id(1)

    @pl.when(j == 0)
    def _():
        x = x_ref[...]
        ms = jnp.mean(x * x, axis=-1, keepdims=True)
        u = (x * lax.rsqrt(ms + NORM_EPS) * g_ref[...]).astype(BF16)
        u_ref[...] = u
        dt_ref[...] = jnp.dot(u, wdt_ref[...], preferred_element_type=F32)

    acc = jnp.dot(u_ref[...], w_ref[...], preferred_element_type=F32)
    proj_ref[...] = acc.astype(BF16)

    @pl.when(j == k_tile)
    def _():
        blk = MOBA_BLOCK
        for r in range(acc.shape[0] // blk):
            kmean_ref[0, r:r + 1, :] = jnp.sum(acc[r * blk:(r + 1) * blk], axis=0, keepdims=True) * (1.0 / blk)


def _in_proj(x2, g, w_cat, w_dt, tm, tn):
    t = x2.shape[0]
    kw = ATTN_HEADS * ATTN_HEAD_DIM
    assert tn == kw and COL_K % tn == 0 and tm % MOBA_BLOCK == 0
    return pl.pallas_call(
        functools.partial(_inproj_kernel, k_tile=COL_K // tn),
        grid=(t // tm, PROJ_WIDTH // tn),
        in_specs=[
            pl.BlockSpec((tm, D_MODEL), lambda i, j: (i, 0)),
            pl.BlockSpec((1, D_MODEL), lambda i, j: (0, 0)),
            pl.BlockSpec((D_MODEL, tn), lambda i, j: (0, j)),
            pl.BlockSpec((D_MODEL, LANES), lambda i, j: (0, 0)),
        ],
        out_specs=[
            pl.BlockSpec((tm, tn), lambda i, j: (i, j)),
            pl.BlockSpec((tm, LANES), lambda i, j: (i, 0)),
            pl.BlockSpec((1, tm // MOBA_BLOCK, kw), lambda i, j: (i, 0, 0)),
        ],
        out_shape=[
            jax.ShapeDtypeStruct((t, PROJ_WIDTH), BF16),
            jax.ShapeDtypeStruct((t, LANES), F32),
            jax.ShapeDtypeStruct((t // tm, tm // MOBA_BLOCK, kw), F32),
        ],
        scratch_shapes=[pltpu.VMEM((tm, D_MODEL), BF16)],
        compiler_params=_cparams(("arbitrary", "arbitrary")),
        name="in_proj",
    )(x2, g, w_cat, w_dt)


def _t5_bucket(dist):
    n = jnp.maximum(dist, 0)
    max_exact = REL_BUCKETS // 2
    nf = jnp.maximum(n, max_exact).astype(F32)
    large = max_exact + (jnp.log(nf * (1.0 / max_exact)) / math.log(REL_MAX_DIST / max_exact)
                         * (REL_BUCKETS - max_exact)).astype(jnp.int32)
    large = jnp.minimum(large, REL_BUCKETS - 1)
    return jnp.where(n < max_exact, n, large)


def _moba_kernel(rel_ref, q_ref, k_ref, v_ref, km_ref, o_ref, kmean_ref, bias_ref, vt_ref, neg_ref, s_ref, m_ref,
                 l_ref, acc_ref, *, nb):
    h = pl.program_id(0)
    b = pl.program_id(1)
    i = pl.program_id(2)
    blk = MOBA_BLOCK
    scale = ATTN_HEAD_DIM ** -0.5


    @pl.when((b == 0) & (i == 0))
    def _():
        key = lax.broadcasted_iota(jnp.int32, (blk, blk), 0)
        qry = lax.broadcasted_iota(jnp.int32, (blk, blk), 1)
        for d in range(MOBA_NEAR):
            bucket = _t5_bucket(d * blk + qry - key)
            tile = jnp.zeros((blk, blk), F32)
            for bk in range(REL_BUCKETS):
                tile = jnp.where(bucket == bk, rel_ref[h, bk], tile)
            bias_ref[d] = tile * LOG2E

    @pl.when(i == 0)
    def _():
        kmean_ref[...] = jnp.zeros_like(kmean_ref)
        kmean_ref[0:nb, :] = km_ref[0]
        for jb in range(nb):
            vt_ref[jb] = v_ref[0, jb * blk:(jb + 1) * blk, :].astype(F32).T.astype(BF16)

    qw = MOBA_QBLOCKS * blk
    i0 = i * MOBA_QBLOCKS
    q_t = q_ref[0].astype(F32).T
    qs_t = (q_t * (scale * LOG2E)).astype(BF16)

    gate = jnp.dot(kmean_ref[...].astype(BF16), q_t.astype(BF16), preferred_element_type=F32)
    row = lax.broadcasted_iota(jnp.int32, (LANES, qw), 0).astype(F32)
    own = (i0 + lax.broadcasted_iota(jnp.int32, (1, qw), 1) // blk).astype(F32)
    g = jnp.where(row < own, gate, -jnp.inf)
    sel = jnp.zeros((LANES, qw), F32)
    for t in range(MOBA_TOPK):
        mx = jnp.max(g, axis=0, keepdims=True)
        idx = jnp.min(jnp.where(g == mx, row, float(LANES)), axis=0, keepdims=True)
        hit = row == idx
        sel = jnp.maximum(sel, jnp.where(hit & (own > t), 1.0, 0.0))
        g = jnp.where(hit, -jnp.inf, g)
    neg_ref[...] = jnp.where(sel > 0.5, 0.0, NEG_BIG)

    def far_scores(j0):
        start = pl.multiple_of(j0 * blk, blk)
        return jnp.dot(k_ref[0, pl.ds(start, MOBA_GROUP * blk), :], qs_t, preferred_element_type=F32)

    s_ref[0] = far_scores(0)

    def softmax_group(scores, offsets, v_idx, m_old, clamp=False):
        m_new = m_old
        for sj, off in zip(scores, offsets):
            m_new = jnp.maximum(m_new, jnp.max(sj, axis=0, keepdims=True) + off)
        l_sum = jnp.zeros((1, qw), F32)
        pv = jnp.zeros((ATTN_HEAD_DIM, qw), F32)
        for sj, off, vj in zip(scores, offsets, v_idx):
            ej = sj + (off - m_new)
            pj = jnp.exp2(jnp.minimum(ej, 0.0) if clamp else ej)
            l_sum = l_sum + jnp.sum(pj, axis=0, keepdims=True)
            pv = pv + jnp.dot(vt_ref[vj], pj.astype(BF16), preferred_element_type=F32)
        return m_new, l_sum, pv

    far_bias = rel_ref[h, REL_BUCKETS - 1] * LOG2E
    neg_row = jnp.full((1, blk), NEG_BIG, F32)

    key = lax.broadcasted_iota(jnp.int32, (blk, blk), 0)
    qry = lax.broadcasted_iota(jnp.int32, (blk, blk), 1)
    scores, offsets, v_idx = [], [], []
    for e in range(MOBA_NEAR - 1 + MOBA_QBLOCKS):
        jn = i0 - (MOBA_NEAR - 1) + e
        jc = jnp.maximum(jn, 0)
        start = pl.multiple_of(jc * blk, blk)
        se = jnp.dot(k_ref[0, pl.ds(start, blk), :], qs_t, preferred_element_type=F32)
        sel_row = neg_ref[pl.ds(jnp.where(jn >= 0, jn, LANES - 1), 1), :]
        s_parts, o_parts = [], []
        for w in range(MOBA_QBLOCKS):
            d = w + (MOBA_NEAR - 1) - e
            sw = se[:, w * blk:(w + 1) * blk]
            ow = sel_row[:, w * blk:(w + 1) * blk]
            if d < 0:
                ow = neg_row
            elif d == 0:
                sw = jnp.where(qry >= key, sw + bias_ref[0], NEG_BIG)
                ow = jnp.zeros((1, blk), F32)
            elif d < MOBA_NEAR:
                sw = sw + bias_ref[d]
            else:
                ow = ow + far_bias
            s_parts.append(sw)
            o_parts.append(ow)
        scores.append(jnp.concatenate(s_parts, axis=1))
        offsets.append(jnp.concatenate(o_parts, axis=1))
        v_idx.append(jc)
    parts = [softmax_group([sj], [off], [vj], jnp.full((1, qw), NEG_BIG, F32), clamp=True)
             for sj, off, vj in zip(scores, offsets, v_idx)]
    m0 = parts[0][0]
    for mu, _, _ in parts[1:]:
        m0 = jnp.maximum(m0, mu)
    l0 = jnp.zeros((1, qw), F32)
    pv0 = jnp.zeros((ATTN_HEAD_DIM, qw), F32)
    for mu, lu, pvu in parts:
        wu = jnp.exp2(mu - m0)
        l0 = l0 + wu * lu
        pv0 = pv0 + wu * pvu
    m_ref[...] = m0
    l_ref[...] = l0
    acc_ref[...] = pv0

    n_far = jnp.maximum(i0 - (MOBA_NEAR - 1), 0)

    n_groups = (n_far + MOBA_GROUP - 1) // MOBA_GROUP

    def far_group(gi, slot):
        j0 = gi * MOBA_GROUP
        s_ref[1 - slot] = far_scores(jnp.minimum(j0 + MOBA_GROUP, nb - MOBA_GROUP))
        scores, offsets, v_idx = [], [], []
        for u in range(MOBA_GROUP):
            ju = j0 + u
            scores.append(s_ref[slot, u * blk:(u + 1) * blk, :])
            offsets.append(neg_ref[pl.ds(jnp.where(ju < n_far, ju, LANES - 1), 1), :] + far_bias)
            v_idx.append(ju)
        m_old = m_ref[...]
        m_new, l_sum, pv = softmax_group(scores, offsets, v_idx, m_old)
        alpha = jnp.exp2(m_old - m_new)
        l_ref[...] = alpha * l_ref[...] + l_sum
        acc_ref[...] = alpha * acc_ref[...] + pv
        m_ref[...] = m_new

    def far_body(t, carry):
        far_group(2 * t, 0)

        @pl.when(2 * t + 1 < n_groups)
        def _():
            far_group(2 * t + 1, 1)

        return carry

    lax.fori_loop(0, (n_groups + 1) // 2, far_body, 0)

    o_ref[0] = (acc_ref[...] / l_ref[...]).T.astype(o_ref.dtype)


def _moba(proj3, kmean3, rel_t):
    bsz, s_len, _ = proj3.shape
    blk = MOBA_BLOCK
    nb = s_len // blk
    qw = MOBA_QBLOCKS * blk
    assert nb * blk == s_len and nb < LANES and nb % MOBA_QBLOCKS == 0 and nb >= MOBA_GROUP
    cb = LANES
    return pl.pallas_call(
        functools.partial(_moba_kernel, nb=nb),
        grid=(ATTN_HEADS, bsz, nb // MOBA_QBLOCKS),
        in_specs=[
            pl.BlockSpec(memory_space=pltpu.SMEM),
            pl.BlockSpec((1, qw, cb), lambda h, b, i: (b, i, COL_Q // cb + h)),
            pl.BlockSpec((1, s_len, cb), lambda h, b, i: (b, 0, COL_K // cb + h)),
            pl.BlockSpec((1, s_len, cb), lambda h, b, i: (b, 0, COL_V // cb + h)),
            pl.BlockSpec((1, nb, cb), lambda h, b, i: (b, 0, h)),
        ],
        out_specs=pl.BlockSpec((1, qw, cb), lambda h, b, i: (b, i, h)),
        out_shape=jax.ShapeDtypeStruct((bsz, s_len, ATTN_HEADS * ATTN_HEAD_DIM), BF16),
        scratch_shapes=[
            pltpu.VMEM((LANES, ATTN_HEAD_DIM), F32),
            pltpu.VMEM((MOBA_NEAR, blk, blk), F32),
            pltpu.VMEM((nb, ATTN_HEAD_DIM, blk), BF16),
            pltpu.VMEM((LANES, qw), F32),
            pltpu.VMEM((2, MOBA_GROUP * blk, qw), F32),
            pltpu.VMEM((1, qw), F32),
            pltpu.VMEM((1, qw), F32),
            pltpu.VMEM((ATTN_HEAD_DIM, qw), F32),
        ],
        compiler_params=_cparams(("arbitrary", "arbitrary", "arbitrary")),
        name="moba",
    )(rel_t, proj3, proj3, proj3, kmean3)


def _ssd_kernel(xbc_ref, z_ref, dt_ref, cw_ref, cbias_ref, dtb_ref, alog_ref, dskip_ref, ng_ref,
                tril_ref, e_ref, o_ref, xpad_ref, state_ref):
    L = SSM_CHUNK
    G = SSM_GROUPS
    N = SSM_STATE
    GW = SSM_INNER // G
    pad = SUBLANES
    nt = (((1,), (1,)), ((), ()))

    @pl.when(pl.program_id(1) == 0)
    def _():
        xpad_ref[0:pad, :] = jnp.zeros((pad, SSM_CONV_DIM), F32)
        state_ref[...] = jnp.zeros_like(state_ref)

    xpad_ref[pad:pad + L, :] = xbc_ref[0].astype(F32)
    conv = cbias_ref[...]
    for k in range(SSM_CONV):
        off = pad - (SSM_CONV - 1) + k
        conv = conv + xpad_ref[off:off + L, :] * cw_ref[k:k + 1, :]
    xpad_ref[0:pad, :] = xpad_ref[L:L + pad, :]
    xa = _silu(conv)
    xs = xa[:, :SSM_INNER]
    bm = xa[:, SSM_INNER:SSM_INNER + G * N]
    cm = xa[:, SSM_INNER + G * N:].astype(BF16)

    dtr = dt_ref[0] + dtb_ref[...]
    dt = jnp.maximum(dtr, 0.0) + jnp.log1p(jnp.exp(-jnp.abs(dtr)))
    a = dt * (-jnp.exp(alog_ref[...]))
    a_cat = jnp.concatenate(_split_bf16(a, 3), axis=0)
    acs = jnp.dot(tril_ref[...], a_cat, preferred_element_type=F32)
    acs_t = acs.T

    def expand(v):
        v_cat = jnp.concatenate(_split_bf16(v, 2), axis=1)
        return jnp.dot(v_cat, e_ref[...], preferred_element_type=F32)

    dt_x = expand(dt)
    acs_x = expand(acs)
    xc = xs * dt_x
    xc_b = xc.astype(BF16)
    last = acs_x[L - 1:L, :]
    ea_x = jnp.exp(acs_x)
    w_end = (jnp.exp(last - acs_x) * xc).astype(BF16)
    ea_last = jnp.exp(last)

    row = lax.broadcasted_iota(jnp.int32, (L, L), 0)
    col = lax.broadcasted_iota(jnp.int32, (L, L), 1)
    causal = row >= col
    lane = lax.broadcasted_iota(jnp.int32, (L, LANES), 1)
    low_half = lane < SSM_HEAD_DIM

    y_parts = []
    for g in range(G):
        bg = bm[:, g * N:(g + 1) * N]
        cg = cm[:, g * N:(g + 1) * N]
        cb = lax.dot_general(cg, bg.astype(BF16), nt, preferred_element_type=F32)
        st = state_ref[g]
        y_off = jnp.dot(cg, st.astype(BF16), preferred_element_type=F32) * ea_x[:, g * GW:(g + 1) * GW]
        for pr in range(GW // LANES):
            c0 = g * GW + pr * LANES
            x_pair = xc_b[:, c0:c0 + LANES]
            zero = jnp.zeros_like(x_pair)
            y_pair = y_off[:, pr * LANES:(pr + 1) * LANES]
            for half in range(2):
                hd = c0 // SSM_HEAD_DIM + half
                seg = acs[:, hd:hd + 1] - acs_t[hd:hd + 1, :]
                decay = jnp.exp(jnp.where(causal, seg, -jnp.inf))
                mmat = (cb * decay).astype(BF16)
                x_half = jnp.where(low_half, x_pair, zero) if half == 0 else jnp.where(low_half, zero, x_pair)
                y_pair = y_pair + jnp.dot(mmat, x_half, preferred_element_type=F32)
            y_parts.append(y_pair)
        bg_t = bg.T.astype(BF16)
        state_ref[g] = (st * ea_last[:, g * GW:(g + 1) * GW]
                        + jnp.dot(bg_t, w_end[:, g * GW:(g + 1) * GW], preferred_element_type=F32))

    y = jnp.concatenate(y_parts, axis=1) + dskip_ref[...] * xs
    v = y * _silu(z_ref[0].astype(F32))
    outs = []
    for g in range(G):
        vg = v[:, g * GW:(g + 1) * GW]
        ms = jnp.mean(vg * vg, axis=-1, keepdims=True)
        outs.append(vg * lax.rsqrt(ms + NORM_EPS))
    o_ref[0] = (jnp.concatenate(outs, axis=1) * ng_ref[...]).astype(o_ref.dtype)


def _ssd(proj3, dt3, conv_w, conv_b, dt_bias, a_log, d_skip_x, norm_g, tril3, e2):
    bsz, s_len, _ = proj3.shape
    L = SSM_CHUNK
    nc = s_len // L
    assert nc * L == s_len
    const = lambda b, c: (0, 0)
    return pl.pallas_call(
        _ssd_kernel,
        grid=(bsz, nc),
        in_specs=[
            pl.BlockSpec((1, L, SSM_CONV_DIM), lambda b, c: (b, c, COL_XBC // SSM_CONV_DIM)),
            pl.BlockSpec((1, L, SSM_INNER), lambda b, c: (b, c, COL_Z // SSM_INNER)),
            pl.BlockSpec((1, L, LANES), lambda b, c: (b, c, 0)),
            pl.BlockSpec((SSM_CONV, SSM_CONV_DIM), const),
            pl.BlockSpec((1, SSM_CONV_DIM), const),
            pl.BlockSpec((1, LANES), const),
            pl.BlockSpec((1, LANES), const),
            pl.BlockSpec((1, SSM_INNER), const),
            pl.BlockSpec((1, SSM_INNER), const),
            pl.BlockSpec((L, 3 * L), const),
            pl.BlockSpec((2 * LANES, SSM_INNER), const),
        ],
        out_specs=pl.BlockSpec((1, L, SSM_INNER), lambda b, c: (b, c, 0)),
        out_shape=jax.ShapeDtypeStruct((bsz, s_len, SSM_INNER), BF16),
        scratch_shapes=[
            pltpu.VMEM((L + 2 * SUBLANES, SSM_CONV_DIM), F32),
            pltpu.VMEM((SSM_GROUPS, SSM_STATE, SSM_INNER // SSM_GROUPS), F32),
        ],
        compiler_params=_cparams(("arbitrary", "arbitrary")),
        name="ssd",
    )(proj3, proj3, dt3, conv_w, conv_b, dt_bias, a_log, d_skip_x, norm_g, tril3, e2)


def _memkv_kernel(mem_ref, g_ref, w_ref, kv_ref):
    x = mem_ref[...]
    ms = jnp.mean(x * x, axis=-1, keepdims=True)
    u = (x * lax.rsqrt(ms + NORM_EPS) * g_ref[...]).astype(BF16)
    kv_ref[...] = jnp.dot(u, w_ref[...], preferred_element_type=F32).astype(BF16)


def _mem_kv(mem2, g, w_kv):
    rows = mem2.shape[0]
    width = w_kv.shape[1]
    return pl.pallas_call(
        _memkv_kernel,
        grid=(1,),
        in_specs=[
            pl.BlockSpec((rows, D_MODEL), lambda i: (0, 0)),
            pl.BlockSpec((1, D_MODEL), lambda i: (0, 0)),
            pl.BlockSpec((D_MODEL, width), lambda i: (0, 0)),
        ],
        out_specs=pl.BlockSpec((rows, width), lambda i: (0, 0)),
        out_shape=jax.ShapeDtypeStruct((rows, width), BF16),
        compiler_params=_cparams(("arbitrary",)),
        name="mem_kv",
    )(mem2, g, w_kv)


def _merge_kernel(x_ref, oa_ref, os_ref, qm_ref, gl_ref, kv_ref, bg_ref, wa_ref, ws_ref, wm_ref, wo_ref, h_ref):
    nt = (((1,), (1,)), ((), ()))
    hd = MEM_HEAD_DIM
    width = MEM_HEADS * hd
    scale = hd ** -0.5
    qm = qm_ref[0]
    kv = kv_ref[0]
    outs = []
    for hh in range(MEM_HEADS):
        q = qm[:, hh * hd:(hh + 1) * hd]
        km = kv[:, hh * hd:(hh + 1) * hd]
        vm = kv[:, width + hh * hd:width + (hh + 1) * hd]
        s = lax.dot_general(q, km, nt, preferred_element_type=F32) * scale
        p = jnp.exp(s - jnp.max(s, axis=1, keepdims=True))
        o = jnp.dot(p.astype(BF16), vm, preferred_element_type=F32)
        outs.append(o / jnp.sum(p, axis=1, keepdims=True))
    o_mem = jnp.concatenate(outs, axis=1).astype(BF16)

    gates = _sigmoid(gl_ref[0].astype(F32) + bg_ref[...])
    merged = (gates[:, :D_MODEL] * jnp.dot(oa_ref[0], wa_ref[...], preferred_element_type=F32)
              + gates[:, D_MODEL:2 * D_MODEL] * jnp.dot(os_ref[0], ws_ref[...], preferred_element_type=F32)
              + gates[:, 2 * D_MODEL:] * jnp.dot(o_mem, wm_ref[...], preferred_element_type=F32))
    h_ref[0] = x_ref[0] + jnp.dot(merged.astype(BF16), wo_ref[...], preferred_element_type=F32)


def _resident(shape):
    return pl.BlockSpec(shape, lambda *_: (0,) * len(shape), pipeline_mode=pl.Buffered(1))


def _merge(x, o_attn, o_ssm, proj3, kv3, b_gate, wa, ws, wm, wo, tm):
    bsz, s_len, _ = x.shape
    mem_len = kv3.shape[1]
    return pl.pallas_call(
        _merge_kernel,
        grid=(bsz, s_len // tm),
        in_specs=[
            pl.BlockSpec((1, tm, D_MODEL), lambda b, i: (b, i, 0)),
            pl.BlockSpec((1, tm, D_MODEL), lambda b, i: (b, i, 0)),
            pl.BlockSpec((1, tm, SSM_INNER), lambda b, i: (b, i, 0)),
            pl.BlockSpec((1, tm, D_MODEL), lambda b, i: (b, i, COL_QM // D_MODEL)),
            pl.BlockSpec((1, tm, 3 * D_MODEL), lambda b, i: (b, i, COL_GATE // (3 * D_MODEL))),
            pl.BlockSpec((1, mem_len, 2 * D_MODEL), lambda b, i: (b, 0, 0)),
            _resident((1, 3 * D_MODEL)),
            _resident((D_MODEL, D_MODEL)),
            _resident((SSM_INNER, D_MODEL)),
            _resident((D_MODEL, D_MODEL)),
            _resident((D_MODEL, D_MODEL)),
        ],
        out_specs=pl.BlockSpec((1, tm, D_MODEL), lambda b, i: (b, i, 0)),
        out_shape=jax.ShapeDtypeStruct((bsz, s_len, D_MODEL), F32),
        compiler_params=_cparams(("arbitrary", "arbitrary")),
        name="merge",
    )(x, o_attn, o_ssm, proj3, proj3, kv3, b_gate, wa, ws, wm, wo)


def _ffn_kernel(h_ref, ng_ref, wup_ref, cw_ref, cb_ref, wdn_ref, fg_ref, o_ref, hid_ref, *, tm):
    pad = SUBLANES

    @pl.when(pl.program_id(1) == 0)
    def _():
        hid_ref[0:pad, :] = jnp.zeros((pad, 2 * FFN_HIDDEN), F32)

    h = h_ref[0]
    ms = jnp.mean(h * h, axis=-1, keepdims=True)
    u = (h * lax.rsqrt(ms + NORM_EPS) * ng_ref[...]).astype(BF16)
    hid_ref[pad:pad + tm, :] = jnp.dot(u, wup_ref[...], preferred_element_type=F32)
    conv = cb_ref[...]
    for k in range(FFN_CONV):
        off = pad - (FFN_CONV - 1) + k
        conv = conv + hid_ref[off:off + tm, :] * cw_ref[k:k + 1, :]
    hid_ref[0:pad, :] = hid_ref[tm:tm + pad, :]
    act = (_silu(conv[:, :FFN_HIDDEN]) * conv[:, FFN_HIDDEN:]).astype(BF16)
    y = h + jnp.dot(act, wdn_ref[...], preferred_element_type=F32)
    ms2 = jnp.mean(y * y, axis=-1, keepdims=True)
    o_ref[0] = y * lax.rsqrt(ms2 + NORM_EPS) * fg_ref[...]


def _ffn(h, norm_g, w_up, conv_w, conv_b, w_down, final_g, tm):
    bsz, s_len, _ = h.shape
    f2 = 2 * FFN_HIDDEN
    return pl.pallas_call(
        functools.partial(_ffn_kernel, tm=tm),
        grid=(bsz, s_len // tm),
        in_specs=[
            pl.BlockSpec((1, tm, D_MODEL), lambda b, i: (b, i, 0)),
            _resident((1, D_MODEL)),
            _resident((D_MODEL, f2)),
            _resident((FFN_CONV, f2)),
            _resident((1, f2)),
            _resident((FFN_HIDDEN, D_MODEL)),
            _resident((1, D_MODEL)),
        ],
        out_specs=pl.BlockSpec((1, tm, D_MODEL), lambda b, i: (b, i, 0)),
        out_shape=jax.ShapeDtypeStruct((bsz, s_len, D_MODEL), F32),
        scratch_shapes=[pltpu.VMEM((tm + 2 * SUBLANES, f2), F32)],
        compiler_params=_cparams(("arbitrary", "arbitrary")),
        name="ffn",
    )(h, norm_g, w_up, conv_w, conv_b, w_down, final_g)


def _ssd_constants():
    L = SSM_CHUNK
    tril = np.tril(np.ones((L, L), np.float32))
    tril3 = np.concatenate([tril, tril, tril], axis=1)
    e = np.zeros((LANES, SSM_INNER), np.float32)
    for hd in range(SSM_HEADS):
        e[hd, hd * SSM_HEAD_DIM:(hd + 1) * SSM_HEAD_DIM] = 1.0
    e2 = np.concatenate([e, e], axis=0)
    return jnp.asarray(tril3, BF16), jnp.asarray(e2, BF16)


def _pad_lanes(v):
    return jnp.pad(v, ((0, 0), (0, LANES - v.shape[1])))


def _layer(h, mem, rel_bias, mix_norm_g, w_in, b_gate, ssm_conv_w, ssm_conv_b, ssm_dt_bias, ssm_A_log, ssm_D,
           ssm_norm_g, mem_norm_g, w_mem_kv, w_br_attn, w_br_ssm, w_br_mem, w_out, ffn_norm_g, w_ffn_up,
           ffn_conv_w, ffn_conv_b, w_ffn_down, final_g):
    bsz, s_len, _ = h.shape
    w_cat = jnp.concatenate([w_in[:, 5120:8192], w_in[:, 9248:12320], w_in[:, 3072:5120], w_in[:, 0:3072],
                             w_in[:, 8224:9248]], axis=1).astype(BF16)
    w_dt = _pad_lanes(w_in[:, 8192:8224]).astype(BF16)

    proj, dt_raw, kmean = _in_proj(h.reshape(bsz * s_len, D_MODEL), mix_norm_g[None, :], w_cat, w_dt,
                                   tm=1024, tn=1024)
    proj3 = proj.reshape(bsz, s_len, PROJ_WIDTH)
    dt3 = dt_raw.reshape(bsz, s_len, LANES)
    kmean3 = kmean.reshape(bsz, s_len // MOBA_BLOCK, ATTN_HEADS * ATTN_HEAD_DIM)

    o_attn = _moba(proj3, kmean3, rel_bias.T)

    tril3, e2 = _ssd_constants()
    o_ssm = _ssd(proj3, dt3, ssm_conv_w, ssm_conv_b[None, :], _pad_lanes(ssm_dt_bias[None, :]),
                 _pad_lanes(ssm_A_log[None, :]), jnp.repeat(ssm_D, SSM_HEAD_DIM)[None, :], ssm_norm_g[None, :],
                 tril3, e2)

    mem_len = mem.shape[1]
    kv = _mem_kv(mem.reshape(bsz * mem_len, D_MODEL), mem_norm_g[None, :], w_mem_kv.astype(BF16))
    kv3 = kv.reshape(bsz, mem_len, 2 * D_MODEL)

    h1 = _merge(h, o_attn, o_ssm, proj3, kv3, b_gate[None, :], w_br_attn.astype(BF16), w_br_ssm.astype(BF16),
                w_br_mem.astype(BF16), w_out.astype(BF16), tm=512)

    return _ffn(h1, ffn_norm_g[None, :], w_ffn_up.astype(BF16), ffn_conv_w, ffn_conv_b[None, :],
                w_ffn_down.astype(BF16), final_g[None, :], tm=256)


def kernel(x, mem, rel_bias, mix_norm_g, w_in, b_gate, ssm_conv_w, ssm_conv_b, ssm_dt_bias, ssm_A_log, ssm_D,
           ssm_norm_g, mem_norm_g, w_mem_kv, w_br_attn, w_br_ssm, w_br_mem, w_out, ffn_norm_g, w_ffn_up,
           ffn_conv_w, ffn_conv_b, w_ffn_down, final_norm_g):
    assert w_in.shape[0] == 1, "single-layer trunk"
    return _layer(x, mem, rel_bias, mix_norm_g[0], w_in[0], b_gate[0], ssm_conv_w[0], ssm_conv_b[0],
                  ssm_dt_bias[0], ssm_A_log[0], ssm_D[0], ssm_norm_g[0], mem_norm_g[0], w_mem_kv[0],
                  w_br_attn[0], w_br_ssm[0], w_br_mem[0], w_out[0], ffn_norm_g[0], w_ffn_up[0],
                  ffn_conv_w[0], ffn_conv_b[0], w_ffn_down[0], final_norm_g)
```

```python
import functools
import math

import numpy as np
import jax
import jax.numpy as jnp
from jax import lax
from jax.experimental import pallas as pl
from jax.experimental.pallas import tpu as pltpu

F32 = jnp.float32
BF16 = jnp.bfloat16

D_MODEL = 1024
ATTN_HEADS = 8
ATTN_HEAD_DIM = 128
MOBA_BLOCK = 256
MOBA_TOPK = 3
REL_BUCKETS = 32
REL_MAX_DIST = 1024
SSM_INNER = 2048
SSM_HEAD_DIM = 64
SSM_HEADS = 32
SSM_GROUPS = 4
SSM_STATE = 128
SSM_CONV = 4
SSM_CHUNK = 256
SSM_CONV_DIM = 3072
MEM_HEADS = 4
MEM_HEAD_DIM = 256
FFN_HIDDEN = 2816
FFN_CONV = 3
NORM_EPS = 1e-6

LANES = 128
SUBLANES = 8
VMEM_LIMIT = 56 * 1024 * 1024

COL_XBC = 0
COL_GATE = 3072
COL_Z = 6144
COL_Q = 8192
COL_K = 9216
COL_V = 10240
COL_QM = 11264
PROJ_WIDTH = 12288

MOBA_NEAR = -(-(REL_MAX_DIST + MOBA_BLOCK - 1) // MOBA_BLOCK)
NEG_BIG = -1e30
LOG2E = math.log2(math.e)
MOBA_QBLOCKS = 2
MOBA_GROUP = 4
assert MOBA_GROUP <= MOBA_NEAR + 1


def _cparams(sem):
    return pltpu.CompilerParams(dimension_semantics=sem, vmem_limit_bytes=VMEM_LIMIT)


def _sigmoid(x):
    return 1.0 / (1.0 + jnp.exp(-x))


def _silu(x):
    h = 0.5 * x
    return h + h * jnp.tanh(h)


def _split_bf16(x, parts):
    out = []
    r = x
    for _ in range(parts):
        hi = r.astype(BF16)
        out.append(hi)
        r = r - hi.astype(F32)
    return out


def _inproj_kernel(x_ref, g_ref, w_ref, wdt_ref, proj_ref, dt_ref, kmean_ref, u_ref, *, k_tile):
    j = pl.program_id(1)

    @pl.when(j == 0)
    def _():
        x = x_ref[...]
        ms = jnp.mean(x * x, axis=-1, keepdims=True)
        u = (x * lax.rsqrt(ms + NORM_EPS) * g_ref[...]).astype(BF16)
        u_ref[...] = u
        dt_ref[...] = jnp.dot(u, wdt_ref[...], preferred_element_type=F32)

    acc = jnp.dot(u_ref[...], w_ref[...], preferred_element_type=F32)
    proj_ref[...] = acc.astype(BF16)

    @pl.when(j == k_tile)
    def _():
        blk = MOBA_BLOCK
        for r in range(acc.shape[0] // blk):
            kmean_ref[0, r:r + 1, :] = jnp.sum(acc[r * blk:(r + 1) * blk], axis=0, keepdims=True) * (1.0 / blk)


def _in_proj(x2, g, w_cat, w_dt, tm, tn):
    t = x2.shape[0]
    kw = ATTN_HEADS * ATTN_HEAD_DIM
    assert tn == kw and COL_K % tn == 0 and tm % MOBA_BLOCK == 0
    return pl.pallas_call(
        functools.partial(_inproj_kernel, k_tile=COL_K // tn),
        grid=(t // tm, PROJ_WIDTH // tn),
        in_specs=[
            pl.BlockSpec((tm, D_MODEL), lambda i, j: (i, 0)),
            pl.BlockSpec((1, D_MODEL), lambda i, j: (0, 0)),
            pl.BlockSpec((D_MODEL, tn), lambda i, j: (0, j)),
            pl.BlockSpec((D_MODEL, LANES), lambda i, j: (0, 0)),
        ],
        out_specs=[
            pl.BlockSpec((tm, tn), lambda i, j: (i, j)),
            pl.BlockSpec((tm, LANES), lambda i, j: (i, 0)),
            pl.BlockSpec((1, tm // MOBA_BLOCK, kw), lambda i, j: (i, 0, 0)),
        ],
        out_shape=[
            jax.ShapeDtypeStruct((t, PROJ_WIDTH), BF16),
            jax.ShapeDtypeStruct((t, LANES), F32),
            jax.ShapeDtypeStruct((t // tm, tm // MOBA_BLOCK, kw), F32),
        ],
        scratch_shapes=[pltpu.VMEM((tm, D_MODEL), BF16)],
        compiler_params=_cparams(("arbitrary", "arbitrary")),
        name="in_proj",
    )(x2, g, w_cat, w_dt)


def _t5_bucket(dist):
    n = jnp.maximum(dist, 0)
    max_exact = REL_BUCKETS // 2
    nf = jnp.maximum(n, max_exact).astype(F32)
    large = max_exact + (jnp.log(nf * (1.0 / max_exact)) / math.log(REL_MAX_DIST / max_exact)
                         * (REL_BUCKETS - max_exact)).astype(jnp.int32)
    large = jnp.minimum(large, REL_BUCKETS - 1)
    return jnp.where(n < max_exact, n, large)


def _moba_kernel(rel_ref, q_ref, k_ref, v_ref, km_ref, o_ref, kmean_ref, bias_ref, vt_ref, neg_ref, s_ref, m_ref,
                 l_ref, acc_ref, *, nb):
    h = pl.program_id(0)
    b = pl.program_id(1)
    i = pl.program_id(2)
    blk = MOBA_BLOCK
    scale = ATTN_HEAD_DIM ** -0.5


    @pl.when((b == 0) & (i == 0))
    def _():
        key = lax.broadcasted_iota(jnp.int32, (blk, blk), 0)
        qry = lax.broadcasted_iota(jnp.int32, (blk, blk), 1)
        for d in range(MOBA_NEAR):
            bucket = _t5_bucket(d * blk + qry - key)
            tile = jnp.zeros((blk, blk), F32)
            for bk in range(REL_BUCKETS):
                tile = jnp.where(bucket == bk, rel_ref[h, bk], tile)
            bias_ref[d] = tile * LOG2E

    @pl.when(i == 0)
    def _():
        kmean_ref[...] = jnp.zeros_like(kmean_ref)
        kmean_ref[0:nb, :] = km_ref[0]
        for jb in range(nb):
            vt_ref[jb] = v_ref[0, jb * blk:(jb + 1) * blk, :].astype(F32).T.astype(BF16)

    qw = MOBA_QBLOCKS * blk
    i0 = i * MOBA_QBLOCKS
    q_t = q_ref[0].astype(F32).T
    qs_t = (q_t * (scale * LOG2E)).astype(BF16)

    gate = jnp.dot(kmean_ref[...].astype(BF16), q_t.astype(BF16), preferred_element_type=F32)
    nbp = kmean_ref.shape[0]
    row = lax.broadcasted_iota(jnp.int32, (nbp, qw), 0).astype(F32)
    own = (i0 + lax.broadcasted_iota(jnp.int32, (1, qw), 1) // blk).astype(F32)
    g = jnp.where(row < own, gate, -jnp.inf)
    sel = jnp.zeros((nbp, qw), F32)
    for t in range(MOBA_TOPK):
        mx = jnp.max(g, axis=0, keepdims=True)
        idx = jnp.min(jnp.where(g == mx, row, float(nbp)), axis=0, keepdims=True)
        hit = row == idx
        sel = jnp.maximum(sel, jnp.where(hit & (own > t), 1.0, 0.0))
        g = jnp.where(hit, -jnp.inf, g)
    neg_ref[...] = jnp.where(sel > 0.5, 0.0, NEG_BIG)

    def far_scores(j0):
        start = pl.multiple_of(j0 * blk, blk)
        return jnp.dot(k_ref[0, pl.ds(start, MOBA_GROUP * blk), :], qs_t, preferred_element_type=F32)

    s_ref[0] = far_scores(0)

    def softmax_group(scores, offsets, v_idx, m_old):
        m_new = m_old
        for sj, off in zip(scores, offsets):
            m_new = jnp.maximum(m_new, jnp.max(sj, axis=0, keepdims=True) + off)
        l_sum = jnp.zeros((1, qw), F32)
        pv = jnp.zeros((ATTN_HEAD_DIM, qw), F32)
        for sj, off, vj in zip(scores, offsets, v_idx):
            pj = jnp.exp2(sj + (off - m_new))
            l_sum = l_sum + jnp.sum(pj, axis=0, keepdims=True)
            pv = pv + jnp.dot(vt_ref[vj], pj.astype(BF16), preferred_element_type=F32)
        return m_new, l_sum, pv

    far_bias = rel_ref[h, REL_BUCKETS - 1] * LOG2E
    neg_row = jnp.full((1, blk), NEG_BIG, F32)

    key = lax.broadcasted_iota(jnp.int32, (blk, blk), 0)
    qry = lax.broadcasted_iota(jnp.int32, (blk, blk), 1)
    scores, offsets, v_idx = [], [], []
    for e in range(MOBA_NEAR - 1 + MOBA_QBLOCKS):
        jn = i0 - (MOBA_NEAR - 1) + e
        jc = jnp.maximum(jn, 0)
        start = pl.multiple_of(jc * blk, blk)
        se = jnp.dot(k_ref[0, pl.ds(start, blk), :], qs_t, preferred_element_type=F32)
        sel_row = neg_ref[pl.ds(jnp.where(jn >= 0, jn, nbp - 1), 1), :]
        s_parts, o_parts = [], []
        for w in range(MOBA_QBLOCKS):
            d = w + (MOBA_NEAR - 1) - e
            sw = se[:, w * blk:(w + 1) * blk]
            ow = sel_row[:, w * blk:(w + 1) * blk]
            if d < 0:
                ow = neg_row
            elif d == 0:
                sw = jnp.where(qry >= key, sw + bias_ref[0], NEG_BIG)
                ow = jnp.zeros((1, blk), F32)
            elif d < MOBA_NEAR:
                sw = sw + bias_ref[d]
            else:
                ow = ow + far_bias
            s_parts.append(sw)
            o_parts.append(ow)
        scores.append(jnp.concatenate(s_parts, axis=1))
        offsets.append(jnp.concatenate(o_parts, axis=1))
        v_idx.append(jc)
    m_floor = jnp.concatenate(
        [jnp.max(scores[MOBA_NEAR - 1 + w][:, w * blk:(w + 1) * blk], axis=0, keepdims=True)
         for w in range(MOBA_QBLOCKS)], axis=1)
    parts = [softmax_group([sj], [off], [vj], m_floor) for sj, off, vj in zip(scores, offsets, v_idx)]
    m0 = parts[0][0]
    for mu, _, _ in parts[1:]:
        m0 = jnp.maximum(m0, mu)
    l0 = jnp.zeros((1, qw), F32)
    pv0 = jnp.zeros((ATTN_HEAD_DIM, qw), F32)
    for mu, lu, pvu in parts:
        wu = jnp.exp2(mu - m0)
        l0 = l0 + wu * lu
        pv0 = pv0 + wu * pvu
    m_ref[...] = m0
    l_ref[...] = l0
    acc_ref[...] = pv0

    n_far = jnp.maximum(i0 - (MOBA_NEAR - 1), 0)

    n_groups = (n_far + MOBA_GROUP - 1) // MOBA_GROUP

    def far_group(gi, slot):
        j0 = gi * MOBA_GROUP
        s_ref[1 - slot] = far_scores(jnp.minimum(j0 + MOBA_GROUP, nb - MOBA_GROUP))
        scores, offsets, v_idx = [], [], []
        for u in range(MOBA_GROUP):
            ju = j0 + u
            scores.append(s_ref[slot, u * blk:(u + 1) * blk, :])
            offsets.append(neg_ref[pl.ds(jnp.where(ju < n_far, ju, nbp - 1), 1), :] + far_bias)
            v_idx.append(ju)
        m_old = m_ref[...]
        m_new, l_sum, pv = softmax_group(scores, offsets, v_idx, m_old)
        alpha = jnp.exp2(m_old - m_new)
        l_ref[...] = alpha * l_ref[...] + l_sum
        acc_ref[...] = alpha * acc_ref[...] + pv
        m_ref[...] = m_new

    def far_body(t, carry):
        far_group(2 * t, 0)

        @pl.when(2 * t + 1 < n_groups)
        def _():
            far_group(2 * t + 1, 1)

        return carry

    lax.fori_loop(0, (n_groups + 1) // 2, far_body, 0)

    o_ref[0] = (acc_ref[...] / l_ref[...]).T.astype(o_ref.dtype)


def _moba(proj3, kmean3, rel_t):
    bsz, s_len, _ = proj3.shape
    blk = MOBA_BLOCK
    nb = s_len // blk
    qw = MOBA_QBLOCKS * blk
    assert nb * blk == s_len and nb % MOBA_QBLOCKS == 0 and nb >= MOBA_GROUP
    nbp = -(-(nb + 1) // 16) * 16
    cb = LANES
    return pl.pallas_call(
        functools.partial(_moba_kernel, nb=nb),
        grid=(ATTN_HEADS, bsz, nb // MOBA_QBLOCKS),
        in_specs=[
            pl.BlockSpec(memory_space=pltpu.SMEM),
            pl.BlockSpec((1, qw, cb), lambda h, b, i: (b, i, COL_Q // cb + h)),
            pl.BlockSpec((1, s_len, cb), lambda h, b, i: (b, 0, COL_K // cb + h)),
            pl.BlockSpec((1, s_len, cb), lambda h, b, i: (b, 0, COL_V // cb + h)),
            pl.BlockSpec((1, nb, cb), lambda h, b, i: (b, 0, h)),
        ],
        out_specs=pl.BlockSpec((1, qw, cb), lambda h, b, i: (b, i, h)),
        out_shape=jax.ShapeDtypeStruct((bsz, s_len, ATTN_HEADS * ATTN_HEAD_DIM), BF16),
        scratch_shapes=[
            pltpu.VMEM((nbp, ATTN_HEAD_DIM), F32),
            pltpu.VMEM((MOBA_NEAR, blk, blk), F32),
            pltpu.VMEM((nb, ATTN_HEAD_DIM, blk), BF16),
            pltpu.VMEM((nbp, qw), F32),
            pltpu.VMEM((2, MOBA_GROUP * blk, qw), F32),
            pltpu.VMEM((1, qw), F32),
            pltpu.VMEM((1, qw), F32),
            pltpu.VMEM((ATTN_HEAD_DIM, qw), F32),
        ],
        compiler_params=_cparams(("arbitrary", "arbitrary", "arbitrary")),
        name="moba",
    )(rel_t, proj3, proj3, proj3, kmean3)


def _ssd_kernel(xbc_ref, z_ref, dt_ref, cw_ref, cbias_ref, dtb_ref, alog_ref, dskip_ref, ng_ref,
                tril_ref, e_ref, shift_ref, o_ref, tail_ref, state_ref):
    L = SSM_CHUNK
    G = SSM_GROUPS
    N = SSM_STATE
    GW = SSM_INNER // G
    pad = SUBLANES
    nt = (((1,), (1,)), ((), ()))

    @pl.when(pl.program_id(1) == 0)
    def _():
        tail_ref[...] = jnp.zeros_like(tail_ref)
        state_ref[...] = jnp.zeros_like(state_ref)

    xb = xbc_ref[0]
    x = xb.astype(F32)
    conv = cbias_ref[...] + x * cw_ref[SSM_CONV - 1:SSM_CONV, :]
    head = jnp.zeros((pad, SSM_CONV_DIM), F32)
    for k in range(SSM_CONV - 1):
        s = SSM_CONV - 1 - k
        w_k = cw_ref[k:k + 1, :]
        conv = conv + jnp.dot(shift_ref[s - 1], xb, preferred_element_type=F32) * w_k
        head = head + tail_ref[pad - s:2 * pad - s, :] * w_k
    conv = jnp.concatenate([conv[0:pad] + head, conv[pad:]], axis=0)
    tail_ref[0:pad, :] = x[L - pad:L]
    xa = _silu(conv)
    xs = xa[:, :SSM_INNER]
    bm = xa[:, SSM_INNER:SSM_INNER + G * N]
    cm = xa[:, SSM_INNER + G * N:].astype(BF16)

    dtr = dt_ref[0] + dtb_ref[...]
    dt = jnp.maximum(dtr, 0.0) + jnp.log1p(jnp.exp(-jnp.abs(dtr)))
    a = dt * (-jnp.exp(alog_ref[...]))
    a_cat = jnp.concatenate(_split_bf16(a, 3), axis=0)
    acs = jnp.dot(tril_ref[...], a_cat, preferred_element_type=F32)
    acs_t = acs.T

    def expand(v):
        v_cat = jnp.concatenate(_split_bf16(v, 2), axis=1)
        return jnp.dot(v_cat, e_ref[...], preferred_element_type=F32)

    dt_x = expand(dt)
    acs_x = expand(acs)
    xc = xs * dt_x
    xc_b = xc.astype(BF16)
    last = acs_x[L - 1:L, :]
    ea_x = jnp.exp(acs_x)
    w_end = (jnp.exp(last - acs_x) * xc).astype(BF16)
    ea_last = jnp.exp(last)

    row = lax.broadcasted_iota(jnp.int32, (L, L), 0)
    col = lax.broadcasted_iota(jnp.int32, (L, L), 1)
    causal = row >= col
    lane = lax.broadcasted_iota(jnp.int32, (L, LANES), 1)
    low_half = lane < SSM_HEAD_DIM

    y_parts = []
    for g in range(G):
        bg = bm[:, g * N:(g + 1) * N]
        cg = cm[:, g * N:(g + 1) * N]
        cb = lax.dot_general(cg, bg.astype(BF16), nt, preferred_element_type=F32)
        st = state_ref[g]
        y_off = jnp.dot(cg, st.astype(BF16), preferred_element_type=F32) * ea_x[:, g * GW:(g + 1) * GW]
        for pr in range(GW // LANES):
            c0 = g * GW + pr * LANES
            x_pair = xc_b[:, c0:c0 + LANES]
            zero = jnp.zeros_like(x_pair)
            y_pair = y_off[:, pr * LANES:(pr + 1) * LANES]
            for half in range(2):
                hd = c0 // SSM_HEAD_DIM + half
                seg = acs[:, hd:hd + 1] - acs_t[hd:hd + 1, :]
                decay = jnp.exp(jnp.where(causal, seg, -jnp.inf))
                mmat = (cb * decay).astype(BF16)
                x_half = jnp.where(low_half, x_pair, zero) if half == 0 else jnp.where(low_half, zero, x_pair)
                y_pair = y_pair + jnp.dot(mmat, x_half, preferred_element_type=F32)
            y_parts.append(y_pair)
        bg_t = bg.T.astype(BF16)
        state_ref[g] = (st * ea_last[:, g * GW:(g + 1) * GW]
                        + jnp.dot(bg_t, w_end[:, g * GW:(g + 1) * GW], preferred_element_type=F32))

    y = jnp.concatenate(y_parts, axis=1) + dskip_ref[...] * xs
    v = y * _silu(z_ref[0].astype(F32))
    outs = []
    for g in range(G):
        vg = v[:, g * GW:(g + 1) * GW]
        ms = jnp.mean(vg * vg, axis=-1, keepdims=True)
        outs.append(vg * lax.rsqrt(ms + NORM_EPS))
    o_ref[0] = (jnp.concatenate(outs, axis=1) * ng_ref[...]).astype(o_ref.dtype)


def _ssd(proj3, dt3, conv_w, conv_b, dt_bias, a_log, d_skip_x, norm_g, tril3, e2, shifts):
    bsz, s_len, _ = proj3.shape
    L = SSM_CHUNK
    nc = s_len // L
    assert nc * L == s_len
    const = lambda b, c: (0, 0)
    return pl.pallas_call(
        _ssd_kernel,
        grid=(bsz, nc),
        in_specs=[
            pl.BlockSpec((1, L, SSM_CONV_DIM), lambda b, c: (b, c, COL_XBC // SSM_CONV_DIM)),
            pl.BlockSpec((1, L, SSM_INNER), lambda b, c: (b, c, COL_Z // SSM_INNER)),
            pl.BlockSpec((1, L, LANES), lambda b, c: (b, c, 0)),
            pl.BlockSpec((SSM_CONV, SSM_CONV_DIM), const),
            pl.BlockSpec((1, SSM_CONV_DIM), const),
            pl.BlockSpec((1, LANES), const),
            pl.BlockSpec((1, LANES), const),
            pl.BlockSpec((1, SSM_INNER), const),
            pl.BlockSpec((1, SSM_INNER), const),
            pl.BlockSpec((L, 3 * L), const),
            pl.BlockSpec((2 * LANES, SSM_INNER), const),
            pl.BlockSpec((SSM_CONV - 1, L, L), lambda b, c: (0, 0, 0)),
        ],
        out_specs=pl.BlockSpec((1, L, SSM_INNER), lambda b, c: (b, c, 0)),
        out_shape=jax.ShapeDtypeStruct((bsz, s_len, SSM_INNER), BF16),
        scratch_shapes=[
            pltpu.VMEM((2 * SUBLANES, SSM_CONV_DIM), F32),
            pltpu.VMEM((SSM_GROUPS, SSM_STATE, SSM_INNER // SSM_GROUPS), F32),
        ],
        compiler_params=_cparams(("arbitrary", "arbitrary")),
        name="ssd",
    )(proj3, proj3, dt3, conv_w, conv_b, dt_bias, a_log, d_skip_x, norm_g, tril3, e2, shifts)


def _memkv_kernel(mem_ref, g_ref, w_ref, kv_ref):
    x = mem_ref[...]
    ms = jnp.mean(x * x, axis=-1, keepdims=True)
    u = (x * lax.rsqrt(ms + NORM_EPS) * g_ref[...]).astype(BF16)
    kv_ref[...] = jnp.dot(u, w_ref[...], preferred_element_type=F32).astype(BF16)


def _mem_kv(mem2, g, w_kv):
    rows = mem2.shape[0]
    width = w_kv.shape[1]
    return pl.pallas_call(
        _memkv_kernel,
        grid=(1,),
        in_specs=[
            pl.BlockSpec((rows, D_MODEL), lambda i: (0, 0)),
            pl.BlockSpec((1, D_MODEL), lambda i: (0, 0)),
            pl.BlockSpec((D_MODEL, width), lambda i: (0, 0)),
        ],
        out_specs=pl.BlockSpec((rows, width), lambda i: (0, 0)),
        out_shape=jax.ShapeDtypeStruct((rows, width), BF16),
        compiler_params=_cparams(("arbitrary",)),
        name="mem_kv",
    )(mem2, g, w_kv)


def _merge_kernel(x_ref, oa_ref, os_ref, qm_ref, gl_ref, kv_ref, bg_ref, wa_ref, ws_ref, wm_ref, wo_ref, h_ref):
    nt = (((1,), (1,)), ((), ()))
    hd = MEM_HEAD_DIM
    width = MEM_HEADS * hd
    scale = hd ** -0.5
    qm = qm_ref[0]
    kv = kv_ref[0]
    outs = []
    for hh in range(MEM_HEADS):
        q = qm[:, hh * hd:(hh + 1) * hd]
        km = kv[:, hh * hd:(hh + 1) * hd]
        vm = kv[:, width + hh * hd:width + (hh + 1) * hd]
        s = lax.dot_general(q, km, nt, preferred_element_type=F32) * scale
        p = jnp.exp(s - jnp.max(s, axis=1, keepdims=True))
        o = jnp.dot(p.astype(BF16), vm, preferred_element_type=F32)
        outs.append(o / jnp.sum(p, axis=1, keepdims=True))
    o_mem = jnp.concatenate(outs, axis=1).astype(BF16)

    gates = _sigmoid(gl_ref[0].astype(F32) + bg_ref[...])
    merged = (gates[:, :D_MODEL] * jnp.dot(oa_ref[0], wa_ref[...], preferred_element_type=F32)
              + gates[:, D_MODEL:2 * D_MODEL] * jnp.dot(os_ref[0], ws_ref[...], preferred_element_type=F32)
              + gates[:, 2 * D_MODEL:] * jnp.dot(o_mem, wm_ref[...], preferred_element_type=F32))
    h_ref[0] = x_ref[0] + jnp.dot(merged.astype(BF16), wo_ref[...], preferred_element_type=F32)


def _resident(shape):
    return pl.BlockSpec(shape, lambda *_: (0,) * len(shape), pipeline_mode=pl.Buffered(1))


def _merge(x, o_attn, o_ssm, proj3, kv3, b_gate, wa, ws, wm, wo, tm):
    bsz, s_len, _ = x.shape
    mem_len = kv3.shape[1]
    return pl.pallas_call(
        _merge_kernel,
        grid=(bsz, s_len // tm),
        in_specs=[
            pl.BlockSpec((1, tm, D_MODEL), lambda b, i: (b, i, 0)),
            pl.BlockSpec((1, tm, D_MODEL), lambda b, i: (b, i, 0)),
            pl.BlockSpec((1, tm, SSM_INNER), lambda b, i: (b, i, 0)),
            pl.BlockSpec((1, tm, D_MODEL), lambda b, i: (b, i, COL_QM // D_MODEL)),
            pl.BlockSpec((1, tm, 3 * D_MODEL), lambda b, i: (b, i, COL_GATE // (3 * D_MODEL))),
            pl.BlockSpec((1, mem_len, 2 * D_MODEL), lambda b, i: (b, 0, 0)),
            _resident((1, 3 * D_MODEL)),
            _resident((D_MODEL, D_MODEL)),
            _resident((SSM_INNER, D_MODEL)),
            _resident((D_MODEL, D_MODEL)),
            _resident((D_MODEL, D_MODEL)),
        ],
        out_specs=pl.BlockSpec((1, tm, D_MODEL), lambda b, i: (b, i, 0)),
        out_shape=jax.ShapeDtypeStruct((bsz, s_len, D_MODEL), F32),
        compiler_params=_cparams(("arbitrary", "arbitrary")),
        name="merge",
    )(x, o_attn, o_ssm, proj3, proj3, kv3, b_gate, wa, ws, wm, wo)


def _ffn_kernel(h_ref, ng_ref, wup_ref, cw_ref, cb_ref, wdn_ref, fg_ref, o_ref, hid_ref, *, tm):
    pad = SUBLANES

    @pl.when(pl.program_id(1) == 0)
    def _():
        hid_ref[0:pad, :] = jnp.zeros((pad, 2 * FFN_HIDDEN), F32)

    h = h_ref[0]
    ms = jnp.mean(h * h, axis=-1, keepdims=True)
    u = (h * lax.rsqrt(ms + NORM_EPS) * ng_ref[...]).astype(BF16)
    hid_ref[pad:pad + tm, :] = jnp.dot(u, wup_ref[...], preferred_element_type=F32)
    conv = cb_ref[...]
    for k in range(FFN_CONV):
        off = pad - (FFN_CONV - 1) + k
        conv = conv + hid_ref[off:off + tm, :] * cw_ref[k:k + 1, :]
    hid_ref[0:pad, :] = hid_ref[tm:tm + pad, :]
    act = (_silu(conv[:, :FFN_HIDDEN]) * conv[:, FFN_HIDDEN:]).astype(BF16)
    y = h + jnp.dot(act, wdn_ref[...], preferred_element_type=F32)
    ms2 = jnp.mean(y * y, axis=-1, keepdims=True)
    o_ref[0] = y * lax.rsqrt(ms2 + NORM_EPS) * fg_ref[...]


def _ffn(h, norm_g, w_up, conv_w, conv_b, w_down, final_g, tm):
    bsz, s_len, _ = h.shape
    f2 = 2 * FFN_HIDDEN
    return pl.pallas_call(
        functools.partial(_ffn_kernel, tm=tm),
        grid=(bsz, s_len // tm),
        in_specs=[
            pl.BlockSpec((1, tm, D_MODEL), lambda b, i: (b, i, 0)),
            _resident((1, D_MODEL)),
            _resident((D_MODEL, f2)),
            _resident((FFN_CONV, f2)),
            _resident((1, f2)),
            _resident((FFN_HIDDEN, D_MODEL)),
            _resident((1, D_MODEL)),
        ],
        out_specs=pl.BlockSpec((1, tm, D_MODEL), lambda b, i: (b, i, 0)),
        out_shape=jax.ShapeDtypeStruct((bsz, s_len, D_MODEL), F32),
        scratch_shapes=[pltpu.VMEM((tm + 2 * SUBLANES, f2), F32)],
        compiler_params=_cparams(("arbitrary", "arbitrary")),
        name="ffn",
    )(h, norm_g, w_up, conv_w, conv_b, w_down, final_g)


def _ssd_constants():
    L = SSM_CHUNK
    tril = np.tril(np.ones((L, L), np.float32))
    tril3 = np.concatenate([tril, tril, tril], axis=1)
    e = np.zeros((LANES, SSM_INNER), np.float32)
    for hd in range(SSM_HEADS):
        e[hd, hd * SSM_HEAD_DIM:(hd + 1) * SSM_HEAD_DIM] = 1.0
    e2 = np.concatenate([e, e], axis=0)
    shifts = np.stack([np.eye(L, k=-s, dtype=np.float32) for s in range(1, SSM_CONV)])
    return jnp.asarray(tril3, BF16), jnp.asarray(e2, BF16), jnp.asarray(shifts, BF16)


def _pad_lanes(v):
    return jnp.pad(v, ((0, 0), (0, LANES - v.shape[1])))


def _layer(h, mem, rel_bias, mix_norm_g, w_in, b_gate, ssm_conv_w, ssm_conv_b, ssm_dt_bias, ssm_A_log, ssm_D,
           ssm_norm_g, mem_norm_g, w_mem_kv, w_br_attn, w_br_ssm, w_br_mem, w_out, ffn_norm_g, w_ffn_up,
           ffn_conv_w, ffn_conv_b, w_ffn_down, final_g):
    bsz, s_len, _ = h.shape
    w_cat = jnp.concatenate([w_in[:, 5120:8192], w_in[:, 9248:12320], w_in[:, 3072:5120], w_in[:, 0:3072],
                             w_in[:, 8224:9248]], axis=1).astype(BF16)
    w_dt = _pad_lanes(w_in[:, 8192:8224]).astype(BF16)

    proj, dt_raw, kmean = _in_proj(h.reshape(bsz * s_len, D_MODEL), mix_norm_g[None, :], w_cat, w_dt,
                                   tm=1024, tn=1024)
    proj3 = proj.reshape(bsz, s_len, PROJ_WIDTH)
    dt3 = dt_raw.reshape(bsz, s_len, LANES)
    kmean3 = kmean.reshape(bsz, s_len // MOBA_BLOCK, ATTN_HEADS * ATTN_HEAD_DIM)

    o_attn = _moba(proj3, kmean3, rel_bias.T)

    tril3, e2, shifts = _ssd_constants()
    o_ssm = _ssd(proj3, dt3, ssm_conv_w, ssm_conv_b[None, :], _pad_lanes(ssm_dt_bias[None, :]),
                 _pad_lanes(ssm_A_log[None, :]), jnp.repeat(ssm_D, SSM_HEAD_DIM)[None, :], ssm_norm_g[None, :],
                 tril3, e2, shifts)

    mem_len = mem.shape[1]
    kv = _mem_kv(mem.reshape(bsz * mem_len, D_MODEL), mem_norm_g[None, :], w_mem_kv.astype(BF16))
    kv3 = kv.reshape(bsz, mem_len, 2 * D_MODEL)

    h1 = _merge(h, o_attn, o_ssm, proj3, kv3, b_gate[None, :], w_br_attn.astype(BF16), w_br_ssm.astype(BF16),
                w_br_mem.astype(BF16), w_out.astype(BF16), tm=512)

    return _ffn(h1, ffn_norm_g[None, :], w_ffn_up.astype(BF16), ffn_conv_w, ffn_conv_b[None, :],
                w_ffn_down.astype(BF16), final_g[None, :], tm=256)


def kernel(x, mem, rel_bias, mix_norm_g, w_in, b_gate, ssm_conv_w, ssm_conv_b, ssm_dt_bias, ssm_A_log, ssm_D,
           ssm_norm_g, mem_norm_g, w_mem_kv, w_br_attn, w_br_ssm, w_br_mem, w_out, ffn_norm_g, w_ffn_up,
           ffn_conv_w, ffn_conv_b, w_ffn_down, final_norm_g):
    assert w_in.shape[0] == 1, "single-layer trunk"
    return _layer(x, mem, rel_bias, mix_norm_g[0], w_in[0], b_gate[0], ssm_conv_w[0], ssm_conv_b[0],
                  ssm_dt_bias[0], ssm_A_log[0], ssm_D[0], ssm_norm_g[0], mem_norm_g[0], w_mem_kv[0],
                  w_br_attn[0], w_br_ssm[0], w_br_mem[0], w_out[0], ffn_norm_g[0], w_ffn_up[0],
                  ffn_conv_w[0], ffn_conv_b[0], w_ffn_down[0], final_norm_g)
```

```python
import functools
import math

import numpy as np
import jax
import jax.numpy as jnp
from jax import lax
from jax.experimental import pallas as pl
from jax.experimental.pallas import tpu as pltpu

F32 = jnp.float32
BF16 = jnp.bfloat16

D_MODEL = 1024
ATTN_HEADS = 8
ATTN_HEAD_DIM = 128
MOBA_BLOCK = 256
MOBA_TOPK = 3
REL_BUCKETS = 32
REL_MAX_DIST = 1024
SSM_INNER = 2048
SSM_HEAD_DIM = 64
SSM_HEADS = 32
SSM_GROUPS = 4
SSM_STATE = 128
SSM_CONV = 4
SSM_CHUNK = 256
SSM_CONV_DIM = 3072
MEM_HEADS = 4
MEM_HEAD_DIM = 256
FFN_HIDDEN = 2816
FFN_CONV = 3
NORM_EPS = 1e-6

LANES = 128
SUBLANES = 8
VMEM_LIMIT = 56 * 1024 * 1024

IN_PROJ_TM = 2048
IN_PROJ_TN = 1024
MERGE_TM = 512
FFN_TM = 512

COL_XBC = 0
COL_GATE = 3072
COL_Z = 6144
COL_Q = 8192
COL_K = 9216
COL_V = 10240
COL_QM = 11264
PROJ_WIDTH = 12288

MOBA_NEAR = -(-(REL_MAX_DIST + MOBA_BLOCK - 1) // MOBA_BLOCK)
NEG_BIG = -1e30
LOG2E = math.log2(math.e)
MOBA_QBLOCKS = 2
MOBA_GROUP = 4
assert MOBA_GROUP <= MOBA_NEAR + 1


def _cparams(sem):
    return pltpu.CompilerParams(dimension_semantics=sem, vmem_limit_bytes=VMEM_LIMIT)


def _sigmoid(x):
    return 1.0 / (1.0 + jnp.exp(-x))


def _silu(x):
    h = 0.5 * x
    return h + h * jnp.tanh(h)


def _split_bf16(x, parts):
    out = []
    r = x
    for _ in range(parts):
        hi = r.astype(BF16)
        out.append(hi)
        r = r - hi.astype(F32)
    return out


def _inproj_kernel(x_ref, g_ref, w_ref, wdt_ref, proj_ref, dt_ref, kmean_ref, u_ref, *, k_tile):
    j = pl.program_id(1)

    @pl.when(j == 0)
    def _():
        x = x_ref[...]
        ms = jnp.mean(x * x, axis=-1, keepdims=True)
        u = (x * lax.rsqrt(ms + NORM_EPS) * g_ref[...]).astype(BF16)
        u_ref[...] = u
        dt_ref[...] = jnp.dot(u, wdt_ref[...], preferred_element_type=F32)

    acc = jnp.dot(u_ref[...], w_ref[...], preferred_element_type=F32)
    proj_ref[...] = acc.astype(BF16)

    @pl.when(j == k_tile)
    def _():
        blk = MOBA_BLOCK
        for r in range(acc.shape[0] // blk):
            kmean_ref[0, r:r + 1, :] = jnp.sum(acc[r * blk:(r + 1) * blk], axis=0, keepdims=True) * (1.0 / blk)


def _in_proj(x2, g, w_cat, w_dt, tm, tn):
    t = x2.shape[0]
    kw = ATTN_HEADS * ATTN_HEAD_DIM
    assert tn == kw and COL_K % tn == 0 and tm % MOBA_BLOCK == 0
    return pl.pallas_call(
        functools.partial(_inproj_kernel, k_tile=COL_K // tn),
        grid=(t // tm, PROJ_WIDTH // tn),
        in_specs=[
            pl.BlockSpec((tm, D_MODEL), lambda i, j: (i, 0)),
            pl.BlockSpec((1, D_MODEL), lambda i, j: (0, 0)),
            pl.BlockSpec((D_MODEL, tn), lambda i, j: (0, j)),
            pl.BlockSpec((D_MODEL, LANES), lambda i, j: (0, 0)),
        ],
        out_specs=[
            pl.BlockSpec((tm, tn), lambda i, j: (i, j)),
            pl.BlockSpec((tm, LANES), lambda i, j: (i, 0)),
            pl.BlockSpec((1, tm // MOBA_BLOCK, kw), lambda i, j: (i, 0, 0)),
        ],
        out_shape=[
            jax.ShapeDtypeStruct((t, PROJ_WIDTH), BF16),
            jax.ShapeDtypeStruct((t, LANES), F32),
            jax.ShapeDtypeStruct((t // tm, tm // MOBA_BLOCK, kw), F32),
        ],
        scratch_shapes=[pltpu.VMEM((tm, D_MODEL), BF16)],
        compiler_params=_cparams(("arbitrary", "arbitrary")),
        name="in_proj",
    )(x2, g, w_cat, w_dt)


def _t5_bucket(dist):
    n = jnp.maximum(dist, 0)
    max_exact = REL_BUCKETS // 2
    nf = jnp.maximum(n, max_exact).astype(F32)
    large = max_exact + (jnp.log(nf * (1.0 / max_exact)) / math.log(REL_MAX_DIST / max_exact)
                         * (REL_BUCKETS - max_exact)).astype(jnp.int32)
    large = jnp.minimum(large, REL_BUCKETS - 1)
    return jnp.where(n < max_exact, n, large)


def _moba_kernel(rel_ref, q_ref, k_ref, v_ref, km_ref, o_ref, kmean_ref, bias_ref, vt_ref, neg_ref, s_ref, m_ref,
                 l_ref, acc_ref, *, nb):
    h = pl.program_id(0)
    b = pl.program_id(1)
    i = pl.program_id(2)
    blk = MOBA_BLOCK
    scale = ATTN_HEAD_DIM ** -0.5


    @pl.when((b == 0) & (i == 0))
    def _():
        key = lax.broadcasted_iota(jnp.int32, (blk, blk), 0)
        qry = lax.broadcasted_iota(jnp.int32, (blk, blk), 1)
        for d in range(MOBA_NEAR):
            bucket = _t5_bucket(d * blk + qry - key)
            tile = jnp.zeros((blk, blk), F32)
            for bk in range(REL_BUCKETS):
                tile = jnp.where(bucket == bk, rel_ref[h, bk], tile)
            bias_ref[d] = tile * LOG2E

    @pl.when(i == 0)
    def _():
        kmean_ref[...] = jnp.zeros_like(kmean_ref)
        kmean_ref[0:nb, :] = km_ref[0]
        for jb in range(nb):
            vt_ref[jb] = v_ref[0, jb * blk:(jb + 1) * blk, :].astype(F32).T.astype(BF16)

    qw = MOBA_QBLOCKS * blk
    i0 = i * MOBA_QBLOCKS
    q_t = q_ref[0].astype(F32).T
    qs_t = (q_t * (scale * LOG2E)).astype(BF16)

    gate = jnp.dot(kmean_ref[...].astype(BF16), q_t.astype(BF16), preferred_element_type=F32)
    nbp = kmean_ref.shape[0]
    row = lax.broadcasted_iota(jnp.int32, (nbp, qw), 0).astype(F32)
    own = (i0 + lax.broadcasted_iota(jnp.int32, (1, qw), 1) // blk).astype(F32)
    g = jnp.where(row < own, gate, -jnp.inf)
    sel = jnp.zeros((nbp, qw), F32)
    for t in range(MOBA_TOPK):
        mx = jnp.max(g, axis=0, keepdims=True)
        idx = jnp.min(jnp.where(g == mx, row, float(nbp)), axis=0, keepdims=True)
        hit = row == idx
        sel = jnp.maximum(sel, jnp.where(hit & (own > t), 1.0, 0.0))
        g = jnp.where(hit, -jnp.inf, g)
    neg_ref[...] = jnp.where(sel > 0.5, 0.0, NEG_BIG)

    def far_scores(j0):
        start = pl.multiple_of(j0 * blk, blk)
        return jnp.dot(k_ref[0, pl.ds(start, MOBA_GROUP * blk), :], qs_t, preferred_element_type=F32)

    s_ref[0] = far_scores(0)

    def softmax_group(scores, offsets, v_idx, m_old):
        m_new = m_old
        for sj, off in zip(scores, offsets):
            m_new = jnp.maximum(m_new, jnp.max(sj, axis=0, keepdims=True) + off)
        l_sum = jnp.zeros((1, qw), F32)
        pv = jnp.zeros((ATTN_HEAD_DIM, qw), F32)
        for sj, off, vj in zip(scores, offsets, v_idx):
            pj = jnp.exp2(sj + (off - m_new))
            l_sum = l_sum + jnp.sum(pj, axis=0, keepdims=True)
            pv = pv + jnp.dot(vt_ref[vj], pj.astype(BF16), preferred_element_type=F32)
        return m_new, l_sum, pv

    far_bias = rel_ref[h, REL_BUCKETS - 1] * LOG2E
    neg_row = jnp.full((1, blk), NEG_BIG, F32)

    key = lax.broadcasted_iota(jnp.int32, (blk, blk), 0)
    qry = lax.broadcasted_iota(jnp.int32, (blk, blk), 1)
    scores, offsets, v_idx = [], [], []
    for e in range(MOBA_NEAR - 1 + MOBA_QBLOCKS):
        jn = i0 - (MOBA_NEAR - 1) + e
        jc = jnp.maximum(jn, 0)
        start = pl.multiple_of(jc * blk, blk)
        se = jnp.dot(k_ref[0, pl.ds(start, blk), :], qs_t, preferred_element_type=F32)
        sel_row = neg_ref[pl.ds(jnp.where(jn >= 0, jn, nbp - 1), 1), :]
        s_parts, o_parts = [], []
        for w in range(MOBA_QBLOCKS):
            d = w + (MOBA_NEAR - 1) - e
            sw = se[:, w * blk:(w + 1) * blk]
            ow = sel_row[:, w * blk:(w + 1) * blk]
            if d < 0:
                ow = neg_row
            elif d == 0:
                sw = jnp.where(qry >= key, sw + bias_ref[0], NEG_BIG)
                ow = jnp.zeros((1, blk), F32)
            elif d < MOBA_NEAR:
                sw = sw + bias_ref[d]
            else:
                ow = ow + far_bias
            s_parts.append(sw)
            o_parts.append(ow)
        scores.append(jnp.concatenate(s_parts, axis=1))
        offsets.append(jnp.concatenate(o_parts, axis=1))
        v_idx.append(jc)
    m_floor = jnp.concatenate(
        [jnp.max(scores[MOBA_NEAR - 1 + w][:, w * blk:(w + 1) * blk], axis=0, keepdims=True)
         for w in range(MOBA_QBLOCKS)], axis=1)
    parts = [softmax_group([sj], [off], [vj], m_floor) for sj, off, vj in zip(scores, offsets, v_idx)]
    m0 = parts[0][0]
    for mu, _, _ in parts[1:]:
        m0 = jnp.maximum(m0, mu)
    l0 = jnp.zeros((1, qw), F32)
    pv0 = jnp.zeros((ATTN_HEAD_DIM, qw), F32)
    for mu, lu, pvu in parts:
        wu = jnp.exp2(mu - m0)
        l0 = l0 + wu * lu
        pv0 = pv0 + wu * pvu
    m_ref[...] = m0
    l_ref[...] = l0
    acc_ref[...] = pv0

    n_far = jnp.maximum(i0 - (MOBA_NEAR - 1), 0)

    n_groups = (n_far + MOBA_GROUP - 1) // MOBA_GROUP

    def far_group(gi, slot):
        j0 = gi * MOBA_GROUP
        s_ref[1 - slot] = far_scores(jnp.minimum(j0 + MOBA_GROUP, nb - MOBA_GROUP))
        scores, offsets, v_idx = [], [], []
        for u in range(MOBA_GROUP):
            ju = j0 + u
            scores.append(s_ref[slot, u * blk:(u + 1) * blk, :])
            offsets.append(neg_ref[pl.ds(jnp.where(ju < n_far, ju, nbp - 1), 1), :] + far_bias)
            v_idx.append(ju)
        m_old = m_ref[...]
        m_new, l_sum, pv = softmax_group(scores, offsets, v_idx, m_old)
        alpha = jnp.exp2(m_old - m_new)
        l_ref[...] = alpha * l_ref[...] + l_sum
        acc_ref[...] = alpha * acc_ref[...] + pv
        m_ref[...] = m_new

    def far_body(t, carry):
        far_group(2 * t, 0)

        @pl.when(2 * t + 1 < n_groups)
        def _():
            far_group(2 * t + 1, 1)

        return carry

    lax.fori_loop(0, (n_groups + 1) // 2, far_body, 0)

    o_ref[0] = (acc_ref[...] / l_ref[...]).T.astype(o_ref.dtype)


def _moba(proj3, kmean3, rel_t):
    bsz, s_len, _ = proj3.shape
    blk = MOBA_BLOCK
    nb = s_len // blk
    qw = MOBA_QBLOCKS * blk
    assert nb * blk == s_len and nb % MOBA_QBLOCKS == 0 and nb >= MOBA_GROUP
    nbp = -(-(nb + 1) // 16) * 16
    cb = LANES
    return pl.pallas_call(
        functools.partial(_moba_kernel, nb=nb),
        grid=(ATTN_HEADS, bsz, nb // MOBA_QBLOCKS),
        in_specs=[
            pl.BlockSpec(memory_space=pltpu.SMEM),
            pl.BlockSpec((1, qw, cb), lambda h, b, i: (b, i, COL_Q // cb + h)),
            pl.BlockSpec((1, s_len, cb), lambda h, b, i: (b, 0, COL_K // cb + h)),
            pl.BlockSpec((1, s_len, cb), lambda h, b, i: (b, 0, COL_V // cb + h)),
            pl.BlockSpec((1, nb, cb), lambda h, b, i: (b, 0, h)),
        ],
        out_specs=pl.BlockSpec((1, qw, cb), lambda h, b, i: (b, i, h)),
        out_shape=jax.ShapeDtypeStruct((bsz, s_len, ATTN_HEADS * ATTN_HEAD_DIM), BF16),
        scratch_shapes=[
            pltpu.VMEM((nbp, ATTN_HEAD_DIM), F32),
            pltpu.VMEM((MOBA_NEAR, blk, blk), F32),
            pltpu.VMEM((nb, ATTN_HEAD_DIM, blk), BF16),
            pltpu.VMEM((nbp, qw), F32),
            pltpu.VMEM((2, MOBA_GROUP * blk, qw), F32),
            pltpu.VMEM((1, qw), F32),
            pltpu.VMEM((1, qw), F32),
            pltpu.VMEM((ATTN_HEAD_DIM, qw), F32),
        ],
        compiler_params=_cparams(("arbitrary", "arbitrary", "arbitrary")),
        name="moba",
    )(rel_t, proj3, proj3, proj3, kmean3)


def _ssd_kernel(xbc_ref, z_ref, dt_ref, cw_ref, cbias_ref, dtb_ref, alog_ref, dskip_ref, ng_ref,
                tril_ref, e_ref, shift_ref, o_ref, tail_ref, state_ref):
    L = SSM_CHUNK
    G = SSM_GROUPS
    N = SSM_STATE
    GW = SSM_INNER // G
    pad = SUBLANES
    nt = (((1,), (1,)), ((), ()))

    @pl.when(pl.program_id(1) == 0)
    def _():
        tail_ref[...] = jnp.zeros_like(tail_ref)
        state_ref[...] = jnp.zeros_like(state_ref)

    xb = xbc_ref[0]
    x = xb.astype(F32)
    conv = cbias_ref[...] + x * cw_ref[SSM_CONV - 1:SSM_CONV, :]
    head = jnp.zeros((pad, SSM_CONV_DIM), F32)
    for k in range(SSM_CONV - 1):
        s = SSM_CONV - 1 - k
        w_k = cw_ref[k:k + 1, :]
        conv = conv + jnp.dot(shift_ref[s - 1], xb, preferred_element_type=F32) * w_k
        head = head + tail_ref[pad - s:2 * pad - s, :] * w_k
    conv = jnp.concatenate([conv[0:pad] + head, conv[pad:]], axis=0)
    tail_ref[0:pad, :] = x[L - pad:L]
    xa = _silu(conv)
    xs = xa[:, :SSM_INNER]
    bm = xa[:, SSM_INNER:SSM_INNER + G * N]
    cm = xa[:, SSM_INNER + G * N:].astype(BF16)

    dtr = dt_ref[0] + dtb_ref[...]
    dt = jnp.maximum(dtr, 0.0) + jnp.log1p(jnp.exp(-jnp.abs(dtr)))
    a = dt * (-jnp.exp(alog_ref[...]))
    a_cat = jnp.concatenate(_split_bf16(a, 3), axis=0)
    acs = jnp.dot(tril_ref[...], a_cat, preferred_element_type=F32)
    acs_t = acs.T

    def expand(v):
        v_cat = jnp.concatenate(_split_bf16(v, 2), axis=1)
        return jnp.dot(v_cat, e_ref[...], preferred_element_type=F32)

    dt_x = expand(dt)
    acs_x = expand(acs)
    xc = xs * dt_x
    xc_b = xc.astype(BF16)
    last = acs_x[L - 1:L, :]
    ea_x = jnp.exp(acs_x)
    w_end = (jnp.exp(last - acs_x) * xc).astype(BF16)
    ea_last = jnp.exp(last)

    row = lax.broadcasted_iota(jnp.int32, (L, L), 0)
    col = lax.broadcasted_iota(jnp.int32, (L, L), 1)
    causal = row >= col
    lane = lax.broadcasted_iota(jnp.int32, (L, LANES), 1)
    low_half = lane < SSM_HEAD_DIM

    y_parts = []
    for g in range(G):
        bg = bm[:, g * N:(g + 1) * N]
        cg = cm[:, g * N:(g + 1) * N]
        cb = lax.dot_general(cg, bg.astype(BF16), nt, preferred_element_type=F32)
        st = state_ref[g]
        y_off = jnp.dot(cg, st.astype(BF16), preferred_element_type=F32) * ea_x[:, g * GW:(g + 1) * GW]
        for pr in range(GW // LANES):
            c0 = g * GW + pr * LANES
            x_pair = xc_b[:, c0:c0 + LANES]
            zero = jnp.zeros_like(x_pair)
            y_pair = y_off[:, pr * LANES:(pr + 1) * LANES]
            for half in range(2):
                hd = c0 // SSM_HEAD_DIM + half
                seg = acs[:, hd:hd + 1] - acs_t[hd:hd + 1, :]
                decay = jnp.exp(jnp.where(causal, seg, -jnp.inf))
                mmat = (cb * decay).astype(BF16)
                x_half = jnp.where(low_half, x_pair, zero) if half == 0 else jnp.where(low_half, zero, x_pair)
                y_pair = y_pair + jnp.dot(mmat, x_half, preferred_element_type=F32)
            y_parts.append(y_pair)
        bg_t = bg.T.astype(BF16)
        state_ref[g] = (st * ea_last[:, g * GW:(g + 1) * GW]
                        + jnp.dot(bg_t, w_end[:, g * GW:(g + 1) * GW], preferred_element_type=F32))

    y = jnp.concatenate(y_parts, axis=1) + dskip_ref[...] * xs
    v = y * _silu(z_ref[0].astype(F32))
    outs = []
    for g in range(G):
        vg = v[:, g * GW:(g + 1) * GW]
        ms = jnp.mean(vg * vg, axis=-1, keepdims=True)
        outs.append(vg * lax.rsqrt(ms + NORM_EPS))
    o_ref[0] = (jnp.concatenate(outs, axis=1) * ng_ref[...]).astype(o_ref.dtype)


def _ssd(proj3, dt3, conv_w, conv_b, dt_bias, a_log, d_skip_x, norm_g, tril3, e2, shifts):
    bsz, s_len, _ = proj3.shape
    L = SSM_CHUNK
    nc = s_len // L
    assert nc * L == s_len
    const = lambda b, c: (0, 0)
    return pl.pallas_call(
        _ssd_kernel,
        grid=(bsz, nc),
        in_specs=[
            pl.BlockSpec((1, L, SSM_CONV_DIM), lambda b, c: (b, c, COL_XBC // SSM_CONV_DIM)),
            pl.BlockSpec((1, L, SSM_INNER), lambda b, c: (b, c, COL_Z // SSM_INNER)),
            pl.BlockSpec((1, L, LANES), lambda b, c: (b, c, 0)),
            pl.BlockSpec((SSM_CONV, SSM_CONV_DIM), const),
            pl.BlockSpec((1, SSM_CONV_DIM), const),
            pl.BlockSpec((1, LANES), const),
            pl.BlockSpec((1, LANES), const),
            pl.BlockSpec((1, SSM_INNER), const),
            pl.BlockSpec((1, SSM_INNER), const),
            pl.BlockSpec((L, 3 * L), const),
            pl.BlockSpec((2 * LANES, SSM_INNER), const),
            pl.BlockSpec((SSM_CONV - 1, L, L), lambda b, c: (0, 0, 0)),
        ],
        out_specs=pl.BlockSpec((1, L, SSM_INNER), lambda b, c: (b, c, 0)),
        out_shape=jax.ShapeDtypeStruct((bsz, s_len, SSM_INNER), BF16),
        scratch_shapes=[
            pltpu.VMEM((2 * SUBLANES, SSM_CONV_DIM), F32),
            pltpu.VMEM((SSM_GROUPS, SSM_STATE, SSM_INNER // SSM_GROUPS), F32),
        ],
        compiler_params=_cparams(("arbitrary", "arbitrary")),
        name="ssd",
    )(proj3, proj3, dt3, conv_w, conv_b, dt_bias, a_log, d_skip_x, norm_g, tril3, e2, shifts)


def _memkv_kernel(mem_ref, g_ref, w_ref, kv_ref):
    x = mem_ref[...]
    ms = jnp.mean(x * x, axis=-1, keepdims=True)
    u = (x * lax.rsqrt(ms + NORM_EPS) * g_ref[...]).astype(BF16)
    kv_ref[...] = jnp.dot(u, w_ref[...], preferred_element_type=F32).astype(BF16)


def _mem_kv(mem2, g, w_kv):
    rows = mem2.shape[0]
    width = w_kv.shape[1]
    return pl.pallas_call(
        _memkv_kernel,
        grid=(1,),
        in_specs=[
            pl.BlockSpec((rows, D_MODEL), lambda i: (0, 0)),
            pl.BlockSpec((1, D_MODEL), lambda i: (0, 0)),
            pl.BlockSpec((D_MODEL, width), lambda i: (0, 0)),
        ],
        out_specs=pl.BlockSpec((rows, width), lambda i: (0, 0)),
        out_shape=jax.ShapeDtypeStruct((rows, width), BF16),
        compiler_params=_cparams(("arbitrary",)),
        name="mem_kv",
    )(mem2, g, w_kv)


def _merge_kernel(x_ref, oa_ref, os_ref, qm_ref, gl_ref, kv_ref, bg_ref, wa_ref, ws_ref, wm_ref, wo_ref, h_ref):
    nt = (((1,), (1,)), ((), ()))
    hd = MEM_HEAD_DIM
    width = MEM_HEADS * hd
    scale = hd ** -0.5
    qm = qm_ref[0]
    kv = kv_ref[0]
    outs = []
    for hh in range(MEM_HEADS):
        q = qm[:, hh * hd:(hh + 1) * hd]
        km = kv[:, hh * hd:(hh + 1) * hd]
        vm = kv[:, width + hh * hd:width + (hh + 1) * hd]
        s = lax.dot_general(q, km, nt, preferred_element_type=F32) * scale
        p = jnp.exp(s - jnp.max(s, axis=1, keepdims=True))
        o = jnp.dot(p.astype(BF16), vm, preferred_element_type=F32)
        outs.append(o / jnp.sum(p, axis=1, keepdims=True))
    o_mem = jnp.concatenate(outs, axis=1).astype(BF16)

    gates = _sigmoid(gl_ref[0].astype(F32) + bg_ref[...])
    merged = (gates[:, :D_MODEL] * jnp.dot(oa_ref[0], wa_ref[...], preferred_element_type=F32)
              + gates[:, D_MODEL:2 * D_MODEL] * jnp.dot(os_ref[0], ws_ref[...], preferred_element_type=F32)
              + gates[:, 2 * D_MODEL:] * jnp.dot(o_mem, wm_ref[...], preferred_element_type=F32))
    h_ref[0] = x_ref[0] + jnp.dot(merged.astype(BF16), wo_ref[...], preferred_element_type=F32)


def _resident(shape):
    return pl.BlockSpec(shape, lambda *_: (0,) * len(shape), pipeline_mode=pl.Buffered(1))


def _merge(x, o_attn, o_ssm, proj3, kv3, b_gate, wa, ws, wm, wo, tm):
    bsz, s_len, _ = x.shape
    mem_len = kv3.shape[1]
    return pl.pallas_call(
        _merge_kernel,
        grid=(bsz, s_len // tm),
        in_specs=[
            pl.BlockSpec((1, tm, D_MODEL), lambda b, i: (b, i, 0)),
            pl.BlockSpec((1, tm, D_MODEL), lambda b, i: (b, i, 0)),
            pl.BlockSpec((1, tm, SSM_INNER), lambda b, i: (b, i, 0)),
            pl.BlockSpec((1, tm, D_MODEL), lambda b, i: (b, i, COL_QM // D_MODEL)),
            pl.BlockSpec((1, tm, 3 * D_MODEL), lambda b, i: (b, i, COL_GATE // (3 * D_MODEL))),
            pl.BlockSpec((1, mem_len, 2 * D_MODEL), lambda b, i: (b, 0, 0)),
            _resident((1, 3 * D_MODEL)),
            _resident((D_MODEL, D_MODEL)),
            _resident((SSM_INNER, D_MODEL)),
            _resident((D_MODEL, D_MODEL)),
            _resident((D_MODEL, D_MODEL)),
        ],
        out_specs=pl.BlockSpec((1, tm, D_MODEL), lambda b, i: (b, i, 0)),
        out_shape=jax.ShapeDtypeStruct((bsz, s_len, D_MODEL), F32),
        compiler_params=_cparams(("arbitrary", "arbitrary")),
        name="merge",
    )(x, o_attn, o_ssm, proj3, proj3, kv3, b_gate, wa, ws, wm, wo)


def _ffn_kernel(h_ref, ng_ref, wup_ref, cw_ref, cb_ref, wdn_ref, fg_ref, o_ref, hid_ref, *, tm):
    pad = SUBLANES

    @pl.when(pl.program_id(1) == 0)
    def _():
        hid_ref[0:pad, :] = jnp.zeros((pad, 2 * FFN_HIDDEN), F32)

    h = h_ref[0]
    ms = jnp.mean(h * h, axis=-1, keepdims=True)
    u = (h * lax.rsqrt(ms + NORM_EPS) * ng_ref[...]).astype(BF16)
    hid_ref[pad:pad + tm, :] = jnp.dot(u, wup_ref[...], preferred_element_type=F32)
    conv = cb_ref[...]
    for k in range(FFN_CONV):
        off = pad - (FFN_CONV - 1) + k
        conv = conv + hid_ref[off:off + tm, :] * cw_ref[k:k + 1, :]
    hid_ref[0:pad, :] = hid_ref[tm:tm + pad, :]
    act = (_silu(conv[:, :FFN_HIDDEN]) * conv[:, FFN_HIDDEN:]).astype(BF16)
    y = h + jnp.dot(act, wdn_ref[...], preferred_element_type=F32)
    ms2 = jnp.mean(y * y, axis=-1, keepdims=True)
    o_ref[0] = y * lax.rsqrt(ms2 + NORM_EPS) * fg_ref[...]


def _ffn(h, norm_g, w_up, conv_w, conv_b, w_down, final_g, tm):
    bsz, s_len, _ = h.shape
    f2 = 2 * FFN_HIDDEN
    return pl.pallas_call(
        functools.partial(_ffn_kernel, tm=tm),
        grid=(bsz, s_len // tm),
        in_specs=[
            pl.BlockSpec((1, tm, D_MODEL), lambda b, i: (b, i, 0)),
            _resident((1, D_MODEL)),
            _resident((D_MODEL, f2)),
            _resident((FFN_CONV, f2)),
            _resident((1, f2)),
            _resident((FFN_HIDDEN, D_MODEL)),
            _resident((1, D_MODEL)),
        ],
        out_specs=pl.BlockSpec((1, tm, D_MODEL), lambda b, i: (b, i, 0)),
        out_shape=jax.ShapeDtypeStruct((bsz, s_len, D_MODEL), F32),
        scratch_shapes=[pltpu.VMEM((tm + 2 * SUBLANES, f2), F32)],
        compiler_params=_cparams(("arbitrary", "arbitrary")),
        name="ffn",
    )(h, norm_g, w_up, conv_w, conv_b, w_down, final_g)


def _ssd_constants():
    L = SSM_CHUNK
    tril = np.tril(np.ones((L, L), np.float32))
    tril3 = np.concatenate([tril, tril, tril], axis=1)
    e = np.zeros((LANES, SSM_INNER), np.float32)
    for hd in range(SSM_HEADS):
        e[hd, hd * SSM_HEAD_DIM:(hd + 1) * SSM_HEAD_DIM] = 1.0
    e2 = np.concatenate([e, e], axis=0)
    shifts = np.stack([np.eye(L, k=-s, dtype=np.float32) for s in range(1, SSM_CONV)])
    return jnp.asarray(tril3, BF16), jnp.asarray(e2, BF16), jnp.asarray(shifts, BF16)


def _pad_lanes(v):
    return jnp.pad(v, ((0, 0), (0, LANES - v.shape[1])))


def _layer(h, mem, rel_bias, mix_norm_g, w_in, b_gate, ssm_conv_w, ssm_conv_b, ssm_dt_bias, ssm_A_log, ssm_D,
           ssm_norm_g, mem_norm_g, w_mem_kv, w_br_attn, w_br_ssm, w_br_mem, w_out, ffn_norm_g, w_ffn_up,
           ffn_conv_w, ffn_conv_b, w_ffn_down, final_g):
    bsz, s_len, _ = h.shape
    w_cat = jnp.concatenate([w_in[:, a:b].astype(BF16) for a, b in
                             ((5120, 8192), (9248, 12320), (3072, 5120), (0, 3072), (8224, 9248))], axis=1)
    w_dt = _pad_lanes(w_in[:, 8192:8224]).astype(BF16)

    proj, dt_raw, kmean = _in_proj(h.reshape(bsz * s_len, D_MODEL), mix_norm_g[None, :], w_cat, w_dt,
                                   tm=IN_PROJ_TM, tn=IN_PROJ_TN)
    proj3 = proj.reshape(bsz, s_len, PROJ_WIDTH)
    dt3 = dt_raw.reshape(bsz, s_len, LANES)
    kmean3 = kmean.reshape(bsz, s_len // MOBA_BLOCK, ATTN_HEADS * ATTN_HEAD_DIM)

    o_attn = _moba(proj3, kmean3, rel_bias.T)

    tril3, e2, shifts = _ssd_constants()
    o_ssm = _ssd(proj3, dt3, ssm_conv_w, ssm_conv_b[None, :], _pad_lanes(ssm_dt_bias[None, :]),
                 _pad_lanes(ssm_A_log[None, :]), jnp.repeat(ssm_D, SSM_HEAD_DIM)[None, :], ssm_norm_g[None, :],
                 tril3, e2, shifts)

    mem_len = mem.shape[1]
    kv = _mem_kv(mem.reshape(bsz * mem_len, D_MODEL), mem_norm_g[None, :], w_mem_kv.astype(BF16))
    kv3 = kv.reshape(bsz, mem_len, 2 * D_MODEL)

    h1 = _merge(h, o_attn, o_ssm, proj3, kv3, b_gate[None, :], w_br_attn.astype(BF16), w_br_ssm.astype(BF16),
                w_br_mem.astype(BF16), w_out.astype(BF16), tm=MERGE_TM)

    return _ffn(h1, ffn_norm_g[None, :], w_ffn_up.astype(BF16), ffn_conv_w, ffn_conv_b[None, :],
                w_ffn_down.astype(BF16), final_g[None, :], tm=FFN_TM)


def kernel(x, mem, rel_bias, mix_norm_g, w_in, b_gate, ssm_conv_w, ssm_conv_b, ssm_dt_bias, ssm_A_log, ssm_D,
           ssm_norm_g, mem_norm_g, w_mem_kv, w_br_attn, w_br_ssm, w_br_mem, w_out, ffn_norm_g, w_ffn_up,
           ffn_conv_w, ffn_conv_b, w_ffn_down, final_norm_g):
    assert w_in.shape[0] == 1, "single-layer trunk"
    return _layer(x, mem, rel_bias, mix_norm_g[0], w_in[0], b_gate[0], ssm_conv_w[0], ssm_conv_b[0],
                  ssm_dt_bias[0], ssm_A_log[0], ssm_D[0], ssm_norm_g[0], mem_norm_g[0], w_mem_kv[0],
                  w_br_attn[0], w_br_ssm[0], w_br_mem[0], w_out[0], ffn_norm_g[0], w_ffn_up[0],
                  ffn_conv_w[0], ffn_conv_b[0], w_ffn_down[0], final_norm_g)
```

```python
import functools
import math

import numpy as np
import jax
import jax.numpy as jnp
from jax import lax
from jax.experimental import pallas as pl
from jax.experimental.pallas import tpu as pltpu

F32 = jnp.float32
BF16 = jnp.bfloat16

D_MODEL = 1024
ATTN_HEADS = 8
ATTN_HEAD_DIM = 128
MOBA_BLOCK = 256
MOBA_TOPK = 3
REL_BUCKETS = 32
REL_MAX_DIST = 1024
SSM_INNER = 2048
SSM_HEAD_DIM = 64
SSM_HEADS = 32
SSM_GROUPS = 4
SSM_STATE = 128
SSM_CONV = 4
SSM_CHUNK = 256
SSM_CONV_DIM = 3072
MEM_HEADS = 4
MEM_HEAD_DIM = 256
FFN_HIDDEN = 2816
FFN_CONV = 3
NORM_EPS = 1e-6

LANES = 128
SUBLANES = 8
VMEM_LIMIT = 56 * 1024 * 1024

IN_PROJ_TM = 2048
IN_PROJ_TN = 1024
MERGE_TM = 512
FFN_TM = 512

COL_XBC = 0
COL_GATE = 3072
COL_Z = 6144
COL_Q = 8192
COL_K = 9216
COL_V = 10240
COL_QM = 11264
PROJ_WIDTH = 12288

MOBA_NEAR = -(-(REL_MAX_DIST + MOBA_BLOCK - 1) // MOBA_BLOCK)
NEG_BIG = -1e30
LOG2E = math.log2(math.e)
MOBA_QBLOCKS = 2
MOBA_GROUP = 4
MOBA_VT_ROWS = ATTN_HEAD_DIM + 16
assert MOBA_GROUP <= MOBA_NEAR + 1


def _cparams(sem):
    return pltpu.CompilerParams(dimension_semantics=sem, vmem_limit_bytes=VMEM_LIMIT)


def _sigmoid(x):
    return 1.0 / (1.0 + jnp.exp(-x))


def _silu(x):
    h = 0.5 * x
    return h + h * jnp.tanh(h)


def _split_bf16(x, parts):
    out = []
    r = x
    for _ in range(parts):
        hi = r.astype(BF16)
        out.append(hi)
        r = r - hi.astype(F32)
    return out


def _inproj_kernel(x_ref, g_ref, w_ref, wdt_ref, proj_ref, dt_ref, kmean_ref, u_ref, *, k_tile):
    j = pl.program_id(1)

    @pl.when(j == 0)
    def _():
        x = x_ref[...]
        ms = jnp.mean(x * x, axis=-1, keepdims=True)
        u = (x * lax.rsqrt(ms + NORM_EPS) * g_ref[...]).astype(BF16)
        u_ref[...] = u
        dt_ref[...] = jnp.dot(u, wdt_ref[...], preferred_element_type=F32)

    acc = jnp.dot(u_ref[...], w_ref[...], preferred_element_type=F32)
    proj_ref[...] = acc.astype(BF16)

    @pl.when(j == k_tile)
    def _():
        blk = MOBA_BLOCK
        for r in range(acc.shape[0] // blk):
            kmean_ref[0, r:r + 1, :] = jnp.sum(acc[r * blk:(r + 1) * blk], axis=0, keepdims=True) * (1.0 / blk)


def _in_proj(x2, g, w_cat, w_dt, tm, tn):
    t = x2.shape[0]
    kw = ATTN_HEADS * ATTN_HEAD_DIM
    assert tn == kw and COL_K % tn == 0 and tm % MOBA_BLOCK == 0
    return pl.pallas_call(
        functools.partial(_inproj_kernel, k_tile=COL_K // tn),
        grid=(t // tm, PROJ_WIDTH // tn),
        in_specs=[
            pl.BlockSpec((tm, D_MODEL), lambda i, j: (i, 0)),
            pl.BlockSpec((1, D_MODEL), lambda i, j: (0, 0)),
            pl.BlockSpec((D_MODEL, tn), lambda i, j: (0, j)),
            pl.BlockSpec((D_MODEL, LANES), lambda i, j: (0, 0)),
        ],
        out_specs=[
            pl.BlockSpec((tm, tn), lambda i, j: (i, j)),
            pl.BlockSpec((tm, LANES), lambda i, j: (i, 0)),
            pl.BlockSpec((1, tm // MOBA_BLOCK, kw), lambda i, j: (i, 0, 0)),
        ],
        out_shape=[
            jax.ShapeDtypeStruct((t, PROJ_WIDTH), BF16),
            jax.ShapeDtypeStruct((t, LANES), F32),
            jax.ShapeDtypeStruct((t // tm, tm // MOBA_BLOCK, kw), F32),
        ],
        scratch_shapes=[pltpu.VMEM((tm, D_MODEL), BF16)],
        compiler_params=_cparams(("arbitrary", "arbitrary")),
        name="in_proj",
    )(x2, g, w_cat, w_dt)


def _t5_bucket(dist):
    n = jnp.maximum(dist, 0)
    max_exact = REL_BUCKETS // 2
    nf = jnp.maximum(n, max_exact).astype(F32)
    large = max_exact + (jnp.log(nf * (1.0 / max_exact)) / math.log(REL_MAX_DIST / max_exact)
                         * (REL_BUCKETS - max_exact)).astype(jnp.int32)
    large = jnp.minimum(large, REL_BUCKETS - 1)
    return jnp.where(n < max_exact, n, large)


def _moba_kernel(rel_ref, q_ref, k_ref, v_ref, km_ref, o_ref, kmean_ref, bias_ref, vt_ref, neg_ref, s_ref, m_ref,
                 acc_ref, *, nb):
    h = pl.program_id(0)
    b = pl.program_id(1)
    i = pl.program_id(2)
    blk = MOBA_BLOCK
    scale = ATTN_HEAD_DIM ** -0.5


    @pl.when((b == 0) & (i == 0))
    def _():
        key = lax.broadcasted_iota(jnp.int32, (blk, blk), 0)
        qry = lax.broadcasted_iota(jnp.int32, (blk, blk), 1)
        for d in range(MOBA_NEAR):
            bucket = _t5_bucket(d * blk + qry - key)
            tile = jnp.zeros((blk, blk), F32)
            for bk in range(REL_BUCKETS):
                tile = jnp.where(bucket == bk, rel_ref[h, bk], tile)
            bias_ref[d] = tile * LOG2E

    @pl.when(i == 0)
    def _():
        kmean_ref[...] = jnp.zeros_like(kmean_ref)
        kmean_ref[0:nb, :] = km_ref[0]
        ones_rows = (lax.broadcasted_iota(jnp.int32, (MOBA_VT_ROWS - ATTN_HEAD_DIM, blk), 0) == 0).astype(BF16)
        for jb in range(nb):
            vt_ref[jb, 0:ATTN_HEAD_DIM, :] = v_ref[0, jb * blk:(jb + 1) * blk, :].astype(F32).T.astype(BF16)
            vt_ref[jb, ATTN_HEAD_DIM:MOBA_VT_ROWS, :] = ones_rows

    qw = MOBA_QBLOCKS * blk
    i0 = i * MOBA_QBLOCKS
    q_t = q_ref[0].astype(F32).T
    qs_t = (q_t * (scale * LOG2E)).astype(BF16)

    gate = jnp.dot(kmean_ref[...].astype(BF16), q_t.astype(BF16), preferred_element_type=F32)
    nbp = kmean_ref.shape[0]
    row = lax.broadcasted_iota(jnp.int32, (nbp, qw), 0).astype(F32)
    own = (i0 + lax.broadcasted_iota(jnp.int32, (1, qw), 1) // blk).astype(F32)
    g = jnp.where(row < own, gate, -jnp.inf)
    sel = jnp.zeros((nbp, qw), F32)
    for t in range(MOBA_TOPK):
        mx = jnp.max(g, axis=0, keepdims=True)
        idx = jnp.min(jnp.where(g == mx, row, float(nbp)), axis=0, keepdims=True)
        hit = row == idx
        sel = jnp.maximum(sel, jnp.where(hit & (own > t), 1.0, 0.0))
        g = jnp.where(hit, -jnp.inf, g)
    neg_ref[...] = jnp.where(sel > 0.5, 0.0, NEG_BIG)

    def far_scores(j0):
        start = pl.multiple_of(j0 * blk, blk)
        return jnp.dot(k_ref[0, pl.ds(start, MOBA_GROUP * blk), :], qs_t, preferred_element_type=F32)

    s_ref[0] = far_scores(0)

    def softmax_group(scores, offsets, v_idx, m_old):
        m_new = m_old
        for sj, off in zip(scores, offsets):
            m_new = jnp.maximum(m_new, jnp.max(sj, axis=0, keepdims=True) + off)
        pv = jnp.zeros((MOBA_VT_ROWS, m_old.shape[1]), F32)
        for sj, off, vj in zip(scores, offsets, v_idx):
            pj = jnp.exp2(sj + (off - m_new))
            pv = pv + jnp.dot(vt_ref[vj], pj.astype(BF16), preferred_element_type=F32)
        return m_new, pv

    far_bias = rel_ref[h, REL_BUCKETS - 1] * LOG2E

    key = lax.broadcasted_iota(jnp.int32, (blk, blk), 0)
    qry = lax.broadcasted_iota(jnp.int32, (blk, blk), 1)
    scores, offsets, v_idx, first_lane = [], [], [], []
    for e in range(MOBA_NEAR - 1 + MOBA_QBLOCKS):
        jn = i0 - (MOBA_NEAR - 1) + e
        jc = jnp.maximum(jn, 0)
        start = pl.multiple_of(jc * blk, blk)
        w_min = max(0, e - (MOBA_NEAR - 1))
        lo = w_min * blk
        se = jnp.dot(k_ref[0, pl.ds(start, blk), :], qs_t[:, lo:], preferred_element_type=F32)
        sel_row = neg_ref[pl.ds(jnp.where(jn >= 0, jn, nbp - 1), 1), :]
        s_parts, o_parts = [], []
        for w in range(w_min, MOBA_QBLOCKS):
            d = w + (MOBA_NEAR - 1) - e
            sw = se[:, (w - w_min) * blk:(w - w_min + 1) * blk]
            ow = sel_row[:, w * blk:(w + 1) * blk]
            if d == 0:
                sw = jnp.where(qry >= key, sw + bias_ref[0], NEG_BIG)
                ow = jnp.zeros((1, blk), F32)
            elif d < MOBA_NEAR:
                sw = sw + bias_ref[d]
            else:
                ow = ow + far_bias
            s_parts.append(sw)
            o_parts.append(ow)
        scores.append(jnp.concatenate(s_parts, axis=1))
        offsets.append(jnp.concatenate(o_parts, axis=1))
        v_idx.append(jc)
        first_lane.append(lo)
    m_floor = jnp.concatenate(
        [jnp.max(scores[MOBA_NEAR - 1 + w][:, 0:blk], axis=0, keepdims=True) for w in range(MOBA_QBLOCKS)], axis=1)
    parts = []
    for sj, off, vj, lo in zip(scores, offsets, v_idx, first_lane):
        mu, pvu = softmax_group([sj], [off], [vj], m_floor[:, lo:])
        if lo:
            mu = jnp.concatenate([m_floor[:, :lo], mu], axis=1)
            pvu = jnp.concatenate([jnp.zeros((MOBA_VT_ROWS, lo), F32), pvu], axis=1)
        parts.append((mu, pvu))
    m0 = parts[0][0]
    for mu, _ in parts[1:]:
        m0 = jnp.maximum(m0, mu)
    pv0 = jnp.zeros((MOBA_VT_ROWS, qw), F32)
    for mu, pvu in parts:
        pv0 = pv0 + jnp.exp2(mu - m0) * pvu
    m_ref[...] = m0
    acc_ref[...] = pv0

    n_far = jnp.maximum(i0 - (MOBA_NEAR - 1), 0)

    n_groups = (n_far + MOBA_GROUP - 1) // MOBA_GROUP

    def far_group(gi, slot):
        j0 = gi * MOBA_GROUP
        s_ref[1 - slot] = far_scores(jnp.minimum(j0 + MOBA_GROUP, nb - MOBA_GROUP))
        scores, offsets, v_idx = [], [], []
        for u in range(MOBA_GROUP):
            ju = j0 + u
            scores.append(s_ref[slot, u * blk:(u + 1) * blk, :])
            offsets.append(neg_ref[pl.ds(jnp.where(ju < n_far, ju, nbp - 1), 1), :] + far_bias)
            v_idx.append(ju)
        m_old = m_ref[...]
        m_new, pv = softmax_group(scores, offsets, v_idx, m_old)
        acc_ref[...] = jnp.exp2(m_old - m_new) * acc_ref[...] + pv
        m_ref[...] = m_new

    def far_body(t, carry):
        far_group(2 * t, 0)

        @pl.when(2 * t + 1 < n_groups)
        def _():
            far_group(2 * t + 1, 1)

        return carry

    lax.fori_loop(0, (n_groups + 1) // 2, far_body, 0)

    out_t = acc_ref[0:ATTN_HEAD_DIM, :] / acc_ref[ATTN_HEAD_DIM:ATTN_HEAD_DIM + 1, :]
    o_ref[0] = out_t.T.astype(o_ref.dtype)


def _moba(proj3, kmean3, rel_t):
    bsz, s_len, _ = proj3.shape
    blk = MOBA_BLOCK
    nb = s_len // blk
    qw = MOBA_QBLOCKS * blk
    assert nb * blk == s_len and nb % MOBA_QBLOCKS == 0 and nb >= MOBA_GROUP
    nbp = -(-(nb + 1) // 16) * 16
    cb = LANES
    return pl.pallas_call(
        functools.partial(_moba_kernel, nb=nb),
        grid=(ATTN_HEADS, bsz, nb // MOBA_QBLOCKS),
        in_specs=[
            pl.BlockSpec(memory_space=pltpu.SMEM),
            pl.BlockSpec((1, qw, cb), lambda h, b, i: (b, i, COL_Q // cb + h)),
            pl.BlockSpec((1, s_len, cb), lambda h, b, i: (b, 0, COL_K // cb + h)),
            pl.BlockSpec((1, s_len, cb), lambda h, b, i: (b, 0, COL_V // cb + h)),
            pl.BlockSpec((1, nb, cb), lambda h, b, i: (b, 0, h)),
        ],
        out_specs=pl.BlockSpec((1, qw, cb), lambda h, b, i: (b, i, h)),
        out_shape=jax.ShapeDtypeStruct((bsz, s_len, ATTN_HEADS * ATTN_HEAD_DIM), BF16),
        scratch_shapes=[
            pltpu.VMEM((nbp, ATTN_HEAD_DIM), F32),
            pltpu.VMEM((MOBA_NEAR, blk, blk), F32),
            pltpu.VMEM((nb, MOBA_VT_ROWS, blk), BF16),
            pltpu.VMEM((nbp, qw), F32),
            pltpu.VMEM((2, MOBA_GROUP * blk, qw), F32),
            pltpu.VMEM((1, qw), F32),
            pltpu.VMEM((MOBA_VT_ROWS, qw), F32),
        ],
        compiler_params=_cparams(("arbitrary", "arbitrary", "arbitrary")),
        name="moba",
    )(rel_t, proj3, proj3, proj3, kmean3)


def _ssd_kernel(xbc_ref, z_ref, dt_ref, cw_ref, cbias_ref, dtb_ref, alog_ref, dskip_ref, ng_ref,
                tril_ref, e_ref, shift_ref, o_ref, tail_ref, state_ref):
    L = SSM_CHUNK
    G = SSM_GROUPS
    N = SSM_STATE
    GW = SSM_INNER // G
    pad = SUBLANES
    nt = (((1,), (1,)), ((), ()))

    @pl.when(pl.program_id(1) == 0)
    def _():
        tail_ref[...] = jnp.zeros_like(tail_ref)
        state_ref[...] = jnp.zeros_like(state_ref)

    xb = xbc_ref[0]
    x = xb.astype(F32)
    conv = cbias_ref[...] + x * cw_ref[SSM_CONV - 1:SSM_CONV, :]
    head = jnp.zeros((pad, SSM_CONV_DIM), F32)
    for k in range(SSM_CONV - 1):
        s = SSM_CONV - 1 - k
        w_k = cw_ref[k:k + 1, :]
        conv = conv + jnp.dot(shift_ref[s - 1], xb, preferred_element_type=F32) * w_k
        head = head + tail_ref[pad - s:2 * pad - s, :] * w_k
    conv = jnp.concatenate([conv[0:pad] + head, conv[pad:]], axis=0)
    tail_ref[0:pad, :] = x[L - pad:L]
    xa = _silu(conv)
    xs = xa[:, :SSM_INNER]
    bm = xa[:, SSM_INNER:SSM_INNER + G * N]
    cm = xa[:, SSM_INNER + G * N:].astype(BF16)

    dtr = dt_ref[0] + dtb_ref[...]
    dt = jnp.maximum(dtr, 0.0) + jnp.log1p(jnp.exp(-jnp.abs(dtr)))
    a = dt * (-jnp.exp(alog_ref[...]))
    a_cat = jnp.concatenate(_split_bf16(a, 3), axis=0)
    acs = jnp.dot(tril_ref[...], a_cat, preferred_element_type=F32)
    acs_t = acs.T

    def expand(v):
        v_cat = jnp.concatenate(_split_bf16(v, 2), axis=1)
        return jnp.dot(v_cat, e_ref[...], preferred_element_type=F32)

    dt_x = expand(dt)
    acs_x = expand(acs)
    xc = xs * dt_x
    xc_b = xc.astype(BF16)
    last = acs_x[L - 1:L, :]
    ea_x = jnp.exp(acs_x)
    w_end = (jnp.exp(last - acs_x) * xc).astype(BF16)
    ea_last = jnp.exp(last)

    row = lax.broadcasted_iota(jnp.int32, (L, L), 0)
    col = lax.broadcasted_iota(jnp.int32, (L, L), 1)
    causal = row >= col
    lane = lax.broadcasted_iota(jnp.int32, (L, LANES), 1)
    low_half = lane < SSM_HEAD_DIM

    y_parts = []
    for g in range(G):
        bg = bm[:, g * N:(g + 1) * N]
        cg = cm[:, g * N:(g + 1) * N]
        cb = lax.dot_general(cg, bg.astype(BF16), nt, preferred_element_type=F32)
        st = state_ref[g]
        y_off = jnp.dot(cg, st.astype(BF16), preferred_element_type=F32) * ea_x[:, g * GW:(g + 1) * GW]
        for pr in range(GW // LANES):
            c0 = g * GW + pr * LANES
            x_pair = xc_b[:, c0:c0 + LANES]
            zero = jnp.zeros_like(x_pair)
            y_pair = y_off[:, pr * LANES:(pr + 1) * LANES]
            for half in range(2):
                hd = c0 // SSM_HEAD_DIM + half
                seg = acs[:, hd:hd + 1] - acs_t[hd:hd + 1, :]
                decay = jnp.exp(jnp.where(causal, seg, -jnp.inf))
                mmat = (cb * decay).astype(BF16)
                x_half = jnp.where(low_half, x_pair, zero) if half == 0 else jnp.where(low_half, zero, x_pair)
                y_pair = y_pair + jnp.dot(mmat, x_half, preferred_element_type=F32)
            y_parts.append(y_pair)
        bg_t = bg.T.astype(BF16)
        state_ref[g] = (st * ea_last[:, g * GW:(g + 1) * GW]
                        + jnp.dot(bg_t, w_end[:, g * GW:(g + 1) * GW], preferred_element_type=F32))

    y = jnp.concatenate(y_parts, axis=1) + dskip_ref[...] * xs
    v = y * _silu(z_ref[0].astype(F32))
    outs = []
    for g in range(G):
        vg = v[:, g * GW:(g + 1) * GW]
        ms = jnp.mean(vg * vg, axis=-1, keepdims=True)
        outs.append(vg * lax.rsqrt(ms + NORM_EPS))
    o_ref[0] = (jnp.concatenate(outs, axis=1) * ng_ref[...]).astype(o_ref.dtype)


def _ssd(proj3, dt3, conv_w, conv_b, dt_bias, a_log, d_skip_x, norm_g, tril3, e2, shifts):
    bsz, s_len, _ = proj3.shape
    L = SSM_CHUNK
    nc = s_len // L
    assert nc * L == s_len
    const = lambda b, c: (0, 0)
    return pl.pallas_call(
        _ssd_kernel,
        grid=(bsz, nc),
        in_specs=[
            pl.BlockSpec((1, L, SSM_CONV_DIM), lambda b, c: (b, c, COL_XBC // SSM_CONV_DIM)),
            pl.BlockSpec((1, L, SSM_INNER), lambda b, c: (b, c, COL_Z // SSM_INNER)),
            pl.BlockSpec((1, L, LANES), lambda b, c: (b, c, 0)),
            pl.BlockSpec((SSM_CONV, SSM_CONV_DIM), const),
            pl.BlockSpec((1, SSM_CONV_DIM), const),
            pl.BlockSpec((1, LANES), const),
            pl.BlockSpec((1, LANES), const),
            pl.BlockSpec((1, SSM_INNER), const),
            pl.BlockSpec((1, SSM_INNER), const),
            pl.BlockSpec((L, 3 * L), const),
            pl.BlockSpec((2 * LANES, SSM_INNER), const),
            pl.BlockSpec((SSM_CONV - 1, L, L), lambda b, c: (0, 0, 0)),
        ],
        out_specs=pl.BlockSpec((1, L, SSM_INNER), lambda b, c: (b, c, 0)),
        out_shape=jax.ShapeDtypeStruct((bsz, s_len, SSM_INNER), BF16),
        scratch_shapes=[
            pltpu.VMEM((2 * SUBLANES, SSM_CONV_DIM), F32),
            pltpu.VMEM((SSM_GROUPS, SSM_STATE, SSM_INNER // SSM_GROUPS), F32),
        ],
        compiler_params=_cparams(("arbitrary", "arbitrary")),
        name="ssd",
    )(proj3, proj3, dt3, conv_w, conv_b, dt_bias, a_log, d_skip_x, norm_g, tril3, e2, shifts)


def _memkv_kernel(mem_ref, g_ref, w_ref, kv_ref):
    x = mem_ref[...]
    ms = jnp.mean(x * x, axis=-1, keepdims=True)
    u = (x * lax.rsqrt(ms + NORM_EPS) * g_ref[...]).astype(BF16)
    kv_ref[...] = jnp.dot(u, w_ref[...], preferred_element_type=F32).astype(BF16)


def _mem_kv(mem2, g, w_kv):
    rows = mem2.shape[0]
    width = w_kv.shape[1]
    return pl.pallas_call(
        _memkv_kernel,
        grid=(1,),
        in_specs=[
            pl.BlockSpec((rows, D_MODEL), lambda i: (0, 0)),
            pl.BlockSpec((1, D_MODEL), lambda i: (0, 0)),
            pl.BlockSpec((D_MODEL, width), lambda i: (0, 0)),
        ],
        out_specs=pl.BlockSpec((rows, width), lambda i: (0, 0)),
        out_shape=jax.ShapeDtypeStruct((rows, width), BF16),
        compiler_params=_cparams(("arbitrary",)),
        name="mem_kv",
    )(mem2, g, w_kv)


def _merge_kernel(x_ref, oa_ref, os_ref, qm_ref, gl_ref, kv_ref, bg_ref, wa_ref, ws_ref, wm_ref, wo_ref, h_ref):
    nt = (((1,), (1,)), ((), ()))
    hd = MEM_HEAD_DIM
    width = MEM_HEADS * hd
    scale = hd ** -0.5
    qm = qm_ref[0]
    kv = kv_ref[0]
    outs = []
    for hh in range(MEM_HEADS):
        q = qm[:, hh * hd:(hh + 1) * hd]
        km = kv[:, hh * hd:(hh + 1) * hd]
        vm = kv[:, width + hh * hd:width + (hh + 1) * hd]
        s = lax.dot_general(q, km, nt, preferred_element_type=F32) * scale
        p = jnp.exp(s - jnp.max(s, axis=1, keepdims=True))
        o = jnp.dot(p.astype(BF16), vm, preferred_element_type=F32)
        outs.append(o / jnp.sum(p, axis=1, keepdims=True))
    o_mem = jnp.concatenate(outs, axis=1).astype(BF16)

    gates = _sigmoid(gl_ref[0].astype(F32) + bg_ref[...])
    merged = (gates[:, :D_MODEL] * jnp.dot(oa_ref[0], wa_ref[...], preferred_element_type=F32)
              + gates[:, D_MODEL:2 * D_MODEL] * jnp.dot(os_ref[0], ws_ref[...], preferred_element_type=F32)
              + gates[:, 2 * D_MODEL:] * jnp.dot(o_mem, wm_ref[...], preferred_element_type=F32))
    h_ref[0] = x_ref[0] + jnp.dot(merged.astype(BF16), wo_ref[...], preferred_element_type=F32)


def _resident(shape):
    return pl.BlockSpec(shape, lambda *_: (0,) * len(shape), pipeline_mode=pl.Buffered(1))


def _merge(x, o_attn, o_ssm, proj3, kv3, b_gate, wa, ws, wm, wo, tm):
    bsz, s_len, _ = x.shape
    mem_len = kv3.shape[1]
    return pl.pallas_call(
        _merge_kernel,
        grid=(bsz, s_len // tm),
        in_specs=[
            pl.BlockSpec((1, tm, D_MODEL), lambda b, i: (b, i, 0)),
            pl.BlockSpec((1, tm, D_MODEL), lambda b, i: (b, i, 0)),
            pl.BlockSpec((1, tm, SSM_INNER), lambda b, i: (b, i, 0)),
            pl.BlockSpec((1, tm, D_MODEL), lambda b, i: (b, i, COL_QM // D_MODEL)),
            pl.BlockSpec((1, tm, 3 * D_MODEL), lambda b, i: (b, i, COL_GATE // (3 * D_MODEL))),
            pl.BlockSpec((1, mem_len, 2 * D_MODEL), lambda b, i: (b, 0, 0)),
            _resident((1, 3 * D_MODEL)),
            _resident((D_MODEL, D_MODEL)),
            _resident((SSM_INNER, D_MODEL)),
            _resident((D_MODEL, D_MODEL)),
            _resident((D_MODEL, D_MODEL)),
        ],
        out_specs=pl.BlockSpec((1, tm, D_MODEL), lambda b, i: (b, i, 0)),
        out_shape=jax.ShapeDtypeStruct((bsz, s_len, D_MODEL), F32),
        compiler_params=_cparams(("arbitrary", "arbitrary")),
        name="merge",
    )(x, o_attn, o_ssm, proj3, proj3, kv3, b_gate, wa, ws, wm, wo)


def _ffn_kernel(h_ref, ng_ref, wup_ref, cw_ref, cb_ref, wdn_ref, fg_ref, o_ref, hid_ref, *, tm):
    pad = SUBLANES

    @pl.when(pl.program_id(1) == 0)
    def _():
        hid_ref[0:pad, :] = jnp.zeros((pad, 2 * FFN_HIDDEN), F32)

    h = h_ref[0]
    ms = jnp.mean(h * h, axis=-1, keepdims=True)
    u = (h * lax.rsqrt(ms + NORM_EPS) * ng_ref[...]).astype(BF16)
    hid_ref[pad:pad + tm, :] = jnp.dot(u, wup_ref[...], preferred_element_type=F32)
    conv = cb_ref[...]
    for k in range(FFN_CONV):
        off = pad - (FFN_CONV - 1) + k
        conv = conv + hid_ref[off:off + tm, :] * cw_ref[k:k + 1, :]
    hid_ref[0:pad, :] = hid_ref[tm:tm + pad, :]
    act = (_silu(conv[:, :FFN_HIDDEN]) * conv[:, FFN_HIDDEN:]).astype(BF16)
    y = h + jnp.dot(act, wdn_ref[...], preferred_element_type=F32)
    ms2 = jnp.mean(y * y, axis=-1, keepdims=True)
    o_ref[0] = y * lax.rsqrt(ms2 + NORM_EPS) * fg_ref[...]


def _ffn(h, norm_g, w_up, conv_w, conv_b, w_down, final_g, tm):
    bsz, s_len, _ = h.shape
    f2 = 2 * FFN_HIDDEN
    return pl.pallas_call(
        functools.partial(_ffn_kernel, tm=tm),
        grid=(bsz, s_len // tm),
        in_specs=[
            pl.BlockSpec((1, tm, D_MODEL), lambda b, i: (b, i, 0)),
            _resident((1, D_MODEL)),
            _resident((D_MODEL, f2)),
            _resident((FFN_CONV, f2)),
            _resident((1, f2)),
            _resident((FFN_HIDDEN, D_MODEL)),
            _resident((1, D_MODEL)),
        ],
        out_specs=pl.BlockSpec((1, tm, D_MODEL), lambda b, i: (b, i, 0)),
        out_shape=jax.ShapeDtypeStruct((bsz, s_len, D_MODEL), F32),
        scratch_shapes=[pltpu.VMEM((tm + 2 * SUBLANES, f2), F32)],
        compiler_params=_cparams(("arbitrary", "arbitrary")),
        name="ffn",
    )(h, norm_g, w_up, conv_w, conv_b, w_down, final_g)


def _ssd_constants():
    L = SSM_CHUNK
    tril = np.tril(np.ones((L, L), np.float32))
    tril3 = np.concatenate([tril, tril, tril], axis=1)
    e = np.zeros((LANES, SSM_INNER), np.float32)
    for hd in range(SSM_HEADS):
        e[hd, hd * SSM_HEAD_DIM:(hd + 1) * SSM_HEAD_DIM] = 1.0
    e2 = np.concatenate([e, e], axis=0)
    shifts = np.stack([np.eye(L, k=-s, dtype=np.float32) for s in range(1, SSM_CONV)])
    return jnp.asarray(tril3, BF16), jnp.asarray(e2, BF16), jnp.asarray(shifts, BF16)


def _pad_lanes(v):
    return jnp.pad(v, ((0, 0), (0, LANES - v.shape[1])))


def _layer(h, mem, rel_bias, mix_norm_g, w_in, b_gate, ssm_conv_w, ssm_conv_b, ssm_dt_bias, ssm_A_log, ssm_D,
           ssm_norm_g, mem_norm_g, w_mem_kv, w_br_attn, w_br_ssm, w_br_mem, w_out, ffn_norm_g, w_ffn_up,
           ffn_conv_w, ffn_conv_b, w_ffn_down, final_g):
    bsz, s_len, _ = h.shape
    w_cat = jnp.concatenate([w_in[:, a:b].astype(BF16) for a, b in
                             ((5120, 8192), (9248, 12320), (3072, 5120), (0, 3072), (8224, 9248))], axis=1)
    w_dt = _pad_lanes(w_in[:, 8192:8224]).astype(BF16)

    proj, dt_raw, kmean = _in_proj(h.reshape(bsz * s_len, D_MODEL), mix_norm_g[None, :], w_cat, w_dt,
                                   tm=IN_PROJ_TM, tn=IN_PROJ_TN)
    proj3 = proj.reshape(bsz, s_len, PROJ_WIDTH)
    dt3 = dt_raw.reshape(bsz, s_len, LANES)
    kmean3 = kmean.reshape(bsz, s_len // MOBA_BLOCK, ATTN_HEADS * ATTN_HEAD_DIM)

    o_attn = _moba(proj3, kmean3, rel_bias.T)

    tril3, e2, shifts = _ssd_constants()
    o_ssm = _ssd(proj3, dt3, ssm_conv_w, ssm_conv_b[None, :], _pad_lanes(ssm_dt_bias[None, :]),
                 _pad_lanes(ssm_A_log[None, :]), jnp.repeat(ssm_D, SSM_HEAD_DIM)[None, :], ssm_norm_g[None, :],
                 tril3, e2, shifts)

    mem_len = mem.shape[1]
    kv = _mem_kv(mem.reshape(bsz * mem_len, D_MODEL), mem_norm_g[None, :], w_mem_kv.astype(BF16))
    kv3 = kv.reshape(bsz, mem_len, 2 * D_MODEL)

    h1 = _merge(h, o_attn, o_ssm, proj3, kv3, b_gate[None, :], w_br_attn.astype(BF16), w_br_ssm.astype(BF16),
                w_br_mem.astype(BF16), w_out.astype(BF16), tm=MERGE_TM)

    return _ffn(h1, ffn_norm_g[None, :], w_ffn_up.astype(BF16), ffn_conv_w, ffn_conv_b[None, :],
                w_ffn_down.astype(BF16), final_g[None, :], tm=FFN_TM)


def kernel(x, mem, rel_bias, mix_norm_g, w_in, b_gate, ssm_conv_w, ssm_conv_b, ssm_dt_bias, ssm_A_log, ssm_D,
           ssm_norm_g, mem_norm_g, w_mem_kv, w_br_attn, w_br_ssm, w_br_mem, w_out, ffn_norm_g, w_ffn_up,
           ffn_conv_w, ffn_conv_b, w_ffn_down, final_norm_g):
    assert w_in.shape[0] == 1, "single-layer trunk"
    return _layer(x, mem, rel_bias, mix_norm_g[0], w_in[0], b_gate[0], ssm_conv_w[0], ssm_conv_b[0],
                  ssm_dt_bias[0], ssm_A_log[0], ssm_D[0], ssm_norm_g[0], mem_norm_g[0], w_mem_kv[0],
                  w_br_attn[0], w_br_ssm[0], w_br_mem[0], w_out[0], ffn_norm_g[0], w_ffn_up[0],
                  ffn_conv_w[0], ffn_conv_b[0], w_ffn_down[0], final_norm_g)
```

```python
import functools
import math

import numpy as np
import jax
import jax.numpy as jnp
from jax import lax
from jax.experimental import pallas as pl
from jax.experimental.pallas import tpu as pltpu

F32 = jnp.float32
BF16 = jnp.bfloat16

D_MODEL = 1024
ATTN_HEADS = 8
ATTN_HEAD_DIM = 128
MOBA_BLOCK = 256
MOBA_TOPK = 3
REL_BUCKETS = 32
REL_MAX_DIST = 1024
SSM_INNER = 2048
SSM_HEAD_DIM = 64
SSM_HEADS = 32
SSM_GROUPS = 4
SSM_STATE = 128
SSM_CONV = 4
SSM_CHUNK = 256
SSM_CONV_DIM = 3072
MEM_HEADS = 4
MEM_HEAD_DIM = 256
FFN_HIDDEN = 2816
FFN_CONV = 3
NORM_EPS = 1e-6

LANES = 128
SUBLANES = 8
VMEM_LIMIT = 56 * 1024 * 1024

IN_PROJ_TM = 2048
IN_PROJ_TN = 1024
MERGE_TM = 512
FFN_TM = 512

COL_XBC = 0
COL_GATE = 3072
COL_Z = 6144
COL_Q = 8192
COL_K = 9216
COL_V = 10240
COL_QM = 11264
PROJ_WIDTH = 12288

MOBA_NEAR = -(-(REL_MAX_DIST + MOBA_BLOCK - 1) // MOBA_BLOCK)
NEG_BIG = -1e30
LOG2E = math.log2(math.e)
MOBA_QBLOCKS = 2
MOBA_GROUP = 4
MOBA_VT_ROWS = ATTN_HEAD_DIM + 16
assert MOBA_GROUP <= MOBA_NEAR + 1


def _cparams(sem):
    return pltpu.CompilerParams(dimension_semantics=sem, vmem_limit_bytes=VMEM_LIMIT)


def _sigmoid(x):
    return 1.0 / (1.0 + jnp.exp(-x))


def _silu(x):
    h = 0.5 * x
    return h + h * jnp.tanh(h)


def _split_bf16(x, parts):
    out = []
    r = x
    for _ in range(parts):
        hi = r.astype(BF16)
        out.append(hi)
        r = r - hi.astype(F32)
    return out


def _inproj_kernel(x_ref, g_ref, w_ref, wdt_ref, proj_ref, dt_ref, kmean_ref, u_ref, *, k_tile):
    j = pl.program_id(1)

    @pl.when(j == 0)
    def _():
        x = x_ref[...]
        ms = jnp.mean(x * x, axis=-1, keepdims=True)
        u = (x * lax.rsqrt(ms + NORM_EPS) * g_ref[...]).astype(BF16)
        u_ref[...] = u
        dt_ref[...] = jnp.dot(u, wdt_ref[...], preferred_element_type=F32)

    acc = jnp.dot(u_ref[...], w_ref[...], preferred_element_type=F32)
    proj_ref[...] = acc.astype(BF16)

    @pl.when(j == k_tile)
    def _():
        blk = MOBA_BLOCK
        for r in range(acc.shape[0] // blk):
            kmean_ref[0, r:r + 1, :] = jnp.sum(acc[r * blk:(r + 1) * blk], axis=0, keepdims=True) * (1.0 / blk)


def _in_proj(x2, g, w_cat, w_dt, tm, tn):
    t = x2.shape[0]
    kw = ATTN_HEADS * ATTN_HEAD_DIM
    assert tn == kw and COL_K % tn == 0 and tm % MOBA_BLOCK == 0
    return pl.pallas_call(
        functools.partial(_inproj_kernel, k_tile=COL_K // tn),
        grid=(t // tm, PROJ_WIDTH // tn),
        in_specs=[
            pl.BlockSpec((tm, D_MODEL), lambda i, j: (i, 0)),
            pl.BlockSpec((1, D_MODEL), lambda i, j: (0, 0)),
            pl.BlockSpec((D_MODEL, tn), lambda i, j: (0, j)),
            pl.BlockSpec((D_MODEL, LANES), lambda i, j: (0, 0)),
        ],
        out_specs=[
            pl.BlockSpec((tm, tn), lambda i, j: (i, j)),
            pl.BlockSpec((tm, LANES), lambda i, j: (i, 0)),
            pl.BlockSpec((1, tm // MOBA_BLOCK, kw), lambda i, j: (i, 0, 0)),
        ],
        out_shape=[
            jax.ShapeDtypeStruct((t, PROJ_WIDTH), BF16),
            jax.ShapeDtypeStruct((t, LANES), F32),
            jax.ShapeDtypeStruct((t // tm, tm // MOBA_BLOCK, kw), F32),
        ],
        scratch_shapes=[pltpu.VMEM((tm, D_MODEL), BF16)],
        compiler_params=_cparams(("arbitrary", "arbitrary")),
        name="in_proj",
    )(x2, g, w_cat, w_dt)


def _t5_bucket(dist):
    n = jnp.maximum(dist, 0)
    max_exact = REL_BUCKETS // 2
    nf = jnp.maximum(n, max_exact).astype(F32)
    large = max_exact + (jnp.log(nf * (1.0 / max_exact)) / math.log(REL_MAX_DIST / max_exact)
                         * (REL_BUCKETS - max_exact)).astype(jnp.int32)
    large = jnp.minimum(large, REL_BUCKETS - 1)
    return jnp.where(n < max_exact, n, large)


def _moba_kernel(rel_ref, q_ref, k_ref, v_ref, km_ref, o_ref, kmean_ref, bias_ref, vt_ref, neg_ref, s_ref, m_ref,
                 acc_ref, *, nb):
    h = pl.program_id(0)
    b = pl.program_id(1)
    i = pl.program_id(2)
    blk = MOBA_BLOCK
    scale = ATTN_HEAD_DIM ** -0.5


    @pl.when((b == 0) & (i == 0))
    def _():
        key = lax.broadcasted_iota(jnp.int32, (blk, blk), 0)
        qry = lax.broadcasted_iota(jnp.int32, (blk, blk), 1)
        for d in range(MOBA_NEAR):
            bucket = _t5_bucket(d * blk + qry - key)
            tile = jnp.zeros((blk, blk), F32)
            for bk in range(REL_BUCKETS):
                tile = jnp.where(bucket == bk, rel_ref[h, bk], tile)
            bias_ref[d] = tile * LOG2E

    @pl.when(i == 0)
    def _():
        kmean_ref[...] = jnp.zeros_like(kmean_ref)
        kmean_ref[0:nb, :] = km_ref[0]
        ones_rows = (lax.broadcasted_iota(jnp.int32, (MOBA_VT_ROWS - ATTN_HEAD_DIM, blk), 0) == 0).astype(BF16)
        for jb in range(nb):
            vt_ref[jb, 0:ATTN_HEAD_DIM, :] = v_ref[0, jb * blk:(jb + 1) * blk, :].astype(F32).T.astype(BF16)
            vt_ref[jb, ATTN_HEAD_DIM:MOBA_VT_ROWS, :] = ones_rows

    qw = MOBA_QBLOCKS * blk
    i0 = i * MOBA_QBLOCKS
    q_t = q_ref[0].astype(F32).T
    qs_t = (q_t * (scale * LOG2E)).astype(BF16)

    gate = jnp.dot(kmean_ref[...].astype(BF16), q_t.astype(BF16), preferred_element_type=F32)
    nbp = kmean_ref.shape[0]
    row = lax.broadcasted_iota(jnp.int32, (nbp, qw), 0).astype(F32)
    own = (i0 + lax.broadcasted_iota(jnp.int32, (1, qw), 1) // blk).astype(F32)
    g = jnp.where(row < own, gate, -jnp.inf)
    sel = jnp.zeros((nbp, qw), F32)
    for t in range(MOBA_TOPK):
        mx = jnp.max(g, axis=0, keepdims=True)
        idx = jnp.min(jnp.where(g == mx, row, float(nbp)), axis=0, keepdims=True)
        hit = row == idx
        sel = jnp.maximum(sel, jnp.where(hit & (own > t), 1.0, 0.0))
        g = jnp.where(hit, -jnp.inf, g)
    neg_ref[...] = jnp.where(sel > 0.5, 0.0, NEG_BIG)

    def far_scores(j0):
        start = pl.multiple_of(j0 * blk, blk)
        return jnp.dot(k_ref[0, pl.ds(start, MOBA_GROUP * blk), :], qs_t, preferred_element_type=F32)

    s_ref[0] = far_scores(0)

    def softmax_group(scores, offsets, v_idx, m_old):
        m_new = m_old
        for sj, off in zip(scores, offsets):
            m_new = jnp.maximum(m_new, jnp.max(sj, axis=0, keepdims=True) + off)
        pv = jnp.zeros((MOBA_VT_ROWS, m_old.shape[1]), F32)
        for sj, off, vj in zip(scores, offsets, v_idx):
            pj = jnp.exp2(sj + (off - m_new))
            pv = pv + jnp.dot(vt_ref[vj], pj.astype(BF16), preferred_element_type=F32)
        return m_new, pv

    far_bias = rel_ref[h, REL_BUCKETS - 1] * LOG2E

    key = lax.broadcasted_iota(jnp.int32, (blk, blk), 0)
    qry = lax.broadcasted_iota(jnp.int32, (blk, blk), 1)
    scores, offsets, v_idx, first_lane = [], [], [], []
    for e in range(MOBA_NEAR - 1 + MOBA_QBLOCKS):
        jn = i0 - (MOBA_NEAR - 1) + e
        jc = jnp.maximum(jn, 0)
        start = pl.multiple_of(jc * blk, blk)
        w_min = max(0, e - (MOBA_NEAR - 1))
        lo = w_min * blk
        se = jnp.dot(k_ref[0, pl.ds(start, blk), :], qs_t[:, lo:], preferred_element_type=F32)
        sel_row = neg_ref[pl.ds(jnp.where(jn >= 0, jn, nbp - 1), 1), :]
        s_parts, o_parts = [], []
        for w in range(w_min, MOBA_QBLOCKS):
            d = w + (MOBA_NEAR - 1) - e
            sw = se[:, (w - w_min) * blk:(w - w_min + 1) * blk]
            ow = sel_row[:, w * blk:(w + 1) * blk]
            if d == 0:
                sw = jnp.where(qry >= key, sw + bias_ref[0], NEG_BIG)
                ow = jnp.zeros((1, blk), F32)
            elif d < MOBA_NEAR:
                sw = sw + bias_ref[d]
            else:
                ow = ow + far_bias
            s_parts.append(sw)
            o_parts.append(ow)
        scores.append(jnp.concatenate(s_parts, axis=1))
        offsets.append(jnp.concatenate(o_parts, axis=1))
        v_idx.append(jc)
        first_lane.append(lo)
    m_floor = jnp.concatenate(
        [jnp.max(scores[MOBA_NEAR - 1 + w][:, 0:blk], axis=0, keepdims=True) for w in range(MOBA_QBLOCKS)], axis=1)
    parts = []
    for sj, off, vj, lo in zip(scores, offsets, v_idx, first_lane):
        mu, pvu = softmax_group([sj], [off], [vj], m_floor[:, lo:])
        if lo:
            mu = jnp.concatenate([m_floor[:, :lo], mu], axis=1)
            pvu = jnp.concatenate([jnp.zeros((MOBA_VT_ROWS, lo), F32), pvu], axis=1)
        parts.append((mu, pvu))
    m0 = parts[0][0]
    for mu, _ in parts[1:]:
        m0 = jnp.maximum(m0, mu)
    pv0 = jnp.zeros((MOBA_VT_ROWS, qw), F32)
    for mu, pvu in parts:
        pv0 = pv0 + jnp.exp2(mu - m0) * pvu
    m_ref[...] = m0
    acc_ref[...] = pv0

    n_far = jnp.maximum(i0 - (MOBA_NEAR - 1), 0)

    n_groups = (n_far + MOBA_GROUP - 1) // MOBA_GROUP

    def far_group(gi, slot, prefetch):
        j0 = gi * MOBA_GROUP
        if prefetch:
            s_ref[1 - slot] = far_scores(j0 + MOBA_GROUP)
        scores, offsets, v_idx = [], [], []
        for u in range(MOBA_GROUP):
            ju = j0 + u
            scores.append(s_ref[slot, u * blk:(u + 1) * blk, :])
            offsets.append(neg_ref[pl.ds(jnp.where(ju < n_far, ju, nbp - 1), 1), :] + far_bias)
            v_idx.append(ju)
        m_old = m_ref[...]
        m_new, pv = softmax_group(scores, offsets, v_idx, m_old)
        acc_ref[...] = jnp.exp2(m_old - m_new) * acc_ref[...] + pv
        m_ref[...] = m_new

    def far_body(t, carry):
        far_group(2 * t, 0, True)
        far_group(2 * t + 1, 1, True)
        return carry

    n_pairs = jnp.maximum(n_groups - 1, 0) // 2
    lax.fori_loop(0, n_pairs, far_body, 0)
    g_rest = 2 * n_pairs
    n_rest = n_groups - g_rest

    @pl.when(n_rest == 1)
    def _():
        far_group(g_rest, 0, False)

    @pl.when(n_rest == 2)
    def _():
        far_group(g_rest, 0, True)
        far_group(g_rest + 1, 1, False)

    out_t = acc_ref[0:ATTN_HEAD_DIM, :] / acc_ref[ATTN_HEAD_DIM:ATTN_HEAD_DIM + 1, :]
    o_ref[0] = out_t.T.astype(o_ref.dtype)


def _moba(proj3, kmean3, rel_t):
    bsz, s_len, _ = proj3.shape
    blk = MOBA_BLOCK
    nb = s_len // blk
    qw = MOBA_QBLOCKS * blk
    assert nb * blk == s_len and nb % MOBA_QBLOCKS == 0 and nb >= MOBA_GROUP
    nbp = -(-(nb + 1) // 16) * 16
    cb = LANES
    return pl.pallas_call(
        functools.partial(_moba_kernel, nb=nb),
        grid=(ATTN_HEADS, bsz, nb // MOBA_QBLOCKS),
        in_specs=[
            pl.BlockSpec(memory_space=pltpu.SMEM),
            pl.BlockSpec((1, qw, cb), lambda h, b, i: (b, i, COL_Q // cb + h)),
            pl.BlockSpec((1, s_len, cb), lambda h, b, i: (b, 0, COL_K // cb + h)),
            pl.BlockSpec((1, s_len, cb), lambda h, b, i: (b, 0, COL_V // cb + h)),
            pl.BlockSpec((1, nb, cb), lambda h, b, i: (b, 0, h)),
        ],
        out_specs=pl.BlockSpec((1, qw, cb), lambda h, b, i: (b, i, h)),
        out_shape=jax.ShapeDtypeStruct((bsz, s_len, ATTN_HEADS * ATTN_HEAD_DIM), BF16),
        scratch_shapes=[
            pltpu.VMEM((nbp, ATTN_HEAD_DIM), F32),
            pltpu.VMEM((MOBA_NEAR, blk, blk), F32),
            pltpu.VMEM((nb, MOBA_VT_ROWS, blk), BF16),
            pltpu.VMEM((nbp, qw), F32),
            pltpu.VMEM((2, MOBA_GROUP * blk, qw), F32),
            pltpu.VMEM((1, qw), F32),
            pltpu.VMEM((MOBA_VT_ROWS, qw), F32),
        ],
        compiler_params=_cparams(("arbitrary", "arbitrary", "arbitrary")),
        name="moba",
    )(rel_t, proj3, proj3, proj3, kmean3)


def _ssd_kernel(xbc_ref, z_ref, dt_ref, cw_ref, cbias_ref, dtb_ref, alog_ref, dskip_ref, ng_ref,
                tril_ref, e_ref, shift_ref, o_ref, tail_ref, state_ref):
    L = SSM_CHUNK
    G = SSM_GROUPS
    N = SSM_STATE
    GW = SSM_INNER // G
    pad = SUBLANES
    nt = (((1,), (1,)), ((), ()))

    @pl.when(pl.program_id(1) == 0)
    def _():
        tail_ref[...] = jnp.zeros_like(tail_ref)
        state_ref[...] = jnp.zeros_like(state_ref)

    xb = xbc_ref[0]
    x = xb.astype(F32)
    conv = cbias_ref[...] + x * cw_ref[SSM_CONV - 1:SSM_CONV, :]
    head = jnp.zeros((pad, SSM_CONV_DIM), F32)
    for k in range(SSM_CONV - 1):
        s = SSM_CONV - 1 - k
        w_k = cw_ref[k:k + 1, :]
        conv = conv + jnp.dot(shift_ref[s - 1], xb, preferred_element_type=F32) * w_k
        head = head + tail_ref[pad - s:2 * pad - s, :] * w_k
    conv = jnp.concatenate([conv[0:pad] + head, conv[pad:]], axis=0)
    tail_ref[0:pad, :] = x[L - pad:L]
    xa = _silu(conv)
    xs = xa[:, :SSM_INNER]
    bm = xa[:, SSM_INNER:SSM_INNER + G * N]
    cm = xa[:, SSM_INNER + G * N:].astype(BF16)

    dtr = dt_ref[0] + dtb_ref[...]
    dt = jnp.maximum(dtr, 0.0) + jnp.log1p(jnp.exp(-jnp.abs(dtr)))
    a = dt * (-jnp.exp(alog_ref[...]))
    a_cat = jnp.concatenate(_split_bf16(a, 3), axis=0)
    acs = jnp.dot(tril_ref[...], a_cat, preferred_element_type=F32) * LOG2E
    acs_t = acs.T

    def expand(v):
        v_cat = jnp.concatenate(_split_bf16(v, 2), axis=1)
        return jnp.dot(v_cat, e_ref[...], preferred_element_type=F32)

    dt_x = expand(dt)
    acs_x = expand(acs)
    xc = xs * dt_x
    xc_b = xc.astype(BF16)
    last = acs_x[L - 1:L, :]
    ea_x = jnp.exp2(acs_x)
    w_end = (jnp.exp2(last - acs_x) * xc).astype(BF16)
    ea_last = jnp.exp2(last)

    row = lax.broadcasted_iota(jnp.int32, (L, L), 0)
    col = lax.broadcasted_iota(jnp.int32, (L, L), 1)
    causal = row >= col
    lane = lax.broadcasted_iota(jnp.int32, (L, LANES), 1)
    low_half = lane < SSM_HEAD_DIM

    y_parts = []
    for g in range(G):
        bg = bm[:, g * N:(g + 1) * N]
        cg = cm[:, g * N:(g + 1) * N]
        cb = lax.dot_general(cg, bg.astype(BF16), nt, preferred_element_type=F32)
        st = state_ref[g]
        y_off = jnp.dot(cg, st.astype(BF16), preferred_element_type=F32) * ea_x[:, g * GW:(g + 1) * GW]
        for pr in range(GW // LANES):
            c0 = g * GW + pr * LANES
            x_pair = xc_b[:, c0:c0 + LANES]
            zero = jnp.zeros_like(x_pair)
            y_pair = y_off[:, pr * LANES:(pr + 1) * LANES]
            for half in range(2):
                hd = c0 // SSM_HEAD_DIM + half
                seg = acs[:, hd:hd + 1] - acs_t[hd:hd + 1, :]
                decay = jnp.exp2(jnp.where(causal, seg, -jnp.inf))
                mmat = (cb * decay).astype(BF16)
                x_half = jnp.where(low_half, x_pair, zero) if half == 0 else jnp.where(low_half, zero, x_pair)
                y_pair = y_pair + jnp.dot(mmat, x_half, preferred_element_type=F32)
            y_parts.append(y_pair)
        bg_t = bg.T.astype(BF16)
        state_ref[g] = (st * ea_last[:, g * GW:(g + 1) * GW]
                        + jnp.dot(bg_t, w_end[:, g * GW:(g + 1) * GW], preferred_element_type=F32))

    y = jnp.concatenate(y_parts, axis=1) + dskip_ref[...] * xs
    v = y * _silu(z_ref[0].astype(F32))
    outs = []
    for g in range(G):
        vg = v[:, g * GW:(g + 1) * GW]
        ms = jnp.mean(vg * vg, axis=-1, keepdims=True)
        outs.append(vg * lax.rsqrt(ms + NORM_EPS))
    o_ref[0] = (jnp.concatenate(outs, axis=1) * ng_ref[...]).astype(o_ref.dtype)


def _ssd(proj3, dt3, conv_w, conv_b, dt_bias, a_log, d_skip_x, norm_g, tril3, e2, shifts):
    bsz, s_len, _ = proj3.shape
    L = SSM_CHUNK
    nc = s_len // L
    assert nc * L == s_len
    const = lambda b, c: (0, 0)
    return pl.pallas_call(
        _ssd_kernel,
        grid=(bsz, nc),
        in_specs=[
            pl.BlockSpec((1, L, SSM_CONV_DIM), lambda b, c: (b, c, COL_XBC // SSM_CONV_DIM)),
            pl.BlockSpec((1, L, SSM_INNER), lambda b, c: (b, c, COL_Z // SSM_INNER)),
            pl.BlockSpec((1, L, LANES), lambda b, c: (b, c, 0)),
            pl.BlockSpec((SSM_CONV, SSM_CONV_DIM), const),
            pl.BlockSpec((1, SSM_CONV_DIM), const),
            pl.BlockSpec((1, LANES), const),
            pl.BlockSpec((1, LANES), const),
            pl.BlockSpec((1, SSM_INNER), const),
            pl.BlockSpec((1, SSM_INNER), const),
            pl.BlockSpec((L, 3 * L), const),
            pl.BlockSpec((2 * LANES, SSM_INNER), const),
            pl.BlockSpec((SSM_CONV - 1, L, L), lambda b, c: (0, 0, 0)),
        ],
        out_specs=pl.BlockSpec((1, L, SSM_INNER), lambda b, c: (b, c, 0)),
        out_shape=jax.ShapeDtypeStruct((bsz, s_len, SSM_INNER), BF16),
        scratch_shapes=[
            pltpu.VMEM((2 * SUBLANES, SSM_CONV_DIM), F32),
            pltpu.VMEM((SSM_GROUPS, SSM_STATE, SSM_INNER // SSM_GROUPS), F32),
        ],
        compiler_params=_cparams(("arbitrary", "arbitrary")),
        name="ssd",
    )(proj3, proj3, dt3, conv_w, conv_b, dt_bias, a_log, d_skip_x, norm_g, tril3, e2, shifts)


def _memkv_kernel(mem_ref, g_ref, w_ref, kv_ref):
    x = mem_ref[...]
    ms = jnp.mean(x * x, axis=-1, keepdims=True)
    u = (x * lax.rsqrt(ms + NORM_EPS) * g_ref[...]).astype(BF16)
    kv_ref[...] = jnp.dot(u, w_ref[...], preferred_element_type=F32).astype(BF16)


def _mem_kv(mem2, g, w_kv):
    rows = mem2.shape[0]
    width = w_kv.shape[1]
    return pl.pallas_call(
        _memkv_kernel,
        grid=(1,),
        in_specs=[
            pl.BlockSpec((rows, D_MODEL), lambda i: (0, 0)),
            pl.BlockSpec((1, D_MODEL), lambda i: (0, 0)),
            pl.BlockSpec((D_MODEL, width), lambda i: (0, 0)),
        ],
        out_specs=pl.BlockSpec((rows, width), lambda i: (0, 0)),
        out_shape=jax.ShapeDtypeStruct((rows, width), BF16),
        compiler_params=_cparams(("arbitrary",)),
        name="mem_kv",
    )(mem2, g, w_kv)


def _merge_kernel(x_ref, oa_ref, os_ref, qm_ref, gl_ref, kv_ref, bg_ref, wa_ref, ws_ref, wm_ref, wo_ref, h_ref):
    nt = (((1,), (1,)), ((), ()))
    hd = MEM_HEAD_DIM
    width = MEM_HEADS * hd
    scale = hd ** -0.5
    qm = qm_ref[0]
    kv = kv_ref[0]
    outs = []
    for hh in range(MEM_HEADS):
        q = qm[:, hh * hd:(hh + 1) * hd]
        km = kv[:, hh * hd:(hh + 1) * hd]
        vm = kv[:, width + hh * hd:width + (hh + 1) * hd]
        s = lax.dot_general(q, km, nt, preferred_element_type=F32) * scale
        p = jnp.exp(s - jnp.max(s, axis=1, keepdims=True))
        o = jnp.dot(p.astype(BF16), vm, preferred_element_type=F32)
        outs.append(o / jnp.sum(p, axis=1, keepdims=True))
    o_mem = jnp.concatenate(outs, axis=1).astype(BF16)

    gates = _sigmoid(gl_ref[0].astype(F32) + bg_ref[...])
    merged = (gates[:, :D_MODEL] * jnp.dot(oa_ref[0], wa_ref[...], preferred_element_type=F32)
              + gates[:, D_MODEL:2 * D_MODEL] * jnp.dot(os_ref[0], ws_ref[...], preferred_element_type=F32)
              + gates[:, 2 * D_MODEL:] * jnp.dot(o_mem, wm_ref[...], preferred_element_type=F32))
    h_ref[0] = x_ref[0] + jnp.dot(merged.astype(BF16), wo_ref[...], preferred_element_type=F32)


def _resident(shape):
    return pl.BlockSpec(shape, lambda *_: (0,) * len(shape), pipeline_mode=pl.Buffered(1))


def _merge(x, o_attn, o_ssm, proj3, kv3, b_gate, wa, ws, wm, wo, tm):
    bsz, s_len, _ = x.shape
    mem_len = kv3.shape[1]
    return pl.pallas_call(
        _merge_kernel,
        grid=(bsz, s_len // tm),
        in_specs=[
            pl.BlockSpec((1, tm, D_MODEL), lambda b, i: (b, i, 0)),
            pl.BlockSpec((1, tm, D_MODEL), lambda b, i: (b, i, 0)),
            pl.BlockSpec((1, tm, SSM_INNER), lambda b, i: (b, i, 0)),
            pl.BlockSpec((1, tm, D_MODEL), lambda b, i: (b, i, COL_QM // D_MODEL)),
            pl.BlockSpec((1, tm, 3 * D_MODEL), lambda b, i: (b, i, COL_GATE // (3 * D_MODEL))),
            pl.BlockSpec((1, mem_len, 2 * D_MODEL), lambda b, i: (b, 0, 0)),
            _resident((1, 3 * D_MODEL)),
            _resident((D_MODEL, D_MODEL)),
            _resident((SSM_INNER, D_MODEL)),
            _resident((D_MODEL, D_MODEL)),
            _resident((D_MODEL, D_MODEL)),
        ],
        out_specs=pl.BlockSpec((1, tm, D_MODEL), lambda b, i: (b, i, 0)),
        out_shape=jax.ShapeDtypeStruct((bsz, s_len, D_MODEL), F32),
        compiler_params=_cparams(("arbitrary", "arbitrary")),
        name="merge",
    )(x, o_attn, o_ssm, proj3, proj3, kv3, b_gate, wa, ws, wm, wo)


def _ffn_kernel(h_ref, ng_ref, wup_ref, cw_ref, cb_ref, wdn_ref, fg_ref, o_ref, hid_ref, *, tm):
    pad = SUBLANES

    @pl.when(pl.program_id(1) == 0)
    def _():
        hid_ref[0:pad, :] = jnp.zeros((pad, 2 * FFN_HIDDEN), F32)

    h = h_ref[0]
    ms = jnp.mean(h * h, axis=-1, keepdims=True)
    u = (h * lax.rsqrt(ms + NORM_EPS) * ng_ref[...]).astype(BF16)
    hid_ref[pad:pad + tm, :] = jnp.dot(u, wup_ref[...], preferred_element_type=F32)
    conv = cb_ref[...]
    for k in range(FFN_CONV):
        off = pad - (FFN_CONV - 1) + k
        conv = conv + hid_ref[off:off + tm, :] * cw_ref[k:k + 1, :]
    hid_ref[0:pad, :] = hid_ref[tm:tm + pad, :]
    act = (_silu(conv[:, :FFN_HIDDEN]) * conv[:, FFN_HIDDEN:]).astype(BF16)
    y = h + jnp.dot(act, wdn_ref[...], preferred_element_type=F32)
    ms2 = jnp.mean(y * y, axis=-1, keepdims=True)
    o_ref[0] = y * lax.rsqrt(ms2 + NORM_EPS) * fg_ref[...]


def _ffn(h, norm_g, w_up, conv_w, conv_b, w_down, final_g, tm):
    bsz, s_len, _ = h.shape
    f2 = 2 * FFN_HIDDEN
    return pl.pallas_call(
        functools.partial(_ffn_kernel, tm=tm),
        grid=(bsz, s_len // tm),
        in_specs=[
            pl.BlockSpec((1, tm, D_MODEL), lambda b, i: (b, i, 0)),
            _resident((1, D_MODEL)),
            _resident((D_MODEL, f2)),
            _resident((FFN_CONV, f2)),
            _resident((1, f2)),
            _resident((FFN_HIDDEN, D_MODEL)),
            _resident((1, D_MODEL)),
        ],
        out_specs=pl.BlockSpec((1, tm, D_MODEL), lambda b, i: (b, i, 0)),
        out_shape=jax.ShapeDtypeStruct((bsz, s_len, D_MODEL), F32),
        scratch_shapes=[pltpu.VMEM((tm + 2 * SUBLANES, f2), F32)],
        compiler_params=_cparams(("arbitrary", "arbitrary")),
        name="ffn",
    )(h, norm_g, w_up, conv_w, conv_b, w_down, final_g)


def _ssd_constants():
    L = SSM_CHUNK
    tril = np.tril(np.ones((L, L), np.float32))
    tril3 = np.concatenate([tril, tril, tril], axis=1)
    e = np.zeros((LANES, SSM_INNER), np.float32)
    for hd in range(SSM_HEADS):
        e[hd, hd * SSM_HEAD_DIM:(hd + 1) * SSM_HEAD_DIM] = 1.0
    e2 = np.concatenate([e, e], axis=0)
    shifts = np.stack([np.eye(L, k=-s, dtype=np.float32) for s in range(1, SSM_CONV)])
    return jnp.asarray(tril3, BF16), jnp.asarray(e2, BF16), jnp.asarray(shifts, BF16)


def _pad_lanes(v):
    return jnp.pad(v, ((0, 0), (0, LANES - v.shape[1])))


def _layer(h, mem, rel_bias, mix_norm_g, w_in, b_gate, ssm_conv_w, ssm_conv_b, ssm_dt_bias, ssm_A_log, ssm_D,
           ssm_norm_g, mem_norm_g, w_mem_kv, w_br_attn, w_br_ssm, w_br_mem, w_out, ffn_norm_g, w_ffn_up,
           ffn_conv_w, ffn_conv_b, w_ffn_down, final_g):
    bsz, s_len, _ = h.shape
    w_cat = jnp.concatenate([w_in[:, a:b].astype(BF16) for a, b in
                             ((5120, 8192), (9248, 12320), (3072, 5120), (0, 3072), (8224, 9248))], axis=1)
    w_dt = _pad_lanes(w_in[:, 8192:8224]).astype(BF16)

    proj, dt_raw, kmean = _in_proj(h.reshape(bsz * s_len, D_MODEL), mix_norm_g[None, :], w_cat, w_dt,
                                   tm=IN_PROJ_TM, tn=IN_PROJ_TN)
    proj3 = proj.reshape(bsz, s_len, PROJ_WIDTH)
    dt3 = dt_raw.reshape(bsz, s_len, LANES)
    kmean3 = kmean.reshape(bsz, s_len // MOBA_BLOCK, ATTN_HEADS * ATTN_HEAD_DIM)

    o_attn = _moba(proj3, kmean3, rel_bias.T)

    tril3, e2, shifts = _ssd_constants()
    o_ssm = _ssd(proj3, dt3, ssm_conv_w, ssm_conv_b[None, :], _pad_lanes(ssm_dt_bias[None, :]),
                 _pad_lanes(ssm_A_log[None, :]), jnp.repeat(ssm_D, SSM_HEAD_DIM)[None, :], ssm_norm_g[None, :],
                 tril3, e2, shifts)

    mem_len = mem.shape[1]
    kv = _mem_kv(mem.reshape(bsz * mem_len, D_MODEL), mem_norm_g[None, :], w_mem_kv.astype(BF16))
    kv3 = kv.reshape(bsz, mem_len, 2 * D_MODEL)

    h1 = _merge(h, o_attn, o_ssm, proj3, kv3, b_gate[None, :], w_br_attn.astype(BF16), w_br_ssm.astype(BF16),
                w_br_mem.astype(BF16), w_out.astype(BF16), tm=MERGE_TM)

    return _ffn(h1, ffn_norm_g[None, :], w_ffn_up.astype(BF16), ffn_conv_w, ffn_conv_b[None, :],
                w_ffn_down.astype(BF16), final_g[None, :], tm=FFN_TM)


def kernel(x, mem, rel_bias, mix_norm_g, w_in, b_gate, ssm_conv_w, ssm_conv_b, ssm_dt_bias, ssm_A_log, ssm_D,
           ssm_norm_g, mem_norm_g, w_mem_kv, w_br_attn, w_br_ssm, w_br_mem, w_out, ffn_norm_g, w_ffn_up,
           ffn_conv_w, ffn_conv_b, w_ffn_down, final_norm_g):
    assert w_in.shape[0] == 1, "single-layer trunk"
    return _layer(x, mem, rel_bias, mix_norm_g[0], w_in[0], b_gate[0], ssm_conv_w[0], ssm_conv_b[0],
                  ssm_dt_bias[0], ssm_A_log[0], ssm_D[0], ssm_norm_g[0], mem_norm_g[0], w_mem_kv[0],
                  w_br_attn[0], w_br_ssm[0], w_br_mem[0], w_out[0], ffn_norm_g[0], w_ffn_up[0],
                  ffn_conv_w[0], ffn_conv_b[0], w_ffn_down[0], final_norm_g)
```

```python
import functools
import math

import numpy as np
import jax
import jax.numpy as jnp
from jax import lax
from jax.experimental import pallas as pl
from jax.experimental.pallas import tpu as pltpu

F32 = jnp.float32
BF16 = jnp.bfloat16

D_MODEL = 1024
ATTN_HEADS = 8
ATTN_HEAD_DIM = 128
MOBA_BLOCK = 256
MOBA_TOPK = 3
REL_BUCKETS = 32
REL_MAX_DIST = 1024
SSM_INNER = 2048
SSM_HEAD_DIM = 64
SSM_HEADS = 32
SSM_GROUPS = 4
SSM_STATE = 128
SSM_CONV = 4
SSM_CHUNK = 256
SSM_CONV_DIM = 3072
MEM_HEADS = 4
MEM_HEAD_DIM = 256
FFN_HIDDEN = 2816
FFN_CONV = 3
NORM_EPS = 1e-6

LANES = 128
SUBLANES = 8
VMEM_LIMIT = 56 * 1024 * 1024

IN_PROJ_TM = 2048
IN_PROJ_TN = 1024
MERGE_TM = 512
FFN_TM = 512

COL_XBC = 0
COL_GATE = 3072
COL_Z = 6144
COL_Q = 8192
COL_K = 9216
COL_V = 10240
COL_QM = 11264
PROJ_WIDTH = 12288
IN_PROJ_PERM = (8, 9, 10, 6, 7, 0, 1, 2, 11, 3, 4, 5)
IN_PROJ_K_TILE = 1

MOBA_NEAR = -(-(REL_MAX_DIST + MOBA_BLOCK - 1) // MOBA_BLOCK)
NEG_BIG = -1e30
LOG2E = math.log2(math.e)
MOBA_QBLOCKS = 2
MOBA_GROUP = 4
MOBA_VT_ROWS = ATTN_HEAD_DIM + 16
assert MOBA_GROUP <= MOBA_NEAR + 1


def _cparams(sem):
    return pltpu.CompilerParams(dimension_semantics=sem, vmem_limit_bytes=VMEM_LIMIT)


def _sigmoid(x):
    return 1.0 / (1.0 + jnp.exp(-x))


def _silu(x):
    h = 0.5 * x
    return h + h * jnp.tanh(h)


def _split_bf16(x, parts):
    out = []
    r = x
    for _ in range(parts):
        hi = r.astype(BF16)
        out.append(hi)
        r = r - hi.astype(F32)
    return out


def _inproj_kernel(perm_ref, x_ref, g_ref, wa_ref, wb_ref, wdt_ref, proj_ref, dt_ref, kmean_ref, u_ref, *,
                   n_a, k_tile):
    del perm_ref
    j = pl.program_id(1)

    @pl.when(j == 0)
    def _():
        x = x_ref[...]
        ms = jnp.mean(x * x, axis=-1, keepdims=True)
        u = (x * lax.rsqrt(ms + NORM_EPS) * g_ref[...]).astype(BF16)
        u_ref[...] = u
        dt_ref[...] = jnp.dot(u, wdt_ref[...], preferred_element_type=F32)

    def column_tile(w_ref):
        acc = jnp.dot(u_ref[...], w_ref[...], preferred_element_type=F32)
        proj_ref[...] = acc.astype(BF16)
        return acc

    @pl.when(j < n_a)
    def _():
        acc = column_tile(wa_ref)

        @pl.when(j == k_tile)
        def _():
            blk = MOBA_BLOCK
            for r in range(acc.shape[0] // blk):
                kmean_ref[0, r:r + 1, :] = jnp.sum(acc[r * blk:(r + 1) * blk], axis=0, keepdims=True) * (1.0 / blk)

    @pl.when(j >= n_a)
    def _():
        column_tile(wb_ref)


def _in_proj(x2, g, w_a, w_b, w_dt, tm, tn):
    t = x2.shape[0]
    kw = ATTN_HEADS * ATTN_HEAD_DIM
    n_a, n_b = w_a.shape[1] // tn, w_b.shape[1] // tn
    assert tn == kw and tm % MOBA_BLOCK == 0 and (n_a + n_b) * tn == PROJ_WIDTH == len(IN_PROJ_PERM) * tn
    grid_spec = pltpu.PrefetchScalarGridSpec(
        num_scalar_prefetch=1,
        grid=(t // tm, n_a + n_b),
        in_specs=[
            pl.BlockSpec((tm, D_MODEL), lambda i, j, perm: (i, 0)),
            pl.BlockSpec((1, D_MODEL), lambda i, j, perm: (0, 0)),
            pl.BlockSpec((D_MODEL, tn), lambda i, j, perm: (0, jnp.minimum(j, n_a - 1))),
            pl.BlockSpec((D_MODEL, tn), lambda i, j, perm: (0, jnp.maximum(j - n_a, 0))),
            pl.BlockSpec((D_MODEL, LANES), lambda i, j, perm: (0, 0)),
        ],
        out_specs=[
            pl.BlockSpec((tm, tn), lambda i, j, perm: (i, perm[j])),
            pl.BlockSpec((tm, LANES), lambda i, j, perm: (i, 0)),
            pl.BlockSpec((1, tm // MOBA_BLOCK, kw), lambda i, j, perm: (i, 0, 0)),
        ],
        scratch_shapes=[pltpu.VMEM((tm, D_MODEL), BF16)],
    )
    return pl.pallas_call(
        functools.partial(_inproj_kernel, n_a=n_a, k_tile=IN_PROJ_K_TILE),
        grid_spec=grid_spec,
        out_shape=[
            jax.ShapeDtypeStruct((t, PROJ_WIDTH), BF16),
            jax.ShapeDtypeStruct((t, LANES), F32),
            jax.ShapeDtypeStruct((t // tm, tm // MOBA_BLOCK, kw), F32),
        ],
        compiler_params=_cparams(("arbitrary", "arbitrary")),
        name="in_proj",
    )(jnp.asarray(IN_PROJ_PERM, jnp.int32), x2, g, w_a, w_b, w_dt)


def _t5_bucket(dist):
    n = jnp.maximum(dist, 0)
    max_exact = REL_BUCKETS // 2
    nf = jnp.maximum(n, max_exact).astype(F32)
    large = max_exact + (jnp.log(nf * (1.0 / max_exact)) / math.log(REL_MAX_DIST / max_exact)
                         * (REL_BUCKETS - max_exact)).astype(jnp.int32)
    large = jnp.minimum(large, REL_BUCKETS - 1)
    return jnp.where(n < max_exact, n, large)


def _moba_kernel(rel_ref, q_ref, k_ref, v_ref, km_ref, o_ref, kmean_ref, bias_ref, vt_ref, neg_ref, s_ref, m_ref,
                 acc_ref, *, nb):
    h = pl.program_id(0)
    b = pl.program_id(1)
    i = pl.program_id(2)
    blk = MOBA_BLOCK
    scale = ATTN_HEAD_DIM ** -0.5


    @pl.when((b == 0) & (i == 0))
    def _():
        key = lax.broadcasted_iota(jnp.int32, (blk, blk), 0)
        qry = lax.broadcasted_iota(jnp.int32, (blk, blk), 1)
        for d in range(MOBA_NEAR):
            bucket = _t5_bucket(d * blk + qry - key)
            tile = jnp.zeros((blk, blk), F32)
            for bk in range(REL_BUCKETS):
                tile = jnp.where(bucket == bk, rel_ref[h, bk], tile)
            bias_ref[d] = tile * LOG2E

    @pl.when(i == 0)
    def _():
        kmean_ref[...] = jnp.zeros_like(kmean_ref)
        kmean_ref[0:nb, :] = km_ref[0]
        ones_rows = (lax.broadcasted_iota(jnp.int32, (MOBA_VT_ROWS - ATTN_HEAD_DIM, blk), 0) == 0).astype(BF16)
        for jb in range(nb):
            vt_ref[jb, 0:ATTN_HEAD_DIM, :] = v_ref[0, jb * blk:(jb + 1) * blk, :].astype(F32).T.astype(BF16)
            vt_ref[jb, ATTN_HEAD_DIM:MOBA_VT_ROWS, :] = ones_rows

    qw = MOBA_QBLOCKS * blk
    i0 = i * MOBA_QBLOCKS
    q_t = q_ref[0].astype(F32).T
    qs_t = (q_t * (scale * LOG2E)).astype(BF16)

    gate = jnp.dot(kmean_ref[...].astype(BF16), q_t.astype(BF16), preferred_element_type=F32)
    nbp = kmean_ref.shape[0]
    row = lax.broadcasted_iota(jnp.int32, (nbp, qw), 0).astype(F32)
    own = (i0 + lax.broadcasted_iota(jnp.int32, (1, qw), 1) // blk).astype(F32)
    g = jnp.where(row < own, gate, -jnp.inf)
    sel = jnp.zeros((nbp, qw), F32)
    for t in range(MOBA_TOPK):
        mx = jnp.max(g, axis=0, keepdims=True)
        idx = jnp.min(jnp.where(g == mx, row, float(nbp)), axis=0, keepdims=True)
        hit = row == idx
        sel = jnp.maximum(sel, jnp.where(hit & (own > t), 1.0, 0.0))
        g = jnp.where(hit, -jnp.inf, g)
    neg_ref[...] = jnp.where(sel > 0.5, 0.0, NEG_BIG)

    def far_scores(j0):
        start = pl.multiple_of(j0 * blk, blk)
        return jnp.dot(k_ref[0, pl.ds(start, MOBA_GROUP * blk), :], qs_t, preferred_element_type=F32)

    s_ref[0] = far_scores(0)

    def softmax_group(scores, offsets, v_idx, m_old):
        m_new = m_old
        for sj, off in zip(scores, offsets):
            m_new = jnp.maximum(m_new, jnp.max(sj, axis=0, keepdims=True) + off)
        pv = jnp.zeros((MOBA_VT_ROWS, m_old.shape[1]), F32)
        for sj, off, vj in zip(scores, offsets, v_idx):
            pj = jnp.exp2(sj + (off - m_new))
            pv = pv + jnp.dot(vt_ref[vj], pj.astype(BF16), preferred_element_type=F32)
        return m_new, pv

    far_bias = rel_ref[h, REL_BUCKETS - 1] * LOG2E

    key = lax.broadcasted_iota(jnp.int32, (blk, blk), 0)
    qry = lax.broadcasted_iota(jnp.int32, (blk, blk), 1)
    scores, offsets, v_idx, first_lane = [], [], [], []
    for e in range(MOBA_NEAR - 1 + MOBA_QBLOCKS):
        jn = i0 - (MOBA_NEAR - 1) + e
        jc = jnp.maximum(jn, 0)
        start = pl.multiple_of(jc * blk, blk)
        w_min = max(0, e - (MOBA_NEAR - 1))
        lo = w_min * blk
        se = jnp.dot(k_ref[0, pl.ds(start, blk), :], qs_t[:, lo:], preferred_element_type=F32)
        sel_row = neg_ref[pl.ds(jnp.where(jn >= 0, jn, nbp - 1), 1), :]
        s_parts, o_parts = [], []
        for w in range(w_min, MOBA_QBLOCKS):
            d = w + (MOBA_NEAR - 1) - e
            sw = se[:, (w - w_min) * blk:(w - w_min + 1) * blk]
            ow = sel_row[:, w * blk:(w + 1) * blk]
            if d == 0:
                sw = jnp.where(qry >= key, sw + bias_ref[0], NEG_BIG)
                ow = jnp.zeros((1, blk), F32)
            elif d < MOBA_NEAR:
                sw = sw + bias_ref[d]
            else:
                ow = ow + far_bias
            s_parts.append(sw)
            o_parts.append(ow)
        scores.append(jnp.concatenate(s_parts, axis=1))
        offsets.append(jnp.concatenate(o_parts, axis=1))
        v_idx.append(jc)
        first_lane.append(lo)
    m_floor = jnp.concatenate(
        [jnp.max(scores[MOBA_NEAR - 1 + w][:, 0:blk], axis=0, keepdims=True) for w in range(MOBA_QBLOCKS)], axis=1)
    parts = []
    for sj, off, vj, lo in zip(scores, offsets, v_idx, first_lane):
        mu, pvu = softmax_group([sj], [off], [vj], m_floor[:, lo:])
        if lo:
            mu = jnp.concatenate([m_floor[:, :lo], mu], axis=1)
            pvu = jnp.concatenate([jnp.zeros((MOBA_VT_ROWS, lo), F32), pvu], axis=1)
        parts.append((mu, pvu))
    m0 = parts[0][0]
    for mu, _ in parts[1:]:
        m0 = jnp.maximum(m0, mu)
    pv0 = jnp.zeros((MOBA_VT_ROWS, qw), F32)
    for mu, pvu in parts:
        pv0 = pv0 + jnp.exp2(mu - m0) * pvu
    m_ref[...] = m0
    acc_ref[...] = pv0

    n_far = jnp.maximum(i0 - (MOBA_NEAR - 1), 0)

    n_groups = (n_far + MOBA_GROUP - 1) // MOBA_GROUP

    def far_group(gi, slot, prefetch):
        j0 = gi * MOBA_GROUP
        if prefetch:
            s_ref[1 - slot] = far_scores(j0 + MOBA_GROUP)
        scores, offsets, v_idx = [], [], []
        for u in range(MOBA_GROUP):
            ju = j0 + u
            scores.append(s_ref[slot, u * blk:(u + 1) * blk, :])
            offsets.append(neg_ref[pl.ds(jnp.where(ju < n_far, ju, nbp - 1), 1), :] + far_bias)
            v_idx.append(ju)
        m_old = m_ref[...]
        m_new, pv = softmax_group(scores, offsets, v_idx, m_old)
        acc_ref[...] = jnp.exp2(m_old - m_new) * acc_ref[...] + pv
        m_ref[...] = m_new

    def far_body(t, carry):
        far_group(2 * t, 0, True)
        far_group(2 * t + 1, 1, True)
        return carry

    n_pairs = jnp.maximum(n_groups - 1, 0) // 2
    lax.fori_loop(0, n_pairs, far_body, 0)
    g_rest = 2 * n_pairs
    n_rest = n_groups - g_rest

    @pl.when(n_rest == 1)
    def _():
        far_group(g_rest, 0, False)

    @pl.when(n_rest == 2)
    def _():
        far_group(g_rest, 0, True)
        far_group(g_rest + 1, 1, False)

    out_t = acc_ref[0:ATTN_HEAD_DIM, :] / acc_ref[ATTN_HEAD_DIM:ATTN_HEAD_DIM + 1, :]
    o_ref[0] = out_t.T.astype(o_ref.dtype)


def _moba(proj3, kmean3, rel_t):
    bsz, s_len, _ = proj3.shape
    blk = MOBA_BLOCK
    nb = s_len // blk
    qw = MOBA_QBLOCKS * blk
    assert nb * blk == s_len and nb % MOBA_QBLOCKS == 0 and nb >= MOBA_GROUP
    nbp = -(-(nb + 1) // 16) * 16
    cb = LANES
    return pl.pallas_call(
        functools.partial(_moba_kernel, nb=nb),
        grid=(ATTN_HEADS, bsz, nb // MOBA_QBLOCKS),
        in_specs=[
            pl.BlockSpec(memory_space=pltpu.SMEM),
            pl.BlockSpec((1, qw, cb), lambda h, b, i: (b, i, COL_Q // cb + h)),
            pl.BlockSpec((1, s_len, cb), lambda h, b, i: (b, 0, COL_K // cb + h)),
            pl.BlockSpec((1, s_len, cb), lambda h, b, i: (b, 0, COL_V // cb + h)),
            pl.BlockSpec((1, nb, cb), lambda h, b, i: (b, 0, h)),
        ],
        out_specs=pl.BlockSpec((1, qw, cb), lambda h, b, i: (b, i, h)),
        out_shape=jax.ShapeDtypeStruct((bsz, s_len, ATTN_HEADS * ATTN_HEAD_DIM), BF16),
        scratch_shapes=[
            pltpu.VMEM((nbp, ATTN_HEAD_DIM), F32),
            pltpu.VMEM((MOBA_NEAR, blk, blk), F32),
            pltpu.VMEM((nb, MOBA_VT_ROWS, blk), BF16),
            pltpu.VMEM((nbp, qw), F32),
            pltpu.VMEM((2, MOBA_GROUP * blk, qw), F32),
            pltpu.VMEM((1, qw), F32),
            pltpu.VMEM((MOBA_VT_ROWS, qw), F32),
        ],
        compiler_params=_cparams(("arbitrary", "arbitrary", "arbitrary")),
        name="moba",
    )(rel_t, proj3, proj3, proj3, kmean3)


def _ssd_kernel(xbc_ref, z_ref, dt_ref, cw_ref, cbias_ref, dtb_ref, alog_ref, dskip_ref, ng_ref,
                tril_ref, e_ref, shift_ref, o_ref, tail_ref, state_ref):
    L = SSM_CHUNK
    G = SSM_GROUPS
    N = SSM_STATE
    GW = SSM_INNER // G
    pad = SUBLANES
    nt = (((1,), (1,)), ((), ()))

    @pl.when(pl.program_id(1) == 0)
    def _():
        tail_ref[...] = jnp.zeros_like(tail_ref)
        state_ref[...] = jnp.zeros_like(state_ref)

    xb = xbc_ref[0]
    x = xb.astype(F32)
    conv = cbias_ref[...] + x * cw_ref[SSM_CONV - 1:SSM_CONV, :]
    head = jnp.zeros((pad, SSM_CONV_DIM), F32)
    for k in range(SSM_CONV - 1):
        s = SSM_CONV - 1 - k
        w_k = cw_ref[k:k + 1, :]
        conv = conv + jnp.dot(shift_ref[s - 1], xb, preferred_element_type=F32) * w_k
        head = head + tail_ref[pad - s:2 * pad - s, :] * w_k
    conv = jnp.concatenate([conv[0:pad] + head, conv[pad:]], axis=0)
    tail_ref[0:pad, :] = x[L - pad:L]
    xa = _silu(conv)
    xs = xa[:, :SSM_INNER]
    bm = xa[:, SSM_INNER:SSM_INNER + G * N]
    cm = xa[:, SSM_INNER + G * N:].astype(BF16)

    dtr = dt_ref[0] + dtb_ref[...]
    dt = jnp.maximum(dtr, 0.0) + jnp.log1p(jnp.exp(-jnp.abs(dtr)))
    a = dt * (-jnp.exp(alog_ref[...]))
    a_cat = jnp.concatenate(_split_bf16(a, 3), axis=0)
    acs = jnp.dot(tril_ref[...], a_cat, preferred_element_type=F32) * LOG2E
    acs_t = acs.T

    def expand(v):
        v_cat = jnp.concatenate(_split_bf16(v, 2), axis=1)
        return jnp.dot(v_cat, e_ref[...], preferred_element_type=F32)

    dt_x = expand(dt)
    acs_x = expand(acs)
    xc = xs * dt_x
    xc_b = xc.astype(BF16)
    last = acs_x[L - 1:L, :]
    ea_x = jnp.exp2(acs_x)
    w_end = (jnp.exp2(last - acs_x) * xc).astype(BF16)
    ea_last = jnp.exp2(last)

    row = lax.broadcasted_iota(jnp.int32, (L, L), 0)
    col = lax.broadcasted_iota(jnp.int32, (L, L), 1)
    causal = row >= col
    lane = lax.broadcasted_iota(jnp.int32, (L, LANES), 1)
    low_half = lane < SSM_HEAD_DIM

    y_parts = []
    for g in range(G):
        bg = bm[:, g * N:(g + 1) * N]
        cg = cm[:, g * N:(g + 1) * N]
        cb = lax.dot_general(cg, bg.astype(BF16), nt, preferred_element_type=F32)
        st = state_ref[g]
        y_off = jnp.dot(cg, st.astype(BF16), preferred_element_type=F32) * ea_x[:, g * GW:(g + 1) * GW]
        for pr in range(GW // LANES):
            c0 = g * GW + pr * LANES
            x_pair = xc_b[:, c0:c0 + LANES]
            zero = jnp.zeros_like(x_pair)
            y_pair = y_off[:, pr * LANES:(pr + 1) * LANES]
            for half in range(2):
                hd = c0 // SSM_HEAD_DIM + half
                seg = acs[:, hd:hd + 1] - acs_t[hd:hd + 1, :]
                decay = jnp.exp2(jnp.where(causal, seg, -jnp.inf))
                mmat = (cb * decay).astype(BF16)
                x_half = jnp.where(low_half, x_pair, zero) if half == 0 else jnp.where(low_half, zero, x_pair)
                y_pair = y_pair + jnp.dot(mmat, x_half, preferred_element_type=F32)
            y_parts.append(y_pair)
        bg_t = bg.T.astype(BF16)
        state_ref[g] = (st * ea_last[:, g * GW:(g + 1) * GW]
                        + jnp.dot(bg_t, w_end[:, g * GW:(g + 1) * GW], preferred_element_type=F32))

    y = jnp.concatenate(y_parts, axis=1) + dskip_ref[...] * xs
    v = y * _silu(z_ref[0].astype(F32))
    outs = []
    for g in range(G):
        vg = v[:, g * GW:(g + 1) * GW]
        ms = jnp.mean(vg * vg, axis=-1, keepdims=True)
        outs.append(vg * lax.rsqrt(ms + NORM_EPS))
    o_ref[0] = (jnp.concatenate(outs, axis=1) * ng_ref[...]).astype(o_ref.dtype)


def _ssd(proj3, dt3, conv_w, conv_b, dt_bias, a_log, d_skip_x, norm_g, tril3, e2, shifts):
    bsz, s_len, _ = proj3.shape
    L = SSM_CHUNK
    nc = s_len // L
    assert nc * L == s_len
    const = lambda b, c: (0, 0)
    return pl.pallas_call(
        _ssd_kernel,
        grid=(bsz, nc),
        in_specs=[
            pl.BlockSpec((1, L, SSM_CONV_DIM), lambda b, c: (b, c, COL_XBC // SSM_CONV_DIM)),
            pl.BlockSpec((1, L, SSM_INNER), lambda b, c: (b, c, COL_Z // SSM_INNER)),
            pl.BlockSpec((1, L, LANES), lambda b, c: (b, c, 0)),
            pl.BlockSpec((SSM_CONV, SSM_CONV_DIM), const),
            pl.BlockSpec((1, SSM_CONV_DIM), const),
            pl.BlockSpec((1, LANES), const),
            pl.BlockSpec((1, LANES), const),
            pl.BlockSpec((1, SSM_INNER), const),
            pl.BlockSpec((1, SSM_INNER), const),
            pl.BlockSpec((L, 3 * L), const),
            pl.BlockSpec((2 * LANES, SSM_INNER), const),
            pl.BlockSpec((SSM_CONV - 1, L, L), lambda b, c: (0, 0, 0)),
        ],
        out_specs=pl.BlockSpec((1, L, SSM_INNER), lambda b, c: (b, c, 0)),
        out_shape=jax.ShapeDtypeStruct((bsz, s_len, SSM_INNER), BF16),
        scratch_shapes=[
            pltpu.VMEM((2 * SUBLANES, SSM_CONV_DIM), F32),
            pltpu.VMEM((SSM_GROUPS, SSM_STATE, SSM_INNER // SSM_GROUPS), F32),
        ],
        compiler_params=_cparams(("arbitrary", "arbitrary")),
        name="ssd",
    )(proj3, proj3, dt3, conv_w, conv_b, dt_bias, a_log, d_skip_x, norm_g, tril3, e2, shifts)


def _memkv_kernel(mem_ref, g_ref, w_ref, kv_ref):
    x = mem_ref[...]
    ms = jnp.mean(x * x, axis=-1, keepdims=True)
    u = (x * lax.rsqrt(ms + NORM_EPS) * g_ref[...]).astype(BF16)
    kv_ref[...] = jnp.dot(u, w_ref[...], preferred_element_type=F32).astype(BF16)


def _mem_kv(mem2, g, w_kv):
    rows = mem2.shape[0]
    width = w_kv.shape[1]
    return pl.pallas_call(
        _memkv_kernel,
        grid=(1,),
        in_specs=[
            pl.BlockSpec((rows, D_MODEL), lambda i: (0, 0)),
            pl.BlockSpec((1, D_MODEL), lambda i: (0, 0)),
            pl.BlockSpec((D_MODEL, width), lambda i: (0, 0)),
        ],
        out_specs=pl.BlockSpec((rows, width), lambda i: (0, 0)),
        out_shape=jax.ShapeDtypeStruct((rows, width), BF16),
        compiler_params=_cparams(("arbitrary",)),
        name="mem_kv",
    )(mem2, g, w_kv)


def _merge_kernel(x_ref, oa_ref, os_ref, qm_ref, gl_ref, kv_ref, bg_ref, wa_ref, ws_ref, wm_ref, wo_ref, h_ref):
    nt = (((1,), (1,)), ((), ()))
    hd = MEM_HEAD_DIM
    width = MEM_HEADS * hd
    scale = hd ** -0.5
    qm = qm_ref[0]
    kv = kv_ref[0]
    outs = []
    for hh in range(MEM_HEADS):
        q = qm[:, hh * hd:(hh + 1) * hd]
        km = kv[:, hh * hd:(hh + 1) * hd]
        vm = kv[:, width + hh * hd:width + (hh + 1) * hd]
        s = lax.dot_general(q, km, nt, preferred_element_type=F32) * scale
        p = jnp.exp(s - jnp.max(s, axis=1, keepdims=True))
        o = jnp.dot(p.astype(BF16), vm, preferred_element_type=F32)
        outs.append(o / jnp.sum(p, axis=1, keepdims=True))
    o_mem = jnp.concatenate(outs, axis=1).astype(BF16)

    gates = _sigmoid(gl_ref[0].astype(F32) + bg_ref[...])
    merged = (gates[:, :D_MODEL] * jnp.dot(oa_ref[0], wa_ref[...], preferred_element_type=F32)
              + gates[:, D_MODEL:2 * D_MODEL] * jnp.dot(os_ref[0], ws_ref[...], preferred_element_type=F32)
              + gates[:, 2 * D_MODEL:] * jnp.dot(o_mem, wm_ref[...], preferred_element_type=F32))
    h_ref[0] = x_ref[0] + jnp.dot(merged.astype(BF16), wo_ref[...], preferred_element_type=F32)


def _resident(shape):
    return pl.BlockSpec(shape, lambda *_: (0,) * len(shape), pipeline_mode=pl.Buffered(1))


def _merge(x, o_attn, o_ssm, proj3, kv3, b_gate, wa, ws, wm, wo, tm):
    bsz, s_len, _ = x.shape
    mem_len = kv3.shape[1]
    return pl.pallas_call(
        _merge_kernel,
        grid=(bsz, s_len // tm),
        in_specs=[
            pl.BlockSpec((1, tm, D_MODEL), lambda b, i: (b, i, 0)),
            pl.BlockSpec((1, tm, D_MODEL), lambda b, i: (b, i, 0)),
            pl.BlockSpec((1, tm, SSM_INNER), lambda b, i: (b, i, 0)),
            pl.BlockSpec((1, tm, D_MODEL), lambda b, i: (b, i, COL_QM // D_MODEL)),
            pl.BlockSpec((1, tm, 3 * D_MODEL), lambda b, i: (b, i, COL_GATE // (3 * D_MODEL))),
            pl.BlockSpec((1, mem_len, 2 * D_MODEL), lambda b, i: (b, 0, 0)),
            _resident((1, 3 * D_MODEL)),
            _resident((D_MODEL, D_MODEL)),
            _resident((SSM_INNER, D_MODEL)),
            _resident((D_MODEL, D_MODEL)),
            _resident((D_MODEL, D_MODEL)),
        ],
        out_specs=pl.BlockSpec((1, tm, D_MODEL), lambda b, i: (b, i, 0)),
        out_shape=jax.ShapeDtypeStruct((bsz, s_len, D_MODEL), F32),
        compiler_params=_cparams(("arbitrary", "arbitrary")),
        name="merge",
    )(x, o_attn, o_ssm, proj3, proj3, kv3, b_gate, wa, ws, wm, wo)


def _ffn_kernel(h_ref, ng_ref, wup_ref, cw_ref, cb_ref, wdn_ref, fg_ref, o_ref, hid_ref, *, tm):
    pad = SUBLANES

    @pl.when(pl.program_id(1) == 0)
    def _():
        hid_ref[0:pad, :] = jnp.zeros((pad, 2 * FFN_HIDDEN), F32)

    h = h_ref[0]
    ms = jnp.mean(h * h, axis=-1, keepdims=True)
    u = (h * lax.rsqrt(ms + NORM_EPS) * ng_ref[...]).astype(BF16)
    hid_ref[pad:pad + tm, :] = jnp.dot(u, wup_ref[...], preferred_element_type=F32)
    conv = cb_ref[...]
    for k in range(FFN_CONV):
        off = pad - (FFN_CONV - 1) + k
        conv = conv + hid_ref[off:off + tm, :] * cw_ref[k:k + 1, :]
    hid_ref[0:pad, :] = hid_ref[tm:tm + pad, :]
    act = (_silu(conv[:, :FFN_HIDDEN]) * conv[:, FFN_HIDDEN:]).astype(BF16)
    y = h + jnp.dot(act, wdn_ref[...], preferred_element_type=F32)
    ms2 = jnp.mean(y * y, axis=-1, keepdims=True)
    o_ref[0] = y * lax.rsqrt(ms2 + NORM_EPS) * fg_ref[...]


def _ffn(h, norm_g, w_up, conv_w, conv_b, w_down, final_g, tm):
    bsz, s_len, _ = h.shape
    f2 = 2 * FFN_HIDDEN
    return pl.pallas_call(
        functools.partial(_ffn_kernel, tm=tm),
        grid=(bsz, s_len // tm),
        in_specs=[
            pl.BlockSpec((1, tm, D_MODEL), lambda b, i: (b, i, 0)),
            _resident((1, D_MODEL)),
            _resident((D_MODEL, f2)),
            _resident((FFN_CONV, f2)),
            _resident((1, f2)),
            _resident((FFN_HIDDEN, D_MODEL)),
            _resident((1, D_MODEL)),
        ],
        out_specs=pl.BlockSpec((1, tm, D_MODEL), lambda b, i: (b, i, 0)),
        out_shape=jax.ShapeDtypeStruct((bsz, s_len, D_MODEL), F32),
        scratch_shapes=[pltpu.VMEM((tm + 2 * SUBLANES, f2), F32)],
        compiler_params=_cparams(("arbitrary", "arbitrary")),
        name="ffn",
    )(h, norm_g, w_up, conv_w, conv_b, w_down, final_g)


def _ssd_constants():
    L = SSM_CHUNK
    tril = np.tril(np.ones((L, L), np.float32))
    tril3 = np.concatenate([tril, tril, tril], axis=1)
    e = np.zeros((LANES, SSM_INNER), np.float32)
    for hd in range(SSM_HEADS):
        e[hd, hd * SSM_HEAD_DIM:(hd + 1) * SSM_HEAD_DIM] = 1.0
    e2 = np.concatenate([e, e], axis=0)
    shifts = np.stack([np.eye(L, k=-s, dtype=np.float32) for s in range(1, SSM_CONV)])
    return jnp.asarray(tril3, BF16), jnp.asarray(e2, BF16), jnp.asarray(shifts, BF16)


def _pad_lanes(v):
    return jnp.pad(v, ((0, 0), (0, LANES - v.shape[1])))


def _layer(h, mem, rel_bias, mix_norm_g, w_in, b_gate, ssm_conv_w, ssm_conv_b, ssm_dt_bias, ssm_A_log, ssm_D,
           ssm_norm_g, mem_norm_g, w_mem_kv, w_br_attn, w_br_ssm, w_br_mem, w_out, ffn_norm_g, w_ffn_up,
           ffn_conv_w, ffn_conv_b, w_ffn_down, final_g):
    bsz, s_len, _ = h.shape
    w_a = w_in[:, :8192].astype(BF16)
    w_b = w_in[:, 8224:].astype(BF16)
    w_dt = _pad_lanes(w_in[:, 8192:8224]).astype(BF16)

    proj, dt_raw, kmean = _in_proj(h.reshape(bsz * s_len, D_MODEL), mix_norm_g[None, :], w_a, w_b, w_dt,
                                   tm=IN_PROJ_TM, tn=IN_PROJ_TN)
    proj3 = proj.reshape(bsz, s_len, PROJ_WIDTH)
    dt3 = dt_raw.reshape(bsz, s_len, LANES)
    kmean3 = kmean.reshape(bsz, s_len // MOBA_BLOCK, ATTN_HEADS * ATTN_HEAD_DIM)

    o_attn = _moba(proj3, kmean3, rel_bias.T)

    tril3, e2, shifts = _ssd_constants()
    o_ssm = _ssd(proj3, dt3, ssm_conv_w, ssm_conv_b[None, :], _pad_lanes(ssm_dt_bias[None, :]),
                 _pad_lanes(ssm_A_log[None, :]), jnp.repeat(ssm_D, SSM_HEAD_DIM)[None, :], ssm_norm_g[None, :],
                 tril3, e2, shifts)

    mem_len = mem.shape[1]
    kv = _mem_kv(mem.reshape(bsz * mem_len, D_MODEL), mem_norm_g[None, :], w_mem_kv.astype(BF16))
    kv3 = kv.reshape(bsz, mem_len, 2 * D_MODEL)

    h1 = _merge(h, o_attn, o_ssm, proj3, kv3, b_gate[None, :], w_br_attn.astype(BF16), w_br_ssm.astype(BF16),
                w_br_mem.astype(BF16), w_out.astype(BF16), tm=MERGE_TM)

    return _ffn(h1, ffn_norm_g[None, :], w_ffn_up.astype(BF16), ffn_conv_w, ffn_conv_b[None, :],
                w_ffn_down.astype(BF16), final_g[None, :], tm=FFN_TM)


def kernel(x, mem, rel_bias, mix_norm_g, w_in, b_gate, ssm_conv_w, ssm_conv_b, ssm_dt_bias, ssm_A_log, ssm_D,
           ssm_norm_g, mem_norm_g, w_mem_kv, w_br_attn, w_br_ssm, w_br_mem, w_out, ffn_norm_g, w_ffn_up,
           ffn_conv_w, ffn_conv_b, w_ffn_down, final_norm_g):
    assert w_in.shape[0] == 1, "single-layer trunk"
    return _layer(x, mem, rel_bias, mix_norm_g[0], w_in[0], b_gate[0], ssm_conv_w[0], ssm_conv_b[0],
                  ssm_dt_bias[0], ssm_A_log[0], ssm_D[0], ssm_norm_g[0], mem_norm_g[0], w_mem_kv[0],
                  w_br_attn[0], w_br_ssm[0], w_br_mem[0], w_out[0], ffn_norm_g[0], w_ffn_up[0],
                  ffn_conv_w[0], ffn_conv_b[0], w_ffn_down[0], final_norm_g)
```

```python
import functools
import math

import numpy as np
import jax
import jax.numpy as jnp
from jax import lax
from jax.experimental import pallas as pl
from jax.experimental.pallas import tpu as pltpu

F32 = jnp.float32
BF16 = jnp.bfloat16

D_MODEL = 1024
ATTN_HEADS = 8
ATTN_HEAD_DIM = 128
MOBA_BLOCK = 256
MOBA_TOPK = 3
REL_BUCKETS = 32
REL_MAX_DIST = 1024
SSM_INNER = 2048
SSM_HEAD_DIM = 64
SSM_HEADS = 32
SSM_GROUPS = 4
SSM_STATE = 128
SSM_CONV = 4
SSM_CHUNK = 256
SSM_CONV_DIM = 3072
MEM_HEADS = 4
MEM_HEAD_DIM = 256
FFN_HIDDEN = 2816
FFN_CONV = 3
NORM_EPS = 1e-6

LANES = 128
SUBLANES = 8
VMEM_LIMIT = 56 * 1024 * 1024

IN_PROJ_TM = 2048
IN_PROJ_TN = 1024
MERGE_TM = 512
FFN_TM = 512

COL_XBC = 0
COL_GATE = 3072
COL_Z = 6144
COL_Q = 8192
COL_K = 9216
COL_V = 10240
COL_QM = 11264
PROJ_WIDTH = 12288
IN_PROJ_PERM = (8, 9, 10, 6, 7, 0, 1, 2, 11, 3, 4, 5)
IN_PROJ_K_TILE = 1
IN_PROJ_SPLIT = 8192
SSM_DT_COLS = 32

MOBA_NEAR = -(-(REL_MAX_DIST + MOBA_BLOCK - 1) // MOBA_BLOCK)
NEG_BIG = -1e30
LOG2E = math.log2(math.e)
MOBA_QBLOCKS = 2
MOBA_GROUP = 4
MOBA_VT_ROWS = ATTN_HEAD_DIM + 16
assert MOBA_GROUP <= MOBA_NEAR + 1


def _cparams(sem):
    return pltpu.CompilerParams(dimension_semantics=sem, vmem_limit_bytes=VMEM_LIMIT)


def _sigmoid(x):
    return 1.0 / (1.0 + jnp.exp(-x))


def _silu(x):
    h = 0.5 * x
    return h + h * jnp.tanh(h)


def _split_bf16(x, parts):
    out = []
    r = x
    for _ in range(parts):
        hi = r.astype(BF16)
        out.append(hi)
        r = r - hi.astype(F32)
    return out


def _inproj_kernel(perm_ref, x_ref, g_ref, wa_ref, wb_ref, wdt_ref, proj_ref, dt_ref, kmean_ref, u_ref, *,
                   n_a, k_tile):
    del perm_ref
    j = pl.program_id(1)

    @pl.when(j == 0)
    def _():
        x = x_ref[...]
        ms = jnp.mean(x * x, axis=-1, keepdims=True)
        u = (x * lax.rsqrt(ms + NORM_EPS) * g_ref[...]).astype(BF16)
        u_ref[...] = u
        dt_ref[...] = jnp.dot(u, wdt_ref[...], preferred_element_type=F32)

    def column_tile(w_ref):
        acc = jnp.dot(u_ref[...], w_ref[...].astype(BF16), preferred_element_type=F32)
        proj_ref[...] = acc.astype(BF16)
        return acc

    @pl.when(j < n_a)
    def _():
        acc = column_tile(wa_ref)

        @pl.when(j == k_tile)
        def _():
            blk = MOBA_BLOCK
            for r in range(acc.shape[0] // blk):
                kmean_ref[0, r:r + 1, :] = jnp.sum(acc[r * blk:(r + 1) * blk], axis=0, keepdims=True) * (1.0 / blk)

    @pl.when(j >= n_a)
    def _():
        column_tile(wb_ref)


def _in_proj(x2, g, w_a, w_b, w_dt, tm, tn):
    t = x2.shape[0]
    kw = ATTN_HEADS * ATTN_HEAD_DIM
    n_a, n_b = IN_PROJ_SPLIT // tn, w_b.shape[1] // tn
    assert tn == kw and tm % MOBA_BLOCK == 0 and (n_a + n_b) * tn == PROJ_WIDTH == len(IN_PROJ_PERM) * tn
    grid_spec = pltpu.PrefetchScalarGridSpec(
        num_scalar_prefetch=1,
        grid=(t // tm, n_a + n_b),
        in_specs=[
            pl.BlockSpec((tm, D_MODEL), lambda i, j, perm: (i, 0)),
            pl.BlockSpec((1, D_MODEL), lambda i, j, perm: (0, 0)),
            pl.BlockSpec((D_MODEL, tn), lambda i, j, perm: (0, jnp.minimum(j, n_a - 1))),
            pl.BlockSpec((D_MODEL, tn), lambda i, j, perm: (0, jnp.maximum(j - n_a, 0))),
            pl.BlockSpec((D_MODEL, LANES), lambda i, j, perm: (0, 0)),
        ],
        out_specs=[
            pl.BlockSpec((tm, tn), lambda i, j, perm: (i, perm[j])),
            pl.BlockSpec((tm, LANES), lambda i, j, perm: (i, 0)),
            pl.BlockSpec((1, tm // MOBA_BLOCK, kw), lambda i, j, perm: (i, 0, 0)),
        ],
        scratch_shapes=[pltpu.VMEM((tm, D_MODEL), BF16)],
    )
    return pl.pallas_call(
        functools.partial(_inproj_kernel, n_a=n_a, k_tile=IN_PROJ_K_TILE),
        grid_spec=grid_spec,
        out_shape=[
            jax.ShapeDtypeStruct((t, PROJ_WIDTH), BF16),
            jax.ShapeDtypeStruct((t, LANES), F32),
            jax.ShapeDtypeStruct((t // tm, tm // MOBA_BLOCK, kw), F32),
        ],
        compiler_params=_cparams(("arbitrary", "arbitrary")),
        name="in_proj",
    )(jnp.asarray(IN_PROJ_PERM, jnp.int32), x2, g, w_a, w_b, w_dt)


def _t5_bucket(dist):
    n = jnp.maximum(dist, 0)
    max_exact = REL_BUCKETS // 2
    nf = jnp.maximum(n, max_exact).astype(F32)
    large = max_exact + (jnp.log(nf * (1.0 / max_exact)) / math.log(REL_MAX_DIST / max_exact)
                         * (REL_BUCKETS - max_exact)).astype(jnp.int32)
    large = jnp.minimum(large, REL_BUCKETS - 1)
    return jnp.where(n < max_exact, n, large)


def _moba_kernel(rel_ref, q_ref, k_ref, v_ref, km_ref, o_ref, kmean_ref, bias_ref, vt_ref, neg_ref, s_ref, m_ref,
                 acc_ref, *, nb):
    h = pl.program_id(0)
    b = pl.program_id(1)
    i = pl.program_id(2)
    blk = MOBA_BLOCK
    scale = ATTN_HEAD_DIM ** -0.5


    @pl.when((b == 0) & (i == 0))
    def _():
        key = lax.broadcasted_iota(jnp.int32, (blk, blk), 0)
        qry = lax.broadcasted_iota(jnp.int32, (blk, blk), 1)
        for d in range(MOBA_NEAR):
            bucket = _t5_bucket(d * blk + qry - key)
            tile = jnp.zeros((blk, blk), F32)
            for bk in range(REL_BUCKETS):
                tile = jnp.where(bucket == bk, rel_ref[h, bk], tile)
            bias_ref[d] = tile * LOG2E

    @pl.when(i == 0)
    def _():
        kmean_ref[...] = jnp.zeros_like(kmean_ref)
        kmean_ref[0:nb, :] = km_ref[0]
        ones_rows = (lax.broadcasted_iota(jnp.int32, (MOBA_VT_ROWS - ATTN_HEAD_DIM, blk), 0) == 0).astype(BF16)
        for jb in range(nb):
            vt_ref[jb, 0:ATTN_HEAD_DIM, :] = v_ref[0, jb * blk:(jb + 1) * blk, :].astype(F32).T.astype(BF16)
            vt_ref[jb, ATTN_HEAD_DIM:MOBA_VT_ROWS, :] = ones_rows

    qw = MOBA_QBLOCKS * blk
    i0 = i * MOBA_QBLOCKS
    q_t = q_ref[0].astype(F32).T
    qs_t = (q_t * (scale * LOG2E)).astype(BF16)

    gate = jnp.dot(kmean_ref[...].astype(BF16), q_t.astype(BF16), preferred_element_type=F32)
    nbp = kmean_ref.shape[0]
    row = lax.broadcasted_iota(jnp.int32, (nbp, qw), 0).astype(F32)
    own = (i0 + lax.broadcasted_iota(jnp.int32, (1, qw), 1) // blk).astype(F32)
    g = jnp.where(row < own, gate, -jnp.inf)
    sel = jnp.zeros((nbp, qw), F32)
    for t in range(MOBA_TOPK):
        mx = jnp.max(g, axis=0, keepdims=True)
        idx = jnp.min(jnp.where(g == mx, row, float(nbp)), axis=0, keepdims=True)
        hit = row == idx
        sel = jnp.maximum(sel, jnp.where(hit & (own > t), 1.0, 0.0))
        g = jnp.where(hit, -jnp.inf, g)
    neg_ref[...] = jnp.where(sel > 0.5, 0.0, NEG_BIG)

    def far_scores(j0):
        start = pl.multiple_of(j0 * blk, blk)
        return jnp.dot(k_ref[0, pl.ds(start, MOBA_GROUP * blk), :], qs_t, preferred_element_type=F32)

    s_ref[0] = far_scores(0)

    def softmax_group(scores, offsets, v_idx, m_old):
        m_new = m_old
        for sj, off in zip(scores, offsets):
            m_new = jnp.maximum(m_new, jnp.max(sj, axis=0, keepdims=True) + off)
        pv = jnp.zeros((MOBA_VT_ROWS, m_old.shape[1]), F32)
        for sj, off, vj in zip(scores, offsets, v_idx):
            pj = jnp.exp2(sj + (off - m_new))
            pv = pv + jnp.dot(vt_ref[vj], pj.astype(BF16), preferred_element_type=F32)
        return m_new, pv

    far_bias = rel_ref[h, REL_BUCKETS - 1] * LOG2E

    key = lax.broadcasted_iota(jnp.int32, (blk, blk), 0)
    qry = lax.broadcasted_iota(jnp.int32, (blk, blk), 1)
    scores, offsets, v_idx, first_lane = [], [], [], []
    for e in range(MOBA_NEAR - 1 + MOBA_QBLOCKS):
        jn = i0 - (MOBA_NEAR - 1) + e
        jc = jnp.maximum(jn, 0)
        start = pl.multiple_of(jc * blk, blk)
        w_min = max(0, e - (MOBA_NEAR - 1))
        lo = w_min * blk
        se = jnp.dot(k_ref[0, pl.ds(start, blk), :], qs_t[:, lo:], preferred_element_type=F32)
        sel_row = neg_ref[pl.ds(jnp.where(jn >= 0, jn, nbp - 1), 1), :]
        s_parts, o_parts = [], []
        for w in range(w_min, MOBA_QBLOCKS):
            d = w + (MOBA_NEAR - 1) - e
            sw = se[:, (w - w_min) * blk:(w - w_min + 1) * blk]
            ow = sel_row[:, w * blk:(w + 1) * blk]
            if d == 0:
                sw = jnp.where(qry >= key, sw + bias_ref[0], NEG_BIG)
                ow = jnp.zeros((1, blk), F32)
            elif d < MOBA_NEAR:
                sw = sw + bias_ref[d]
            else:
                ow = ow + far_bias
            s_parts.append(sw)
            o_parts.append(ow)
        scores.append(jnp.concatenate(s_parts, axis=1))
        offsets.append(jnp.concatenate(o_parts, axis=1))
        v_idx.append(jc)
        first_lane.append(lo)
    m_floor = jnp.concatenate(
        [jnp.max(scores[MOBA_NEAR - 1 + w][:, 0:blk], axis=0, keepdims=True) for w in range(MOBA_QBLOCKS)], axis=1)
    parts = []
    for sj, off, vj, lo in zip(scores, offsets, v_idx, first_lane):
        mu, pvu = softmax_group([sj], [off], [vj], m_floor[:, lo:])
        if lo:
            mu = jnp.concatenate([m_floor[:, :lo], mu], axis=1)
            pvu = jnp.concatenate([jnp.zeros((MOBA_VT_ROWS, lo), F32), pvu], axis=1)
        parts.append((mu, pvu))
    m0 = parts[0][0]
    for mu, _ in parts[1:]:
        m0 = jnp.maximum(m0, mu)
    pv0 = jnp.zeros((MOBA_VT_ROWS, qw), F32)
    for mu, pvu in parts:
        pv0 = pv0 + jnp.exp2(mu - m0) * pvu
    m_ref[...] = m0
    acc_ref[...] = pv0

    n_far = jnp.maximum(i0 - (MOBA_NEAR - 1), 0)

    n_groups = (n_far + MOBA_GROUP - 1) // MOBA_GROUP

    def far_group(gi, slot, prefetch):
        j0 = gi * MOBA_GROUP
        if prefetch:
            s_ref[1 - slot] = far_scores(j0 + MOBA_GROUP)
        scores, offsets, v_idx = [], [], []
        for u in range(MOBA_GROUP):
            ju = j0 + u
            scores.append(s_ref[slot, u * blk:(u + 1) * blk, :])
            offsets.append(neg_ref[pl.ds(jnp.where(ju < n_far, ju, nbp - 1), 1), :] + far_bias)
            v_idx.append(ju)
        m_old = m_ref[...]
        m_new, pv = softmax_group(scores, offsets, v_idx, m_old)
        acc_ref[...] = jnp.exp2(m_old - m_new) * acc_ref[...] + pv
        m_ref[...] = m_new

    def far_body(t, carry):
        far_group(2 * t, 0, True)
        far_group(2 * t + 1, 1, True)
        return carry

    n_pairs = jnp.maximum(n_groups - 1, 0) // 2
    lax.fori_loop(0, n_pairs, far_body, 0)
    g_rest = 2 * n_pairs
    n_rest = n_groups - g_rest

    @pl.when(n_rest == 1)
    def _():
        far_group(g_rest, 0, False)

    @pl.when(n_rest == 2)
    def _():
        far_group(g_rest, 0, True)
        far_group(g_rest + 1, 1, False)

    out_t = acc_ref[0:ATTN_HEAD_DIM, :] / acc_ref[ATTN_HEAD_DIM:ATTN_HEAD_DIM + 1, :]
    o_ref[0] = out_t.T.astype(o_ref.dtype)


def _moba(proj3, kmean3, rel_t):
    bsz, s_len, _ = proj3.shape
    blk = MOBA_BLOCK
    nb = s_len // blk
    qw = MOBA_QBLOCKS * blk
    assert nb * blk == s_len and nb % MOBA_QBLOCKS == 0 and nb >= MOBA_GROUP
    nbp = -(-(nb + 1) // 16) * 16
    cb = LANES
    return pl.pallas_call(
        functools.partial(_moba_kernel, nb=nb),
        grid=(ATTN_HEADS, bsz, nb // MOBA_QBLOCKS),
        in_specs=[
            pl.BlockSpec(memory_space=pltpu.SMEM),
            pl.BlockSpec((1, qw, cb), lambda h, b, i: (b, i, COL_Q // cb + h)),
            pl.BlockSpec((1, s_len, cb), lambda h, b, i: (b, 0, COL_K // cb + h)),
            pl.BlockSpec((1, s_len, cb), lambda h, b, i: (b, 0, COL_V // cb + h)),
            pl.BlockSpec((1, nb, cb), lambda h, b, i: (b, 0, h)),
        ],
        out_specs=pl.BlockSpec((1, qw, cb), lambda h, b, i: (b, i, h)),
        out_shape=jax.ShapeDtypeStruct((bsz, s_len, ATTN_HEADS * ATTN_HEAD_DIM), BF16),
        scratch_shapes=[
            pltpu.VMEM((nbp, ATTN_HEAD_DIM), F32),
            pltpu.VMEM((MOBA_NEAR, blk, blk), F32),
            pltpu.VMEM((nb, MOBA_VT_ROWS, blk), BF16),
            pltpu.VMEM((nbp, qw), F32),
            pltpu.VMEM((2, MOBA_GROUP * blk, qw), F32),
            pltpu.VMEM((1, qw), F32),
            pltpu.VMEM((MOBA_VT_ROWS, qw), F32),
        ],
        compiler_params=_cparams(("arbitrary", "arbitrary", "arbitrary")),
        name="moba",
    )(rel_t, proj3, proj3, proj3, kmean3)


def _ssd_kernel(xbc_ref, z_ref, dt_ref, cw_ref, cbias_ref, dtb_ref, alog_ref, dskip_ref, ng_ref,
                tril_ref, e_ref, shift_ref, o_ref, tail_ref, state_ref):
    L = SSM_CHUNK
    G = SSM_GROUPS
    N = SSM_STATE
    GW = SSM_INNER // G
    pad = SUBLANES
    nt = (((1,), (1,)), ((), ()))

    @pl.when(pl.program_id(1) == 0)
    def _():
        tail_ref[...] = jnp.zeros_like(tail_ref)
        state_ref[...] = jnp.zeros_like(state_ref)

    xb = xbc_ref[0]
    x = xb.astype(F32)
    conv = cbias_ref[...] + x * cw_ref[SSM_CONV - 1:SSM_CONV, :]
    head = jnp.zeros((pad, SSM_CONV_DIM), F32)
    for k in range(SSM_CONV - 1):
        s = SSM_CONV - 1 - k
        w_k = cw_ref[k:k + 1, :]
        conv = conv + jnp.dot(shift_ref[s - 1], xb, preferred_element_type=F32) * w_k
        head = head + tail_ref[pad - s:2 * pad - s, :] * w_k
    conv = jnp.concatenate([conv[0:pad] + head, conv[pad:]], axis=0)
    tail_ref[0:pad, :] = x[L - pad:L]
    xa = _silu(conv)
    xs = xa[:, :SSM_INNER]
    bm = xa[:, SSM_INNER:SSM_INNER + G * N]
    cm = xa[:, SSM_INNER + G * N:].astype(BF16)

    dtr = dt_ref[0] + dtb_ref[...]
    dt = jnp.maximum(dtr, 0.0) + jnp.log1p(jnp.exp(-jnp.abs(dtr)))
    a = dt * (-jnp.exp(alog_ref[...]))
    a_cat = jnp.concatenate(_split_bf16(a, 3), axis=0)
    acs = jnp.dot(tril_ref[...], a_cat, preferred_element_type=F32) * LOG2E
    acs_t = acs.T

    def expand(v):
        v_cat = jnp.concatenate(_split_bf16(v, 2), axis=1)
        return jnp.dot(v_cat, e_ref[...], preferred_element_type=F32)

    dt_x = expand(dt)
    acs_x = expand(acs)
    xc = xs * dt_x
    xc_b = xc.astype(BF16)
    last = acs_x[L - 1:L, :]
    ea_x = jnp.exp2(acs_x)
    w_end = (jnp.exp2(last - acs_x) * xc).astype(BF16)
    ea_last = jnp.exp2(last)

    row = lax.broadcasted_iota(jnp.int32, (L, L), 0)
    col = lax.broadcasted_iota(jnp.int32, (L, L), 1)
    causal = row >= col
    lane = lax.broadcasted_iota(jnp.int32, (L, LANES), 1)
    low_half = lane < SSM_HEAD_DIM

    y_parts = []
    for g in range(G):
        bg = bm[:, g * N:(g + 1) * N]
        cg = cm[:, g * N:(g + 1) * N]
        cb = lax.dot_general(cg, bg.astype(BF16), nt, preferred_element_type=F32)
        st = state_ref[g]
        y_off = jnp.dot(cg, st.astype(BF16), preferred_element_type=F32) * ea_x[:, g * GW:(g + 1) * GW]
        for pr in range(GW // LANES):
            c0 = g * GW + pr * LANES
            x_pair = xc_b[:, c0:c0 + LANES]
            zero = jnp.zeros_like(x_pair)
            y_pair = y_off[:, pr * LANES:(pr + 1) * LANES]
            for half in range(2):
                hd = c0 // SSM_HEAD_DIM + half
                seg = acs[:, hd:hd + 1] - acs_t[hd:hd + 1, :]
                decay = jnp.exp2(jnp.where(causal, seg, -jnp.inf))
                mmat = (cb * decay).astype(BF16)
                x_half = jnp.where(low_half, x_pair, zero) if half == 0 else jnp.where(low_half, zero, x_pair)
                y_pair = y_pair + jnp.dot(mmat, x_half, preferred_element_type=F32)
            y_parts.append(y_pair)
        bg_t = bg.T.astype(BF16)
        state_ref[g] = (st * ea_last[:, g * GW:(g + 1) * GW]
                        + jnp.dot(bg_t, w_end[:, g * GW:(g + 1) * GW], preferred_element_type=F32))

    y = jnp.concatenate(y_parts, axis=1) + dskip_ref[...] * xs
    v = y * _silu(z_ref[0].astype(F32))
    outs = []
    for g in range(G):
        vg = v[:, g * GW:(g + 1) * GW]
        ms = jnp.mean(vg * vg, axis=-1, keepdims=True)
        outs.append(vg * lax.rsqrt(ms + NORM_EPS))
    o_ref[0] = (jnp.concatenate(outs, axis=1) * ng_ref[...]).astype(o_ref.dtype)


def _ssd(proj3, dt3, conv_w, conv_b, dt_bias, a_log, d_skip_x, norm_g, tril3, e2, shifts):
    bsz, s_len, _ = proj3.shape
    L = SSM_CHUNK
    nc = s_len // L
    assert nc * L == s_len
    const = lambda b, c: (0, 0)
    return pl.pallas_call(
        _ssd_kernel,
        grid=(bsz, nc),
        in_specs=[
            pl.BlockSpec((1, L, SSM_CONV_DIM), lambda b, c: (b, c, COL_XBC // SSM_CONV_DIM)),
            pl.BlockSpec((1, L, SSM_INNER), lambda b, c: (b, c, COL_Z // SSM_INNER)),
            pl.BlockSpec((1, L, LANES), lambda b, c: (b, c, 0)),
            pl.BlockSpec((SSM_CONV, SSM_CONV_DIM), const),
            pl.BlockSpec((1, SSM_CONV_DIM), const),
            pl.BlockSpec((1, LANES), const),
            pl.BlockSpec((1, LANES), const),
            pl.BlockSpec((1, SSM_INNER), const),
            pl.BlockSpec((1, SSM_INNER), const),
            pl.BlockSpec((L, 3 * L), const),
            pl.BlockSpec((2 * LANES, SSM_INNER), const),
            pl.BlockSpec((SSM_CONV - 1, L, L), lambda b, c: (0, 0, 0)),
        ],
        out_specs=pl.BlockSpec((1, L, SSM_INNER), lambda b, c: (b, c, 0)),
        out_shape=jax.ShapeDtypeStruct((bsz, s_len, SSM_INNER), BF16),
        scratch_shapes=[
            pltpu.VMEM((2 * SUBLANES, SSM_CONV_DIM), F32),
            pltpu.VMEM((SSM_GROUPS, SSM_STATE, SSM_INNER // SSM_GROUPS), F32),
        ],
        compiler_params=_cparams(("arbitrary", "arbitrary")),
        name="ssd",
    )(proj3, proj3, dt3, conv_w, conv_b, dt_bias, a_log, d_skip_x, norm_g, tril3, e2, shifts)


def _memkv_kernel(mem_ref, g_ref, w_ref, kv_ref):
    x = mem_ref[...]
    ms = jnp.mean(x * x, axis=-1, keepdims=True)
    u = (x * lax.rsqrt(ms + NORM_EPS) * g_ref[...]).astype(BF16)
    kv_ref[...] = jnp.dot(u, w_ref[...], preferred_element_type=F32).astype(BF16)


def _mem_kv(mem2, g, w_kv):
    rows = mem2.shape[0]
    width = w_kv.shape[1]
    return pl.pallas_call(
        _memkv_kernel,
        grid=(1,),
        in_specs=[
            pl.BlockSpec((rows, D_MODEL), lambda i: (0, 0)),
            pl.BlockSpec((1, D_MODEL), lambda i: (0, 0)),
            pl.BlockSpec((D_MODEL, width), lambda i: (0, 0)),
        ],
        out_specs=pl.BlockSpec((rows, width), lambda i: (0, 0)),
        out_shape=jax.ShapeDtypeStruct((rows, width), BF16),
        compiler_params=_cparams(("arbitrary",)),
        name="mem_kv",
    )(mem2, g, w_kv)


def _merge_kernel(x_ref, oa_ref, os_ref, qm_ref, gl_ref, kv_ref, bg_ref, wa_ref, ws_ref, wm_ref, wo_ref, h_ref):
    nt = (((1,), (1,)), ((), ()))
    hd = MEM_HEAD_DIM
    width = MEM_HEADS * hd
    scale = hd ** -0.5
    qm = qm_ref[0]
    kv = kv_ref[0]
    outs = []
    for hh in range(MEM_HEADS):
        q = qm[:, hh * hd:(hh + 1) * hd]
        km = kv[:, hh * hd:(hh + 1) * hd]
        vm = kv[:, width + hh * hd:width + (hh + 1) * hd]
        s = lax.dot_general(q, km, nt, preferred_element_type=F32) * scale
        p = jnp.exp(s - jnp.max(s, axis=1, keepdims=True))
        o = jnp.dot(p.astype(BF16), vm, preferred_element_type=F32)
        outs.append(o / jnp.sum(p, axis=1, keepdims=True))
    o_mem = jnp.concatenate(outs, axis=1).astype(BF16)

    gates = _sigmoid(gl_ref[0].astype(F32) + bg_ref[...])
    merged = (gates[:, :D_MODEL] * jnp.dot(oa_ref[0], wa_ref[...], preferred_element_type=F32)
              + gates[:, D_MODEL:2 * D_MODEL] * jnp.dot(os_ref[0], ws_ref[...], preferred_element_type=F32)
              + gates[:, 2 * D_MODEL:] * jnp.dot(o_mem, wm_ref[...], preferred_element_type=F32))
    h_ref[0] = x_ref[0] + jnp.dot(merged.astype(BF16), wo_ref[...], preferred_element_type=F32)


def _resident(shape):
    return pl.BlockSpec(shape, lambda *_: (0,) * len(shape), pipeline_mode=pl.Buffered(1))


def _merge(x, o_attn, o_ssm, proj3, kv3, b_gate, wa, ws, wm, wo, tm):
    bsz, s_len, _ = x.shape
    mem_len = kv3.shape[1]
    return pl.pallas_call(
        _merge_kernel,
        grid=(bsz, s_len // tm),
        in_specs=[
            pl.BlockSpec((1, tm, D_MODEL), lambda b, i: (b, i, 0)),
            pl.BlockSpec((1, tm, D_MODEL), lambda b, i: (b, i, 0)),
            pl.BlockSpec((1, tm, SSM_INNER), lambda b, i: (b, i, 0)),
            pl.BlockSpec((1, tm, D_MODEL), lambda b, i: (b, i, COL_QM // D_MODEL)),
            pl.BlockSpec((1, tm, 3 * D_MODEL), lambda b, i: (b, i, COL_GATE // (3 * D_MODEL))),
            pl.BlockSpec((1, mem_len, 2 * D_MODEL), lambda b, i: (b, 0, 0)),
            _resident((1, 3 * D_MODEL)),
            _resident((D_MODEL, D_MODEL)),
            _resident((SSM_INNER, D_MODEL)),
            _resident((D_MODEL, D_MODEL)),
            _resident((D_MODEL, D_MODEL)),
        ],
        out_specs=pl.BlockSpec((1, tm, D_MODEL), lambda b, i: (b, i, 0)),
        out_shape=jax.ShapeDtypeStruct((bsz, s_len, D_MODEL), F32),
        compiler_params=_cparams(("arbitrary", "arbitrary")),
        name="merge",
    )(x, o_attn, o_ssm, proj3, proj3, kv3, b_gate, wa, ws, wm, wo)


def _ffn_kernel(h_ref, ng_ref, wup_ref, cw_ref, cb_ref, wdn_ref, fg_ref, o_ref, hid_ref, *, tm):
    pad = SUBLANES

    @pl.when(pl.program_id(1) == 0)
    def _():
        hid_ref[0:pad, :] = jnp.zeros((pad, 2 * FFN_HIDDEN), F32)

    h = h_ref[0]
    ms = jnp.mean(h * h, axis=-1, keepdims=True)
    u = (h * lax.rsqrt(ms + NORM_EPS) * ng_ref[...]).astype(BF16)
    hid_ref[pad:pad + tm, :] = jnp.dot(u, wup_ref[...], preferred_element_type=F32)
    conv = cb_ref[...]
    for k in range(FFN_CONV):
        off = pad - (FFN_CONV - 1) + k
        conv = conv + hid_ref[off:off + tm, :] * cw_ref[k:k + 1, :]
    hid_ref[0:pad, :] = hid_ref[tm:tm + pad, :]
    act = (_silu(conv[:, :FFN_HIDDEN]) * conv[:, FFN_HIDDEN:]).astype(BF16)
    y = h + jnp.dot(act, wdn_ref[...], preferred_element_type=F32)
    ms2 = jnp.mean(y * y, axis=-1, keepdims=True)
    o_ref[0] = y * lax.rsqrt(ms2 + NORM_EPS) * fg_ref[...]


def _ffn(h, norm_g, w_up, conv_w, conv_b, w_down, final_g, tm):
    bsz, s_len, _ = h.shape
    f2 = 2 * FFN_HIDDEN
    return pl.pallas_call(
        functools.partial(_ffn_kernel, tm=tm),
        grid=(bsz, s_len // tm),
        in_specs=[
            pl.BlockSpec((1, tm, D_MODEL), lambda b, i: (b, i, 0)),
            _resident((1, D_MODEL)),
            _resident((D_MODEL, f2)),
            _resident((FFN_CONV, f2)),
            _resident((1, f2)),
            _resident((FFN_HIDDEN, D_MODEL)),
            _resident((1, D_MODEL)),
        ],
        out_specs=pl.BlockSpec((1, tm, D_MODEL), lambda b, i: (b, i, 0)),
        out_shape=jax.ShapeDtypeStruct((bsz, s_len, D_MODEL), F32),
        scratch_shapes=[pltpu.VMEM((tm + 2 * SUBLANES, f2), F32)],
        compiler_params=_cparams(("arbitrary", "arbitrary")),
        name="ffn",
    )(h, norm_g, w_up, conv_w, conv_b, w_down, final_g)


def _ssd_constants():
    L = SSM_CHUNK
    tril = np.tril(np.ones((L, L), np.float32))
    tril3 = np.concatenate([tril, tril, tril], axis=1)
    e = np.zeros((LANES, SSM_INNER), np.float32)
    for hd in range(SSM_HEADS):
        e[hd, hd * SSM_HEAD_DIM:(hd + 1) * SSM_HEAD_DIM] = 1.0
    e2 = np.concatenate([e, e], axis=0)
    shifts = np.stack([np.eye(L, k=-s, dtype=np.float32) for s in range(1, SSM_CONV)])
    return jnp.asarray(tril3, BF16), jnp.asarray(e2, BF16), jnp.asarray(shifts, BF16)


def _pad_lanes(v):
    return jnp.pad(v, ((0, 0), (0, LANES - v.shape[1])))


def _layer(h, mem, rel_bias, mix_norm_g, w_in, b_gate, ssm_conv_w, ssm_conv_b, ssm_dt_bias, ssm_A_log, ssm_D,
           ssm_norm_g, mem_norm_g, w_mem_kv, w_br_attn, w_br_ssm, w_br_mem, w_out, ffn_norm_g, w_ffn_up,
           ffn_conv_w, ffn_conv_b, w_ffn_down, final_g):
    bsz, s_len, _ = h.shape
    w_a = w_in
    w_b = w_in[:, IN_PROJ_SPLIT + SSM_DT_COLS:]
    w_dt = _pad_lanes(w_in[:, IN_PROJ_SPLIT:IN_PROJ_SPLIT + SSM_DT_COLS]).astype(BF16)

    proj, dt_raw, kmean = _in_proj(h.reshape(bsz * s_len, D_MODEL), mix_norm_g[None, :], w_a, w_b, w_dt,
                                   tm=IN_PROJ_TM, tn=IN_PROJ_TN)
    proj3 = proj.reshape(bsz, s_len, PROJ_WIDTH)
    dt3 = dt_raw.reshape(bsz, s_len, LANES)
    kmean3 = kmean.reshape(bsz, s_len // MOBA_BLOCK, ATTN_HEADS * ATTN_HEAD_DIM)

    o_attn = _moba(proj3, kmean3, rel_bias.T)

    tril3, e2, shifts = _ssd_constants()
    o_ssm = _ssd(proj3, dt3, ssm_conv_w, ssm_conv_b[None, :], _pad_lanes(ssm_dt_bias[None, :]),
                 _pad_lanes(ssm_A_log[None, :]), jnp.repeat(ssm_D, SSM_HEAD_DIM)[None, :], ssm_norm_g[None, :],
                 tril3, e2, shifts)

    mem_len = mem.shape[1]
    kv = _mem_kv(mem.reshape(bsz * mem_len, D_MODEL), mem_norm_g[None, :], w_mem_kv.astype(BF16))
    kv3 = kv.reshape(bsz, mem_len, 2 * D_MODEL)

    h1 = _merge(h, o_attn, o_ssm, proj3, kv3, b_gate[None, :], w_br_attn.astype(BF16), w_br_ssm.astype(BF16),
                w_br_mem.astype(BF16), w_out.astype(BF16), tm=MERGE_TM)

    return _ffn(h1, ffn_norm_g[None, :], w_ffn_up.astype(BF16), ffn_conv_w, ffn_conv_b[None, :],
                w_ffn_down.astype(BF16), final_g[None, :], tm=FFN_TM)


def kernel(x, mem, rel_bias, mix_norm_g, w_in, b_gate, ssm_conv_w, ssm_conv_b, ssm_dt_bias, ssm_A_log, ssm_D,
           ssm_norm_g, mem_norm_g, w_mem_kv, w_br_attn, w_br_ssm, w_br_mem, w_out, ffn_norm_g, w_ffn_up,
           ffn_conv_w, ffn_conv_b, w_ffn_down, final_norm_g):
    assert w_in.shape[0] == 1, "single-layer trunk"
    return _layer(x, mem, rel_bias, mix_norm_g[0], w_in[0], b_gate[0], ssm_conv_w[0], ssm_conv_b[0],
                  ssm_dt_bias[0], ssm_A_log[0], ssm_D[0], ssm_norm_g[0], mem_norm_g[0], w_mem_kv[0],
                  w_br_attn[0], w_br_ssm[0], w_br_mem[0], w_out[0], ffn_norm_g[0], w_ffn_up[0],
                  ffn_conv_w[0], ffn_conv_b[0], w_ffn_down[0], final_norm_g)
```

```python
import functools
import math

import numpy as np
import jax
import jax.numpy as jnp
from jax import lax
from jax.experimental import pallas as pl
from jax.experimental.pallas import tpu as pltpu

F32 = jnp.float32
BF16 = jnp.bfloat16

D_MODEL = 1024
ATTN_HEADS = 8
ATTN_HEAD_DIM = 128
MOBA_BLOCK = 256
MOBA_TOPK = 3
REL_BUCKETS = 32
REL_MAX_DIST = 1024
SSM_INNER = 2048
SSM_HEAD_DIM = 64
SSM_HEADS = 32
SSM_GROUPS = 4
SSM_STATE = 128
SSM_CONV = 4
SSM_CHUNK = 256
SSM_CONV_DIM = 3072
MEM_HEADS = 4
MEM_HEAD_DIM = 256
FFN_HIDDEN = 2816
FFN_CONV = 3
NORM_EPS = 1e-6

LANES = 128
SUBLANES = 8
VMEM_LIMIT = 56 * 1024 * 1024

IN_PROJ_TM = 2048
IN_PROJ_TN = 1024
MERGE_TM = 512
FFN_TM = 512

COL_XBC = 0
COL_GATE = 3072
COL_Z = 6144
COL_Q = 8192
COL_K = 9216
COL_V = 10240
COL_QM = 11264
PROJ_WIDTH = 12288
IN_PROJ_PERM = (8, 9, 10, 6, 7, 0, 1, 2, 11, 3, 4, 5)
IN_PROJ_K_TILE = 1
IN_PROJ_SPLIT = 8192
SSM_DT_COLS = 32

MOBA_NEAR = -(-(REL_MAX_DIST + MOBA_BLOCK - 1) // MOBA_BLOCK)
NEG_BIG = -1e30
LOG2E = math.log2(math.e)
MOBA_QBLOCKS = 2
MOBA_GROUP = 4
MOBA_VT_ROWS = ATTN_HEAD_DIM + 16
assert MOBA_GROUP <= MOBA_NEAR + 1


def _cparams(sem):
    return pltpu.CompilerParams(dimension_semantics=sem, vmem_limit_bytes=VMEM_LIMIT)


def _sigmoid(x):
    return 1.0 / (1.0 + jnp.exp(-x))


def _silu(x):
    h = 0.5 * x
    return h + h * jnp.tanh(h)


def _split_bf16(x, parts):
    out = []
    r = x
    for _ in range(parts):
        hi = r.astype(BF16)
        out.append(hi)
        r = r - hi.astype(F32)
    return out


def _inproj_kernel(perm_ref, x_ref, g_ref, wa_ref, wb_ref, wdt_ref, proj_ref, dt_ref, kmean_ref, u_ref, *,
                   n_a, k_tile):
    del perm_ref
    j = pl.program_id(1)

    @pl.when(j == 0)
    def _():
        x = x_ref[...]
        ms = jnp.mean(x * x, axis=-1, keepdims=True)
        u = (x * lax.rsqrt(ms + NORM_EPS) * g_ref[...]).astype(BF16)
        u_ref[...] = u
        dt_ref[...] = jnp.dot(u, wdt_ref[...], preferred_element_type=F32)

    def column_tile(w_ref):
        acc = jnp.dot(u_ref[...], w_ref[...], preferred_element_type=F32)
        proj_ref[...] = acc.astype(BF16)
        return acc

    @pl.when(j < n_a)
    def _():
        acc = column_tile(wa_ref)

        @pl.when(j == k_tile)
        def _():
            blk = MOBA_BLOCK
            for r in range(acc.shape[0] // blk):
                kmean_ref[0, r:r + 1, :] = jnp.sum(acc[r * blk:(r + 1) * blk], axis=0, keepdims=True) * (1.0 / blk)

    @pl.when(j >= n_a)
    def _():
        column_tile(wb_ref)


def _in_proj(x2, g, w_a, w_b, w_dt, tm, tn):
    t = x2.shape[0]
    kw = ATTN_HEADS * ATTN_HEAD_DIM
    n_a, n_b = w_a.shape[1] // tn, w_b.shape[1] // tn
    assert tn == kw and tm % MOBA_BLOCK == 0 and (n_a + n_b) * tn == PROJ_WIDTH == len(IN_PROJ_PERM) * tn
    grid_spec = pltpu.PrefetchScalarGridSpec(
        num_scalar_prefetch=1,
        grid=(t // tm, n_a + n_b),
        in_specs=[
            pl.BlockSpec((tm, D_MODEL), lambda i, j, perm: (i, 0)),
            pl.BlockSpec((1, D_MODEL), lambda i, j, perm: (0, 0)),
            pl.BlockSpec((D_MODEL, tn), lambda i, j, perm: (0, jnp.minimum(j, n_a - 1))),
            pl.BlockSpec((D_MODEL, tn), lambda i, j, perm: (0, jnp.maximum(j - n_a, 0))),
            pl.BlockSpec((D_MODEL, LANES), lambda i, j, perm: (0, 0)),
        ],
        out_specs=[
            pl.BlockSpec((tm, tn), lambda i, j, perm: (i, perm[j])),
            pl.BlockSpec((tm, LANES), lambda i, j, perm: (i, 0)),
            pl.BlockSpec((1, tm // MOBA_BLOCK, kw), lambda i, j, perm: (i, 0, 0)),
        ],
        scratch_shapes=[pltpu.VMEM((tm, D_MODEL), BF16)],
    )
    return pl.pallas_call(
        functools.partial(_inproj_kernel, n_a=n_a, k_tile=IN_PROJ_K_TILE),
        grid_spec=grid_spec,
        out_shape=[
            jax.ShapeDtypeStruct((t, PROJ_WIDTH), BF16),
            jax.ShapeDtypeStruct((t, LANES), F32),
            jax.ShapeDtypeStruct((t // tm, tm // MOBA_BLOCK, kw), F32),
        ],
        compiler_params=_cparams(("arbitrary", "arbitrary")),
        name="in_proj",
    )(jnp.asarray(IN_PROJ_PERM, jnp.int32), x2, g, w_a, w_b, w_dt)


def _t5_bucket(dist):
    n = jnp.maximum(dist, 0)
    max_exact = REL_BUCKETS // 2
    nf = jnp.maximum(n, max_exact).astype(F32)
    large = max_exact + (jnp.log(nf * (1.0 / max_exact)) / math.log(REL_MAX_DIST / max_exact)
                         * (REL_BUCKETS - max_exact)).astype(jnp.int32)
    large = jnp.minimum(large, REL_BUCKETS - 1)
    return jnp.where(n < max_exact, n, large)


def _moba_kernel(rel_ref, q_ref, k_ref, v_ref, km_ref, o_ref, kmean_ref, bias_ref, vt_ref, neg_ref, s_ref, m_ref,
                 acc_ref, *, nb):
    h = pl.program_id(0)
    b = pl.program_id(1)
    i = pl.program_id(2)
    blk = MOBA_BLOCK
    scale = ATTN_HEAD_DIM ** -0.5


    @pl.when((b == 0) & (i == 0))
    def _():
        key = lax.broadcasted_iota(jnp.int32, (blk, blk), 0)
        qry = lax.broadcasted_iota(jnp.int32, (blk, blk), 1)
        for d in range(MOBA_NEAR):
            bucket = _t5_bucket(d * blk + qry - key)
            tile = jnp.zeros((blk, blk), F32)
            for bk in range(REL_BUCKETS):
                tile = jnp.where(bucket == bk, rel_ref[h, bk], tile)
            bias_ref[d] = tile * LOG2E

    @pl.when(i == 0)
    def _():
        kmean_ref[...] = jnp.zeros_like(kmean_ref)
        kmean_ref[0:nb, :] = km_ref[0]
        ones_rows = (lax.broadcasted_iota(jnp.int32, (MOBA_VT_ROWS - ATTN_HEAD_DIM, blk), 0) == 0).astype(BF16)
        for jb in range(nb):
            vt_ref[jb, 0:ATTN_HEAD_DIM, :] = v_ref[0, jb * blk:(jb + 1) * blk, :].astype(F32).T.astype(BF16)
            vt_ref[jb, ATTN_HEAD_DIM:MOBA_VT_ROWS, :] = ones_rows

    qw = MOBA_QBLOCKS * blk
    i0 = i * MOBA_QBLOCKS
    q_t = q_ref[0].astype(F32).T
    qs_t = (q_t * (scale * LOG2E)).astype(BF16)

    gate = jnp.dot(kmean_ref[...].astype(BF16), q_t.astype(BF16), preferred_element_type=F32)
    nbp = kmean_ref.shape[0]
    row = lax.broadcasted_iota(jnp.int32, (nbp, qw), 0).astype(F32)
    own = (i0 + lax.broadcasted_iota(jnp.int32, (1, qw), 1) // blk).astype(F32)
    g = jnp.where(row < own, gate, -jnp.inf)
    sel = jnp.zeros((nbp, qw), F32)
    for t in range(MOBA_TOPK):
        mx = jnp.max(g, axis=0, keepdims=True)
        idx = jnp.min(jnp.where(g == mx, row, float(nbp)), axis=0, keepdims=True)
        hit = row == idx
        sel = jnp.maximum(sel, jnp.where(hit & (own > t), 1.0, 0.0))
        g = jnp.where(hit, -jnp.inf, g)
    neg_ref[...] = jnp.where(sel > 0.5, 0.0, NEG_BIG)

    def far_scores(j0):
        start = pl.multiple_of(j0 * blk, blk)
        return jnp.dot(k_ref[0, pl.ds(start, MOBA_GROUP * blk), :], qs_t, preferred_element_type=F32)

    s_ref[0] = far_scores(0)

    def softmax_group(scores, offsets, v_idx, m_old):
        m_new = m_old
        for sj, off in zip(scores, offsets):
            m_new = jnp.maximum(m_new, jnp.max(sj, axis=0, keepdims=True) + off)
        pv = jnp.zeros((MOBA_VT_ROWS, m_old.shape[1]), F32)
        for sj, off, vj in zip(scores, offsets, v_idx):
            pj = jnp.exp2(sj + (off - m_new))
            pv = pv + jnp.dot(vt_ref[vj], pj.astype(BF16), preferred_element_type=F32)
        return m_new, pv

    far_bias = rel_ref[h, REL_BUCKETS - 1] * LOG2E

    key = lax.broadcasted_iota(jnp.int32, (blk, blk), 0)
    qry = lax.broadcasted_iota(jnp.int32, (blk, blk), 1)
    scores, offsets, v_idx, first_lane = [], [], [], []
    for e in range(MOBA_NEAR - 1 + MOBA_QBLOCKS):
        jn = i0 - (MOBA_NEAR - 1) + e
        jc = jnp.maximum(jn, 0)
        start = pl.multiple_of(jc * blk, blk)
        w_min = max(0, e - (MOBA_NEAR - 1))
        lo = w_min * blk
        se = jnp.dot(k_ref[0, pl.ds(start, blk), :], qs_t[:, lo:], preferred_element_type=F32)
        sel_row = neg_ref[pl.ds(jnp.where(jn >= 0, jn, nbp - 1), 1), :]
        s_parts, o_parts = [], []
        for w in range(w_min, MOBA_QBLOCKS):
            d = w + (MOBA_NEAR - 1) - e
            sw = se[:, (w - w_min) * blk:(w - w_min + 1) * blk]
            ow = sel_row[:, w * blk:(w + 1) * blk]
            if d == 0:
                sw = jnp.where(qry >= key, sw + bias_ref[0], NEG_BIG)
                ow = jnp.zeros((1, blk), F32)
            elif d < MOBA_NEAR:
                sw = sw + bias_ref[d]
            else:
                ow = ow + far_bias
            s_parts.append(sw)
            o_parts.append(ow)
        scores.append(jnp.concatenate(s_parts, axis=1))
        offsets.append(jnp.concatenate(o_parts, axis=1))
        v_idx.append(jc)
        first_lane.append(lo)
    m_floor = jnp.concatenate(
        [jnp.max(scores[MOBA_NEAR - 1 + w][:, 0:blk], axis=0, keepdims=True) for w in range(MOBA_QBLOCKS)], axis=1)
    parts = []
    for sj, off, vj, lo in zip(scores, offsets, v_idx, first_lane):
        mu, pvu = softmax_group([sj], [off], [vj], m_floor[:, lo:])
        if lo:
            mu = jnp.concatenate([m_floor[:, :lo], mu], axis=1)
            pvu = jnp.concatenate([jnp.zeros((MOBA_VT_ROWS, lo), F32), pvu], axis=1)
        parts.append((mu, pvu))
    m0 = parts[0][0]
    for mu, _ in parts[1:]:
        m0 = jnp.maximum(m0, mu)
    pv0 = jnp.zeros((MOBA_VT_ROWS, qw), F32)
    for mu, pvu in parts:
        pv0 = pv0 + jnp.exp2(mu - m0) * pvu
    m_ref[...] = m0
    acc_ref[...] = pv0

    n_far = jnp.maximum(i0 - (MOBA_NEAR - 1), 0)

    n_groups = (n_far + MOBA_GROUP - 1) // MOBA_GROUP

    def far_group(gi, slot, prefetch):
        j0 = gi * MOBA_GROUP
        if prefetch:
            s_ref[1 - slot] = far_scores(j0 + MOBA_GROUP)
        scores, offsets, v_idx = [], [], []
        for u in range(MOBA_GROUP):
            ju = j0 + u
            scores.append(s_ref[slot, u * blk:(u + 1) * blk, :])
            offsets.append(neg_ref[pl.ds(jnp.where(ju < n_far, ju, nbp - 1), 1), :] + far_bias)
            v_idx.append(ju)
        m_old = m_ref[...]
        m_new, pv = softmax_group(scores, offsets, v_idx, m_old)
        acc_ref[...] = jnp.exp2(m_old - m_new) * acc_ref[...] + pv
        m_ref[...] = m_new

    def far_body(t, carry):
        far_group(2 * t, 0, True)
        far_group(2 * t + 1, 1, True)
        return carry

    n_pairs = jnp.maximum(n_groups - 1, 0) // 2
    lax.fori_loop(0, n_pairs, far_body, 0)
    g_rest = 2 * n_pairs
    n_rest = n_groups - g_rest

    @pl.when(n_rest == 1)
    def _():
        far_group(g_rest, 0, False)

    @pl.when(n_rest == 2)
    def _():
        far_group(g_rest, 0, True)
        far_group(g_rest + 1, 1, False)

    out_t = acc_ref[0:ATTN_HEAD_DIM, :] / acc_ref[ATTN_HEAD_DIM:ATTN_HEAD_DIM + 1, :]
    o_ref[0] = out_t.T.astype(o_ref.dtype)


def _moba(proj3, kmean3, rel_t):
    bsz, s_len, _ = proj3.shape
    blk = MOBA_BLOCK
    nb = s_len // blk
    qw = MOBA_QBLOCKS * blk
    assert nb * blk == s_len and nb % MOBA_QBLOCKS == 0 and nb >= MOBA_GROUP
    nbp = -(-(nb + 1) // 16) * 16
    cb = LANES
    return pl.pallas_call(
        functools.partial(_moba_kernel, nb=nb),
        grid=(ATTN_HEADS, bsz, nb // MOBA_QBLOCKS),
        in_specs=[
            pl.BlockSpec(memory_space=pltpu.SMEM),
            pl.BlockSpec((1, qw, cb), lambda h, b, i: (b, i, COL_Q // cb + h)),
            pl.BlockSpec((1, s_len, cb), lambda h, b, i: (b, 0, COL_K // cb + h)),
            pl.BlockSpec((1, s_len, cb), lambda h, b, i: (b, 0, COL_V // cb + h)),
            pl.BlockSpec((1, nb, cb), lambda h, b, i: (b, 0, h)),
        ],
        out_specs=pl.BlockSpec((1, qw, cb), lambda h, b, i: (b, i, h)),
        out_shape=jax.ShapeDtypeStruct((bsz, s_len, ATTN_HEADS * ATTN_HEAD_DIM), BF16),
        scratch_shapes=[
            pltpu.VMEM((nbp, ATTN_HEAD_DIM), F32),
            pltpu.VMEM((MOBA_NEAR, blk, blk), F32),
            pltpu.VMEM((nb, MOBA_VT_ROWS, blk), BF16),
            pltpu.VMEM((nbp, qw), F32),
            pltpu.VMEM((2, MOBA_GROUP * blk, qw), F32),
            pltpu.VMEM((1, qw), F32),
            pltpu.VMEM((MOBA_VT_ROWS, qw), F32),
        ],
        compiler_params=_cparams(("arbitrary", "arbitrary", "arbitrary")),
        name="moba",
    )(rel_t, proj3, proj3, proj3, kmean3)


def _ssd_kernel(xbc_ref, z_ref, dt_ref, cw_ref, cbias_ref, dtb_ref, alog_ref, dskip_ref, ng_ref,
                tril_ref, e_ref, shift_ref, o_ref, tail_ref, state_ref, xs_ref, xcb_ref, ea_ref, wend_ref, bm_ref,
                cm_ref):
    L = SSM_CHUNK
    G = SSM_GROUPS
    N = SSM_STATE
    GW = SSM_INNER // G
    pad = SUBLANES
    nt = (((1,), (1,)), ((), ()))

    @pl.when(pl.program_id(1) == 0)
    def _():
        tail_ref[...] = jnp.zeros_like(tail_ref)
        state_ref[...] = jnp.zeros_like(state_ref)

    dtr = dt_ref[0] + dtb_ref[...]
    dt = jnp.maximum(dtr, 0.0) + jnp.log1p(jnp.exp(-jnp.abs(dtr)))
    a = dt * (-jnp.exp(alog_ref[...]))
    a_cat = jnp.concatenate(_split_bf16(a, 3), axis=0)
    acs = jnp.dot(tril_ref[...], a_cat, preferred_element_type=F32) * LOG2E
    acs_t = acs.T

    dt_cat = jnp.concatenate(_split_bf16(dt, 2), axis=1)
    acs_cat = jnp.concatenate(_split_bf16(acs, 2), axis=1)

    SW = 2 * LANES
    for c in range(SSM_CONV_DIM // SW):
        cols = slice(c * SW, (c + 1) * SW)
        xb = xbc_ref[0, :, cols]
        x = xb.astype(F32)
        conv = cbias_ref[:, cols] + x * cw_ref[SSM_CONV - 1:SSM_CONV, cols]
        head = jnp.zeros((pad, SW), F32)
        for k in range(SSM_CONV - 1):
            s = SSM_CONV - 1 - k
            w_k = cw_ref[k:k + 1, cols]
            conv = conv + jnp.dot(shift_ref[s - 1], xb, preferred_element_type=F32) * w_k
            head = head + tail_ref[pad - s:2 * pad - s, cols] * w_k
        conv = jnp.concatenate([conv[0:pad] + head, conv[pad:]], axis=0)
        tail_ref[0:pad, cols] = x[L - pad:L]
        xa = _silu(conv)
        if c * SW < SSM_INNER:
            dt_x = jnp.dot(dt_cat, e_ref[:, cols], preferred_element_type=F32)
            acs_x = jnp.dot(acs_cat, e_ref[:, cols], preferred_element_type=F32)
            xc = xa * dt_x
            xs_ref[:, cols] = xa
            xcb_ref[:, cols] = xc.astype(BF16)
            ea_ref[:, cols] = jnp.exp2(acs_x)
            wend_ref[:, cols] = (jnp.exp2(acs_x[L - 1:L, :] - acs_x) * xc).astype(BF16)
        elif c * SW < SSM_INNER + G * N:
            bm_ref[:, c * SW - SSM_INNER:(c + 1) * SW - SSM_INNER] = xa
        else:
            cm_ref[:, c * SW - SSM_INNER - G * N:(c + 1) * SW - SSM_INNER - G * N] = xa.astype(BF16)

    row = lax.broadcasted_iota(jnp.int32, (L, L), 0)
    col = lax.broadcasted_iota(jnp.int32, (L, L), 1)
    causal = row >= col
    lane = lax.broadcasted_iota(jnp.int32, (L, LANES), 1)
    low_half = lane < SSM_HEAD_DIM

    for g in range(G):
        gcols = slice(g * GW, (g + 1) * GW)
        bg = bm_ref[:, g * N:(g + 1) * N]
        cg = cm_ref[:, g * N:(g + 1) * N]
        cb = lax.dot_general(cg, bg.astype(BF16), nt, preferred_element_type=F32)
        st = state_ref[g]
        ea_g = ea_ref[:, gcols]
        y_off = jnp.dot(cg, st.astype(BF16), preferred_element_type=F32) * ea_g
        y_parts = []
        for pr in range(GW // LANES):
            c0 = g * GW + pr * LANES
            x_pair = xcb_ref[:, c0:c0 + LANES]
            zero = jnp.zeros_like(x_pair)
            y_pair = y_off[:, pr * LANES:(pr + 1) * LANES]
            for half in range(2):
                hd = c0 // SSM_HEAD_DIM + half
                seg = acs[:, hd:hd + 1] - acs_t[hd:hd + 1, :]
                decay = jnp.exp2(jnp.where(causal, seg, -jnp.inf))
                mmat = (cb * decay).astype(BF16)
                x_half = jnp.where(low_half, x_pair, zero) if half == 0 else jnp.where(low_half, zero, x_pair)
                y_pair = y_pair + jnp.dot(mmat, x_half, preferred_element_type=F32)
            y_parts.append(y_pair)
        bg_t = bg.T.astype(BF16)
        state_ref[g] = (st * ea_g[L - 1:L, :]
                        + jnp.dot(bg_t, wend_ref[:, gcols], preferred_element_type=F32))

        y = jnp.concatenate(y_parts, axis=1) + dskip_ref[:, gcols] * xs_ref[:, gcols]
        v = y * _silu(z_ref[0, :, gcols].astype(F32))
        ms = jnp.mean(v * v, axis=-1, keepdims=True)
        o_ref[0, :, gcols] = (v * lax.rsqrt(ms + NORM_EPS) * ng_ref[:, gcols]).astype(o_ref.dtype)


def _ssd(proj3, dt3, conv_w, conv_b, dt_bias, a_log, d_skip_x, norm_g, tril3, e2, shifts):
    bsz, s_len, _ = proj3.shape
    L = SSM_CHUNK
    nc = s_len // L
    assert nc * L == s_len
    const = lambda b, c: (0, 0)
    return pl.pallas_call(
        _ssd_kernel,
        grid=(bsz, nc),
        in_specs=[
            pl.BlockSpec((1, L, SSM_CONV_DIM), lambda b, c: (b, c, COL_XBC // SSM_CONV_DIM)),
            pl.BlockSpec((1, L, SSM_INNER), lambda b, c: (b, c, COL_Z // SSM_INNER)),
            pl.BlockSpec((1, L, LANES), lambda b, c: (b, c, 0)),
            pl.BlockSpec((SSM_CONV, SSM_CONV_DIM), const),
            pl.BlockSpec((1, SSM_CONV_DIM), const),
            pl.BlockSpec((1, LANES), const),
            pl.BlockSpec((1, LANES), const),
            pl.BlockSpec((1, SSM_INNER), const),
            pl.BlockSpec((1, SSM_INNER), const),
            pl.BlockSpec((L, 3 * L), const),
            pl.BlockSpec((2 * LANES, SSM_INNER), const),
            pl.BlockSpec((SSM_CONV - 1, L, L), lambda b, c: (0, 0, 0)),
        ],
        out_specs=pl.BlockSpec((1, L, SSM_INNER), lambda b, c: (b, c, 0)),
        out_shape=jax.ShapeDtypeStruct((bsz, s_len, SSM_INNER), BF16),
        scratch_shapes=[
            pltpu.VMEM((2 * SUBLANES, SSM_CONV_DIM), F32),
            pltpu.VMEM((SSM_GROUPS, SSM_STATE, SSM_INNER // SSM_GROUPS), F32),
            pltpu.VMEM((L, SSM_INNER), F32),
            pltpu.VMEM((L, SSM_INNER), BF16),
            pltpu.VMEM((L, SSM_INNER), F32),
            pltpu.VMEM((L, SSM_INNER), BF16),
            pltpu.VMEM((L, SSM_GROUPS * SSM_STATE), F32),
            pltpu.VMEM((L, SSM_GROUPS * SSM_STATE), BF16),
        ],
        compiler_params=_cparams(("arbitrary", "arbitrary")),
        name="ssd",
    )(proj3, proj3, dt3, conv_w, conv_b, dt_bias, a_log, d_skip_x, norm_g, tril3, e2, shifts)


def _memkv_kernel(mem_ref, g_ref, w_ref, kv_ref):
    x = mem_ref[...]
    ms = jnp.mean(x * x, axis=-1, keepdims=True)
    u = (x * lax.rsqrt(ms + NORM_EPS) * g_ref[...]).astype(BF16)
    kv_ref[...] = jnp.dot(u, w_ref[...], preferred_element_type=F32).astype(BF16)


def _mem_kv(mem2, g, w_kv):
    rows = mem2.shape[0]
    width = w_kv.shape[1]
    return pl.pallas_call(
        _memkv_kernel,
        grid=(1,),
        in_specs=[
            pl.BlockSpec((rows, D_MODEL), lambda i: (0, 0)),
            pl.BlockSpec((1, D_MODEL), lambda i: (0, 0)),
            pl.BlockSpec((D_MODEL, width), lambda i: (0, 0)),
        ],
        out_specs=pl.BlockSpec((rows, width), lambda i: (0, 0)),
        out_shape=jax.ShapeDtypeStruct((rows, width), BF16),
        compiler_params=_cparams(("arbitrary",)),
        name="mem_kv",
    )(mem2, g, w_kv)


def _merge_kernel(x_ref, oa_ref, os_ref, qm_ref, gl_ref, kv_ref, bg_ref, wa_ref, ws_ref, wm_ref, wo_ref, h_ref):
    nt = (((1,), (1,)), ((), ()))
    hd = MEM_HEAD_DIM
    width = MEM_HEADS * hd
    scale = hd ** -0.5
    qm = qm_ref[0]
    kv = kv_ref[0]
    outs = []
    for hh in range(MEM_HEADS):
        q = qm[:, hh * hd:(hh + 1) * hd]
        km = kv[:, hh * hd:(hh + 1) * hd]
        vm = kv[:, width + hh * hd:width + (hh + 1) * hd]
        s = lax.dot_general(q, km, nt, preferred_element_type=F32) * scale
        p = jnp.exp(s - jnp.max(s, axis=1, keepdims=True))
        o = jnp.dot(p.astype(BF16), vm, preferred_element_type=F32)
        outs.append(o / jnp.sum(p, axis=1, keepdims=True))
    o_mem = jnp.concatenate(outs, axis=1).astype(BF16)

    gates = _sigmoid(gl_ref[0].astype(F32) + bg_ref[...])
    merged = (gates[:, :D_MODEL] * jnp.dot(oa_ref[0], wa_ref[...], preferred_element_type=F32)
              + gates[:, D_MODEL:2 * D_MODEL] * jnp.dot(os_ref[0], ws_ref[...], preferred_element_type=F32)
              + gates[:, 2 * D_MODEL:] * jnp.dot(o_mem, wm_ref[...], preferred_element_type=F32))
    h_ref[0] = x_ref[0] + jnp.dot(merged.astype(BF16), wo_ref[...], preferred_element_type=F32)


def _resident(shape):
    return pl.BlockSpec(shape, lambda *_: (0,) * len(shape), pipeline_mode=pl.Buffered(1))


def _merge(x, o_attn, o_ssm, proj3, kv3, b_gate, wa, ws, wm, wo, tm):
    bsz, s_len, _ = x.shape
    mem_len = kv3.shape[1]
    return pl.pallas_call(
        _merge_kernel,
        grid=(bsz, s_len // tm),
        in_specs=[
            pl.BlockSpec((1, tm, D_MODEL), lambda b, i: (b, i, 0)),
            pl.BlockSpec((1, tm, D_MODEL), lambda b, i: (b, i, 0)),
            pl.BlockSpec((1, tm, SSM_INNER), lambda b, i: (b, i, 0)),
            pl.BlockSpec((1, tm, D_MODEL), lambda b, i: (b, i, COL_QM // D_MODEL)),
            pl.BlockSpec((1, tm, 3 * D_MODEL), lambda b, i: (b, i, COL_GATE // (3 * D_MODEL))),
            pl.BlockSpec((1, mem_len, 2 * D_MODEL), lambda b, i: (b, 0, 0)),
            _resident((1, 3 * D_MODEL)),
            _resident((D_MODEL, D_MODEL)),
            _resident((SSM_INNER, D_MODEL)),
            _resident((D_MODEL, D_MODEL)),
            _resident((D_MODEL, D_MODEL)),
        ],
        out_specs=pl.BlockSpec((1, tm, D_MODEL), lambda b, i: (b, i, 0)),
        out_shape=jax.ShapeDtypeStruct((bsz, s_len, D_MODEL), F32),
        compiler_params=_cparams(("arbitrary", "arbitrary")),
        name="merge",
    )(x, o_attn, o_ssm, proj3, proj3, kv3, b_gate, wa, ws, wm, wo)


def _ffn_kernel(h_ref, ng_ref, wup_ref, cw_ref, cb_ref, wdn_ref, fg_ref, o_ref, hid_ref, *, tm):
    pad = SUBLANES

    @pl.when(pl.program_id(1) == 0)
    def _():
        hid_ref[0:pad, :] = jnp.zeros((pad, 2 * FFN_HIDDEN), F32)

    h = h_ref[0]
    ms = jnp.mean(h * h, axis=-1, keepdims=True)
    u = (h * lax.rsqrt(ms + NORM_EPS) * ng_ref[...]).astype(BF16)
    hid_ref[pad:pad + tm, :] = jnp.dot(u, wup_ref[...], preferred_element_type=F32)
    conv = cb_ref[...]
    for k in range(FFN_CONV):
        off = pad - (FFN_CONV - 1) + k
        conv = conv + hid_ref[off:off + tm, :] * cw_ref[k:k + 1, :]
    hid_ref[0:pad, :] = hid_ref[tm:tm + pad, :]
    act = (_silu(conv[:, :FFN_HIDDEN]) * conv[:, FFN_HIDDEN:]).astype(BF16)
    y = h + jnp.dot(act, wdn_ref[...], preferred_element_type=F32)
    ms2 = jnp.mean(y * y, axis=-1, keepdims=True)
    o_ref[0] = y * lax.rsqrt(ms2 + NORM_EPS) * fg_ref[...]


def _ffn(h, norm_g, w_up, conv_w, conv_b, w_down, final_g, tm):
    bsz, s_len, _ = h.shape
    f2 = 2 * FFN_HIDDEN
    return pl.pallas_call(
        functools.partial(_ffn_kernel, tm=tm),
        grid=(bsz, s_len // tm),
        in_specs=[
            pl.BlockSpec((1, tm, D_MODEL), lambda b, i: (b, i, 0)),
            _resident((1, D_MODEL)),
            _resident((D_MODEL, f2)),
            _resident((FFN_CONV, f2)),
            _resident((1, f2)),
            _resident((FFN_HIDDEN, D_MODEL)),
            _resident((1, D_MODEL)),
        ],
        out_specs=pl.BlockSpec((1, tm, D_MODEL), lambda b, i: (b, i, 0)),
        out_shape=jax.ShapeDtypeStruct((bsz, s_len, D_MODEL), F32),
        scratch_shapes=[pltpu.VMEM((tm + 2 * SUBLANES, f2), F32)],
        compiler_params=_cparams(("arbitrary", "arbitrary")),
        name="ffn",
    )(h, norm_g, w_up, conv_w, conv_b, w_down, final_g)


def _ssd_constants():
    L = SSM_CHUNK
    tril = np.tril(np.ones((L, L), np.float32))
    tril3 = np.concatenate([tril, tril, tril], axis=1)
    e = np.zeros((LANES, SSM_INNER), np.float32)
    for hd in range(SSM_HEADS):
        e[hd, hd * SSM_HEAD_DIM:(hd + 1) * SSM_HEAD_DIM] = 1.0
    e2 = np.concatenate([e, e], axis=0)
    shifts = np.stack([np.eye(L, k=-s, dtype=np.float32) for s in range(1, SSM_CONV)])
    return jnp.asarray(tril3, BF16), jnp.asarray(e2, BF16), jnp.asarray(shifts, BF16)


def _pad_lanes(v):
    return jnp.pad(v, ((0, 0), (0, LANES - v.shape[1])))


def _layer(h, mem, rel_bias, mix_norm_g, w_in, b_gate, ssm_conv_w, ssm_conv_b, ssm_dt_bias, ssm_A_log, ssm_D,
           ssm_norm_g, mem_norm_g, w_mem_kv, w_br_attn, w_br_ssm, w_br_mem, w_out, ffn_norm_g, w_ffn_up,
           ffn_conv_w, ffn_conv_b, w_ffn_down, final_g):
    bsz, s_len, _ = h.shape
    w_a = w_in[:, :IN_PROJ_SPLIT].astype(BF16)
    w_b = w_in[:, IN_PROJ_SPLIT + SSM_DT_COLS:].astype(BF16)
    w_dt = _pad_lanes(w_in[:, IN_PROJ_SPLIT:IN_PROJ_SPLIT + SSM_DT_COLS]).astype(BF16)

    proj, dt_raw, kmean = _in_proj(h.reshape(bsz * s_len, D_MODEL), mix_norm_g[None, :], w_a, w_b, w_dt,
                                   tm=IN_PROJ_TM, tn=IN_PROJ_TN)
    proj3 = proj.reshape(bsz, s_len, PROJ_WIDTH)
    dt3 = dt_raw.reshape(bsz, s_len, LANES)
    kmean3 = kmean.reshape(bsz, s_len // MOBA_BLOCK, ATTN_HEADS * ATTN_HEAD_DIM)

    o_attn = _moba(proj3, kmean3, rel_bias.T)

    tril3, e2, shifts = _ssd_constants()
    o_ssm = _ssd(proj3, dt3, ssm_conv_w, ssm_conv_b[None, :], _pad_lanes(ssm_dt_bias[None, :]),
                 _pad_lanes(ssm_A_log[None, :]), jnp.repeat(ssm_D, SSM_HEAD_DIM)[None, :], ssm_norm_g[None, :],
                 tril3, e2, shifts)

    mem_len = mem.shape[1]
    kv = _mem_kv(mem.reshape(bsz * mem_len, D_MODEL), mem_norm_g[None, :], w_mem_kv.astype(BF16))
    kv3 = kv.reshape(bsz, mem_len, 2 * D_MODEL)

    h1 = _merge(h, o_attn, o_ssm, proj3, kv3, b_gate[None, :], w_br_attn.astype(BF16), w_br_ssm.astype(BF16),
                w_br_mem.astype(BF16), w_out.astype(BF16), tm=MERGE_TM)

    return _ffn(h1, ffn_norm_g[None, :], w_ffn_up.astype(BF16), ffn_conv_w, ffn_conv_b[None, :],
                w_ffn_down.astype(BF16), final_g[None, :], tm=FFN_TM)


def kernel(x, mem, rel_bias, mix_norm_g, w_in, b_gate, ssm_conv_w, ssm_conv_b, ssm_dt_bias, ssm_A_log, ssm_D,
           ssm_norm_g, mem_norm_g, w_mem_kv, w_br_attn, w_br_ssm, w_br_mem, w_out, ffn_norm_g, w_ffn_up,
           ffn_conv_w, ffn_conv_b, w_ffn_down, final_norm_g):
    assert w_in.shape[0] == 1, "single-layer trunk"
    return _layer(x, mem, rel_bias, mix_norm_g[0], w_in[0], b_gate[0], ssm_conv_w[0], ssm_conv_b[0],
                  ssm_dt_bias[0], ssm_A_log[0], ssm_D[0], ssm_norm_g[0], mem_norm_g[0], w_mem_kv[0],
                  w_br_attn[0], w_br_ssm[0], w_br_mem[0], w_out[0], ffn_norm_g[0], w_ffn_up[0],
                  ffn_conv_w[0], ffn_conv_b[0], w_ffn_down[0], final_norm_g)
```

```python
import functools
import math

import numpy as np
import jax
import jax.numpy as jnp
from jax import lax
from jax.experimental import pallas as pl
from jax.experimental.pallas import tpu as pltpu

F32 = jnp.float32
BF16 = jnp.bfloat16

D_MODEL = 1024
ATTN_HEADS = 8
ATTN_HEAD_DIM = 128
MOBA_BLOCK = 256
MOBA_TOPK = 3
REL_BUCKETS = 32
REL_MAX_DIST = 1024
SSM_INNER = 2048
SSM_HEAD_DIM = 64
SSM_HEADS = 32
SSM_GROUPS = 4
SSM_STATE = 128
SSM_CONV = 4
SSM_CHUNK = 256
SSM_CONV_DIM = 3072
MEM_HEADS = 4
MEM_HEAD_DIM = 256
FFN_HIDDEN = 2816
FFN_CONV = 3
NORM_EPS = 1e-6

LANES = 128
SUBLANES = 8
VMEM_LIMIT = 56 * 1024 * 1024

IN_PROJ_TM = 2048
IN_PROJ_TN = 1024
MERGE_TM = 512
FFN_TM = 512

COL_XBC = 0
COL_GATE = 3072
COL_Z = 6144
COL_Q = 8192
COL_K = 9216
COL_V = 10240
COL_QM = 11264
PROJ_WIDTH = 12288
IN_PROJ_PERM = (8, 9, 10, 6, 7, 0, 1, 2, 11, 3, 4, 5)
IN_PROJ_K_TILE = 1
IN_PROJ_SPLIT = 8192
SSM_DT_COLS = 32

MOBA_NEAR = -(-(REL_MAX_DIST + MOBA_BLOCK - 1) // MOBA_BLOCK)
NEG_BIG = -1e30
LOG2E = math.log2(math.e)
MOBA_QBLOCKS = 2
MOBA_GROUP = 4
MOBA_VT_ROWS = ATTN_HEAD_DIM + 16
assert MOBA_GROUP <= MOBA_NEAR + 1
assert MOBA_GROUP <= SUBLANES


def _cparams(sem):
    return pltpu.CompilerParams(dimension_semantics=sem, vmem_limit_bytes=VMEM_LIMIT)


def _sigmoid(x):
    return 1.0 / (1.0 + jnp.exp(-x))


def _silu(x):
    h = 0.5 * x
    return h + h * jnp.tanh(h)


def _split_bf16(x, parts):
    out = []
    r = x
    for _ in range(parts):
        hi = r.astype(BF16)
        out.append(hi)
        r = r - hi.astype(F32)
    return out


def _inproj_kernel(perm_ref, x_ref, g_ref, wa_ref, wb_ref, wdt_ref, proj_ref, dt_ref, kmean_ref, u_ref, *,
                   n_a, k_tile):
    del perm_ref
    j = pl.program_id(1)

    @pl.when(j == 0)
    def _():
        x = x_ref[...]
        ms = jnp.mean(x * x, axis=-1, keepdims=True)
        u = (x * lax.rsqrt(ms + NORM_EPS) * g_ref[...]).astype(BF16)
        u_ref[...] = u
        dt_ref[...] = jnp.dot(u, wdt_ref[...], preferred_element_type=F32)

    def column_tile(w_ref):
        acc = jnp.dot(u_ref[...], w_ref[...], preferred_element_type=F32)
        proj_ref[...] = acc.astype(BF16)
        return acc

    @pl.when(j < n_a)
    def _():
        acc = column_tile(wa_ref)

        @pl.when(j == k_tile)
        def _():
            blk = MOBA_BLOCK
            for r in range(acc.shape[0] // blk):
                kmean_ref[0, r:r + 1, :] = jnp.sum(acc[r * blk:(r + 1) * blk], axis=0, keepdims=True) * (1.0 / blk)

    @pl.when(j >= n_a)
    def _():
        column_tile(wb_ref)


def _in_proj(x2, g, w_a, w_b, w_dt, tm, tn):
    t = x2.shape[0]
    kw = ATTN_HEADS * ATTN_HEAD_DIM
    n_a, n_b = w_a.shape[1] // tn, w_b.shape[1] // tn
    assert tn == kw and tm % MOBA_BLOCK == 0 and (n_a + n_b) * tn == PROJ_WIDTH == len(IN_PROJ_PERM) * tn
    grid_spec = pltpu.PrefetchScalarGridSpec(
        num_scalar_prefetch=1,
        grid=(t // tm, n_a + n_b),
        in_specs=[
            pl.BlockSpec((tm, D_MODEL), lambda i, j, perm: (i, 0)),
            pl.BlockSpec((1, D_MODEL), lambda i, j, perm: (0, 0)),
            pl.BlockSpec((D_MODEL, tn), lambda i, j, perm: (0, jnp.minimum(j, n_a - 1))),
            pl.BlockSpec((D_MODEL, tn), lambda i, j, perm: (0, jnp.maximum(j - n_a, 0))),
            pl.BlockSpec((D_MODEL, LANES), lambda i, j, perm: (0, 0)),
        ],
        out_specs=[
            pl.BlockSpec((tm, tn), lambda i, j, perm: (i, perm[j])),
            pl.BlockSpec((tm, LANES), lambda i, j, perm: (i, 0)),
            pl.BlockSpec((1, tm // MOBA_BLOCK, kw), lambda i, j, perm: (i, 0, 0)),
        ],
        scratch_shapes=[pltpu.VMEM((tm, D_MODEL), BF16)],
    )
    return pl.pallas_call(
        functools.partial(_inproj_kernel, n_a=n_a, k_tile=IN_PROJ_K_TILE),
        grid_spec=grid_spec,
        out_shape=[
            jax.ShapeDtypeStruct((t, PROJ_WIDTH), BF16),
            jax.ShapeDtypeStruct((t, LANES), F32),
            jax.ShapeDtypeStruct((t // tm, tm // MOBA_BLOCK, kw), F32),
        ],
        compiler_params=_cparams(("arbitrary", "arbitrary")),
        name="in_proj",
    )(jnp.asarray(IN_PROJ_PERM, jnp.int32), x2, g, w_a, w_b, w_dt)


def _t5_bucket(dist):
    n = jnp.maximum(dist, 0)
    max_exact = REL_BUCKETS // 2
    nf = jnp.maximum(n, max_exact).astype(F32)
    large = max_exact + (jnp.log(nf * (1.0 / max_exact)) / math.log(REL_MAX_DIST / max_exact)
                         * (REL_BUCKETS - max_exact)).astype(jnp.int32)
    large = jnp.minimum(large, REL_BUCKETS - 1)
    return jnp.where(n < max_exact, n, large)


def _moba_kernel(rel_ref, q_ref, k_ref, v_ref, km_ref, o_ref, kmean_ref, bias_ref, vt_ref, neg_ref, s_ref, cm_ref,
                 m_ref, acc_ref, *, nb):
    h = pl.program_id(0)
    b = pl.program_id(1)
    i = pl.program_id(2)
    blk = MOBA_BLOCK
    scale = ATTN_HEAD_DIM ** -0.5


    @pl.when((b == 0) & (i == 0))
    def _():
        key = lax.broadcasted_iota(jnp.int32, (blk, blk), 0)
        qry = lax.broadcasted_iota(jnp.int32, (blk, blk), 1)
        for d in range(MOBA_NEAR):
            bucket = _t5_bucket(d * blk + qry - key)
            tile = jnp.zeros((blk, blk), F32)
            for bk in range(REL_BUCKETS):
                tile = jnp.where(bucket == bk, rel_ref[h, bk], tile)
            bias_ref[d] = tile * LOG2E

    @pl.when(i == 0)
    def _():
        kmean_ref[...] = jnp.zeros_like(kmean_ref)
        kmean_ref[0:nb, :] = km_ref[0]
        ones_rows = (lax.broadcasted_iota(jnp.int32, (MOBA_VT_ROWS - ATTN_HEAD_DIM, blk), 0) == 0).astype(BF16)
        for jb in range(nb):
            vt_ref[jb, 0:ATTN_HEAD_DIM, :] = v_ref[0, jb * blk:(jb + 1) * blk, :].astype(F32).T.astype(BF16)
            vt_ref[jb, ATTN_HEAD_DIM:MOBA_VT_ROWS, :] = ones_rows

    qw = MOBA_QBLOCKS * blk
    i0 = i * MOBA_QBLOCKS
    q_t = q_ref[0].astype(F32).T
    qs_t = (q_t * (scale * LOG2E)).astype(BF16)

    gate = jnp.dot(kmean_ref[...].astype(BF16), q_t.astype(BF16), preferred_element_type=F32)
    nbp = kmean_ref.shape[0]
    row = lax.broadcasted_iota(jnp.int32, (nbp, qw), 0).astype(F32)
    own = (i0 + lax.broadcasted_iota(jnp.int32, (1, qw), 1) // blk).astype(F32)
    g = jnp.where(row < own, gate, -jnp.inf)
    sel = jnp.zeros((nbp, qw), F32)
    for t in range(MOBA_TOPK):
        mx = jnp.max(g, axis=0, keepdims=True)
        idx = jnp.min(jnp.where(g == mx, row, float(nbp)), axis=0, keepdims=True)
        hit = row == idx
        sel = jnp.maximum(sel, jnp.where(hit & (own > t), 1.0, 0.0))
        g = jnp.where(hit, -jnp.inf, g)
    neg_ref[...] = jnp.where(sel > 0.5, 0.0, NEG_BIG)

    def far_scores(j0, slot):
        start = pl.multiple_of(j0 * blk, blk)
        s = jnp.dot(k_ref[0, pl.ds(start, MOBA_GROUP * blk), :], qs_t, preferred_element_type=F32)
        s_ref[slot] = s
        for u in range(MOBA_GROUP):
            cm_ref[slot, u:u + 1, :] = jnp.max(s[u * blk:(u + 1) * blk], axis=0, keepdims=True)

    far_scores(0, 0)

    def softmax_group(scores, offsets, v_idx, m_old, col_max=None):
        if col_max is None:
            col_max = [jnp.max(sj, axis=0, keepdims=True) for sj in scores]
        m_new = m_old
        for cm, off in zip(col_max, offsets):
            m_new = jnp.maximum(m_new, cm + off)
        pv = jnp.zeros((MOBA_VT_ROWS, m_old.shape[1]), F32)
        for sj, off, vj in zip(scores, offsets, v_idx):
            pj = jnp.exp2(sj + (off - m_new))
            pv = pv + jnp.dot(vt_ref[vj], pj.astype(BF16), preferred_element_type=F32)
        return m_new, pv

    far_bias = rel_ref[h, REL_BUCKETS - 1] * LOG2E

    key = lax.broadcasted_iota(jnp.int32, (blk, blk), 0)
    qry = lax.broadcasted_iota(jnp.int32, (blk, blk), 1)
    scores, offsets, v_idx, first_lane = [], [], [], []
    for e in range(MOBA_NEAR - 1 + MOBA_QBLOCKS):
        jn = i0 - (MOBA_NEAR - 1) + e
        jc = jnp.maximum(jn, 0)
        start = pl.multiple_of(jc * blk, blk)
        w_min = max(0, e - (MOBA_NEAR - 1))
        lo = w_min * blk
        se = jnp.dot(k_ref[0, pl.ds(start, blk), :], qs_t[:, lo:], preferred_element_type=F32)
        sel_row = neg_ref[pl.ds(jnp.where(jn >= 0, jn, nbp - 1), 1), :]
        s_parts, o_parts = [], []
        for w in range(w_min, MOBA_QBLOCKS):
            d = w + (MOBA_NEAR - 1) - e
            sw = se[:, (w - w_min) * blk:(w - w_min + 1) * blk]
            ow = sel_row[:, w * blk:(w + 1) * blk]
            if d == 0:
                sw = jnp.where(qry >= key, sw + bias_ref[0], NEG_BIG)
                ow = jnp.zeros((1, blk), F32)
            elif d < MOBA_NEAR:
                sw = sw + bias_ref[d]
            else:
                ow = ow + far_bias
            s_parts.append(sw)
            o_parts.append(ow)
        scores.append(jnp.concatenate(s_parts, axis=1))
        offsets.append(jnp.concatenate(o_parts, axis=1))
        v_idx.append(jc)
        first_lane.append(lo)
    m_floor = jnp.concatenate(
        [jnp.max(scores[MOBA_NEAR - 1 + w][:, 0:blk], axis=0, keepdims=True) for w in range(MOBA_QBLOCKS)], axis=1)
    parts = []
    for sj, off, vj, lo in zip(scores, offsets, v_idx, first_lane):
        mu, pvu = softmax_group([sj], [off], [vj], m_floor[:, lo:])
        if lo:
            mu = jnp.concatenate([m_floor[:, :lo], mu], axis=1)
            pvu = jnp.concatenate([jnp.zeros((MOBA_VT_ROWS, lo), F32), pvu], axis=1)
        parts.append((mu, pvu))
    m0 = parts[0][0]
    for mu, _ in parts[1:]:
        m0 = jnp.maximum(m0, mu)
    pv0 = jnp.zeros((MOBA_VT_ROWS, qw), F32)
    for mu, pvu in parts:
        pv0 = pv0 + jnp.exp2(mu - m0) * pvu
    m_ref[...] = m0
    acc_ref[...] = pv0

    n_far = jnp.maximum(i0 - (MOBA_NEAR - 1), 0)

    n_groups = (n_far + MOBA_GROUP - 1) // MOBA_GROUP

    def far_group(gi, slot, prefetch):
        j0 = gi * MOBA_GROUP
        if prefetch:
            far_scores(j0 + MOBA_GROUP, 1 - slot)
        scores, offsets, v_idx, col_max = [], [], [], []
        for u in range(MOBA_GROUP):
            ju = j0 + u
            scores.append(s_ref[slot, u * blk:(u + 1) * blk, :])
            col_max.append(cm_ref[slot, u:u + 1, :])
            offsets.append(neg_ref[pl.ds(jnp.where(ju < n_far, ju, nbp - 1), 1), :] + far_bias)
            v_idx.append(ju)
        m_old = m_ref[...]
        m_new, pv = softmax_group(scores, offsets, v_idx, m_old, col_max)
        acc_ref[...] = jnp.exp2(m_old - m_new) * acc_ref[...] + pv
        m_ref[...] = m_new

    def far_body(t, carry):
        far_group(2 * t, 0, True)
        far_group(2 * t + 1, 1, True)
        return carry

    n_pairs = jnp.maximum(n_groups - 1, 0) // 2
    lax.fori_loop(0, n_pairs, far_body, 0)
    g_rest = 2 * n_pairs
    n_rest = n_groups - g_rest

    @pl.when(n_rest == 1)
    def _():
        far_group(g_rest, 0, False)

    @pl.when(n_rest == 2)
    def _():
        far_group(g_rest, 0, True)
        far_group(g_rest + 1, 1, False)

    out_t = acc_ref[0:ATTN_HEAD_DIM, :] / acc_ref[ATTN_HEAD_DIM:ATTN_HEAD_DIM + 1, :]
    o_ref[0] = out_t.T.astype(o_ref.dtype)


def _moba(proj3, kmean3, rel_t):
    bsz, s_len, _ = proj3.shape
    blk = MOBA_BLOCK
    nb = s_len // blk
    qw = MOBA_QBLOCKS * blk
    assert nb * blk == s_len and nb % MOBA_QBLOCKS == 0 and nb >= MOBA_GROUP
    nbp = -(-(nb + 1) // 16) * 16
    cb = LANES
    return pl.pallas_call(
        functools.partial(_moba_kernel, nb=nb),
        grid=(ATTN_HEADS, bsz, nb // MOBA_QBLOCKS),
        in_specs=[
            pl.BlockSpec(memory_space=pltpu.SMEM),
            pl.BlockSpec((1, qw, cb), lambda h, b, i: (b, i, COL_Q // cb + h)),
            pl.BlockSpec((1, s_len, cb), lambda h, b, i: (b, 0, COL_K // cb + h)),
            pl.BlockSpec((1, s_len, cb), lambda h, b, i: (b, 0, COL_V // cb + h)),
            pl.BlockSpec((1, nb, cb), lambda h, b, i: (b, 0, h)),
        ],
        out_specs=pl.BlockSpec((1, qw, cb), lambda h, b, i: (b, i, h)),
        out_shape=jax.ShapeDtypeStruct((bsz, s_len, ATTN_HEADS * ATTN_HEAD_DIM), BF16),
        scratch_shapes=[
            pltpu.VMEM((nbp, ATTN_HEAD_DIM), F32),
            pltpu.VMEM((MOBA_NEAR, blk, blk), F32),
            pltpu.VMEM((nb, MOBA_VT_ROWS, blk), BF16),
            pltpu.VMEM((nbp, qw), F32),
            pltpu.VMEM((2, MOBA_GROUP * blk, qw), F32),
            pltpu.VMEM((2, SUBLANES, qw), F32),
            pltpu.VMEM((1, qw), F32),
            pltpu.VMEM((MOBA_VT_ROWS, qw), F32),
        ],
        compiler_params=_cparams(("arbitrary", "arbitrary", "arbitrary")),
        name="moba",
    )(rel_t, proj3, proj3, proj3, kmean3)


def _ssd_kernel(xbc_ref, z_ref, dt_ref, cw_ref, cbias_ref, dtb_ref, alog_ref, dskip_ref, ng_ref,
                tril_ref, e_ref, shift_ref, o_ref, tail_ref, state_ref, xs_ref, xcb_ref, ea_ref, wend_ref, bm_ref,
                cm_ref):
    L = SSM_CHUNK
    G = SSM_GROUPS
    N = SSM_STATE
    GW = SSM_INNER // G
    pad = SUBLANES
    nt = (((1,), (1,)), ((), ()))

    @pl.when(pl.program_id(1) == 0)
    def _():
        tail_ref[...] = jnp.zeros_like(tail_ref)
        state_ref[...] = jnp.zeros_like(state_ref)

    dtr = dt_ref[0] + dtb_ref[...]
    dt = jnp.maximum(dtr, 0.0) + jnp.log1p(jnp.exp(-jnp.abs(dtr)))
    a = dt * (-jnp.exp(alog_ref[...]))
    a_cat = jnp.concatenate(_split_bf16(a, 3), axis=0)
    acs = jnp.dot(tril_ref[...], a_cat, preferred_element_type=F32) * LOG2E
    acs_t = acs.T

    dt_cat = jnp.concatenate(_split_bf16(dt, 2), axis=1)
    acs_cat = jnp.concatenate(_split_bf16(acs, 2), axis=1)

    SW = 2 * LANES
    for c in range(SSM_CONV_DIM // SW):
        cols = slice(c * SW, (c + 1) * SW)
        xb = xbc_ref[0, :, cols]
        x = xb.astype(F32)
        conv = cbias_ref[:, cols] + x * cw_ref[SSM_CONV - 1:SSM_CONV, cols]
        head = jnp.zeros((pad, SW), F32)
        for k in range(SSM_CONV - 1):
            s = SSM_CONV - 1 - k
            w_k = cw_ref[k:k + 1, cols]
            conv = conv + jnp.dot(shift_ref[s - 1], xb, preferred_element_type=F32) * w_k
            head = head + tail_ref[pad - s:2 * pad - s, cols] * w_k
        conv = jnp.concatenate([conv[0:pad] + head, conv[pad:]], axis=0)
        tail_ref[0:pad, cols] = x[L - pad:L]
        xa = _silu(conv)
        if c * SW < SSM_INNER:
            dt_x = jnp.dot(dt_cat, e_ref[:, cols], preferred_element_type=F32)
            acs_x = jnp.dot(acs_cat, e_ref[:, cols], preferred_element_type=F32)
            xc = xa * dt_x
            xs_ref[:, cols] = xa
            xcb_ref[:, cols] = xc.astype(BF16)
            ea_ref[:, cols] = jnp.exp2(acs_x)
            wend_ref[:, cols] = (jnp.exp2(acs_x[L - 1:L, :] - acs_x) * xc).astype(BF16)
        elif c * SW < SSM_INNER + G * N:
            bm_ref[:, c * SW - SSM_INNER:(c + 1) * SW - SSM_INNER] = xa
        else:
            cm_ref[:, c * SW - SSM_INNER - G * N:(c + 1) * SW - SSM_INNER - G * N] = xa.astype(BF16)

    row = lax.broadcasted_iota(jnp.int32, (L, L), 0)
    col = lax.broadcasted_iota(jnp.int32, (L, L), 1)
    causal = row >= col
    lane = lax.broadcasted_iota(jnp.int32, (L, LANES), 1)
    low_half = lane < SSM_HEAD_DIM

    for g in range(G):
        gcols = slice(g * GW, (g + 1) * GW)
        bg = bm_ref[:, g * N:(g + 1) * N]
        cg = cm_ref[:, g * N:(g + 1) * N]
        cb = lax.dot_general(cg, bg.astype(BF16), nt, preferred_element_type=F32)
        st = state_ref[g]
        ea_g = ea_ref[:, gcols]
        y_off = jnp.dot(cg, st.astype(BF16), preferred_element_type=F32) * ea_g
        y_parts = []
        for pr in range(GW // LANES):
            c0 = g * GW + pr * LANES
            x_pair = xcb_ref[:, c0:c0 + LANES]
            zero = jnp.zeros_like(x_pair)
            y_pair = y_off[:, pr * LANES:(pr + 1) * LANES]
            for half in range(2):
                hd = c0 // SSM_HEAD_DIM + half
                seg = acs[:, hd:hd + 1] - acs_t[hd:hd + 1, :]
                decay = jnp.exp2(jnp.where(causal, seg, -jnp.inf))
                mmat = (cb * decay).astype(BF16)
                x_half = jnp.where(low_half, x_pair, zero) if half == 0 else jnp.where(low_half, zero, x_pair)
                y_pair = y_pair + jnp.dot(mmat, x_half, preferred_element_type=F32)
            y_parts.append(y_pair)
        bg_t = bg.T.astype(BF16)
        state_ref[g] = (st * ea_g[L - 1:L, :]
                        + jnp.dot(bg_t, wend_ref[:, gcols], preferred_element_type=F32))

        y = jnp.concatenate(y_parts, axis=1) + dskip_ref[:, gcols] * xs_ref[:, gcols]
        v = y * _silu(z_ref[0, :, gcols].astype(F32))
        ms = jnp.mean(v * v, axis=-1, keepdims=True)
        o_ref[0, :, gcols] = (v * lax.rsqrt(ms + NORM_EPS) * ng_ref[:, gcols]).astype(o_ref.dtype)


def _ssd(proj3, dt3, conv_w, conv_b, dt_bias, a_log, d_skip_x, norm_g, tril3, e2, shifts):
    bsz, s_len, _ = proj3.shape
    L = SSM_CHUNK
    nc = s_len // L
    assert nc * L == s_len
    const = lambda b, c: (0, 0)
    return pl.pallas_call(
        _ssd_kernel,
        grid=(bsz, nc),
        in_specs=[
            pl.BlockSpec((1, L, SSM_CONV_DIM), lambda b, c: (b, c, COL_XBC // SSM_CONV_DIM)),
            pl.BlockSpec((1, L, SSM_INNER), lambda b, c: (b, c, COL_Z // SSM_INNER)),
            pl.BlockSpec((1, L, LANES), lambda b, c: (b, c, 0)),
            pl.BlockSpec((SSM_CONV, SSM_CONV_DIM), const),
            pl.BlockSpec((1, SSM_CONV_DIM), const),
            pl.BlockSpec((1, LANES), const),
            pl.BlockSpec((1, LANES), const),
            pl.BlockSpec((1, SSM_INNER), const),
            pl.BlockSpec((1, SSM_INNER), const),
            pl.BlockSpec((L, 3 * L), const),
            pl.BlockSpec((2 * LANES, SSM_INNER), const),
            pl.BlockSpec((SSM_CONV - 1, L, L), lambda b, c: (0, 0, 0)),
        ],
        out_specs=pl.BlockSpec((1, L, SSM_INNER), lambda b, c: (b, c, 0)),
        out_shape=jax.ShapeDtypeStruct((bsz, s_len, SSM_INNER), BF16),
        scratch_shapes=[
            pltpu.VMEM((2 * SUBLANES, SSM_CONV_DIM), F32),
            pltpu.VMEM((SSM_GROUPS, SSM_STATE, SSM_INNER // SSM_GROUPS), F32),
            pltpu.VMEM((L, SSM_INNER), F32),
            pltpu.VMEM((L, SSM_INNER), BF16),
            pltpu.VMEM((L, SSM_INNER), F32),
            pltpu.VMEM((L, SSM_INNER), BF16),
            pltpu.VMEM((L, SSM_GROUPS * SSM_STATE), F32),
            pltpu.VMEM((L, SSM_GROUPS * SSM_STATE), BF16),
        ],
        compiler_params=_cparams(("arbitrary", "arbitrary")),
        name="ssd",
    )(proj3, proj3, dt3, conv_w, conv_b, dt_bias, a_log, d_skip_x, norm_g, tril3, e2, shifts)


def _memkv_kernel(mem_ref, g_ref, w_ref, kv_ref):
    x = mem_ref[...]
    ms = jnp.mean(x * x, axis=-1, keepdims=True)
    u = (x * lax.rsqrt(ms + NORM_EPS) * g_ref[...]).astype(BF16)
    kv_ref[...] = jnp.dot(u, w_ref[...], preferred_element_type=F32).astype(BF16)


def _mem_kv(mem2, g, w_kv):
    rows = mem2.shape[0]
    width = w_kv.shape[1]
    return pl.pallas_call(
        _memkv_kernel,
        grid=(1,),
        in_specs=[
            pl.BlockSpec((rows, D_MODEL), lambda i: (0, 0)),
            pl.BlockSpec((1, D_MODEL), lambda i: (0, 0)),
            pl.BlockSpec((D_MODEL, width), lambda i: (0, 0)),
        ],
        out_specs=pl.BlockSpec((rows, width), lambda i: (0, 0)),
        out_shape=jax.ShapeDtypeStruct((rows, width), BF16),
        compiler_params=_cparams(("arbitrary",)),
        name="mem_kv",
    )(mem2, g, w_kv)


def _merge_kernel(x_ref, oa_ref, os_ref, qm_ref, gl_ref, kv_ref, bg_ref, wa_ref, ws_ref, wm_ref, wo_ref, h_ref):
    nt = (((1,), (1,)), ((), ()))
    hd = MEM_HEAD_DIM
    width = MEM_HEADS * hd
    scale = hd ** -0.5
    qm = qm_ref[0]
    kv = kv_ref[0]
    outs = []
    for hh in range(MEM_HEADS):
        q = qm[:, hh * hd:(hh + 1) * hd]
        km = kv[:, hh * hd:(hh + 1) * hd]
        vm = kv[:, width + hh * hd:width + (hh + 1) * hd]
        s = lax.dot_general(q, km, nt, preferred_element_type=F32) * scale
        p = jnp.exp(s - jnp.max(s, axis=1, keepdims=True))
        o = jnp.dot(p.astype(BF16), vm, preferred_element_type=F32)
        outs.append(o / jnp.sum(p, axis=1, keepdims=True))
    o_mem = jnp.concatenate(outs, axis=1).astype(BF16)

    gates = _sigmoid(gl_ref[0].astype(F32) + bg_ref[...])
    merged = (gates[:, :D_MODEL] * jnp.dot(oa_ref[0], wa_ref[...], preferred_element_type=F32)
              + gates[:, D_MODEL:2 * D_MODEL] * jnp.dot(os_ref[0], ws_ref[...], preferred_element_type=F32)
              + gates[:, 2 * D_MODEL:] * jnp.dot(o_mem, wm_ref[...], preferred_element_type=F32))
    h_ref[0] = x_ref[0] + jnp.dot(merged.astype(BF16), wo_ref[...], preferred_element_type=F32)


def _resident(shape):
    return pl.BlockSpec(shape, lambda *_: (0,) * len(shape), pipeline_mode=pl.Buffered(1))


def _merge(x, o_attn, o_ssm, proj3, kv3, b_gate, wa, ws, wm, wo, tm):
    bsz, s_len, _ = x.shape
    mem_len = kv3.shape[1]
    return pl.pallas_call(
        _merge_kernel,
        grid=(bsz, s_len // tm),
        in_specs=[
            pl.BlockSpec((1, tm, D_MODEL), lambda b, i: (b, i, 0)),
            pl.BlockSpec((1, tm, D_MODEL), lambda b, i: (b, i, 0)),
            pl.BlockSpec((1, tm, SSM_INNER), lambda b, i: (b, i, 0)),
            pl.BlockSpec((1, tm, D_MODEL), lambda b, i: (b, i, COL_QM // D_MODEL)),
            pl.BlockSpec((1, tm, 3 * D_MODEL), lambda b, i: (b, i, COL_GATE // (3 * D_MODEL))),
            pl.BlockSpec((1, mem_len, 2 * D_MODEL), lambda b, i: (b, 0, 0)),
            _resident((1, 3 * D_MODEL)),
            _resident((D_MODEL, D_MODEL)),
            _resident((SSM_INNER, D_MODEL)),
            _resident((D_MODEL, D_MODEL)),
            _resident((D_MODEL, D_MODEL)),
        ],
        out_specs=pl.BlockSpec((1, tm, D_MODEL), lambda b, i: (b, i, 0)),
        out_shape=jax.ShapeDtypeStruct((bsz, s_len, D_MODEL), F32),
        compiler_params=_cparams(("arbitrary", "arbitrary")),
        name="merge",
    )(x, o_attn, o_ssm, proj3, proj3, kv3, b_gate, wa, ws, wm, wo)


def _ffn_kernel(h_ref, ng_ref, wup_ref, cw_ref, cb_ref, wdn_ref, fg_ref, o_ref, hid_ref, *, tm):
    pad = SUBLANES

    @pl.when(pl.program_id(1) == 0)
    def _():
        hid_ref[0:pad, :] = jnp.zeros((pad, 2 * FFN_HIDDEN), F32)

    h = h_ref[0]
    ms = jnp.mean(h * h, axis=-1, keepdims=True)
    u = (h * lax.rsqrt(ms + NORM_EPS) * ng_ref[...]).astype(BF16)
    hid_ref[pad:pad + tm, :] = jnp.dot(u, wup_ref[...], preferred_element_type=F32)
    conv = cb_ref[...]
    for k in range(FFN_CONV):
        off = pad - (FFN_CONV - 1) + k
        conv = conv + hid_ref[off:off + tm, :] * cw_ref[k:k + 1, :]
    hid_ref[0:pad, :] = hid_ref[tm:tm + pad, :]
    act = (_silu(conv[:, :FFN_HIDDEN]) * conv[:, FFN_HIDDEN:]).astype(BF16)
    y = h + jnp.dot(act, wdn_ref[...], preferred_element_type=F32)
    ms2 = jnp.mean(y * y, axis=-1, keepdims=True)
    o_ref[0] = y * lax.rsqrt(ms2 + NORM_EPS) * fg_ref[...]


def _ffn(h, norm_g, w_up, conv_w, conv_b, w_down, final_g, tm):
    bsz, s_len, _ = h.shape
    f2 = 2 * FFN_HIDDEN
    return pl.pallas_call(
        functools.partial(_ffn_kernel, tm=tm),
        grid=(bsz, s_len // tm),
        in_specs=[
            pl.BlockSpec((1, tm, D_MODEL), lambda b, i: (b, i, 0)),
            _resident((1, D_MODEL)),
            _resident((D_MODEL, f2)),
            _resident((FFN_CONV, f2)),
            _resident((1, f2)),
            _resident((FFN_HIDDEN, D_MODEL)),
            _resident((1, D_MODEL)),
        ],
        out_specs=pl.BlockSpec((1, tm, D_MODEL), lambda b, i: (b, i, 0)),
        out_shape=jax.ShapeDtypeStruct((bsz, s_len, D_MODEL), F32),
        scratch_shapes=[pltpu.VMEM((tm + 2 * SUBLANES, f2), F32)],
        compiler_params=_cparams(("arbitrary", "arbitrary")),
        name="ffn",
    )(h, norm_g, w_up, conv_w, conv_b, w_down, final_g)


def _ssd_constants():
    L = SSM_CHUNK
    tril = np.tril(np.ones((L, L), np.float32))
    tril3 = np.concatenate([tril, tril, tril], axis=1)
    e = np.zeros((LANES, SSM_INNER), np.float32)
    for hd in range(SSM_HEADS):
        e[hd, hd * SSM_HEAD_DIM:(hd + 1) * SSM_HEAD_DIM] = 1.0
    e2 = np.concatenate([e, e], axis=0)
    shifts = np.stack([np.eye(L, k=-s, dtype=np.float32) for s in range(1, SSM_CONV)])
    return jnp.asarray(tril3, BF16), jnp.asarray(e2, BF16), jnp.asarray(shifts, BF16)


def _pad_lanes(v):
    return jnp.pad(v, ((0, 0), (0, LANES - v.shape[1])))


def _layer(h, mem, rel_bias, mix_norm_g, w_in, b_gate, ssm_conv_w, ssm_conv_b, ssm_dt_bias, ssm_A_log, ssm_D,
           ssm_norm_g, mem_norm_g, w_mem_kv, w_br_attn, w_br_ssm, w_br_mem, w_out, ffn_norm_g, w_ffn_up,
           ffn_conv_w, ffn_conv_b, w_ffn_down, final_g):
    bsz, s_len, _ = h.shape
    w_a = w_in[:, :IN_PROJ_SPLIT].astype(BF16)
    w_b = w_in[:, IN_PROJ_SPLIT + SSM_DT_COLS:].astype(BF16)
    w_dt = _pad_lanes(w_in[:, IN_PROJ_SPLIT:IN_PROJ_SPLIT + SSM_DT_COLS]).astype(BF16)

    proj, dt_raw, kmean = _in_proj(h.reshape(bsz * s_len, D_MODEL), mix_norm_g[None, :], w_a, w_b, w_dt,
                                   tm=IN_PROJ_TM, tn=IN_PROJ_TN)
    proj3 = proj.reshape(bsz, s_len, PROJ_WIDTH)
    dt3 = dt_raw.reshape(bsz, s_len, LANES)
    kmean3 = kmean.reshape(bsz, s_len // MOBA_BLOCK, ATTN_HEADS * ATTN_HEAD_DIM)

    o_attn = _moba(proj3, kmean3, rel_bias.T)

    tril3, e2, shifts = _ssd_constants()
    o_ssm = _ssd(proj3, dt3, ssm_conv_w, ssm_conv_b[None, :], _pad_lanes(ssm_dt_bias[None, :]),
                 _pad_lanes(ssm_A_log[None, :]), jnp.repeat(ssm_D, SSM_HEAD_DIM)[None, :], ssm_norm_g[None, :],
                 tril3, e2, shifts)

    mem_len = mem.shape[1]
    kv = _mem_kv(mem.reshape(bsz * mem_len, D_MODEL), mem_norm_g[None, :], w_mem_kv.astype(BF16))
    kv3 = kv.reshape(bsz, mem_len, 2 * D_MODEL)

    h1 = _merge(h, o_attn, o_ssm, proj3, kv3, b_gate[None, :], w_br_attn.astype(BF16), w_br_ssm.astype(BF16),
                w_br_mem.astype(BF16), w_out.astype(BF16), tm=MERGE_TM)

    return _ffn(h1, ffn_norm_g[None, :], w_ffn_up.astype(BF16), ffn_conv_w, ffn_conv_b[None, :],
                w_ffn_down.astype(BF16), final_g[None, :], tm=FFN_TM)


def kernel(x, mem, rel_bias, mix_norm_g, w_in, b_gate, ssm_conv_w, ssm_conv_b, ssm_dt_bias, ssm_A_log, ssm_D,
           ssm_norm_g, mem_norm_g, w_mem_kv, w_br_attn, w_br_ssm, w_br_mem, w_out, ffn_norm_g, w_ffn_up,
           ffn_conv_w, ffn_conv_b, w_ffn_down, final_norm_g):
    assert w_in.shape[0] == 1, "single-layer trunk"
    return _layer(x, mem, rel_bias, mix_norm_g[0], w_in[0], b_gate[0], ssm_conv_w[0], ssm_conv_b[0],
                  ssm_dt_bias[0], ssm_A_log[0], ssm_D[0], ssm_norm_g[0], mem_norm_g[0], w_mem_kv[0],
                  w_br_attn[0], w_br_ssm[0], w_br_mem[0], w_out[0], ffn_norm_g[0], w_ffn_up[0],
                  ffn_conv_w[0], ffn_conv_b[0], w_ffn_down[0], final_norm_g)
```

```python
import functools
import math

import numpy as np
import jax
import jax.numpy as jnp
from jax import lax
from jax.experimental import pallas as pl
from jax.experimental.pallas import tpu as pltpu

F32 = jnp.float32
BF16 = jnp.bfloat16

D_MODEL = 1024
ATTN_HEADS = 8
ATTN_HEAD_DIM = 128
MOBA_BLOCK = 256
MOBA_TOPK = 3
REL_BUCKETS = 32
REL_MAX_DIST = 1024
SSM_INNER = 2048
SSM_HEAD_DIM = 64
SSM_HEADS = 32
SSM_GROUPS = 4
SSM_STATE = 128
SSM_CONV = 4
SSM_CHUNK = 256
SSM_CONV_DIM = 3072
MEM_HEADS = 4
MEM_HEAD_DIM = 256
FFN_HIDDEN = 2816
FFN_CONV = 3
NORM_EPS = 1e-6

LANES = 128
SUBLANES = 8
VMEM_LIMIT = 56 * 1024 * 1024

IN_PROJ_TM = 2048
IN_PROJ_TN = 1024
MERGE_TM = 512
FFN_TM = 512

COL_XBC = 0
COL_GATE = 3072
COL_Z = 6144
COL_Q = 8192
COL_K = 9216
COL_V = 10240
COL_QM = 11264
PROJ_WIDTH = 12288
IN_PROJ_PERM = (8, 9, 10, 6, 7, 0, 1, 2, 11, 3, 4, 5)
IN_PROJ_K_TILE = 1
IN_PROJ_SPLIT = 8192
SSM_DT_COLS = 32

MOBA_NEAR = -(-(REL_MAX_DIST + MOBA_BLOCK - 1) // MOBA_BLOCK)
NEG_BIG = -1e30
LOG2E = math.log2(math.e)
MOBA_QBLOCKS = 2
MOBA_GROUP = 4
MOBA_VT_ROWS = ATTN_HEAD_DIM + 16
assert MOBA_GROUP <= MOBA_NEAR + 1
assert MOBA_GROUP <= SUBLANES


def _cparams(sem):
    return pltpu.CompilerParams(dimension_semantics=sem, vmem_limit_bytes=VMEM_LIMIT)


def _sigmoid(x):
    return 1.0 / (1.0 + jnp.exp(-x))


def _silu(x):
    h = 0.5 * x
    return h + h * jnp.tanh(h)


def _split_bf16(x, parts):
    out = []
    r = x
    for _ in range(parts):
        hi = r.astype(BF16)
        out.append(hi)
        r = r - hi.astype(F32)
    return out


def _inproj_kernel(perm_ref, x_ref, g_ref, wa_ref, wb_ref, wdt_ref, proj_ref, dt_ref, kmean_ref, u_ref, *,
                   n_a, k_tile):
    del perm_ref
    j = pl.program_id(1)
    nt = (((1,), (1,)), ((), ()))

    @pl.when(j == 0)
    def _():
        x = x_ref[...]
        ms = jnp.mean(x * x, axis=-1, keepdims=True)
        u = (x * lax.rsqrt(ms + NORM_EPS) * g_ref[...]).astype(BF16)
        u_ref[...] = u
        dt_ref[...] = lax.dot_general(u, wdt_ref[...], nt, preferred_element_type=F32)

    def column_tile(w_ref):
        acc = lax.dot_general(u_ref[...], w_ref[...], nt, preferred_element_type=F32)
        proj_ref[...] = acc.astype(BF16)
        return acc

    @pl.when(j < n_a)
    def _():
        acc = column_tile(wa_ref)

        @pl.when(j == k_tile)
        def _():
            blk = MOBA_BLOCK
            for r in range(acc.shape[0] // blk):
                kmean_ref[0, r:r + 1, :] = jnp.sum(acc[r * blk:(r + 1) * blk], axis=0, keepdims=True) * (1.0 / blk)

    @pl.when(j >= n_a)
    def _():
        column_tile(wb_ref)


def _in_proj(x2, g, w_a, w_b, w_dt, tm, tn):
    t = x2.shape[0]
    kw = ATTN_HEADS * ATTN_HEAD_DIM
    n_a, n_b = w_a.shape[0] // tn, w_b.shape[0] // tn
    assert tn == kw and tm % MOBA_BLOCK == 0 and (n_a + n_b) * tn == PROJ_WIDTH == len(IN_PROJ_PERM) * tn
    grid_spec = pltpu.PrefetchScalarGridSpec(
        num_scalar_prefetch=1,
        grid=(t // tm, n_a + n_b),
        in_specs=[
            pl.BlockSpec((tm, D_MODEL), lambda i, j, perm: (i, 0)),
            pl.BlockSpec((1, D_MODEL), lambda i, j, perm: (0, 0)),
            pl.BlockSpec((tn, D_MODEL), lambda i, j, perm: (jnp.minimum(j, n_a - 1), 0)),
            pl.BlockSpec((tn, D_MODEL), lambda i, j, perm: (jnp.maximum(j - n_a, 0), 0)),
            pl.BlockSpec((LANES, D_MODEL), lambda i, j, perm: (0, 0)),
        ],
        out_specs=[
            pl.BlockSpec((tm, tn), lambda i, j, perm: (i, perm[j])),
            pl.BlockSpec((tm, LANES), lambda i, j, perm: (i, 0)),
            pl.BlockSpec((1, tm // MOBA_BLOCK, kw), lambda i, j, perm: (i, 0, 0)),
        ],
        scratch_shapes=[pltpu.VMEM((tm, D_MODEL), BF16)],
    )
    return pl.pallas_call(
        functools.partial(_inproj_kernel, n_a=n_a, k_tile=IN_PROJ_K_TILE),
        grid_spec=grid_spec,
        out_shape=[
            jax.ShapeDtypeStruct((t, PROJ_WIDTH), BF16),
            jax.ShapeDtypeStruct((t, LANES), F32),
            jax.ShapeDtypeStruct((t // tm, tm // MOBA_BLOCK, kw), F32),
        ],
        compiler_params=_cparams(("arbitrary", "arbitrary")),
        name="in_proj",
    )(jnp.asarray(IN_PROJ_PERM, jnp.int32), x2, g, w_a, w_b, w_dt)


def _t5_bucket(dist):
    n = jnp.maximum(dist, 0)
    max_exact = REL_BUCKETS // 2
    nf = jnp.maximum(n, max_exact).astype(F32)
    large = max_exact + (jnp.log(nf * (1.0 / max_exact)) / math.log(REL_MAX_DIST / max_exact)
                         * (REL_BUCKETS - max_exact)).astype(jnp.int32)
    large = jnp.minimum(large, REL_BUCKETS - 1)
    return jnp.where(n < max_exact, n, large)


def _moba_kernel(rel_ref, q_ref, k_ref, v_ref, km_ref, o_ref, kmean_ref, bias_ref, vt_ref, neg_ref, s_ref, cm_ref,
                 m_ref, acc_ref, *, nb):
    h = pl.program_id(0)
    b = pl.program_id(1)
    i = pl.program_id(2)
    blk = MOBA_BLOCK
    scale = ATTN_HEAD_DIM ** -0.5


    @pl.when((b == 0) & (i == 0))
    def _():
        key = lax.broadcasted_iota(jnp.int32, (blk, blk), 0)
        qry = lax.broadcasted_iota(jnp.int32, (blk, blk), 1)
        for d in range(MOBA_NEAR):
            bucket = _t5_bucket(d * blk + qry - key)
            tile = jnp.zeros((blk, blk), F32)
            for bk in range(REL_BUCKETS):
                tile = jnp.where(bucket == bk, rel_ref[h, bk], tile)
            bias_ref[d] = tile * LOG2E

    @pl.when(i == 0)
    def _():
        kmean_ref[...] = jnp.zeros_like(kmean_ref)
        kmean_ref[0:nb, :] = km_ref[0]
        ones_rows = (lax.broadcasted_iota(jnp.int32, (MOBA_VT_ROWS - ATTN_HEAD_DIM, blk), 0) == 0).astype(BF16)
        for jb in range(nb):
            vt_ref[jb, 0:ATTN_HEAD_DIM, :] = v_ref[0, jb * blk:(jb + 1) * blk, :].astype(F32).T.astype(BF16)
            vt_ref[jb, ATTN_HEAD_DIM:MOBA_VT_ROWS, :] = ones_rows

    qw = MOBA_QBLOCKS * blk
    i0 = i * MOBA_QBLOCKS
    q_t = q_ref[0].astype(F32).T
    qs_t = (q_t * (scale * LOG2E)).astype(BF16)

    gate = jnp.dot(kmean_ref[...].astype(BF16), q_t.astype(BF16), preferred_element_type=F32)
    nbp = kmean_ref.shape[0]
    row = lax.broadcasted_iota(jnp.int32, (nbp, qw), 0).astype(F32)
    own = (i0 + lax.broadcasted_iota(jnp.int32, (1, qw), 1) // blk).astype(F32)
    g = jnp.where(row < own, gate, -jnp.inf)
    sel = jnp.zeros((nbp, qw), F32)
    for t in range(MOBA_TOPK):
        mx = jnp.max(g, axis=0, keepdims=True)
        idx = jnp.min(jnp.where(g == mx, row, float(nbp)), axis=0, keepdims=True)
        hit = row == idx
        sel = jnp.maximum(sel, jnp.where(hit & (own > t), 1.0, 0.0))
        g = jnp.where(hit, -jnp.inf, g)
    neg_ref[...] = jnp.where(sel > 0.5, 0.0, NEG_BIG)

    def far_scores(j0, slot):
        start = pl.multiple_of(j0 * blk, blk)
        s = jnp.dot(k_ref[0, pl.ds(start, MOBA_GROUP * blk), :], qs_t, preferred_element_type=F32)
        s_ref[slot] = s
        for u in range(MOBA_GROUP):
            cm_ref[slot, u:u + 1, :] = jnp.max(s[u * blk:(u + 1) * blk], axis=0, keepdims=True)

    far_scores(0, 0)

    def softmax_group(scores, offsets, v_idx, m_old, col_max=None):
        if col_max is None:
            col_max = [jnp.max(sj, axis=0, keepdims=True) for sj in scores]
        m_new = m_old
        for cm, off in zip(col_max, offsets):
            m_new = jnp.maximum(m_new, cm + off)
        pv = jnp.zeros((MOBA_VT_ROWS, m_old.shape[1]), F32)
        for sj, off, vj in zip(scores, offsets, v_idx):
            pj = jnp.exp2(sj + (off - m_new))
            pv = pv + jnp.dot(vt_ref[vj], pj.astype(BF16), preferred_element_type=F32)
        return m_new, pv

    far_bias = rel_ref[h, REL_BUCKETS - 1] * LOG2E

    key = lax.broadcasted_iota(jnp.int32, (blk, blk), 0)
    qry = lax.broadcasted_iota(jnp.int32, (blk, blk), 1)
    scores, offsets, v_idx, first_lane = [], [], [], []
    for e in range(MOBA_NEAR - 1 + MOBA_QBLOCKS):
        jn = i0 - (MOBA_NEAR - 1) + e
        jc = jnp.maximum(jn, 0)
        start = pl.multiple_of(jc * blk, blk)
        w_min = max(0, e - (MOBA_NEAR - 1))
        lo = w_min * blk
        se = jnp.dot(k_ref[0, pl.ds(start, blk), :], qs_t[:, lo:], preferred_element_type=F32)
        sel_row = neg_ref[pl.ds(jnp.where(jn >= 0, jn, nbp - 1), 1), :]
        s_parts, o_parts = [], []
        for w in range(w_min, MOBA_QBLOCKS):
            d = w + (MOBA_NEAR - 1) - e
            sw = se[:, (w - w_min) * blk:(w - w_min + 1) * blk]
            ow = sel_row[:, w * blk:(w + 1) * blk]
            if d == 0:
                sw = jnp.where(qry >= key, sw + bias_ref[0], NEG_BIG)
                ow = jnp.zeros((1, blk), F32)
            elif d < MOBA_NEAR:
                sw = sw + bias_ref[d]
            else:
                ow = ow + far_bias
            s_parts.append(sw)
            o_parts.append(ow)
        scores.append(jnp.concatenate(s_parts, axis=1))
        offsets.append(jnp.concatenate(o_parts, axis=1))
        v_idx.append(jc)
        first_lane.append(lo)
    m_floor = jnp.concatenate(
        [jnp.max(scores[MOBA_NEAR - 1 + w][:, 0:blk], axis=0, keepdims=True) for w in range(MOBA_QBLOCKS)], axis=1)
    parts = []
    for sj, off, vj, lo in zip(scores, offsets, v_idx, first_lane):
        mu, pvu = softmax_group([sj], [off], [vj], m_floor[:, lo:])
        if lo:
            mu = jnp.concatenate([m_floor[:, :lo], mu], axis=1)
            pvu = jnp.concatenate([jnp.zeros((MOBA_VT_ROWS, lo), F32), pvu], axis=1)
        parts.append((mu, pvu))
    m0 = parts[0][0]
    for mu, _ in parts[1:]:
        m0 = jnp.maximum(m0, mu)
    pv0 = jnp.zeros((MOBA_VT_ROWS, qw), F32)
    for mu, pvu in parts:
        pv0 = pv0 + jnp.exp2(mu - m0) * pvu
    m_ref[...] = m0
    acc_ref[...] = pv0

    n_far = jnp.maximum(i0 - (MOBA_NEAR - 1), 0)

    n_groups = (n_far + MOBA_GROUP - 1) // MOBA_GROUP

    def far_group(gi, slot, prefetch):
        j0 = gi * MOBA_GROUP
        if prefetch:
            far_scores(j0 + MOBA_GROUP, 1 - slot)
        scores, offsets, v_idx, col_max = [], [], [], []
        for u in range(MOBA_GROUP):
            ju = j0 + u
            scores.append(s_ref[slot, u * blk:(u + 1) * blk, :])
            col_max.append(cm_ref[slot, u:u + 1, :])
            offsets.append(neg_ref[pl.ds(jnp.where(ju < n_far, ju, nbp - 1), 1), :] + far_bias)
            v_idx.append(ju)
        m_old = m_ref[...]
        m_new, pv = softmax_group(scores, offsets, v_idx, m_old, col_max)
        acc_ref[...] = jnp.exp2(m_old - m_new) * acc_ref[...] + pv
        m_ref[...] = m_new

    def far_body(t, carry):
        far_group(2 * t, 0, True)
        far_group(2 * t + 1, 1, True)
        return carry

    n_pairs = jnp.maximum(n_groups - 1, 0) // 2
    lax.fori_loop(0, n_pairs, far_body, 0)
    g_rest = 2 * n_pairs
    n_rest = n_groups - g_rest

    @pl.when(n_rest == 1)
    def _():
        far_group(g_rest, 0, False)

    @pl.when(n_rest == 2)
    def _():
        far_group(g_rest, 0, True)
        far_group(g_rest + 1, 1, False)

    out_t = acc_ref[0:ATTN_HEAD_DIM, :] / acc_ref[ATTN_HEAD_DIM:ATTN_HEAD_DIM + 1, :]
    o_ref[0] = out_t.T.astype(o_ref.dtype)


def _moba(proj3, kmean3, rel_t):
    bsz, s_len, _ = proj3.shape
    blk = MOBA_BLOCK
    nb = s_len // blk
    qw = MOBA_QBLOCKS * blk
    assert nb * blk == s_len and nb % MOBA_QBLOCKS == 0 and nb >= MOBA_GROUP
    nbp = -(-(nb + 1) // 16) * 16
    cb = LANES
    return pl.pallas_call(
        functools.partial(_moba_kernel, nb=nb),
        grid=(ATTN_HEADS, bsz, nb // MOBA_QBLOCKS),
        in_specs=[
            pl.BlockSpec(memory_space=pltpu.SMEM),
            pl.BlockSpec((1, qw, cb), lambda h, b, i: (b, i, COL_Q // cb + h)),
            pl.BlockSpec((1, s_len, cb), lambda h, b, i: (b, 0, COL_K // cb + h)),
            pl.BlockSpec((1, s_len, cb), lambda h, b, i: (b, 0, COL_V // cb + h)),
            pl.BlockSpec((1, nb, cb), lambda h, b, i: (b, 0, h)),
        ],
        out_specs=pl.BlockSpec((1, qw, cb), lambda h, b, i: (b, i, h)),
        out_shape=jax.ShapeDtypeStruct((bsz, s_len, ATTN_HEADS * ATTN_HEAD_DIM), BF16),
        scratch_shapes=[
            pltpu.VMEM((nbp, ATTN_HEAD_DIM), F32),
            pltpu.VMEM((MOBA_NEAR, blk, blk), F32),
            pltpu.VMEM((nb, MOBA_VT_ROWS, blk), BF16),
            pltpu.VMEM((nbp, qw), F32),
            pltpu.VMEM((2, MOBA_GROUP * blk, qw), F32),
            pltpu.VMEM((2, SUBLANES, qw), F32),
            pltpu.VMEM((1, qw), F32),
            pltpu.VMEM((MOBA_VT_ROWS, qw), F32),
        ],
        compiler_params=_cparams(("arbitrary", "arbitrary", "arbitrary")),
        name="moba",
    )(rel_t, proj3, proj3, proj3, kmean3)


def _ssd_kernel(xbc_ref, z_ref, dt_ref, cw_ref, cbias_ref, dtb_ref, alog_ref, dskip_ref, ng_ref,
                tril_ref, e_ref, shift_ref, o_ref, tail_ref, state_ref, xs_ref, xcb_ref, ea_ref, wend_ref, bm_ref,
                cm_ref):
    L = SSM_CHUNK
    G = SSM_GROUPS
    N = SSM_STATE
    GW = SSM_INNER // G
    pad = SUBLANES
    nt = (((1,), (1,)), ((), ()))

    @pl.when(pl.program_id(1) == 0)
    def _():
        tail_ref[...] = jnp.zeros_like(tail_ref)
        state_ref[...] = jnp.zeros_like(state_ref)

    dtr = dt_ref[0] + dtb_ref[...]
    dt = jnp.maximum(dtr, 0.0) + jnp.log1p(jnp.exp(-jnp.abs(dtr)))
    a = dt * (-jnp.exp(alog_ref[...]))
    a_cat = jnp.concatenate(_split_bf16(a, 3), axis=0)
    acs = jnp.dot(tril_ref[...], a_cat, preferred_element_type=F32) * LOG2E
    acs_t = acs.T

    dt_cat = jnp.concatenate(_split_bf16(dt, 2), axis=1)
    acs_cat = jnp.concatenate(_split_bf16(acs, 2), axis=1)

    SW = 2 * LANES
    for c in range(SSM_CONV_DIM // SW):
        cols = slice(c * SW, (c + 1) * SW)
        xb = xbc_ref[0, :, cols]
        x = xb.astype(F32)
        conv = cbias_ref[:, cols] + x * cw_ref[SSM_CONV - 1:SSM_CONV, cols]
        head = jnp.zeros((pad, SW), F32)
        for k in range(SSM_CONV - 1):
            s = SSM_CONV - 1 - k
            w_k = cw_ref[k:k + 1, cols]
            conv = conv + jnp.dot(shift_ref[s - 1], xb, preferred_element_type=F32) * w_k
            head = head + tail_ref[pad - s:2 * pad - s, cols] * w_k
        conv = jnp.concatenate([conv[0:pad] + head, conv[pad:]], axis=0)
        tail_ref[0:pad, cols] = x[L - pad:L]
        xa = _silu(conv)
        if c * SW < SSM_INNER:
            dt_x = jnp.dot(dt_cat, e_ref[:, cols], preferred_element_type=F32)
            acs_x = jnp.dot(acs_cat, e_ref[:, cols], preferred_element_type=F32)
            xc = xa * dt_x
            xs_ref[:, cols] = xa
            xcb_ref[:, cols] = xc.astype(BF16)
            ea_ref[:, cols] = jnp.exp2(acs_x)
            wend_ref[:, cols] = (jnp.exp2(acs_x[L - 1:L, :] - acs_x) * xc).astype(BF16)
        elif c * SW < SSM_INNER + G * N:
            bm_ref[:, c * SW - SSM_INNER:(c + 1) * SW - SSM_INNER] = xa
        else:
            cm_ref[:, c * SW - SSM_INNER - G * N:(c + 1) * SW - SSM_INNER - G * N] = xa.astype(BF16)

    row = lax.broadcasted_iota(jnp.int32, (L, L), 0)
    col = lax.broadcasted_iota(jnp.int32, (L, L), 1)
    causal = row >= col
    lane = lax.broadcasted_iota(jnp.int32, (L, LANES), 1)
    low_half = lane < SSM_HEAD_DIM

    for g in range(G):
        gcols = slice(g * GW, (g + 1) * GW)
        bg = bm_ref[:, g * N:(g + 1) * N]
        cg = cm_ref[:, g * N:(g + 1) * N]
        cb = lax.dot_general(cg, bg.astype(BF16), nt, preferred_element_type=F32)
        st = state_ref[g]
        ea_g = ea_ref[:, gcols]
        y_off = jnp.dot(cg, st.astype(BF16), preferred_element_type=F32) * ea_g
        y_parts = []
        for pr in range(GW // LANES):
            c0 = g * GW + pr * LANES
            x_pair = xcb_ref[:, c0:c0 + LANES]
            zero = jnp.zeros_like(x_pair)
            y_pair = y_off[:, pr * LANES:(pr + 1) * LANES]
            for half in range(2):
                hd = c0 // SSM_HEAD_DIM + half
                seg = acs[:, hd:hd + 1] - acs_t[hd:hd + 1, :]
                decay = jnp.exp2(jnp.where(causal, seg, -jnp.inf))
                mmat = (cb * decay).astype(BF16)
                x_half = jnp.where(low_half, x_pair, zero) if half == 0 else jnp.where(low_half, zero, x_pair)
                y_pair = y_pair + jnp.dot(mmat, x_half, preferred_element_type=F32)
            y_parts.append(y_pair)
        bg_t = bg.T.astype(BF16)
        state_ref[g] = (st * ea_g[L - 1:L, :]
                        + jnp.dot(bg_t, wend_ref[:, gcols], preferred_element_type=F32))

        y = jnp.concatenate(y_parts, axis=1) + dskip_ref[:, gcols] * xs_ref[:, gcols]
        v = y * _silu(z_ref[0, :, gcols].astype(F32))
        ms = jnp.mean(v * v, axis=-1, keepdims=True)
        o_ref[0, :, gcols] = (v * lax.rsqrt(ms + NORM_EPS) * ng_ref[:, gcols]).astype(o_ref.dtype)


def _ssd(proj3, dt3, conv_w, conv_b, dt_bias, a_log, d_skip_x, norm_g, tril3, e2, shifts):
    bsz, s_len, _ = proj3.shape
    L = SSM_CHUNK
    nc = s_len // L
    assert nc * L == s_len
    const = lambda b, c: (0, 0)
    return pl.pallas_call(
        _ssd_kernel,
        grid=(bsz, nc),
        in_specs=[
            pl.BlockSpec((1, L, SSM_CONV_DIM), lambda b, c: (b, c, COL_XBC // SSM_CONV_DIM)),
            pl.BlockSpec((1, L, SSM_INNER), lambda b, c: (b, c, COL_Z // SSM_INNER)),
            pl.BlockSpec((1, L, LANES), lambda b, c: (b, c, 0)),
            pl.BlockSpec((SSM_CONV, SSM_CONV_DIM), const),
            pl.BlockSpec((1, SSM_CONV_DIM), const),
            pl.BlockSpec((1, LANES), const),
            pl.BlockSpec((1, LANES), const),
            pl.BlockSpec((1, SSM_INNER), const),
            pl.BlockSpec((1, SSM_INNER), const),
            pl.BlockSpec((L, 3 * L), const),
            pl.BlockSpec((2 * LANES, SSM_INNER), const),
            pl.BlockSpec((SSM_CONV - 1, L, L), lambda b, c: (0, 0, 0)),
        ],
        out_specs=pl.BlockSpec((1, L, SSM_INNER), lambda b, c: (b, c, 0)),
        out_shape=jax.ShapeDtypeStruct((bsz, s_len, SSM_INNER), BF16),
        scratch_shapes=[
            pltpu.VMEM((2 * SUBLANES, SSM_CONV_DIM), F32),
            pltpu.VMEM((SSM_GROUPS, SSM_STATE, SSM_INNER // SSM_GROUPS), F32),
            pltpu.VMEM((L, SSM_INNER), F32),
            pltpu.VMEM((L, SSM_INNER), BF16),
            pltpu.VMEM((L, SSM_INNER), F32),
            pltpu.VMEM((L, SSM_INNER), BF16),
            pltpu.VMEM((L, SSM_GROUPS * SSM_STATE), F32),
            pltpu.VMEM((L, SSM_GROUPS * SSM_STATE), BF16),
        ],
        compiler_params=_cparams(("arbitrary", "arbitrary")),
        name="ssd",
    )(proj3, proj3, dt3, conv_w, conv_b, dt_bias, a_log, d_skip_x, norm_g, tril3, e2, shifts)


def _memkv_kernel(mem_ref, g_ref, w_ref, kv_ref):
    x = mem_ref[...]
    ms = jnp.mean(x * x, axis=-1, keepdims=True)
    u = (x * lax.rsqrt(ms + NORM_EPS) * g_ref[...]).astype(BF16)
    kv_ref[...] = jnp.dot(u, w_ref[...], preferred_element_type=F32).astype(BF16)


def _mem_kv(mem2, g, w_kv):
    rows = mem2.shape[0]
    width = w_kv.shape[1]
    return pl.pallas_call(
        _memkv_kernel,
        grid=(1,),
        in_specs=[
            pl.BlockSpec((rows, D_MODEL), lambda i: (0, 0)),
            pl.BlockSpec((1, D_MODEL), lambda i: (0, 0)),
            pl.BlockSpec((D_MODEL, width), lambda i: (0, 0)),
        ],
        out_specs=pl.BlockSpec((rows, width), lambda i: (0, 0)),
        out_shape=jax.ShapeDtypeStruct((rows, width), BF16),
        compiler_params=_cparams(("arbitrary",)),
        name="mem_kv",
    )(mem2, g, w_kv)


def _merge_kernel(x_ref, oa_ref, os_ref, qm_ref, gl_ref, kv_ref, bg_ref, wa_ref, ws_ref, wm_ref, wo_ref, h_ref):
    nt = (((1,), (1,)), ((), ()))
    hd = MEM_HEAD_DIM
    width = MEM_HEADS * hd
    scale = hd ** -0.5
    qm = qm_ref[0]
    kv = kv_ref[0]
    outs = []
    for hh in range(MEM_HEADS):
        q = qm[:, hh * hd:(hh + 1) * hd]
        km = kv[:, hh * hd:(hh + 1) * hd]
        vm = kv[:, width + hh * hd:width + (hh + 1) * hd]
        s = lax.dot_general(q, km, nt, preferred_element_type=F32) * scale
        p = jnp.exp(s - jnp.max(s, axis=1, keepdims=True))
        o = jnp.dot(p.astype(BF16), vm, preferred_element_type=F32)
        outs.append(o / jnp.sum(p, axis=1, keepdims=True))
    o_mem = jnp.concatenate(outs, axis=1).astype(BF16)

    gates = _sigmoid(gl_ref[0].astype(F32) + bg_ref[...])
    merged = (gates[:, :D_MODEL] * jnp.dot(oa_ref[0], wa_ref[...], preferred_element_type=F32)
              + gates[:, D_MODEL:2 * D_MODEL] * jnp.dot(os_ref[0], ws_ref[...], preferred_element_type=F32)
              + gates[:, 2 * D_MODEL:] * jnp.dot(o_mem, wm_ref[...], preferred_element_type=F32))
    h_ref[0] = x_ref[0] + jnp.dot(merged.astype(BF16), wo_ref[...], preferred_element_type=F32)


def _resident(shape):
    return pl.BlockSpec(shape, lambda *_: (0,) * len(shape), pipeline_mode=pl.Buffered(1))


def _merge(x, o_attn, o_ssm, proj3, kv3, b_gate, wa, ws, wm, wo, tm):
    bsz, s_len, _ = x.shape
    mem_len = kv3.shape[1]
    return pl.pallas_call(
        _merge_kernel,
        grid=(bsz, s_len // tm),
        in_specs=[
            pl.BlockSpec((1, tm, D_MODEL), lambda b, i: (b, i, 0)),
            pl.BlockSpec((1, tm, D_MODEL), lambda b, i: (b, i, 0)),
            pl.BlockSpec((1, tm, SSM_INNER), lambda b, i: (b, i, 0)),
            pl.BlockSpec((1, tm, D_MODEL), lambda b, i: (b, i, COL_QM // D_MODEL)),
            pl.BlockSpec((1, tm, 3 * D_MODEL), lambda b, i: (b, i, COL_GATE // (3 * D_MODEL))),
            pl.BlockSpec((1, mem_len, 2 * D_MODEL), lambda b, i: (b, 0, 0)),
            _resident((1, 3 * D_MODEL)),
            _resident((D_MODEL, D_MODEL)),
            _resident((SSM_INNER, D_MODEL)),
            _resident((D_MODEL, D_MODEL)),
            _resident((D_MODEL, D_MODEL)),
        ],
        out_specs=pl.BlockSpec((1, tm, D_MODEL), lambda b, i: (b, i, 0)),
        out_shape=jax.ShapeDtypeStruct((bsz, s_len, D_MODEL), F32),
        compiler_params=_cparams(("arbitrary", "arbitrary")),
        name="merge",
    )(x, o_attn, o_ssm, proj3, proj3, kv3, b_gate, wa, ws, wm, wo)


def _ffn_kernel(h_ref, ng_ref, wup_ref, cw_ref, cb_ref, wdn_ref, fg_ref, o_ref, hid_ref, *, tm):
    pad = SUBLANES

    @pl.when(pl.program_id(1) == 0)
    def _():
        hid_ref[0:pad, :] = jnp.zeros((pad, 2 * FFN_HIDDEN), F32)

    h = h_ref[0]
    ms = jnp.mean(h * h, axis=-1, keepdims=True)
    u = (h * lax.rsqrt(ms + NORM_EPS) * ng_ref[...]).astype(BF16)
    hid_ref[pad:pad + tm, :] = jnp.dot(u, wup_ref[...], preferred_element_type=F32)
    conv = cb_ref[...]
    for k in range(FFN_CONV):
        off = pad - (FFN_CONV - 1) + k
        conv = conv + hid_ref[off:off + tm, :] * cw_ref[k:k + 1, :]
    hid_ref[0:pad, :] = hid_ref[tm:tm + pad, :]
    act = (_silu(conv[:, :FFN_HIDDEN]) * conv[:, FFN_HIDDEN:]).astype(BF16)
    y = h + jnp.dot(act, wdn_ref[...], preferred_element_type=F32)
    ms2 = jnp.mean(y * y, axis=-1, keepdims=True)
    o_ref[0] = y * lax.rsqrt(ms2 + NORM_EPS) * fg_ref[...]


def _ffn(h, norm_g, w_up, conv_w, conv_b, w_down, final_g, tm):
    bsz, s_len, _ = h.shape
    f2 = 2 * FFN_HIDDEN
    return pl.pallas_call(
        functools.partial(_ffn_kernel, tm=tm),
        grid=(bsz, s_len // tm),
        in_specs=[
            pl.BlockSpec((1, tm, D_MODEL), lambda b, i: (b, i, 0)),
            _resident((1, D_MODEL)),
            _resident((D_MODEL, f2)),
            _resident((FFN_CONV, f2)),
            _resident((1, f2)),
            _resident((FFN_HIDDEN, D_MODEL)),
            _resident((1, D_MODEL)),
        ],
        out_specs=pl.BlockSpec((1, tm, D_MODEL), lambda b, i: (b, i, 0)),
        out_shape=jax.ShapeDtypeStruct((bsz, s_len, D_MODEL), F32),
        scratch_shapes=[pltpu.VMEM((tm + 2 * SUBLANES, f2), F32)],
        compiler_params=_cparams(("arbitrary", "arbitrary")),
        name="ffn",
    )(h, norm_g, w_up, conv_w, conv_b, w_down, final_g)


def _ssd_constants():
    L = SSM_CHUNK
    tril = np.tril(np.ones((L, L), np.float32))
    tril3 = np.concatenate([tril, tril, tril], axis=1)
    e = np.zeros((LANES, SSM_INNER), np.float32)
    for hd in range(SSM_HEADS):
        e[hd, hd * SSM_HEAD_DIM:(hd + 1) * SSM_HEAD_DIM] = 1.0
    e2 = np.concatenate([e, e], axis=0)
    shifts = np.stack([np.eye(L, k=-s, dtype=np.float32) for s in range(1, SSM_CONV)])
    return jnp.asarray(tril3, BF16), jnp.asarray(e2, BF16), jnp.asarray(shifts, BF16)


def _pad_lanes(v):
    return jnp.pad(v, ((0, 0), (0, LANES - v.shape[1])))


def _layer(h, mem, rel_bias, mix_norm_g, w_in, b_gate, ssm_conv_w, ssm_conv_b, ssm_dt_bias, ssm_A_log, ssm_D,
           ssm_norm_g, mem_norm_g, w_mem_kv, w_br_attn, w_br_ssm, w_br_mem, w_out, ffn_norm_g, w_ffn_up,
           ffn_conv_w, ffn_conv_b, w_ffn_down, final_g):
    bsz, s_len, _ = h.shape
    w_t = w_in.T
    w_a = w_t[:IN_PROJ_SPLIT].astype(BF16)
    w_b = w_t[IN_PROJ_SPLIT + SSM_DT_COLS:].astype(BF16)
    w_dt = jnp.pad(w_t[IN_PROJ_SPLIT:IN_PROJ_SPLIT + SSM_DT_COLS], ((0, LANES - SSM_DT_COLS), (0, 0))).astype(BF16)

    proj, dt_raw, kmean = _in_proj(h.reshape(bsz * s_len, D_MODEL), mix_norm_g[None, :], w_a, w_b, w_dt,
                                   tm=IN_PROJ_TM, tn=IN_PROJ_TN)
    proj3 = proj.reshape(bsz, s_len, PROJ_WIDTH)
    dt3 = dt_raw.reshape(bsz, s_len, LANES)
    kmean3 = kmean.reshape(bsz, s_len // MOBA_BLOCK, ATTN_HEADS * ATTN_HEAD_DIM)

    o_attn = _moba(proj3, kmean3, rel_bias.T)

    tril3, e2, shifts = _ssd_constants()
    o_ssm = _ssd(proj3, dt3, ssm_conv_w, ssm_conv_b[None, :], _pad_lanes(ssm_dt_bias[None, :]),
                 _pad_lanes(ssm_A_log[None, :]), jnp.repeat(ssm_D, SSM_HEAD_DIM)[None, :], ssm_norm_g[None, :],
                 tril3, e2, shifts)

    mem_len = mem.shape[1]
    kv = _mem_kv(mem.reshape(bsz * mem_len, D_MODEL), mem_norm_g[None, :], w_mem_kv.astype(BF16))
    kv3 = kv.reshape(bsz, mem_len, 2 * D_MODEL)

    h1 = _merge(h, o_attn, o_ssm, proj3, kv3, b_gate[None, :], w_br_attn.astype(BF16), w_br_ssm.astype(BF16),
                w_br_mem.astype(BF16), w_out.astype(BF16), tm=MERGE_TM)

    return _ffn(h1, ffn_norm_g[None, :], w_ffn_up.astype(BF16), ffn_conv_w, ffn_conv_b[None, :],
                w_ffn_down.astype(BF16), final_g[None, :], tm=FFN_TM)


def kernel(x, mem, rel_bias, mix_norm_g, w_in, b_gate, ssm_conv_w, ssm_conv_b, ssm_dt_bias, ssm_A_log, ssm_D,
           ssm_norm_g, mem_norm_g, w_mem_kv, w_br_attn, w_br_ssm, w_br_mem, w_out, ffn_norm_g, w_ffn_up,
           ffn_conv_w, ffn_conv_b, w_ffn_down, final_norm_g):
    assert w_in.shape[0] == 1, "single-layer trunk"
    return _layer(x, mem, rel_bias, mix_norm_g[0], w_in[0], b_gate[0], ssm_conv_w[0], ssm_conv_b[0],
                  ssm_dt_bias[0], ssm_A_log[0], ssm_D[0], ssm_norm_g[0], mem_norm_g[0], w_mem_kv[0],
                  w_br_attn[0], w_br_ssm[0], w_br_mem[0], w_out[0], ffn_norm_g[0], w_ffn_up[0],
                  ffn_conv_w[0], ffn_conv_b[0], w_ffn_down[0], final_norm_g)
```

```python
import functools
import math

import numpy as np
import jax
import jax.numpy as jnp
from jax import lax
from jax.experimental import pallas as pl
from jax.experimental.pallas import tpu as pltpu

F32 = jnp.float32
BF16 = jnp.bfloat16

D_MODEL = 1024
ATTN_HEADS = 8
ATTN_HEAD_DIM = 128
MOBA_BLOCK = 256
MOBA_TOPK = 3
REL_BUCKETS = 32
REL_MAX_DIST = 1024
SSM_INNER = 2048
SSM_HEAD_DIM = 64
SSM_HEADS = 32
SSM_GROUPS = 4
SSM_STATE = 128
SSM_CONV = 4
SSM_CHUNK = 256
SSM_CONV_DIM = 3072
MEM_HEADS = 4
MEM_HEAD_DIM = 256
FFN_HIDDEN = 2816
FFN_CONV = 3
NORM_EPS = 1e-6

LANES = 128
SUBLANES = 8
VMEM_LIMIT = 56 * 1024 * 1024

IN_PROJ_TM = 2048
IN_PROJ_TN = 1024
MERGE_TM = 512
FFN_TM = 512

COL_XBC = 0
COL_GATE = 3072
COL_Z = 6144
COL_Q = 8192
COL_K = 9216
COL_V = 10240
COL_QM = 11264
PROJ_WIDTH = 12288
IN_PROJ_PERM = (8, 9, 10, 6, 7, 0, 1, 2, 11, 3, 4, 5)
IN_PROJ_K_TILE = 1
IN_PROJ_SPLIT = 8192
SSM_DT_COLS = 32

MOBA_NEAR = -(-(REL_MAX_DIST + MOBA_BLOCK - 1) // MOBA_BLOCK)
NEG_BIG = -1e30
LOG2E = math.log2(math.e)
MOBA_QBLOCKS = 2
MOBA_GROUP = 4
MOBA_VT_ROWS = ATTN_HEAD_DIM + 16
assert MOBA_GROUP <= MOBA_NEAR + 1
assert MOBA_GROUP <= SUBLANES


def _cparams(sem):
    return pltpu.CompilerParams(dimension_semantics=sem, vmem_limit_bytes=VMEM_LIMIT)


def _sigmoid(x):
    return 1.0 / (1.0 + jnp.exp(-x))


def _silu(x):
    h = 0.5 * x
    return h + h * jnp.tanh(h)


def _split_bf16(x, parts):
    out = []
    r = x
    for _ in range(parts):
        hi = r.astype(BF16)
        out.append(hi)
        r = r - hi.astype(F32)
    return out


def _inproj_kernel(perm_ref, x_ref, g_ref, w_ref, wdt_ref, proj_ref, dt_ref, kmean_ref, u_ref, *, k_tile):
    del perm_ref
    j = pl.program_id(1)
    nt = (((1,), (1,)), ((), ()))

    @pl.when(j == 0)
    def _():
        x = x_ref[...]
        ms = jnp.mean(x * x, axis=-1, keepdims=True)
        u = (x * lax.rsqrt(ms + NORM_EPS) * g_ref[...]).astype(BF16)
        u_ref[...] = u
        dt_ref[...] = lax.dot_general(u, wdt_ref[...], nt, preferred_element_type=F32)

    acc = lax.dot_general(u_ref[...], w_ref[...], nt, preferred_element_type=F32)
    proj_ref[...] = acc.astype(BF16)

    @pl.when(j == k_tile)
    def _():
        blk = MOBA_BLOCK
        for r in range(acc.shape[0] // blk):
            kmean_ref[0, r:r + 1, :] = jnp.sum(acc[r * blk:(r + 1) * blk], axis=0, keepdims=True) * (1.0 / blk)


def _in_proj(x2, g, w_t, w_dt, tm, tn):
    t = x2.shape[0]
    kw = ATTN_HEADS * ATTN_HEAD_DIM
    n_a = IN_PROJ_SPLIT // tn
    n_b = (w_t.shape[0] - IN_PROJ_SPLIT - SSM_DT_COLS) // tn
    assert tn == kw and tm % MOBA_BLOCK == 0 and (n_a + n_b) * tn == PROJ_WIDTH == len(IN_PROJ_PERM) * tn
    grid_spec = pltpu.PrefetchScalarGridSpec(
        num_scalar_prefetch=1,
        grid=(t // tm, n_a + n_b),
        in_specs=[
            pl.BlockSpec((tm, D_MODEL), lambda i, j, perm: (i, 0)),
            pl.BlockSpec((1, D_MODEL), lambda i, j, perm: (0, 0)),
            pl.BlockSpec((pl.Element(tn), pl.Element(D_MODEL)),
                         lambda i, j, perm: (pl.multiple_of(j * tn + jnp.where(j >= n_a, SSM_DT_COLS, 0), 32), 0)),
            pl.BlockSpec((LANES, D_MODEL), lambda i, j, perm: (0, 0)),
        ],
        out_specs=[
            pl.BlockSpec((tm, tn), lambda i, j, perm: (i, perm[j])),
            pl.BlockSpec((tm, LANES), lambda i, j, perm: (i, 0)),
            pl.BlockSpec((1, tm // MOBA_BLOCK, kw), lambda i, j, perm: (i, 0, 0)),
        ],
        scratch_shapes=[pltpu.VMEM((tm, D_MODEL), BF16)],
    )
    return pl.pallas_call(
        functools.partial(_inproj_kernel, k_tile=IN_PROJ_K_TILE),
        grid_spec=grid_spec,
        out_shape=[
            jax.ShapeDtypeStruct((t, PROJ_WIDTH), BF16),
            jax.ShapeDtypeStruct((t, LANES), F32),
            jax.ShapeDtypeStruct((t // tm, tm // MOBA_BLOCK, kw), F32),
        ],
        compiler_params=_cparams(("arbitrary", "arbitrary")),
        name="in_proj",
    )(jnp.asarray(IN_PROJ_PERM, jnp.int32), x2, g, w_t, w_dt)


def _t5_bucket(dist):
    n = jnp.maximum(dist, 0)
    max_exact = REL_BUCKETS // 2
    nf = jnp.maximum(n, max_exact).astype(F32)
    large = max_exact + (jnp.log(nf * (1.0 / max_exact)) / math.log(REL_MAX_DIST / max_exact)
                         * (REL_BUCKETS - max_exact)).astype(jnp.int32)
    large = jnp.minimum(large, REL_BUCKETS - 1)
    return jnp.where(n < max_exact, n, large)


def _moba_kernel(rel_ref, q_ref, k_ref, v_ref, km_ref, o_ref, kmean_ref, bias_ref, vt_ref, neg_ref, s_ref, cm_ref,
                 m_ref, acc_ref, *, nb):
    h = pl.program_id(0)
    b = pl.program_id(1)
    i = pl.program_id(2)
    blk = MOBA_BLOCK
    scale = ATTN_HEAD_DIM ** -0.5


    @pl.when((b == 0) & (i == 0))
    def _():
        key = lax.broadcasted_iota(jnp.int32, (blk, blk), 0)
        qry = lax.broadcasted_iota(jnp.int32, (blk, blk), 1)
        for d in range(MOBA_NEAR):
            bucket = _t5_bucket(d * blk + qry - key)
            tile = jnp.zeros((blk, blk), F32)
            for bk in range(REL_BUCKETS):
                tile = jnp.where(bucket == bk, rel_ref[h, bk], tile)
            bias_ref[d] = tile * LOG2E

    @pl.when(i == 0)
    def _():
        kmean_ref[...] = jnp.zeros_like(kmean_ref)
        kmean_ref[0:nb, :] = km_ref[0]
        ones_rows = (lax.broadcasted_iota(jnp.int32, (MOBA_VT_ROWS - ATTN_HEAD_DIM, blk), 0) == 0).astype(BF16)
        for jb in range(nb):
            vt_ref[jb, 0:ATTN_HEAD_DIM, :] = v_ref[0, jb * blk:(jb + 1) * blk, :].astype(F32).T.astype(BF16)
            vt_ref[jb, ATTN_HEAD_DIM:MOBA_VT_ROWS, :] = ones_rows

    qw = MOBA_QBLOCKS * blk
    i0 = i * MOBA_QBLOCKS
    q_t = q_ref[0].astype(F32).T
    qs_t = (q_t * (scale * LOG2E)).astype(BF16)

    gate = jnp.dot(kmean_ref[...].astype(BF16), q_t.astype(BF16), preferred_element_type=F32)
    nbp = kmean_ref.shape[0]
    row = lax.broadcasted_iota(jnp.int32, (nbp, qw), 0).astype(F32)
    own = (i0 + lax.broadcasted_iota(jnp.int32, (1, qw), 1) // blk).astype(F32)
    g = jnp.where(row < own, gate, -jnp.inf)
    sel = jnp.zeros((nbp, qw), F32)
    for t in range(MOBA_TOPK):
        mx = jnp.max(g, axis=0, keepdims=True)
        idx = jnp.min(jnp.where(g == mx, row, float(nbp)), axis=0, keepdims=True)
        hit = row == idx
        sel = jnp.maximum(sel, jnp.where(hit & (own > t), 1.0, 0.0))
        g = jnp.where(hit, -jnp.inf, g)
    neg_ref[...] = jnp.where(sel > 0.5, 0.0, NEG_BIG)

    def far_scores(j0, slot):
        start = pl.multiple_of(j0 * blk, blk)
        s = jnp.dot(k_ref[0, pl.ds(start, MOBA_GROUP * blk), :], qs_t, preferred_element_type=F32)
        s_ref[slot] = s
        for u in range(MOBA_GROUP):
            cm_ref[slot, u:u + 1, :] = jnp.max(s[u * blk:(u + 1) * blk], axis=0, keepdims=True)

    far_scores(0, 0)

    def softmax_group(scores, offsets, v_idx, m_old, col_max=None):
        if col_max is None:
            col_max = [jnp.max(sj, axis=0, keepdims=True) for sj in scores]
        m_new = m_old
        for cm, off in zip(col_max, offsets):
            m_new = jnp.maximum(m_new, cm + off)
        pv = jnp.zeros((MOBA_VT_ROWS, m_old.shape[1]), F32)
        for sj, off, vj in zip(scores, offsets, v_idx):
            pj = jnp.exp2(sj + (off - m_new))
            pv = pv + jnp.dot(vt_ref[vj], pj.astype(BF16), preferred_element_type=F32)
        return m_new, pv

    far_bias = rel_ref[h, REL_BUCKETS - 1] * LOG2E

    key = lax.broadcasted_iota(jnp.int32, (blk, blk), 0)
    qry = lax.broadcasted_iota(jnp.int32, (blk, blk), 1)
    scores, offsets, v_idx, first_lane = [], [], [], []
    for e in range(MOBA_NEAR - 1 + MOBA_QBLOCKS):
        jn = i0 - (MOBA_NEAR - 1) + e
        jc = jnp.maximum(jn, 0)
        start = pl.multiple_of(jc * blk, blk)
        w_min = max(0, e - (MOBA_NEAR - 1))
        lo = w_min * blk
        se = jnp.dot(k_ref[0, pl.ds(start, blk), :], qs_t[:, lo:], preferred_element_type=F32)
        sel_row = neg_ref[pl.ds(jnp.where(jn >= 0, jn, nbp - 1), 1), :]
        s_parts, o_parts = [], []
        for w in range(w_min, MOBA_QBLOCKS):
            d = w + (MOBA_NEAR - 1) - e
            sw = se[:, (w - w_min) * blk:(w - w_min + 1) * blk]
            ow = sel_row[:, w * blk:(w + 1) * blk]
            if d == 0:
                sw = jnp.where(qry >= key, sw + bias_ref[0], NEG_BIG)
                ow = jnp.zeros((1, blk), F32)
            elif d < MOBA_NEAR:
                sw = sw + bias_ref[d]
            else:
                ow = ow + far_bias
            s_parts.append(sw)
            o_parts.append(ow)
        scores.append(jnp.concatenate(s_parts, axis=1))
        offsets.append(jnp.concatenate(o_parts, axis=1))
        v_idx.append(jc)
        first_lane.append(lo)
    m_floor = jnp.concatenate(
        [jnp.max(scores[MOBA_NEAR - 1 + w][:, 0:blk], axis=0, keepdims=True) for w in range(MOBA_QBLOCKS)], axis=1)
    parts = []
    for sj, off, vj, lo in zip(scores, offsets, v_idx, first_lane):
        mu, pvu = softmax_group([sj], [off], [vj], m_floor[:, lo:])
        if lo:
            mu = jnp.concatenate([m_floor[:, :lo], mu], axis=1)
            pvu = jnp.concatenate([jnp.zeros((MOBA_VT_ROWS, lo), F32), pvu], axis=1)
        parts.append((mu, pvu))
    m0 = parts[0][0]
    for mu, _ in parts[1:]:
        m0 = jnp.maximum(m0, mu)
    pv0 = jnp.zeros((MOBA_VT_ROWS, qw), F32)
    for mu, pvu in parts:
        pv0 = pv0 + jnp.exp2(mu - m0) * pvu
    m_ref[...] = m0
    acc_ref[...] = pv0

    n_far = jnp.maximum(i0 - (MOBA_NEAR - 1), 0)

    n_groups = (n_far + MOBA_GROUP - 1) // MOBA_GROUP

    def far_group(gi, slot, prefetch):
        j0 = gi * MOBA_GROUP
        if prefetch:
            far_scores(j0 + MOBA_GROUP, 1 - slot)
        scores, offsets, v_idx, col_max = [], [], [], []
        for u in range(MOBA_GROUP):
            ju = j0 + u
            scores.append(s_ref[slot, u * blk:(u + 1) * blk, :])
            col_max.append(cm_ref[slot, u:u + 1, :])
            offsets.append(neg_ref[pl.ds(jnp.where(ju < n_far, ju, nbp - 1), 1), :] + far_bias)
            v_idx.append(ju)
        m_old = m_ref[...]
        m_new, pv = softmax_group(scores, offsets, v_idx, m_old, col_max)
        acc_ref[...] = jnp.exp2(m_old - m_new) * acc_ref[...] + pv
        m_ref[...] = m_new

    def far_body(t, carry):
        far_group(2 * t, 0, True)
        far_group(2 * t + 1, 1, True)
        return carry

    n_pairs = jnp.maximum(n_groups - 1, 0) // 2
    lax.fori_loop(0, n_pairs, far_body, 0)
    g_rest = 2 * n_pairs
    n_rest = n_groups - g_rest

    @pl.when(n_rest == 1)
    def _():
        far_group(g_rest, 0, False)

    @pl.when(n_rest == 2)
    def _():
        far_group(g_rest, 0, True)
        far_group(g_rest + 1, 1, False)

    out_t = acc_ref[0:ATTN_HEAD_DIM, :] / acc_ref[ATTN_HEAD_DIM:ATTN_HEAD_DIM + 1, :]
    o_ref[0] = out_t.T.astype(o_ref.dtype)


def _moba(proj3, kmean3, rel_t):
    bsz, s_len, _ = proj3.shape
    blk = MOBA_BLOCK
    nb = s_len // blk
    qw = MOBA_QBLOCKS * blk
    assert nb * blk == s_len and nb % MOBA_QBLOCKS == 0 and nb >= MOBA_GROUP
    nbp = -(-(nb + 1) // 16) * 16
    cb = LANES
    return pl.pallas_call(
        functools.partial(_moba_kernel, nb=nb),
        grid=(ATTN_HEADS, bsz, nb // MOBA_QBLOCKS),
        in_specs=[
            pl.BlockSpec(memory_space=pltpu.SMEM),
            pl.BlockSpec((1, qw, cb), lambda h, b, i: (b, i, COL_Q // cb + h)),
            pl.BlockSpec((1, s_len, cb), lambda h, b, i: (b, 0, COL_K // cb + h)),
            pl.BlockSpec((1, s_len, cb), lambda h, b, i: (b, 0, COL_V // cb + h)),
            pl.BlockSpec((1, nb, cb), lambda h, b, i: (b, 0, h)),
        ],
        out_specs=pl.BlockSpec((1, qw, cb), lambda h, b, i: (b, i, h)),
        out_shape=jax.ShapeDtypeStruct((bsz, s_len, ATTN_HEADS * ATTN_HEAD_DIM), BF16),
        scratch_shapes=[
            pltpu.VMEM((nbp, ATTN_HEAD_DIM), F32),
            pltpu.VMEM((MOBA_NEAR, blk, blk), F32),
            pltpu.VMEM((nb, MOBA_VT_ROWS, blk), BF16),
            pltpu.VMEM((nbp, qw), F32),
            pltpu.VMEM((2, MOBA_GROUP * blk, qw), F32),
            pltpu.VMEM((2, SUBLANES, qw), F32),
            pltpu.VMEM((1, qw), F32),
            pltpu.VMEM((MOBA_VT_ROWS, qw), F32),
        ],
        compiler_params=_cparams(("arbitrary", "arbitrary", "arbitrary")),
        name="moba",
    )(rel_t, proj3, proj3, proj3, kmean3)


def _ssd_kernel(xbc_ref, z_ref, dt_ref, cw_ref, cbias_ref, dtb_ref, alog_ref, dskip_ref, ng_ref,
                tril_ref, e_ref, shift_ref, o_ref, tail_ref, state_ref, xs_ref, xcb_ref, ea_ref, wend_ref, bm_ref,
                cm_ref):
    L = SSM_CHUNK
    G = SSM_GROUPS
    N = SSM_STATE
    GW = SSM_INNER // G
    pad = SUBLANES
    nt = (((1,), (1,)), ((), ()))

    @pl.when(pl.program_id(1) == 0)
    def _():
        tail_ref[...] = jnp.zeros_like(tail_ref)
        state_ref[...] = jnp.zeros_like(state_ref)

    dtr = dt_ref[0] + dtb_ref[...]
    dt = jnp.maximum(dtr, 0.0) + jnp.log1p(jnp.exp(-jnp.abs(dtr)))
    a = dt * (-jnp.exp(alog_ref[...]))
    a_cat = jnp.concatenate(_split_bf16(a, 3), axis=0)
    acs = jnp.dot(tril_ref[...], a_cat, preferred_element_type=F32) * LOG2E
    acs_t = acs.T

    dt_cat = jnp.concatenate(_split_bf16(dt, 2), axis=1)
    acs_cat = jnp.concatenate(_split_bf16(acs, 2), axis=1)

    SW = 2 * LANES
    for c in range(SSM_CONV_DIM // SW):
        cols = slice(c * SW, (c + 1) * SW)
        xb = xbc_ref[0, :, cols]
        x = xb.astype(F32)
        conv = cbias_ref[:, cols] + x * cw_ref[SSM_CONV - 1:SSM_CONV, cols]
        head = jnp.zeros((pad, SW), F32)
        for k in range(SSM_CONV - 1):
            s = SSM_CONV - 1 - k
            w_k = cw_ref[k:k + 1, cols]
            conv = conv + jnp.dot(shift_ref[s - 1], xb, preferred_element_type=F32) * w_k
            head = head + tail_ref[pad - s:2 * pad - s, cols] * w_k
        conv = jnp.concatenate([conv[0:pad] + head, conv[pad:]], axis=0)
        tail_ref[0:pad, cols] = x[L - pad:L]
        xa = _silu(conv)
        if c * SW < SSM_INNER:
            dt_x = jnp.dot(dt_cat, e_ref[:, cols], preferred_element_type=F32)
            acs_x = jnp.dot(acs_cat, e_ref[:, cols], preferred_element_type=F32)
            xc = xa * dt_x
            xs_ref[:, cols] = xa
            xcb_ref[:, cols] = xc.astype(BF16)
            ea_ref[:, cols] = jnp.exp2(acs_x)
            wend_ref[:, cols] = (jnp.exp2(acs_x[L - 1:L, :] - acs_x) * xc).astype(BF16)
        elif c * SW < SSM_INNER + G * N:
            bm_ref[:, c * SW - SSM_INNER:(c + 1) * SW - SSM_INNER] = xa
        else:
            cm_ref[:, c * SW - SSM_INNER - G * N:(c + 1) * SW - SSM_INNER - G * N] = xa.astype(BF16)

    row = lax.broadcasted_iota(jnp.int32, (L, L), 0)
    col = lax.broadcasted_iota(jnp.int32, (L, L), 1)
    causal = row >= col
    lane = lax.broadcasted_iota(jnp.int32, (L, LANES), 1)
    low_half = lane < SSM_HEAD_DIM

    for g in range(G):
        gcols = slice(g * GW, (g + 1) * GW)
        bg = bm_ref[:, g * N:(g + 1) * N]
        cg = cm_ref[:, g * N:(g + 1) * N]
        cb = lax.dot_general(cg, bg.astype(BF16), nt, preferred_element_type=F32)
        st = state_ref[g]
        ea_g = ea_ref[:, gcols]
        y_off = jnp.dot(cg, st.astype(BF16), preferred_element_type=F32) * ea_g
        y_parts = []
        for pr in range(GW // LANES):
            c0 = g * GW + pr * LANES
            x_pair = xcb_ref[:, c0:c0 + LANES]
            zero = jnp.zeros_like(x_pair)
            y_pair = y_off[:, pr * LANES:(pr + 1) * LANES]
            for half in range(2):
                hd = c0 // SSM_HEAD_DIM + half
                seg = acs[:, hd:hd + 1] - acs_t[hd:hd + 1, :]
                decay = jnp.exp2(jnp.where(causal, seg, -jnp.inf))
                mmat = (cb * decay).astype(BF16)
                x_half = jnp.where(low_half, x_pair, zero) if half == 0 else jnp.where(low_half, zero, x_pair)
                y_pair = y_pair + jnp.dot(mmat, x_half, preferred_element_type=F32)
            y_parts.append(y_pair)
        bg_t = bg.T.astype(BF16)
        state_ref[g] = (st * ea_g[L - 1:L, :]
                        + jnp.dot(bg_t, wend_ref[:, gcols], preferred_element_type=F32))

        y = jnp.concatenate(y_parts, axis=1) + dskip_ref[:, gcols] * xs_ref[:, gcols]
        v = y * _silu(z_ref[0, :, gcols].astype(F32))
        ms = jnp.mean(v * v, axis=-1, keepdims=True)
        o_ref[0, :, gcols] = (v * lax.rsqrt(ms + NORM_EPS) * ng_ref[:, gcols]).astype(o_ref.dtype)


def _ssd(proj3, dt3, conv_w, conv_b, dt_bias, a_log, d_skip_x, norm_g, tril3, e2, shifts):
    bsz, s_len, _ = proj3.shape
    L = SSM_CHUNK
    nc = s_len // L
    assert nc * L == s_len
    const = lambda b, c: (0, 0)
    return pl.pallas_call(
        _ssd_kernel,
        grid=(bsz, nc),
        in_specs=[
            pl.BlockSpec((1, L, SSM_CONV_DIM), lambda b, c: (b, c, COL_XBC // SSM_CONV_DIM)),
            pl.BlockSpec((1, L, SSM_INNER), lambda b, c: (b, c, COL_Z // SSM_INNER)),
            pl.BlockSpec((1, L, LANES), lambda b, c: (b, c, 0)),
            pl.BlockSpec((SSM_CONV, SSM_CONV_DIM), const),
            pl.BlockSpec((1, SSM_CONV_DIM), const),
            pl.BlockSpec((1, LANES), const),
            pl.BlockSpec((1, LANES), const),
            pl.BlockSpec((1, SSM_INNER), const),
            pl.BlockSpec((1, SSM_INNER), const),
            pl.BlockSpec((L, 3 * L), const),
            pl.BlockSpec((2 * LANES, SSM_INNER), const),
            pl.BlockSpec((SSM_CONV - 1, L, L), lambda b, c: (0, 0, 0)),
        ],
        out_specs=pl.BlockSpec((1, L, SSM_INNER), lambda b, c: (b, c, 0)),
        out_shape=jax.ShapeDtypeStruct((bsz, s_len, SSM_INNER), BF16),
        scratch_shapes=[
            pltpu.VMEM((2 * SUBLANES, SSM_CONV_DIM), F32),
            pltpu.VMEM((SSM_GROUPS, SSM_STATE, SSM_INNER // SSM_GROUPS), F32),
            pltpu.VMEM((L, SSM_INNER), F32),
            pltpu.VMEM((L, SSM_INNER), BF16),
            pltpu.VMEM((L, SSM_INNER), F32),
            pltpu.VMEM((L, SSM_INNER), BF16),
            pltpu.VMEM((L, SSM_GROUPS * SSM_STATE), F32),
            pltpu.VMEM((L, SSM_GROUPS * SSM_STATE), BF16),
        ],
        compiler_params=_cparams(("arbitrary", "arbitrary")),
        name="ssd",
    )(proj3, proj3, dt3, conv_w, conv_b, dt_bias, a_log, d_skip_x, norm_g, tril3, e2, shifts)


def _memkv_kernel(mem_ref, g_ref, w_ref, kv_ref):
    x = mem_ref[...]
    ms = jnp.mean(x * x, axis=-1, keepdims=True)
    u = (x * lax.rsqrt(ms + NORM_EPS) * g_ref[...]).astype(BF16)
    kv_ref[...] = jnp.dot(u, w_ref[...], preferred_element_type=F32).astype(BF16)


def _mem_kv(mem2, g, w_kv):
    rows = mem2.shape[0]
    width = w_kv.shape[1]
    return pl.pallas_call(
        _memkv_kernel,
        grid=(1,),
        in_specs=[
            pl.BlockSpec((rows, D_MODEL), lambda i: (0, 0)),
            pl.BlockSpec((1, D_MODEL), lambda i: (0, 0)),
            pl.BlockSpec((D_MODEL, width), lambda i: (0, 0)),
        ],
        out_specs=pl.BlockSpec((rows, width), lambda i: (0, 0)),
        out_shape=jax.ShapeDtypeStruct((rows, width), BF16),
        compiler_params=_cparams(("arbitrary",)),
        name="mem_kv",
    )(mem2, g, w_kv)


def _merge_kernel(x_ref, oa_ref, os_ref, qm_ref, gl_ref, kv_ref, bg_ref, wa_ref, ws_ref, wm_ref, wo_ref, h_ref):
    nt = (((1,), (1,)), ((), ()))
    hd = MEM_HEAD_DIM
    width = MEM_HEADS * hd
    scale = hd ** -0.5
    qm = qm_ref[0]
    kv = kv_ref[0]
    outs = []
    for hh in range(MEM_HEADS):
        q = qm[:, hh * hd:(hh + 1) * hd]
        km = kv[:, hh * hd:(hh + 1) * hd]
        vm = kv[:, width + hh * hd:width + (hh + 1) * hd]
        s = lax.dot_general(q, km, nt, preferred_element_type=F32) * scale
        p = jnp.exp(s - jnp.max(s, axis=1, keepdims=True))
        o = jnp.dot(p.astype(BF16), vm, preferred_element_type=F32)
        outs.append(o / jnp.sum(p, axis=1, keepdims=True))
    o_mem = jnp.concatenate(outs, axis=1).astype(BF16)

    gates = _sigmoid(gl_ref[0].astype(F32) + bg_ref[...])
    merged = (gates[:, :D_MODEL] * jnp.dot(oa_ref[0], wa_ref[...], preferred_element_type=F32)
              + gates[:, D_MODEL:2 * D_MODEL] * jnp.dot(os_ref[0], ws_ref[...], preferred_element_type=F32)
              + gates[:, 2 * D_MODEL:] * jnp.dot(o_mem, wm_ref[...], preferred_element_type=F32))
    h_ref[0] = x_ref[0] + jnp.dot(merged.astype(BF16), wo_ref[...], preferred_element_type=F32)


def _resident(shape):
    return pl.BlockSpec(shape, lambda *_: (0,) * len(shape), pipeline_mode=pl.Buffered(1))


def _merge(x, o_attn, o_ssm, proj3, kv3, b_gate, wa, ws, wm, wo, tm):
    bsz, s_len, _ = x.shape
    mem_len = kv3.shape[1]
    return pl.pallas_call(
        _merge_kernel,
        grid=(bsz, s_len // tm),
        in_specs=[
            pl.BlockSpec((1, tm, D_MODEL), lambda b, i: (b, i, 0)),
            pl.BlockSpec((1, tm, D_MODEL), lambda b, i: (b, i, 0)),
            pl.BlockSpec((1, tm, SSM_INNER), lambda b, i: (b, i, 0)),
            pl.BlockSpec((1, tm, D_MODEL), lambda b, i: (b, i, COL_QM // D_MODEL)),
            pl.BlockSpec((1, tm, 3 * D_MODEL), lambda b, i: (b, i, COL_GATE // (3 * D_MODEL))),
            pl.BlockSpec((1, mem_len, 2 * D_MODEL), lambda b, i: (b, 0, 0)),
            _resident((1, 3 * D_MODEL)),
            _resident((D_MODEL, D_MODEL)),
            _resident((SSM_INNER, D_MODEL)),
            _resident((D_MODEL, D_MODEL)),
            _resident((D_MODEL, D_MODEL)),
        ],
        out_specs=pl.BlockSpec((1, tm, D_MODEL), lambda b, i: (b, i, 0)),
        out_shape=jax.ShapeDtypeStruct((bsz, s_len, D_MODEL), F32),
        compiler_params=_cparams(("arbitrary", "arbitrary")),
        name="merge",
    )(x, o_attn, o_ssm, proj3, proj3, kv3, b_gate, wa, ws, wm, wo)


def _ffn_kernel(h_ref, ng_ref, wup_ref, cw_ref, cb_ref, wdn_ref, fg_ref, o_ref, hid_ref, *, tm):
    pad = SUBLANES

    @pl.when(pl.program_id(1) == 0)
    def _():
        hid_ref[0:pad, :] = jnp.zeros((pad, 2 * FFN_HIDDEN), F32)

    h = h_ref[0]
    ms = jnp.mean(h * h, axis=-1, keepdims=True)
    u = (h * lax.rsqrt(ms + NORM_EPS) * ng_ref[...]).astype(BF16)
    hid_ref[pad:pad + tm, :] = jnp.dot(u, wup_ref[...], preferred_element_type=F32)
    conv = cb_ref[...]
    for k in range(FFN_CONV):
        off = pad - (FFN_CONV - 1) + k
        conv = conv + hid_ref[off:off + tm, :] * cw_ref[k:k + 1, :]
    hid_ref[0:pad, :] = hid_ref[tm:tm + pad, :]
    act = (_silu(conv[:, :FFN_HIDDEN]) * conv[:, FFN_HIDDEN:]).astype(BF16)
    y = h + jnp.dot(act, wdn_ref[...], preferred_element_type=F32)
    ms2 = jnp.mean(y * y, axis=-1, keepdims=True)
    o_ref[0] = y * lax.rsqrt(ms2 + NORM_EPS) * fg_ref[...]


def _ffn(h, norm_g, w_up, conv_w, conv_b, w_down, final_g, tm):
    bsz, s_len, _ = h.shape
    f2 = 2 * FFN_HIDDEN
    return pl.pallas_call(
        functools.partial(_ffn_kernel, tm=tm),
        grid=(bsz, s_len // tm),
        in_specs=[
            pl.BlockSpec((1, tm, D_MODEL), lambda b, i: (b, i, 0)),
            _resident((1, D_MODEL)),
            _resident((D_MODEL, f2)),
            _resident((FFN_CONV, f2)),
            _resident((1, f2)),
            _resident((FFN_HIDDEN, D_MODEL)),
            _resident((1, D_MODEL)),
        ],
        out_specs=pl.BlockSpec((1, tm, D_MODEL), lambda b, i: (b, i, 0)),
        out_shape=jax.ShapeDtypeStruct((bsz, s_len, D_MODEL), F32),
        scratch_shapes=[pltpu.VMEM((tm + 2 * SUBLANES, f2), F32)],
        compiler_params=_cparams(("arbitrary", "arbitrary")),
        name="ffn",
    )(h, norm_g, w_up, conv_w, conv_b, w_down, final_g)


def _ssd_constants():
    L = SSM_CHUNK
    tril = np.tril(np.ones((L, L), np.float32))
    tril3 = np.concatenate([tril, tril, tril], axis=1)
    e = np.zeros((LANES, SSM_INNER), np.float32)
    for hd in range(SSM_HEADS):
        e[hd, hd * SSM_HEAD_DIM:(hd + 1) * SSM_HEAD_DIM] = 1.0
    e2 = np.concatenate([e, e], axis=0)
    shifts = np.stack([np.eye(L, k=-s, dtype=np.float32) for s in range(1, SSM_CONV)])
    return jnp.asarray(tril3, BF16), jnp.asarray(e2, BF16), jnp.asarray(shifts, BF16)


def _pad_lanes(v):
    return jnp.pad(v, ((0, 0), (0, LANES - v.shape[1])))


def _layer(h, mem, rel_bias, mix_norm_g, w_in, b_gate, ssm_conv_w, ssm_conv_b, ssm_dt_bias, ssm_A_log, ssm_D,
           ssm_norm_g, mem_norm_g, w_mem_kv, w_br_attn, w_br_ssm, w_br_mem, w_out, ffn_norm_g, w_ffn_up,
           ffn_conv_w, ffn_conv_b, w_ffn_down, final_g):
    bsz, s_len, _ = h.shape
    w_t = w_in.T.astype(BF16)
    w_dt = jnp.pad(w_t[IN_PROJ_SPLIT:IN_PROJ_SPLIT + SSM_DT_COLS], ((0, LANES - SSM_DT_COLS), (0, 0)))

    proj, dt_raw, kmean = _in_proj(h.reshape(bsz * s_len, D_MODEL), mix_norm_g[None, :], w_t, w_dt,
                                   tm=IN_PROJ_TM, tn=IN_PROJ_TN)
    proj3 = proj.reshape(bsz, s_len, PROJ_WIDTH)
    dt3 = dt_raw.reshape(bsz, s_len, LANES)
    kmean3 = kmean.reshape(bsz, s_len // MOBA_BLOCK, ATTN_HEADS * ATTN_HEAD_DIM)

    o_attn = _moba(proj3, kmean3, rel_bias.T)

    tril3, e2, shifts = _ssd_constants()
    o_ssm = _ssd(proj3, dt3, ssm_conv_w, ssm_conv_b[None, :], _pad_lanes(ssm_dt_bias[None, :]),
                 _pad_lanes(ssm_A_log[None, :]), jnp.repeat(ssm_D, SSM_HEAD_DIM)[None, :], ssm_norm_g[None, :],
                 tril3, e2, shifts)

    mem_len = mem.shape[1]
    kv = _mem_kv(mem.reshape(bsz * mem_len, D_MODEL), mem_norm_g[None, :], w_mem_kv.astype(BF16))
    kv3 = kv.reshape(bsz, mem_len, 2 * D_MODEL)

    h1 = _merge(h, o_attn, o_ssm, proj3, kv3, b_gate[None, :], w_br_attn.astype(BF16), w_br_ssm.astype(BF16),
                w_br_mem.astype(BF16), w_out.astype(BF16), tm=MERGE_TM)

    return _ffn(h1, ffn_norm_g[None, :], w_ffn_up.astype(BF16), ffn_conv_w, ffn_conv_b[None, :],
                w_ffn_down.astype(BF16), final_g[None, :], tm=FFN_TM)


def kernel(x, mem, rel_bias, mix_norm_g, w_in, b_gate, ssm_conv_w, ssm_conv_b, ssm_dt_bias, ssm_A_log, ssm_D,
           ssm_norm_g, mem_norm_g, w_mem_kv, w_br_attn, w_br_ssm, w_br_mem, w_out, ffn_norm_g, w_ffn_up,
           ffn_conv_w, ffn_conv_b, w_ffn_down, final_norm_g):
    assert w_in.shape[0] == 1, "single-layer trunk"
    return _layer(x, mem, rel_bias, mix_norm_g[0], w_in[0], b_gate[0], ssm_conv_w[0], ssm_conv_b[0],
                  ssm_dt_bias[0], ssm_A_log[0], ssm_D[0], ssm_norm_g[0], mem_norm_g[0], w_mem_kv[0],
                  w_br_attn[0], w_br_ssm[0], w_br_mem[0], w_out[0], ffn_norm_g[0], w_ffn_up[0],
                  ffn_conv_w[0], ffn_conv_b[0], w_ffn_down[0], final_norm_g)
```

```python
import functools
import math

import numpy as np
import jax
import jax.numpy as jnp
from jax import lax
from jax.experimental import pallas as pl
from jax.experimental.pallas import tpu as pltpu

F32 = jnp.float32
BF16 = jnp.bfloat16

D_MODEL = 1024
ATTN_HEADS = 8
ATTN_HEAD_DIM = 128
MOBA_BLOCK = 256
MOBA_TOPK = 3
REL_BUCKETS = 32
REL_MAX_DIST = 1024
SSM_INNER = 2048
SSM_HEAD_DIM = 64
SSM_HEADS = 32
SSM_GROUPS = 4
SSM_STATE = 128
SSM_CONV = 4
SSM_CHUNK = 256
SSM_CONV_DIM = 3072
MEM_HEADS = 4
MEM_HEAD_DIM = 256
FFN_HIDDEN = 2816
FFN_CONV = 3
NORM_EPS = 1e-6

LANES = 128
SUBLANES = 8
BF16_SUBLANES = 16
VMEM_LIMIT = 56 * 1024 * 1024

IN_PROJ_TM = 2048
IN_PROJ_TN = ATTN_HEADS * ATTN_HEAD_DIM
MERGE_TM = 512
FFN_TM = 512

ATTN_WIDTH = ATTN_HEADS * ATTN_HEAD_DIM
COL_XBC = 0
COL_GATE = COL_XBC + SSM_CONV_DIM
COL_Z = COL_GATE + 3 * D_MODEL
COL_Q = COL_Z + SSM_INNER
COL_K = COL_Q + ATTN_WIDTH
COL_V = COL_K + ATTN_WIDTH
COL_QM = COL_V + ATTN_WIDTH
PROJ_WIDTH = COL_QM + MEM_HEADS * MEM_HEAD_DIM
IN_PROJ_PERM = (8, 9, 10, 6, 7, 0, 1, 2, 11, 3, 4, 5)
IN_PROJ_K_TILE = 1
IN_PROJ_SPLIT = 3 * ATTN_WIDTH + SSM_INNER + SSM_CONV_DIM
SSM_DT_COLS = SSM_HEADS

MOBA_NEAR = -(-(REL_MAX_DIST + MOBA_BLOCK - 1) // MOBA_BLOCK)
NEG_BIG = -1e30
LOG2E = math.log2(math.e)
MOBA_QBLOCKS = 2
MOBA_GROUP = 4
MOBA_VT_ROWS = ATTN_HEAD_DIM + BF16_SUBLANES
assert MOBA_GROUP <= MOBA_NEAR + 1
assert MOBA_GROUP <= SUBLANES


def _cparams(sem):
    return pltpu.CompilerParams(dimension_semantics=sem, vmem_limit_bytes=VMEM_LIMIT)


def _sigmoid(x):
    return 1.0 / (1.0 + jnp.exp(-x))


def _silu(x):
    h = 0.5 * x
    return h + h * jnp.tanh(h)


def _split_bf16(x, parts):
    out = []
    r = x
    for _ in range(parts):
        hi = r.astype(BF16)
        out.append(hi)
        r = r - hi.astype(F32)
    return out


def _inproj_kernel(perm_ref, x_ref, g_ref, w_ref, wdt_ref, proj_ref, dt_ref, kmean_ref, u_ref, *, k_tile):
    del perm_ref
    j = pl.program_id(1)
    nt = (((1,), (1,)), ((), ()))

    @pl.when(j == 0)
    def _():
        x = x_ref[...]
        ms = jnp.mean(x * x, axis=-1, keepdims=True)
        u = (x * lax.rsqrt(ms + NORM_EPS) * g_ref[...]).astype(BF16)
        u_ref[...] = u
        dt_ref[...] = lax.dot_general(u, wdt_ref[...], nt, preferred_element_type=F32)

    acc = lax.dot_general(u_ref[...], w_ref[...], nt, preferred_element_type=F32)
    proj_ref[...] = acc.astype(BF16)

    @pl.when(j == k_tile)
    def _():
        blk = MOBA_BLOCK
        for r in range(acc.shape[0] // blk):
            kmean_ref[0, r:r + 1, :] = jnp.sum(acc[r * blk:(r + 1) * blk], axis=0, keepdims=True) * (1.0 / blk)


def _in_proj(x2, g, w_t, w_dt, tm, tn):
    t = x2.shape[0]
    kw = ATTN_HEADS * ATTN_HEAD_DIM
    n_a = IN_PROJ_SPLIT // tn
    n_b = (w_t.shape[0] - IN_PROJ_SPLIT - SSM_DT_COLS) // tn
    assert tn == kw and tm % MOBA_BLOCK == 0 and (n_a + n_b) * tn == PROJ_WIDTH == len(IN_PROJ_PERM) * tn
    grid_spec = pltpu.PrefetchScalarGridSpec(
        num_scalar_prefetch=1,
        grid=(t // tm, n_a + n_b),
        in_specs=[
            pl.BlockSpec((tm, D_MODEL), lambda i, j, perm: (i, 0)),
            pl.BlockSpec((1, D_MODEL), lambda i, j, perm: (0, 0)),
            pl.BlockSpec((pl.Element(tn), pl.Element(D_MODEL)),
                         lambda i, j, perm: (pl.multiple_of(j * tn + jnp.where(j >= n_a, SSM_DT_COLS, 0), BF16_SUBLANES), 0)),
            pl.BlockSpec((LANES, D_MODEL), lambda i, j, perm: (0, 0)),
        ],
        out_specs=[
            pl.BlockSpec((tm, tn), lambda i, j, perm: (i, perm[j])),
            pl.BlockSpec((tm, LANES), lambda i, j, perm: (i, 0)),
            pl.BlockSpec((1, tm // MOBA_BLOCK, kw), lambda i, j, perm: (i, 0, 0)),
        ],
        scratch_shapes=[pltpu.VMEM((tm, D_MODEL), BF16)],
    )
    return pl.pallas_call(
        functools.partial(_inproj_kernel, k_tile=IN_PROJ_K_TILE),
        grid_spec=grid_spec,
        out_shape=[
            jax.ShapeDtypeStruct((t, PROJ_WIDTH), BF16),
            jax.ShapeDtypeStruct((t, LANES), F32),
            jax.ShapeDtypeStruct((t // tm, tm // MOBA_BLOCK, kw), F32),
        ],
        compiler_params=_cparams(("arbitrary", "arbitrary")),
        name="in_proj",
    )(jnp.asarray(IN_PROJ_PERM, jnp.int32), x2, g, w_t, w_dt)


def _t5_bucket(dist):
    n = jnp.maximum(dist, 0)
    max_exact = REL_BUCKETS // 2
    nf = jnp.maximum(n, max_exact).astype(F32)
    large = max_exact + (jnp.log(nf * (1.0 / max_exact)) / math.log(REL_MAX_DIST / max_exact)
                         * (REL_BUCKETS - max_exact)).astype(jnp.int32)
    large = jnp.minimum(large, REL_BUCKETS - 1)
    return jnp.where(n < max_exact, n, large)


def _moba_kernel(rel_ref, q_ref, k_ref, v_ref, km_ref, o_ref, kmean_ref, bias_ref, vt_ref, neg_ref, s_ref, cm_ref,
                 m_ref, acc_ref, *, nb):
    h = pl.program_id(0)
    b = pl.program_id(1)
    i = pl.program_id(2)
    blk = MOBA_BLOCK
    scale = ATTN_HEAD_DIM ** -0.5


    @pl.when((b == 0) & (i == 0))
    def _():
        key = lax.broadcasted_iota(jnp.int32, (blk, blk), 0)
        qry = lax.broadcasted_iota(jnp.int32, (blk, blk), 1)
        for d in range(MOBA_NEAR):
            bucket = _t5_bucket(d * blk + qry - key)
            tile = jnp.zeros((blk, blk), F32)
            for bk in range(REL_BUCKETS):
                tile = jnp.where(bucket == bk, rel_ref[h, bk], tile)
            bias_ref[d] = tile * LOG2E

    @pl.when(i == 0)
    def _():
        kmean_ref[...] = jnp.zeros_like(kmean_ref)
        kmean_ref[0:nb, :] = km_ref[0]
        ones_rows = (lax.broadcasted_iota(jnp.int32, (MOBA_VT_ROWS - ATTN_HEAD_DIM, blk), 0) == 0).astype(BF16)
        for jb in range(nb):
            vt_ref[jb, 0:ATTN_HEAD_DIM, :] = v_ref[0, jb * blk:(jb + 1) * blk, :].astype(F32).T.astype(BF16)
            vt_ref[jb, ATTN_HEAD_DIM:MOBA_VT_ROWS, :] = ones_rows

    qw = MOBA_QBLOCKS * blk
    i0 = i * MOBA_QBLOCKS
    q_t = q_ref[0].astype(F32).T
    qs_t = (q_t * (scale * LOG2E)).astype(BF16)

    gate = jnp.dot(kmean_ref[...].astype(BF16), q_t.astype(BF16), preferred_element_type=F32)
    nbp = kmean_ref.shape[0]
    row = lax.broadcasted_iota(jnp.int32, (nbp, qw), 0).astype(F32)
    own = (i0 + lax.broadcasted_iota(jnp.int32, (1, qw), 1) // blk).astype(F32)
    g = jnp.where(row < own, gate, -jnp.inf)
    sel = jnp.zeros((nbp, qw), F32)
    for t in range(MOBA_TOPK):
        mx = jnp.max(g, axis=0, keepdims=True)
        idx = jnp.min(jnp.where(g == mx, row, float(nbp)), axis=0, keepdims=True)
        hit = row == idx
        sel = jnp.maximum(sel, jnp.where(hit & (own > t), 1.0, 0.0))
        g = jnp.where(hit, -jnp.inf, g)
    neg_ref[...] = jnp.where(sel > 0.5, 0.0, NEG_BIG)

    def far_scores(j0, slot):
        start = pl.multiple_of(j0 * blk, blk)
        s = jnp.dot(k_ref[0, pl.ds(start, MOBA_GROUP * blk), :], qs_t, preferred_element_type=F32)
        s_ref[slot] = s
        for u in range(MOBA_GROUP):
            cm_ref[slot, u:u + 1, :] = jnp.max(s[u * blk:(u + 1) * blk], axis=0, keepdims=True)

    far_scores(0, 0)

    def softmax_group(scores, offsets, v_idx, m_old, col_max=None):
        if col_max is None:
            col_max = [jnp.max(sj, axis=0, keepdims=True) for sj in scores]
        m_new = m_old
        for cm, off in zip(col_max, offsets):
            m_new = jnp.maximum(m_new, cm + off)
        pv = jnp.zeros((MOBA_VT_ROWS, m_old.shape[1]), F32)
        for sj, off, vj in zip(scores, offsets, v_idx):
            pj = jnp.exp2(sj + (off - m_new))
            pv = pv + jnp.dot(vt_ref[vj], pj.astype(BF16), preferred_element_type=F32)
        return m_new, pv

    far_bias = rel_ref[h, REL_BUCKETS - 1] * LOG2E

    key = lax.broadcasted_iota(jnp.int32, (blk, blk), 0)
    qry = lax.broadcasted_iota(jnp.int32, (blk, blk), 1)
    scores, offsets, v_idx, first_lane = [], [], [], []
    for e in range(MOBA_NEAR - 1 + MOBA_QBLOCKS):
        jn = i0 - (MOBA_NEAR - 1) + e
        jc = jnp.maximum(jn, 0)
        start = pl.multiple_of(jc * blk, blk)
        w_min = max(0, e - (MOBA_NEAR - 1))
        lo = w_min * blk
        se = jnp.dot(k_ref[0, pl.ds(start, blk), :], qs_t[:, lo:], preferred_element_type=F32)
        sel_row = neg_ref[pl.ds(jnp.where(jn >= 0, jn, nbp - 1), 1), :]
        s_parts, o_parts = [], []
        for w in range(w_min, MOBA_QBLOCKS):
            d = w + (MOBA_NEAR - 1) - e
            sw = se[:, (w - w_min) * blk:(w - w_min + 1) * blk]
            ow = sel_row[:, w * blk:(w + 1) * blk]
            if d == 0:
                sw = jnp.where(qry >= key, sw + bias_ref[0], NEG_BIG)
                ow = jnp.zeros((1, blk), F32)
            elif d < MOBA_NEAR:
                sw = sw + bias_ref[d]
            else:
                ow = ow + far_bias
            s_parts.append(sw)
            o_parts.append(ow)
        scores.append(jnp.concatenate(s_parts, axis=1))
        offsets.append(jnp.concatenate(o_parts, axis=1))
        v_idx.append(jc)
        first_lane.append(lo)
    m_floor = jnp.concatenate(
        [jnp.max(scores[MOBA_NEAR - 1 + w][:, 0:blk], axis=0, keepdims=True) for w in range(MOBA_QBLOCKS)], axis=1)
    parts = []
    for sj, off, vj, lo in zip(scores, offsets, v_idx, first_lane):
        mu, pvu = softmax_group([sj], [off], [vj], m_floor[:, lo:])
        if lo:
            mu = jnp.concatenate([m_floor[:, :lo], mu], axis=1)
            pvu = jnp.concatenate([jnp.zeros((MOBA_VT_ROWS, lo), F32), pvu], axis=1)
        parts.append((mu, pvu))
    m0 = parts[0][0]
    for mu, _ in parts[1:]:
        m0 = jnp.maximum(m0, mu)
    pv0 = jnp.zeros((MOBA_VT_ROWS, qw), F32)
    for mu, pvu in parts:
        pv0 = pv0 + jnp.exp2(mu - m0) * pvu
    m_ref[...] = m0
    acc_ref[...] = pv0

    n_far = jnp.maximum(i0 - (MOBA_NEAR - 1), 0)

    n_groups = (n_far + MOBA_GROUP - 1) // MOBA_GROUP

    def far_group(gi, slot, prefetch):
        j0 = gi * MOBA_GROUP
        if prefetch:
            far_scores(j0 + MOBA_GROUP, 1 - slot)
        scores, offsets, v_idx, col_max = [], [], [], []
        for u in range(MOBA_GROUP):
            ju = j0 + u
            scores.append(s_ref[slot, u * blk:(u + 1) * blk, :])
            col_max.append(cm_ref[slot, u:u + 1, :])
            offsets.append(neg_ref[pl.ds(jnp.where(ju < n_far, ju, nbp - 1), 1), :] + far_bias)
            v_idx.append(ju)
        m_old = m_ref[...]
        m_new, pv = softmax_group(scores, offsets, v_idx, m_old, col_max)
        acc_ref[...] = jnp.exp2(m_old - m_new) * acc_ref[...] + pv
        m_ref[...] = m_new

    def far_body(t, carry):
        far_group(2 * t, 0, True)
        far_group(2 * t + 1, 1, True)
        return carry

    n_pairs = jnp.maximum(n_groups - 1, 0) // 2
    lax.fori_loop(0, n_pairs, far_body, 0)
    g_rest = 2 * n_pairs
    n_rest = n_groups - g_rest

    @pl.when(n_rest == 1)
    def _():
        far_group(g_rest, 0, False)

    @pl.when(n_rest == 2)
    def _():
        far_group(g_rest, 0, True)
        far_group(g_rest + 1, 1, False)

    out_t = acc_ref[0:ATTN_HEAD_DIM, :] / acc_ref[ATTN_HEAD_DIM:ATTN_HEAD_DIM + 1, :]
    o_ref[0] = out_t.T.astype(o_ref.dtype)


def _moba(proj3, kmean3, rel_t):
    bsz, s_len, _ = proj3.shape
    blk = MOBA_BLOCK
    nb = s_len // blk
    qw = MOBA_QBLOCKS * blk
    assert nb * blk == s_len and nb % MOBA_QBLOCKS == 0 and nb >= MOBA_GROUP
    nbp = -(-(nb + 1) // BF16_SUBLANES) * BF16_SUBLANES
    cb = LANES
    return pl.pallas_call(
        functools.partial(_moba_kernel, nb=nb),
        grid=(ATTN_HEADS, bsz, nb // MOBA_QBLOCKS),
        in_specs=[
            pl.BlockSpec(memory_space=pltpu.SMEM),
            pl.BlockSpec((1, qw, cb), lambda h, b, i: (b, i, COL_Q // cb + h)),
            pl.BlockSpec((1, s_len, cb), lambda h, b, i: (b, 0, COL_K // cb + h)),
            pl.BlockSpec((1, s_len, cb), lambda h, b, i: (b, 0, COL_V // cb + h)),
            pl.BlockSpec((1, nb, cb), lambda h, b, i: (b, 0, h)),
        ],
        out_specs=pl.BlockSpec((1, qw, cb), lambda h, b, i: (b, i, h)),
        out_shape=jax.ShapeDtypeStruct((bsz, s_len, ATTN_HEADS * ATTN_HEAD_DIM), BF16),
        scratch_shapes=[
            pltpu.VMEM((nbp, ATTN_HEAD_DIM), F32),
            pltpu.VMEM((MOBA_NEAR, blk, blk), F32),
            pltpu.VMEM((nb, MOBA_VT_ROWS, blk), BF16),
            pltpu.VMEM((nbp, qw), F32),
            pltpu.VMEM((2, MOBA_GROUP * blk, qw), F32),
            pltpu.VMEM((2, SUBLANES, qw), F32),
            pltpu.VMEM((1, qw), F32),
            pltpu.VMEM((MOBA_VT_ROWS, qw), F32),
        ],
        compiler_params=_cparams(("arbitrary", "arbitrary", "arbitrary")),
        name="moba",
    )(rel_t, proj3, proj3, proj3, kmean3)


def _ssd_kernel(xbc_ref, z_ref, dt_ref, cw_ref, cbias_ref, dtb_ref, alog_ref, dskip_ref, ng_ref,
                tril_ref, e_ref, shift_ref, o_ref, tail_ref, state_ref, xs_ref, xcb_ref, ea_ref, wend_ref, bm_ref,
                cm_ref):
    L = SSM_CHUNK
    G = SSM_GROUPS
    N = SSM_STATE
    GW = SSM_INNER // G
    pad = SUBLANES
    nt = (((1,), (1,)), ((), ()))

    @pl.when(pl.program_id(1) == 0)
    def _():
        tail_ref[...] = jnp.zeros_like(tail_ref)
        state_ref[...] = jnp.zeros_like(state_ref)

    dtr = dt_ref[0] + dtb_ref[...]
    dt = jnp.maximum(dtr, 0.0) + jnp.log1p(jnp.exp(-jnp.abs(dtr)))
    a = dt * (-jnp.exp(alog_ref[...]))
    a_cat = jnp.concatenate(_split_bf16(a, 3), axis=0)
    acs = jnp.dot(tril_ref[...], a_cat, preferred_element_type=F32) * LOG2E
    acs_t = acs.T

    dt_cat = jnp.concatenate(_split_bf16(dt, 2), axis=1)
    acs_cat = jnp.concatenate(_split_bf16(acs, 2), axis=1)

    SW = 2 * LANES
    for c in range(SSM_CONV_DIM // SW):
        cols = slice(c * SW, (c + 1) * SW)
        xb = xbc_ref[0, :, cols]
        x = xb.astype(F32)
        conv = cbias_ref[:, cols] + x * cw_ref[SSM_CONV - 1:SSM_CONV, cols]
        head = jnp.zeros((pad, SW), F32)
        for k in range(SSM_CONV - 1):
            s = SSM_CONV - 1 - k
            w_k = cw_ref[k:k + 1, cols]
            conv = conv + jnp.dot(shift_ref[s - 1], xb, preferred_element_type=F32) * w_k
            head = head + tail_ref[pad - s:2 * pad - s, cols] * w_k
        conv = jnp.concatenate([conv[0:pad] + head, conv[pad:]], axis=0)
        tail_ref[0:pad, cols] = x[L - pad:L]
        xa = _silu(conv)
        if c * SW < SSM_INNER:
            dt_x = jnp.dot(dt_cat, e_ref[:, cols], preferred_element_type=F32)
            acs_x = jnp.dot(acs_cat, e_ref[:, cols], preferred_element_type=F32)
            xc = xa * dt_x
            xs_ref[:, cols] = xa
            xcb_ref[:, cols] = xc.astype(BF16)
            ea_ref[:, cols] = jnp.exp2(acs_x)
            wend_ref[:, cols] = (jnp.exp2(acs_x[L - 1:L, :] - acs_x) * xc).astype(BF16)
        elif c * SW < SSM_INNER + G * N:
            bm_ref[:, c * SW - SSM_INNER:(c + 1) * SW - SSM_INNER] = xa
        else:
            cm_ref[:, c * SW - SSM_INNER - G * N:(c + 1) * SW - SSM_INNER - G * N] = xa.astype(BF16)

    row = lax.broadcasted_iota(jnp.int32, (L, L), 0)
    col = lax.broadcasted_iota(jnp.int32, (L, L), 1)
    causal = row >= col
    lane = lax.broadcasted_iota(jnp.int32, (L, LANES), 1)
    low_half = lane < SSM_HEAD_DIM

    for g in range(G):
        gcols = slice(g * GW, (g + 1) * GW)
        bg = bm_ref[:, g * N:(g + 1) * N]
        cg = cm_ref[:, g * N:(g + 1) * N]
        cb = lax.dot_general(cg, bg.astype(BF16), nt, preferred_element_type=F32)
        st = state_ref[g]
        ea_g = ea_ref[:, gcols]
        y_off = jnp.dot(cg, st.astype(BF16), preferred_element_type=F32) * ea_g
        y_parts = []
        for pr in range(GW // LANES):
            c0 = g * GW + pr * LANES
            x_pair = xcb_ref[:, c0:c0 + LANES]
            zero = jnp.zeros_like(x_pair)
            y_pair = y_off[:, pr * LANES:(pr + 1) * LANES]
            for half in range(2):
                hd = c0 // SSM_HEAD_DIM + half
                seg = acs[:, hd:hd + 1] - acs_t[hd:hd + 1, :]
                decay = jnp.exp2(jnp.where(causal, seg, -jnp.inf))
                mmat = (cb * decay).astype(BF16)
                x_half = jnp.where(low_half, x_pair, zero) if half == 0 else jnp.where(low_half, zero, x_pair)
                y_pair = y_pair + jnp.dot(mmat, x_half, preferred_element_type=F32)
            y_parts.append(y_pair)
        bg_t = bg.T.astype(BF16)
        state_ref[g] = (st * ea_g[L - 1:L, :]
                        + jnp.dot(bg_t, wend_ref[:, gcols], preferred_element_type=F32))

        y = jnp.concatenate(y_parts, axis=1) + dskip_ref[:, gcols] * xs_ref[:, gcols]
        v = y * _silu(z_ref[0, :, gcols].astype(F32))
        ms = jnp.mean(v * v, axis=-1, keepdims=True)
        o_ref[0, :, gcols] = (v * lax.rsqrt(ms + NORM_EPS) * ng_ref[:, gcols]).astype(o_ref.dtype)


def _ssd(proj3, dt3, conv_w, conv_b, dt_bias, a_log, d_skip_x, norm_g, tril3, e2, shifts):
    bsz, s_len, _ = proj3.shape
    L = SSM_CHUNK
    nc = s_len // L
    assert nc * L == s_len
    const = lambda b, c: (0, 0)
    return pl.pallas_call(
        _ssd_kernel,
        grid=(bsz, nc),
        in_specs=[
            pl.BlockSpec((1, L, SSM_CONV_DIM), lambda b, c: (b, c, COL_XBC // SSM_CONV_DIM)),
            pl.BlockSpec((1, L, SSM_INNER), lambda b, c: (b, c, COL_Z // SSM_INNER)),
            pl.BlockSpec((1, L, LANES), lambda b, c: (b, c, 0)),
            pl.BlockSpec((SSM_CONV, SSM_CONV_DIM), const),
            pl.BlockSpec((1, SSM_CONV_DIM), const),
            pl.BlockSpec((1, LANES), const),
            pl.BlockSpec((1, LANES), const),
            pl.BlockSpec((1, SSM_INNER), const),
            pl.BlockSpec((1, SSM_INNER), const),
            pl.BlockSpec((L, 3 * L), const),
            pl.BlockSpec((2 * LANES, SSM_INNER), const),
            pl.BlockSpec((SSM_CONV - 1, L, L), lambda b, c: (0, 0, 0)),
        ],
        out_specs=pl.BlockSpec((1, L, SSM_INNER), lambda b, c: (b, c, 0)),
        out_shape=jax.ShapeDtypeStruct((bsz, s_len, SSM_INNER), BF16),
        scratch_shapes=[
            pltpu.VMEM((2 * SUBLANES, SSM_CONV_DIM), F32),
            pltpu.VMEM((SSM_GROUPS, SSM_STATE, SSM_INNER // SSM_GROUPS), F32),
            pltpu.VMEM((L, SSM_INNER), F32),
            pltpu.VMEM((L, SSM_INNER), BF16),
            pltpu.VMEM((L, SSM_INNER), F32),
            pltpu.VMEM((L, SSM_INNER), BF16),
            pltpu.VMEM((L, SSM_GROUPS * SSM_STATE), F32),
            pltpu.VMEM((L, SSM_GROUPS * SSM_STATE), BF16),
        ],
        compiler_params=_cparams(("arbitrary", "arbitrary")),
        name="ssd",
    )(proj3, proj3, dt3, conv_w, conv_b, dt_bias, a_log, d_skip_x, norm_g, tril3, e2, shifts)


def _memkv_kernel(mem_ref, g_ref, w_ref, kv_ref):
    x = mem_ref[...]
    ms = jnp.mean(x * x, axis=-1, keepdims=True)
    u = (x * lax.rsqrt(ms + NORM_EPS) * g_ref[...]).astype(BF16)
    kv_ref[...] = jnp.dot(u, w_ref[...], preferred_element_type=F32).astype(BF16)


def _mem_kv(mem2, g, w_kv):
    rows = mem2.shape[0]
    width = w_kv.shape[1]
    return pl.pallas_call(
        _memkv_kernel,
        grid=(1,),
        in_specs=[
            pl.BlockSpec((rows, D_MODEL), lambda i: (0, 0)),
            pl.BlockSpec((1, D_MODEL), lambda i: (0, 0)),
            pl.BlockSpec((D_MODEL, width), lambda i: (0, 0)),
        ],
        out_specs=pl.BlockSpec((rows, width), lambda i: (0, 0)),
        out_shape=jax.ShapeDtypeStruct((rows, width), BF16),
        compiler_params=_cparams(("arbitrary",)),
        name="mem_kv",
    )(mem2, g, w_kv)


def _merge_kernel(x_ref, oa_ref, os_ref, qm_ref, gl_ref, kv_ref, bg_ref, wa_ref, ws_ref, wm_ref, wo_ref, h_ref):
    nt = (((1,), (1,)), ((), ()))
    hd = MEM_HEAD_DIM
    width = MEM_HEADS * hd
    scale = hd ** -0.5
    qm = qm_ref[0]
    kv = kv_ref[0]
    outs = []
    for hh in range(MEM_HEADS):
        q = qm[:, hh * hd:(hh + 1) * hd]
        km = kv[:, hh * hd:(hh + 1) * hd]
        vm = kv[:, width + hh * hd:width + (hh + 1) * hd]
        s = lax.dot_general(q, km, nt, preferred_element_type=F32) * scale
        p = jnp.exp(s - jnp.max(s, axis=1, keepdims=True))
        o = jnp.dot(p.astype(BF16), vm, preferred_element_type=F32)
        outs.append(o / jnp.sum(p, axis=1, keepdims=True))
    o_mem = jnp.concatenate(outs, axis=1).astype(BF16)

    gates = _sigmoid(gl_ref[0].astype(F32) + bg_ref[...])
    merged = (gates[:, :D_MODEL] * jnp.dot(oa_ref[0], wa_ref[...], preferred_element_type=F32)
              + gates[:, D_MODEL:2 * D_MODEL] * jnp.dot(os_ref[0], ws_ref[...], preferred_element_type=F32)
              + gates[:, 2 * D_MODEL:] * jnp.dot(o_mem, wm_ref[...], preferred_element_type=F32))
    h_ref[0] = x_ref[0] + jnp.dot(merged.astype(BF16), wo_ref[...], preferred_element_type=F32)


def _resident(shape):
    return pl.BlockSpec(shape, lambda *_: (0,) * len(shape), pipeline_mode=pl.Buffered(1))


def _merge(x, o_attn, o_ssm, proj3, kv3, b_gate, wa, ws, wm, wo, tm):
    bsz, s_len, _ = x.shape
    mem_len = kv3.shape[1]
    return pl.pallas_call(
        _merge_kernel,
        grid=(bsz, s_len // tm),
        in_specs=[
            pl.BlockSpec((1, tm, D_MODEL), lambda b, i: (b, i, 0)),
            pl.BlockSpec((1, tm, D_MODEL), lambda b, i: (b, i, 0)),
            pl.BlockSpec((1, tm, SSM_INNER), lambda b, i: (b, i, 0)),
            pl.BlockSpec((1, tm, D_MODEL), lambda b, i: (b, i, COL_QM // D_MODEL)),
            pl.BlockSpec((1, tm, 3 * D_MODEL), lambda b, i: (b, i, COL_GATE // (3 * D_MODEL))),
            pl.BlockSpec((1, mem_len, 2 * D_MODEL), lambda b, i: (b, 0, 0)),
            _resident((1, 3 * D_MODEL)),
            _resident((D_MODEL, D_MODEL)),
            _resident((SSM_INNER, D_MODEL)),
            _resident((D_MODEL, D_MODEL)),
            _resident((D_MODEL, D_MODEL)),
        ],
        out_specs=pl.BlockSpec((1, tm, D_MODEL), lambda b, i: (b, i, 0)),
        out_shape=jax.ShapeDtypeStruct((bsz, s_len, D_MODEL), F32),
        compiler_params=_cparams(("arbitrary", "arbitrary")),
        name="merge",
    )(x, o_attn, o_ssm, proj3, proj3, kv3, b_gate, wa, ws, wm, wo)


def _ffn_kernel(h_ref, ng_ref, wup_ref, cw_ref, cb_ref, wdn_ref, fg_ref, o_ref, hid_ref, *, tm):
    pad = SUBLANES

    @pl.when(pl.program_id(1) == 0)
    def _():
        hid_ref[0:pad, :] = jnp.zeros((pad, 2 * FFN_HIDDEN), F32)

    h = h_ref[0]
    ms = jnp.mean(h * h, axis=-1, keepdims=True)
    u = (h * lax.rsqrt(ms + NORM_EPS) * ng_ref[...]).astype(BF16)
    hid_ref[pad:pad + tm, :] = jnp.dot(u, wup_ref[...], preferred_element_type=F32)
    conv = cb_ref[...]
    for k in range(FFN_CONV):
        off = pad - (FFN_CONV - 1) + k
        conv = conv + hid_ref[off:off + tm, :] * cw_ref[k:k + 1, :]
    hid_ref[0:pad, :] = hid_ref[tm:tm + pad, :]
    act = (_silu(conv[:, :FFN_HIDDEN]) * conv[:, FFN_HIDDEN:]).astype(BF16)
    y = h + jnp.dot(act, wdn_ref[...], preferred_element_type=F32)
    ms2 = jnp.mean(y * y, axis=-1, keepdims=True)
    o_ref[0] = y * lax.rsqrt(ms2 + NORM_EPS) * fg_ref[...]


def _ffn(h, norm_g, w_up, conv_w, conv_b, w_down, final_g, tm):
    bsz, s_len, _ = h.shape
    f2 = 2 * FFN_HIDDEN
    return pl.pallas_call(
        functools.partial(_ffn_kernel, tm=tm),
        grid=(bsz, s_len // tm),
        in_specs=[
            pl.BlockSpec((1, tm, D_MODEL), lambda b, i: (b, i, 0)),
            _resident((1, D_MODEL)),
            _resident((D_MODEL, f2)),
            _resident((FFN_CONV, f2)),
            _resident((1, f2)),
            _resident((FFN_HIDDEN, D_MODEL)),
            _resident((1, D_MODEL)),
        ],
        out_specs=pl.BlockSpec((1, tm, D_MODEL), lambda b, i: (b, i, 0)),
        out_shape=jax.ShapeDtypeStruct((bsz, s_len, D_MODEL), F32),
        scratch_shapes=[pltpu.VMEM((tm + 2 * SUBLANES, f2), F32)],
        compiler_params=_cparams(("arbitrary", "arbitrary")),
        name="ffn",
    )(h, norm_g, w_up, conv_w, conv_b, w_down, final_g)


def _ssd_constants():
    L = SSM_CHUNK
    tril = np.tril(np.ones((L, L), np.float32))
    tril3 = np.concatenate([tril, tril, tril], axis=1)
    e = np.zeros((LANES, SSM_INNER), np.float32)
    for hd in range(SSM_HEADS):
        e[hd, hd * SSM_HEAD_DIM:(hd + 1) * SSM_HEAD_DIM] = 1.0
    e2 = np.concatenate([e, e], axis=0)
    shifts = np.stack([np.eye(L, k=-s, dtype=np.float32) for s in range(1, SSM_CONV)])
    return jnp.asarray(tril3, BF16), jnp.asarray(e2, BF16), jnp.asarray(shifts, BF16)


def _pad_lanes(v):
    return jnp.pad(v, ((0, 0), (0, LANES - v.shape[1])))


def _layer(h, mem, rel_bias, mix_norm_g, w_in, b_gate, ssm_conv_w, ssm_conv_b, ssm_dt_bias, ssm_A_log, ssm_D,
           ssm_norm_g, mem_norm_g, w_mem_kv, w_br_attn, w_br_ssm, w_br_mem, w_out, ffn_norm_g, w_ffn_up,
           ffn_conv_w, ffn_conv_b, w_ffn_down, final_g):
    bsz, s_len, _ = h.shape
    w_t = w_in.T.astype(BF16)
    w_dt = jnp.pad(w_t[IN_PROJ_SPLIT:IN_PROJ_SPLIT + SSM_DT_COLS], ((0, LANES - SSM_DT_COLS), (0, 0)))

    proj, dt_raw, kmean = _in_proj(h.reshape(bsz * s_len, D_MODEL), mix_norm_g[None, :], w_t, w_dt,
                                   tm=IN_PROJ_TM, tn=IN_PROJ_TN)
    proj3 = proj.reshape(bsz, s_len, PROJ_WIDTH)
    dt3 = dt_raw.reshape(bsz, s_len, LANES)
    kmean3 = kmean.reshape(bsz, s_len // MOBA_BLOCK, ATTN_HEADS * ATTN_HEAD_DIM)

    o_attn = _moba(proj3, kmean3, rel_bias.T)

    tril3, e2, shifts = _ssd_constants()
    o_ssm = _ssd(proj3, dt3, ssm_conv_w, ssm_conv_b[None, :], _pad_lanes(ssm_dt_bias[None, :]),
                 _pad_lanes(ssm_A_log[None, :]), jnp.repeat(ssm_D, SSM_HEAD_DIM)[None, :], ssm_norm_g[None, :],
                 tril3, e2, shifts)

    mem_len = mem.shape[1]
    kv = _mem_kv(mem.reshape(bsz * mem_len, D_MODEL), mem_norm_g[None, :], w_mem_kv.astype(BF16))
    kv3 = kv.reshape(bsz, mem_len, 2 * D_MODEL)

    h1 = _merge(h, o_attn, o_ssm, proj3, kv3, b_gate[None, :], w_br_attn.astype(BF16), w_br_ssm.astype(BF16),
                w_br_mem.astype(BF16), w_out.astype(BF16), tm=MERGE_TM)

    return _ffn(h1, ffn_norm_g[None, :], w_ffn_up.astype(BF16), ffn_conv_w, ffn_conv_b[None, :],
                w_ffn_down.astype(BF16), final_g[None, :], tm=FFN_TM)


def kernel(x, mem, rel_bias, mix_norm_g, w_in, b_gate, ssm_conv_w, ssm_conv_b, ssm_dt_bias, ssm_A_log, ssm_D,
           ssm_norm_g, mem_norm_g, w_mem_kv, w_br_attn, w_br_ssm, w_br_mem, w_out, ffn_norm_g, w_ffn_up,
           ffn_conv_w, ffn_conv_b, w_ffn_down, final_norm_g):
    assert w_in.shape[0] == 1, "single-layer trunk"
    return _layer(x, mem, rel_bias, mix_norm_g[0], w_in[0], b_gate[0], ssm_conv_w[0], ssm_conv_b[0],
                  ssm_dt_bias[0], ssm_A_log[0], ssm_D[0], ssm_norm_g[0], mem_norm_g[0], w_mem_kv[0],
                  w_br_attn[0], w_br_ssm[0], w_br_mem[0], w_out[0], ffn_norm_g[0], w_ffn_up[0],
                  ffn_conv_w[0], ffn_conv_b[0], w_ffn_down[0], final_norm_g)
```

```python
import functools
import math

import numpy as np
import jax
import jax.numpy as jnp
from jax import lax
from jax.experimental import pallas as pl
from jax.experimental.pallas import tpu as pltpu

F32 = jnp.float32
BF16 = jnp.bfloat16

D_MODEL = 1024
ATTN_HEADS = 8
ATTN_HEAD_DIM = 128
MOBA_BLOCK = 256
MOBA_TOPK = 3
REL_BUCKETS = 32
REL_MAX_DIST = 1024
SSM_INNER = 2048
SSM_HEAD_DIM = 64
SSM_HEADS = 32
SSM_GROUPS = 4
SSM_STATE = 128
SSM_CONV = 4
SSM_CHUNK = 256
SSM_CONV_DIM = 3072
MEM_HEADS = 4
MEM_HEAD_DIM = 256
FFN_HIDDEN = 2816
FFN_CONV = 3
NORM_EPS = 1e-6

LANES = 128
SUBLANES = 8
BF16_SUBLANES = 16
VMEM_LIMIT = 56 * 1024 * 1024

IN_PROJ_TM = 2048
IN_PROJ_TN = ATTN_HEADS * ATTN_HEAD_DIM
MERGE_TM = 512
FFN_TM = 512

ATTN_WIDTH = ATTN_HEADS * ATTN_HEAD_DIM
COL_XBC = 0
COL_GATE = COL_XBC + SSM_CONV_DIM
COL_Z = COL_GATE + 3 * D_MODEL
COL_Q = COL_Z + SSM_INNER
COL_K = COL_Q + ATTN_WIDTH
COL_V = COL_K + ATTN_WIDTH
COL_QM = COL_V + ATTN_WIDTH
PROJ_WIDTH = COL_QM + MEM_HEADS * MEM_HEAD_DIM
IN_PROJ_PERM = (8, 9, 10, 6, 7, 0, 1, 2, 11, 3, 4, 5)
IN_PROJ_K_TILE = 1
IN_PROJ_SPLIT = 3 * ATTN_WIDTH + SSM_INNER + SSM_CONV_DIM
SSM_DT_COLS = SSM_HEADS

MOBA_NEAR = -(-(REL_MAX_DIST + MOBA_BLOCK - 1) // MOBA_BLOCK)
NEG_BIG = -1e30
LOG2E = math.log2(math.e)
MOBA_QBLOCKS = 4
MOBA_GROUP = 4
MOBA_VT_ROWS = ATTN_HEAD_DIM + BF16_SUBLANES
assert MOBA_GROUP <= MOBA_NEAR + 1
assert MOBA_GROUP <= SUBLANES


def _cparams(sem):
    return pltpu.CompilerParams(dimension_semantics=sem, vmem_limit_bytes=VMEM_LIMIT)


def _sigmoid(x):
    return 1.0 / (1.0 + jnp.exp(-x))


def _silu(x):
    h = 0.5 * x
    return h + h * jnp.tanh(h)


def _split_bf16(x, parts):
    out = []
    r = x
    for _ in range(parts):
        hi = r.astype(BF16)
        out.append(hi)
        r = r - hi.astype(F32)
    return out


def _inproj_kernel(perm_ref, x_ref, g_ref, w_ref, wdt_ref, proj_ref, dt_ref, kmean_ref, u_ref, *, k_tile):
    del perm_ref
    j = pl.program_id(1)
    nt = (((1,), (1,)), ((), ()))

    @pl.when(j == 0)
    def _():
        x = x_ref[...]
        ms = jnp.mean(x * x, axis=-1, keepdims=True)
        u = (x * lax.rsqrt(ms + NORM_EPS) * g_ref[...]).astype(BF16)
        u_ref[...] = u
        dt_ref[...] = lax.dot_general(u, wdt_ref[...], nt, preferred_element_type=F32)

    acc = lax.dot_general(u_ref[...], w_ref[...], nt, preferred_element_type=F32)
    proj_ref[...] = acc.astype(BF16)

    @pl.when(j == k_tile)
    def _():
        blk = MOBA_BLOCK
        for r in range(acc.shape[0] // blk):
            kmean_ref[0, r:r + 1, :] = jnp.sum(acc[r * blk:(r + 1) * blk], axis=0, keepdims=True) * (1.0 / blk)


def _in_proj(x2, g, w_t, w_dt, tm, tn):
    t = x2.shape[0]
    kw = ATTN_HEADS * ATTN_HEAD_DIM
    n_a = IN_PROJ_SPLIT // tn
    n_b = (w_t.shape[0] - IN_PROJ_SPLIT - SSM_DT_COLS) // tn
    assert tn == kw and tm % MOBA_BLOCK == 0 and (n_a + n_b) * tn == PROJ_WIDTH == len(IN_PROJ_PERM) * tn
    grid_spec = pltpu.PrefetchScalarGridSpec(
        num_scalar_prefetch=1,
        grid=(t // tm, n_a + n_b),
        in_specs=[
            pl.BlockSpec((tm, D_MODEL), lambda i, j, perm: (i, 0)),
            pl.BlockSpec((1, D_MODEL), lambda i, j, perm: (0, 0)),
            pl.BlockSpec((pl.Element(tn), pl.Element(D_MODEL)),
                         lambda i, j, perm: (pl.multiple_of(j * tn + jnp.where(j >= n_a, SSM_DT_COLS, 0), BF16_SUBLANES), 0)),
            pl.BlockSpec((LANES, D_MODEL), lambda i, j, perm: (0, 0)),
        ],
        out_specs=[
            pl.BlockSpec((tm, tn), lambda i, j, perm: (i, perm[j])),
            pl.BlockSpec((tm, LANES), lambda i, j, perm: (i, 0)),
            pl.BlockSpec((1, tm // MOBA_BLOCK, kw), lambda i, j, perm: (i, 0, 0)),
        ],
        scratch_shapes=[pltpu.VMEM((tm, D_MODEL), BF16)],
    )
    return pl.pallas_call(
        functools.partial(_inproj_kernel, k_tile=IN_PROJ_K_TILE),
        grid_spec=grid_spec,
        out_shape=[
            jax.ShapeDtypeStruct((t, PROJ_WIDTH), BF16),
            jax.ShapeDtypeStruct((t, LANES), F32),
            jax.ShapeDtypeStruct((t // tm, tm // MOBA_BLOCK, kw), F32),
        ],
        compiler_params=_cparams(("arbitrary", "arbitrary")),
        name="in_proj",
    )(jnp.asarray(IN_PROJ_PERM, jnp.int32), x2, g, w_t, w_dt)


def _t5_bucket(dist):
    n = jnp.maximum(dist, 0)
    max_exact = REL_BUCKETS // 2
    nf = jnp.maximum(n, max_exact).astype(F32)
    large = max_exact + (jnp.log(nf * (1.0 / max_exact)) / math.log(REL_MAX_DIST / max_exact)
                         * (REL_BUCKETS - max_exact)).astype(jnp.int32)
    large = jnp.minimum(large, REL_BUCKETS - 1)
    return jnp.where(n < max_exact, n, large)


def _moba_kernel(rel_ref, q_ref, k_ref, v_ref, km_ref, o_ref, kmean_ref, bias_ref, vt_ref, neg_ref, s_ref, cm_ref,
                 m_ref, acc_ref, *, nb):
    h = pl.program_id(0)
    b = pl.program_id(1)
    i = pl.program_id(2)
    blk = MOBA_BLOCK
    scale = ATTN_HEAD_DIM ** -0.5


    @pl.when((b == 0) & (i == 0))
    def _():
        key = lax.broadcasted_iota(jnp.int32, (blk, blk), 0)
        qry = lax.broadcasted_iota(jnp.int32, (blk, blk), 1)
        for d in range(MOBA_NEAR):
            bucket = _t5_bucket(d * blk + qry - key)
            tile = jnp.zeros((blk, blk), F32)
            for bk in range(REL_BUCKETS):
                tile = jnp.where(bucket == bk, rel_ref[h, bk], tile)
            bias_ref[d] = tile * LOG2E

    @pl.when(i == 0)
    def _():
        kmean_ref[...] = jnp.zeros_like(kmean_ref)
        kmean_ref[0:nb, :] = km_ref[0]
        ones_rows = (lax.broadcasted_iota(jnp.int32, (MOBA_VT_ROWS - ATTN_HEAD_DIM, blk), 0) == 0).astype(BF16)
        for jb in range(nb):
            vt_ref[jb, 0:ATTN_HEAD_DIM, :] = v_ref[0, jb * blk:(jb + 1) * blk, :].astype(F32).T.astype(BF16)
            vt_ref[jb, ATTN_HEAD_DIM:MOBA_VT_ROWS, :] = ones_rows

    qw = MOBA_QBLOCKS * blk
    i0 = i * MOBA_QBLOCKS
    q_t = q_ref[0].astype(F32).T
    qs_t = (q_t * (scale * LOG2E)).astype(BF16)

    gate = jnp.dot(kmean_ref[...].astype(BF16), q_t.astype(BF16), preferred_element_type=F32)
    nbp = kmean_ref.shape[0]
    row = lax.broadcasted_iota(jnp.int32, (nbp, qw), 0).astype(F32)
    own = (i0 + lax.broadcasted_iota(jnp.int32, (1, qw), 1) // blk).astype(F32)
    g = jnp.where(row < own, gate, -jnp.inf)
    sel = jnp.zeros((nbp, qw), F32)
    for t in range(MOBA_TOPK):
        mx = jnp.max(g, axis=0, keepdims=True)
        idx = jnp.min(jnp.where(g == mx, row, float(nbp)), axis=0, keepdims=True)
        hit = row == idx
        sel = jnp.maximum(sel, jnp.where(hit & (own > t), 1.0, 0.0))
        g = jnp.where(hit, -jnp.inf, g)
    neg_ref[...] = jnp.where(sel > 0.5, 0.0, NEG_BIG)

    def far_scores(j0, slot):
        start = pl.multiple_of(j0 * blk, blk)
        s = jnp.dot(k_ref[0, pl.ds(start, MOBA_GROUP * blk), :], qs_t, preferred_element_type=F32)
        s_ref[slot] = s
        for u in range(MOBA_GROUP):
            cm_ref[slot, u:u + 1, :] = jnp.max(s[u * blk:(u + 1) * blk], axis=0, keepdims=True)

    far_scores(0, 0)

    def softmax_group(scores, offsets, v_idx, m_old, col_max=None):
        if col_max is None:
            col_max = [jnp.max(sj, axis=0, keepdims=True) for sj in scores]
        m_new = m_old
        for cm, off in zip(col_max, offsets):
            m_new = jnp.maximum(m_new, cm + off)
        pv = jnp.zeros((MOBA_VT_ROWS, m_old.shape[1]), F32)
        for sj, off, vj in zip(scores, offsets, v_idx):
            pj = jnp.exp2(sj + (off - m_new))
            pv = pv + jnp.dot(vt_ref[vj], pj.astype(BF16), preferred_element_type=F32)
        return m_new, pv

    far_bias = rel_ref[h, REL_BUCKETS - 1] * LOG2E

    key = lax.broadcasted_iota(jnp.int32, (blk, blk), 0)
    qry = lax.broadcasted_iota(jnp.int32, (blk, blk), 1)
    scores, offsets, v_idx, first_lane = [], [], [], []
    for e in range(MOBA_NEAR - 1 + MOBA_QBLOCKS):
        jn = i0 - (MOBA_NEAR - 1) + e
        jc = jnp.maximum(jn, 0)
        start = pl.multiple_of(jc * blk, blk)
        w_min = max(0, e - (MOBA_NEAR - 1))
        lo = w_min * blk
        se = jnp.dot(k_ref[0, pl.ds(start, blk), :], qs_t[:, lo:], preferred_element_type=F32)
        sel_row = neg_ref[pl.ds(jnp.where(jn >= 0, jn, nbp - 1), 1), :]
        s_parts, o_parts = [], []
        for w in range(w_min, MOBA_QBLOCKS):
            d = w + (MOBA_NEAR - 1) - e
            sw = se[:, (w - w_min) * blk:(w - w_min + 1) * blk]
            ow = sel_row[:, w * blk:(w + 1) * blk]
            if d == 0:
                sw = jnp.where(qry >= key, sw + bias_ref[0], NEG_BIG)
                ow = jnp.zeros((1, blk), F32)
            elif d < MOBA_NEAR:
                sw = sw + bias_ref[d]
            else:
                ow = ow + far_bias
            s_parts.append(sw)
            o_parts.append(ow)
        scores.append(jnp.concatenate(s_parts, axis=1))
        offsets.append(jnp.concatenate(o_parts, axis=1))
        v_idx.append(jc)
        first_lane.append(lo)
    m_floor = jnp.concatenate(
        [jnp.max(scores[MOBA_NEAR - 1 + w][:, 0:blk], axis=0, keepdims=True) for w in range(MOBA_QBLOCKS)], axis=1)
    parts = []
    for sj, off, vj, lo in zip(scores, offsets, v_idx, first_lane):
        mu, pvu = softmax_group([sj], [off], [vj], m_floor[:, lo:])
        if lo:
            mu = jnp.concatenate([m_floor[:, :lo], mu], axis=1)
            pvu = jnp.concatenate([jnp.zeros((MOBA_VT_ROWS, lo), F32), pvu], axis=1)
        parts.append((mu, pvu))
    m0 = parts[0][0]
    for mu, _ in parts[1:]:
        m0 = jnp.maximum(m0, mu)
    pv0 = jnp.zeros((MOBA_VT_ROWS, qw), F32)
    for mu, pvu in parts:
        pv0 = pv0 + jnp.exp2(mu - m0) * pvu
    m_ref[...] = m0
    acc_ref[...] = pv0

    n_far = jnp.maximum(i0 - (MOBA_NEAR - 1), 0)

    n_groups = (n_far + MOBA_GROUP - 1) // MOBA_GROUP

    def far_group(gi, slot, prefetch):
        j0 = gi * MOBA_GROUP
        if prefetch:
            far_scores(j0 + MOBA_GROUP, 1 - slot)
        scores, offsets, v_idx, col_max = [], [], [], []
        for u in range(MOBA_GROUP):
            ju = j0 + u
            scores.append(s_ref[slot, u * blk:(u + 1) * blk, :])
            col_max.append(cm_ref[slot, u:u + 1, :])
            offsets.append(neg_ref[pl.ds(jnp.where(ju < n_far, ju, nbp - 1), 1), :] + far_bias)
            v_idx.append(ju)
        m_old = m_ref[...]
        m_new, pv = softmax_group(scores, offsets, v_idx, m_old, col_max)
        acc_ref[...] = jnp.exp2(m_old - m_new) * acc_ref[...] + pv
        m_ref[...] = m_new

    def far_body(t, carry):
        far_group(2 * t, 0, True)
        far_group(2 * t + 1, 1, True)
        return carry

    n_pairs = jnp.maximum(n_groups - 1, 0) // 2
    lax.fori_loop(0, n_pairs, far_body, 0)
    g_rest = 2 * n_pairs
    n_rest = n_groups - g_rest

    @pl.when(n_rest == 1)
    def _():
        far_group(g_rest, 0, False)

    @pl.when(n_rest == 2)
    def _():
        far_group(g_rest, 0, True)
        far_group(g_rest + 1, 1, False)

    out_t = acc_ref[0:ATTN_HEAD_DIM, :] / acc_ref[ATTN_HEAD_DIM:ATTN_HEAD_DIM + 1, :]
    o_ref[0] = out_t.T.astype(o_ref.dtype)


def _moba(proj3, kmean3, rel_t):
    bsz, s_len, _ = proj3.shape
    blk = MOBA_BLOCK
    nb = s_len // blk
    qw = MOBA_QBLOCKS * blk
    assert nb * blk == s_len and nb % MOBA_QBLOCKS == 0 and nb >= MOBA_GROUP
    nbp = -(-(nb + 1) // BF16_SUBLANES) * BF16_SUBLANES
    cb = LANES
    return pl.pallas_call(
        functools.partial(_moba_kernel, nb=nb),
        grid=(ATTN_HEADS, bsz, nb // MOBA_QBLOCKS),
        in_specs=[
            pl.BlockSpec(memory_space=pltpu.SMEM),
            pl.BlockSpec((1, qw, cb), lambda h, b, i: (b, i, COL_Q // cb + h)),
            pl.BlockSpec((1, s_len, cb), lambda h, b, i: (b, 0, COL_K // cb + h)),
            pl.BlockSpec((1, s_len, cb), lambda h, b, i: (b, 0, COL_V // cb + h)),
            pl.BlockSpec((1, nb, cb), lambda h, b, i: (b, 0, h)),
        ],
        out_specs=pl.BlockSpec((1, qw, cb), lambda h, b, i: (b, i, h)),
        out_shape=jax.ShapeDtypeStruct((bsz, s_len, ATTN_HEADS * ATTN_HEAD_DIM), BF16),
        scratch_shapes=[
            pltpu.VMEM((nbp, ATTN_HEAD_DIM), F32),
            pltpu.VMEM((MOBA_NEAR, blk, blk), F32),
            pltpu.VMEM((nb, MOBA_VT_ROWS, blk), BF16),
            pltpu.VMEM((nbp, qw), F32),
            pltpu.VMEM((2, MOBA_GROUP * blk, qw), F32),
            pltpu.VMEM((2, SUBLANES, qw), F32),
            pltpu.VMEM((1, qw), F32),
            pltpu.VMEM((MOBA_VT_ROWS, qw), F32),
        ],
        compiler_params=_cparams(("arbitrary", "arbitrary", "arbitrary")),
        name="moba",
    )(rel_t, proj3, proj3, proj3, kmean3)


def _ssd_kernel(xbc_ref, z_ref, dt_ref, cw_ref, cbias_ref, dtb_ref, alog_ref, dskip_ref, ng_ref,
                tril_ref, e_ref, shift_ref, o_ref, tail_ref, state_ref, xs_ref, xcb_ref, ea_ref, wend_ref, bm_ref,
                cm_ref):
    L = SSM_CHUNK
    G = SSM_GROUPS
    N = SSM_STATE
    GW = SSM_INNER // G
    pad = SUBLANES
    nt = (((1,), (1,)), ((), ()))

    @pl.when(pl.program_id(1) == 0)
    def _():
        tail_ref[...] = jnp.zeros_like(tail_ref)
        state_ref[...] = jnp.zeros_like(state_ref)

    dtr = dt_ref[0] + dtb_ref[...]
    dt = jnp.maximum(dtr, 0.0) + jnp.log1p(jnp.exp(-jnp.abs(dtr)))
    a = dt * (-jnp.exp(alog_ref[...]))
    a_cat = jnp.concatenate(_split_bf16(a, 3), axis=0)
    acs = jnp.dot(tril_ref[...], a_cat, preferred_element_type=F32) * LOG2E
    acs_t = acs.T

    dt_cat = jnp.concatenate(_split_bf16(dt, 2), axis=1)
    acs_cat = jnp.concatenate(_split_bf16(acs, 2), axis=1)

    SW = 2 * LANES
    for c in range(SSM_CONV_DIM // SW):
        cols = slice(c * SW, (c + 1) * SW)
        xb = xbc_ref[0, :, cols]
        x = xb.astype(F32)
        conv = cbias_ref[:, cols] + x * cw_ref[SSM_CONV - 1:SSM_CONV, cols]
        head = jnp.zeros((pad, SW), F32)
        for k in range(SSM_CONV - 1):
            s = SSM_CONV - 1 - k
            w_k = cw_ref[k:k + 1, cols]
            conv = conv + jnp.dot(shift_ref[s - 1], xb, preferred_element_type=F32) * w_k
            head = head + tail_ref[pad - s:2 * pad - s, cols] * w_k
        conv = jnp.concatenate([conv[0:pad] + head, conv[pad:]], axis=0)
        tail_ref[0:pad, cols] = x[L - pad:L]
        xa = _silu(conv)
        if c * SW < SSM_INNER:
            dt_x = jnp.dot(dt_cat, e_ref[:, cols], preferred_element_type=F32)
            acs_x = jnp.dot(acs_cat, e_ref[:, cols], preferred_element_type=F32)
            xc = xa * dt_x
            xs_ref[:, cols] = xa
            xcb_ref[:, cols] = xc.astype(BF16)
            ea_ref[:, cols] = jnp.exp2(acs_x)
            wend_ref[:, cols] = (jnp.exp2(acs_x[L - 1:L, :] - acs_x) * xc).astype(BF16)
        elif c * SW < SSM_INNER + G * N:
            bm_ref[:, c * SW - SSM_INNER:(c + 1) * SW - SSM_INNER] = xa
        else:
            cm_ref[:, c * SW - SSM_INNER - G * N:(c + 1) * SW - SSM_INNER - G * N] = xa.astype(BF16)

    row = lax.broadcasted_iota(jnp.int32, (L, L), 0)
    col = lax.broadcasted_iota(jnp.int32, (L, L), 1)
    causal = row >= col
    lane = lax.broadcasted_iota(jnp.int32, (L, LANES), 1)
    low_half = lane < SSM_HEAD_DIM

    for g in range(G):
        gcols = slice(g * GW, (g + 1) * GW)
        bg = bm_ref[:, g * N:(g + 1) * N]
        cg = cm_ref[:, g * N:(g + 1) * N]
        cb = lax.dot_general(cg, bg.astype(BF16), nt, preferred_element_type=F32)
        st = state_ref[g]
        ea_g = ea_ref[:, gcols]
        y_off = jnp.dot(cg, st.astype(BF16), preferred_element_type=F32) * ea_g
        y_parts = []
        for pr in range(GW // LANES):
            c0 = g * GW + pr * LANES
            x_pair = xcb_ref[:, c0:c0 + LANES]
            zero = jnp.zeros_like(x_pair)
            y_pair = y_off[:, pr * LANES:(pr + 1) * LANES]
            for half in range(2):
                hd = c0 // SSM_HEAD_DIM + half
                seg = acs[:, hd:hd + 1] - acs_t[hd:hd + 1, :]
                decay = jnp.exp2(jnp.where(causal, seg, -jnp.inf))
                mmat = (cb * decay).astype(BF16)
                x_half = jnp.where(low_half, x_pair, zero) if half == 0 else jnp.where(low_half, zero, x_pair)
                y_pair = y_pair + jnp.dot(mmat, x_half, preferred_element_type=F32)
            y_parts.append(y_pair)
        bg_t = bg.T.astype(BF16)
        state_ref[g] = (st * ea_g[L - 1:L, :]
                        + jnp.dot(bg_t, wend_ref[:, gcols], preferred_element_type=F32))

        y = jnp.concatenate(y_parts, axis=1) + dskip_ref[:, gcols] * xs_ref[:, gcols]
        v = y * _silu(z_ref[0, :, gcols].astype(F32))
        ms = jnp.mean(v * v, axis=-1, keepdims=True)
        o_ref[0, :, gcols] = (v * lax.rsqrt(ms + NORM_EPS) * ng_ref[:, gcols]).astype(o_ref.dtype)


def _ssd(proj3, dt3, conv_w, conv_b, dt_bias, a_log, d_skip_x, norm_g, tril3, e2, shifts):
    bsz, s_len, _ = proj3.shape
    L = SSM_CHUNK
    nc = s_len // L
    assert nc * L == s_len
    const = lambda b, c: (0, 0)
    return pl.pallas_call(
        _ssd_kernel,
        grid=(bsz, nc),
        in_specs=[
            pl.BlockSpec((1, L, SSM_CONV_DIM), lambda b, c: (b, c, COL_XBC // SSM_CONV_DIM)),
            pl.BlockSpec((1, L, SSM_INNER), lambda b, c: (b, c, COL_Z // SSM_INNER)),
            pl.BlockSpec((1, L, LANES), lambda b, c: (b, c, 0)),
            pl.BlockSpec((SSM_CONV, SSM_CONV_DIM), const),
            pl.BlockSpec((1, SSM_CONV_DIM), const),
            pl.BlockSpec((1, LANES), const),
            pl.BlockSpec((1, LANES), const),
            pl.BlockSpec((1, SSM_INNER), const),
            pl.BlockSpec((1, SSM_INNER), const),
            pl.BlockSpec((L, 3 * L), const),
            pl.BlockSpec((2 * LANES, SSM_INNER), const),
            pl.BlockSpec((SSM_CONV - 1, L, L), lambda b, c: (0, 0, 0)),
        ],
        out_specs=pl.BlockSpec((1, L, SSM_INNER), lambda b, c: (b, c, 0)),
        out_shape=jax.ShapeDtypeStruct((bsz, s_len, SSM_INNER), BF16),
        scratch_shapes=[
            pltpu.VMEM((2 * SUBLANES, SSM_CONV_DIM), F32),
            pltpu.VMEM((SSM_GROUPS, SSM_STATE, SSM_INNER // SSM_GROUPS), F32),
            pltpu.VMEM((L, SSM_INNER), F32),
            pltpu.VMEM((L, SSM_INNER), BF16),
            pltpu.VMEM((L, SSM_INNER), F32),
            pltpu.VMEM((L, SSM_INNER), BF16),
            pltpu.VMEM((L, SSM_GROUPS * SSM_STATE), F32),
            pltpu.VMEM((L, SSM_GROUPS * SSM_STATE), BF16),
        ],
        compiler_params=_cparams(("arbitrary", "arbitrary")),
        name="ssd",
    )(proj3, proj3, dt3, conv_w, conv_b, dt_bias, a_log, d_skip_x, norm_g, tril3, e2, shifts)


def _memkv_kernel(mem_ref, g_ref, w_ref, kv_ref):
    x = mem_ref[...]
    ms = jnp.mean(x * x, axis=-1, keepdims=True)
    u = (x * lax.rsqrt(ms + NORM_EPS) * g_ref[...]).astype(BF16)
    kv_ref[...] = jnp.dot(u, w_ref[...], preferred_element_type=F32).astype(BF16)


def _mem_kv(mem2, g, w_kv):
    rows = mem2.shape[0]
    width = w_kv.shape[1]
    return pl.pallas_call(
        _memkv_kernel,
        grid=(1,),
        in_specs=[
            pl.BlockSpec((rows, D_MODEL), lambda i: (0, 0)),
            pl.BlockSpec((1, D_MODEL), lambda i: (0, 0)),
            pl.BlockSpec((D_MODEL, width), lambda i: (0, 0)),
        ],
        out_specs=pl.BlockSpec((rows, width), lambda i: (0, 0)),
        out_shape=jax.ShapeDtypeStruct((rows, width), BF16),
        compiler_params=_cparams(("arbitrary",)),
        name="mem_kv",
    )(mem2, g, w_kv)


def _merge_kernel(x_ref, oa_ref, os_ref, qm_ref, gl_ref, kv_ref, bg_ref, wa_ref, ws_ref, wm_ref, wo_ref, h_ref):
    nt = (((1,), (1,)), ((), ()))
    hd = MEM_HEAD_DIM
    width = MEM_HEADS * hd
    scale = hd ** -0.5
    qm = qm_ref[0]
    kv = kv_ref[0]
    outs = []
    for hh in range(MEM_HEADS):
        q = qm[:, hh * hd:(hh + 1) * hd]
        km = kv[:, hh * hd:(hh + 1) * hd]
        vm = kv[:, width + hh * hd:width + (hh + 1) * hd]
        s = lax.dot_general(q, km, nt, preferred_element_type=F32) * scale
        p = jnp.exp(s - jnp.max(s, axis=1, keepdims=True))
        o = jnp.dot(p.astype(BF16), vm, preferred_element_type=F32)
        outs.append(o / jnp.sum(p, axis=1, keepdims=True))
    o_mem = jnp.concatenate(outs, axis=1).astype(BF16)

    gates = _sigmoid(gl_ref[0].astype(F32) + bg_ref[...])
    merged = (gates[:, :D_MODEL] * jnp.dot(oa_ref[0], wa_ref[...], preferred_element_type=F32)
              + gates[:, D_MODEL:2 * D_MODEL] * jnp.dot(os_ref[0], ws_ref[...], preferred_element_type=F32)
              + gates[:, 2 * D_MODEL:] * jnp.dot(o_mem, wm_ref[...], preferred_element_type=F32))
    h_ref[0] = x_ref[0] + jnp.dot(merged.astype(BF16), wo_ref[...], preferred_element_type=F32)


def _resident(shape):
    return pl.BlockSpec(shape, lambda *_: (0,) * len(shape), pipeline_mode=pl.Buffered(1))


def _merge(x, o_attn, o_ssm, proj3, kv3, b_gate, wa, ws, wm, wo, tm):
    bsz, s_len, _ = x.shape
    mem_len = kv3.shape[1]
    return pl.pallas_call(
        _merge_kernel,
        grid=(bsz, s_len // tm),
        in_specs=[
            pl.BlockSpec((1, tm, D_MODEL), lambda b, i: (b, i, 0)),
            pl.BlockSpec((1, tm, D_MODEL), lambda b, i: (b, i, 0)),
            pl.BlockSpec((1, tm, SSM_INNER), lambda b, i: (b, i, 0)),
            pl.BlockSpec((1, tm, D_MODEL), lambda b, i: (b, i, COL_QM // D_MODEL)),
            pl.BlockSpec((1, tm, 3 * D_MODEL), lambda b, i: (b, i, COL_GATE // (3 * D_MODEL))),
            pl.BlockSpec((1, mem_len, 2 * D_MODEL), lambda b, i: (b, 0, 0)),
            _resident((1, 3 * D_MODEL)),
            _resident((D_MODEL, D_MODEL)),
            _resident((SSM_INNER, D_MODEL)),
            _resident((D_MODEL, D_MODEL)),
            _resident((D_MODEL, D_MODEL)),
        ],
        out_specs=pl.BlockSpec((1, tm, D_MODEL), lambda b, i: (b, i, 0)),
        out_shape=jax.ShapeDtypeStruct((bsz, s_len, D_MODEL), F32),
        compiler_params=_cparams(("arbitrary", "arbitrary")),
        name="merge",
    )(x, o_attn, o_ssm, proj3, proj3, kv3, b_gate, wa, ws, wm, wo)


def _ffn_kernel(h_ref, ng_ref, wup_ref, cw_ref, cb_ref, wdn_ref, fg_ref, o_ref, hid_ref, *, tm):
    pad = SUBLANES

    @pl.when(pl.program_id(1) == 0)
    def _():
        hid_ref[0:pad, :] = jnp.zeros((pad, 2 * FFN_HIDDEN), F32)

    h = h_ref[0]
    ms = jnp.mean(h * h, axis=-1, keepdims=True)
    u = (h * lax.rsqrt(ms + NORM_EPS) * ng_ref[...]).astype(BF16)
    hid_ref[pad:pad + tm, :] = jnp.dot(u, wup_ref[...], preferred_element_type=F32)
    conv = cb_ref[...]
    for k in range(FFN_CONV):
        off = pad - (FFN_CONV - 1) + k
        conv = conv + hid_ref[off:off + tm, :] * cw_ref[k:k + 1, :]
    hid_ref[0:pad, :] = hid_ref[tm:tm + pad, :]
    act = (_silu(conv[:, :FFN_HIDDEN]) * conv[:, FFN_HIDDEN:]).astype(BF16)
    y = h + jnp.dot(act, wdn_ref[...], preferred_element_type=F32)
    ms2 = jnp.mean(y * y, axis=-1, keepdims=True)
    o_ref[0] = y * lax.rsqrt(ms2 + NORM_EPS) * fg_ref[...]


def _ffn(h, norm_g, w_up, conv_w, conv_b, w_down, final_g, tm):
    bsz, s_len, _ = h.shape
    f2 = 2 * FFN_HIDDEN
    return pl.pallas_call(
        functools.partial(_ffn_kernel, tm=tm),
        grid=(bsz, s_len // tm),
        in_specs=[
            pl.BlockSpec((1, tm, D_MODEL), lambda b, i: (b, i, 0)),
            _resident((1, D_MODEL)),
            _resident((D_MODEL, f2)),
            _resident((FFN_CONV, f2)),
            _resident((1, f2)),
            _resident((FFN_HIDDEN, D_MODEL)),
            _resident((1, D_MODEL)),
        ],
        out_specs=pl.BlockSpec((1, tm, D_MODEL), lambda b, i: (b, i, 0)),
        out_shape=jax.ShapeDtypeStruct((bsz, s_len, D_MODEL), F32),
        scratch_shapes=[pltpu.VMEM((tm + 2 * SUBLANES, f2), F32)],
        compiler_params=_cparams(("arbitrary", "arbitrary")),
        name="ffn",
    )(h, norm_g, w_up, conv_w, conv_b, w_down, final_g)


def _ssd_constants():
    L = SSM_CHUNK
    tril = np.tril(np.ones((L, L), np.float32))
    tril3 = np.concatenate([tril, tril, tril], axis=1)
    e = np.zeros((LANES, SSM_INNER), np.float32)
    for hd in range(SSM_HEADS):
        e[hd, hd * SSM_HEAD_DIM:(hd + 1) * SSM_HEAD_DIM] = 1.0
    e2 = np.concatenate([e, e], axis=0)
    shifts = np.stack([np.eye(L, k=-s, dtype=np.float32) for s in range(1, SSM_CONV)])
    return jnp.asarray(tril3, BF16), jnp.asarray(e2, BF16), jnp.asarray(shifts, BF16)


def _pad_lanes(v):
    return jnp.pad(v, ((0, 0), (0, LANES - v.shape[1])))


def _layer(h, mem, rel_bias, mix_norm_g, w_in, b_gate, ssm_conv_w, ssm_conv_b, ssm_dt_bias, ssm_A_log, ssm_D,
           ssm_norm_g, mem_norm_g, w_mem_kv, w_br_attn, w_br_ssm, w_br_mem, w_out, ffn_norm_g, w_ffn_up,
           ffn_conv_w, ffn_conv_b, w_ffn_down, final_g):
    bsz, s_len, _ = h.shape
    w_t = w_in.T.astype(BF16)
    w_dt = jnp.pad(w_t[IN_PROJ_SPLIT:IN_PROJ_SPLIT + SSM_DT_COLS], ((0, LANES - SSM_DT_COLS), (0, 0)))

    proj, dt_raw, kmean = _in_proj(h.reshape(bsz * s_len, D_MODEL), mix_norm_g[None, :], w_t, w_dt,
                                   tm=IN_PROJ_TM, tn=IN_PROJ_TN)
    proj3 = proj.reshape(bsz, s_len, PROJ_WIDTH)
    dt3 = dt_raw.reshape(bsz, s_len, LANES)
    kmean3 = kmean.reshape(bsz, s_len // MOBA_BLOCK, ATTN_HEADS * ATTN_HEAD_DIM)

    o_attn = _moba(proj3, kmean3, rel_bias.T)

    tril3, e2, shifts = _ssd_constants()
    o_ssm = _ssd(proj3, dt3, ssm_conv_w, ssm_conv_b[None, :], _pad_lanes(ssm_dt_bias[None, :]),
                 _pad_lanes(ssm_A_log[None, :]), jnp.repeat(ssm_D, SSM_HEAD_DIM)[None, :], ssm_norm_g[None, :],
                 tril3, e2, shifts)

    mem_len = mem.shape[1]
    kv = _mem_kv(mem.reshape(bsz * mem_len, D_MODEL), mem_norm_g[None, :], w_mem_kv.astype(BF16))
    kv3 = kv.reshape(bsz, mem_len, 2 * D_MODEL)

    h1 = _merge(h, o_attn, o_ssm, proj3, kv3, b_gate[None, :], w_br_attn.astype(BF16), w_br_ssm.astype(BF16),
                w_br_mem.astype(BF16), w_out.astype(BF16), tm=MERGE_TM)

    return _ffn(h1, ffn_norm_g[None, :], w_ffn_up.astype(BF16), ffn_conv_w, ffn_conv_b[None, :],
                w_ffn_down.astype(BF16), final_g[None, :], tm=FFN_TM)


def kernel(x, mem, rel_bias, mix_norm_g, w_in, b_gate, ssm_conv_w, ssm_conv_b, ssm_dt_bias, ssm_A_log, ssm_D,
           ssm_norm_g, mem_norm_g, w_mem_kv, w_br_attn, w_br_ssm, w_br_mem, w_out, ffn_norm_g, w_ffn_up,
           ffn_conv_w, ffn_conv_b, w_ffn_down, final_norm_g):
    assert w_in.shape[0] == 1, "single-layer trunk"
    return _layer(x, mem, rel_bias, mix_norm_g[0], w_in[0], b_gate[0], ssm_conv_w[0], ssm_conv_b[0],
                  ssm_dt_bias[0], ssm_A_log[0], ssm_D[0], ssm_norm_g[0], mem_norm_g[0], w_mem_kv[0],
                  w_br_attn[0], w_br_ssm[0], w_br_mem[0], w_out[0], ffn_norm_g[0], w_ffn_up[0],
                  ffn_conv_w[0], ffn_conv_b[0], w_ffn_down[0], final_norm_g)
```

```python
import functools
import math

import numpy as np
import jax
import jax.numpy as jnp
from jax import lax
from jax.experimental import pallas as pl
from jax.experimental.pallas import tpu as pltpu

F32 = jnp.float32
BF16 = jnp.bfloat16

D_MODEL = 1024
ATTN_HEADS = 8
ATTN_HEAD_DIM = 128
MOBA_BLOCK = 256
MOBA_TOPK = 3
REL_BUCKETS = 32
REL_MAX_DIST = 1024
SSM_INNER = 2048
SSM_HEAD_DIM = 64
SSM_HEADS = 32
SSM_GROUPS = 4
SSM_STATE = 128
SSM_CONV = 4
SSM_CHUNK = 256
SSM_CONV_DIM = 3072
MEM_HEADS = 4
MEM_HEAD_DIM = 256
FFN_HIDDEN = 2816
FFN_CONV = 3
NORM_EPS = 1e-6

LANES = 128
SUBLANES = 8
BF16_SUBLANES = 16
VMEM_LIMIT = 56 * 1024 * 1024

IN_PROJ_TM = 2048
IN_PROJ_TN = ATTN_HEADS * ATTN_HEAD_DIM
MERGE_TM = 512
FFN_TM = 512

ATTN_WIDTH = ATTN_HEADS * ATTN_HEAD_DIM
COL_XBC = 0
COL_GATE = COL_XBC + SSM_CONV_DIM
COL_Z = COL_GATE + 3 * D_MODEL
COL_Q = COL_Z + SSM_INNER
COL_K = COL_Q + ATTN_WIDTH
COL_V = COL_K + ATTN_WIDTH
COL_QM = COL_V + ATTN_WIDTH
PROJ_WIDTH = COL_QM + MEM_HEADS * MEM_HEAD_DIM
IN_PROJ_PERM = (8, 9, 10, 6, 7, 0, 1, 2, 11, 3, 4, 5)
IN_PROJ_K_TILE = 1
IN_PROJ_SPLIT = 3 * ATTN_WIDTH + SSM_INNER + SSM_CONV_DIM
SSM_DT_COLS = SSM_HEADS

MOBA_NEAR = -(-(REL_MAX_DIST + MOBA_BLOCK - 1) // MOBA_BLOCK)
NEG_BIG = -1e30
LOG2E = math.log2(math.e)
MOBA_QBLOCKS = 4
MOBA_GROUP = 4
MOBA_VT_ROWS = ATTN_HEAD_DIM + BF16_SUBLANES
assert MOBA_GROUP <= MOBA_NEAR + 1
assert MOBA_GROUP <= SUBLANES


def _cparams(sem):
    return pltpu.CompilerParams(dimension_semantics=sem, vmem_limit_bytes=VMEM_LIMIT)


def _sigmoid(x):
    return 1.0 / (1.0 + jnp.exp(-x))


def _silu(x):
    h = 0.5 * x
    return h + h * jnp.tanh(h)


def _split_bf16(x, parts):
    out = []
    r = x
    for _ in range(parts):
        hi = r.astype(BF16)
        out.append(hi)
        r = r - hi.astype(F32)
    return out


def _inproj_kernel(perm_ref, x_ref, g_ref, w_ref, wdt_ref, proj_ref, dt_ref, kmean_ref, u_ref, *, k_tile):
    del perm_ref
    j = pl.program_id(1)
    nt = (((1,), (1,)), ((), ()))

    @pl.when(j == 0)
    def _():
        x = x_ref[...]
        ms = jnp.mean(x * x, axis=-1, keepdims=True)
        u = (x * lax.rsqrt(ms + NORM_EPS) * g_ref[...]).astype(BF16)
        u_ref[...] = u
        dt_ref[...] = lax.dot_general(u, wdt_ref[...], nt, preferred_element_type=F32)

    acc = lax.dot_general(u_ref[...], w_ref[...], nt, preferred_element_type=F32)
    proj_ref[...] = acc.astype(BF16)

    @pl.when(j == k_tile)
    def _():
        blk = MOBA_BLOCK
        for r in range(acc.shape[0] // blk):
            kmean_ref[0, r:r + 1, :] = jnp.sum(acc[r * blk:(r + 1) * blk], axis=0, keepdims=True) * (1.0 / blk)


def _in_proj(x2, g, w_t, w_dt, tm, tn):
    t = x2.shape[0]
    kw = ATTN_HEADS * ATTN_HEAD_DIM
    n_a = IN_PROJ_SPLIT // tn
    n_b = (w_t.shape[0] - IN_PROJ_SPLIT - SSM_DT_COLS) // tn
    assert tn == kw and tm % MOBA_BLOCK == 0 and (n_a + n_b) * tn == PROJ_WIDTH == len(IN_PROJ_PERM) * tn
    grid_spec = pltpu.PrefetchScalarGridSpec(
        num_scalar_prefetch=1,
        grid=(t // tm, n_a + n_b),
        in_specs=[
            pl.BlockSpec((tm, D_MODEL), lambda i, j, perm: (i, 0)),
            pl.BlockSpec((1, D_MODEL), lambda i, j, perm: (0, 0)),
            pl.BlockSpec((pl.Element(tn), pl.Element(D_MODEL)),
                         lambda i, j, perm: (pl.multiple_of(j * tn + jnp.where(j >= n_a, SSM_DT_COLS, 0), BF16_SUBLANES), 0)),
            pl.BlockSpec((LANES, D_MODEL), lambda i, j, perm: (0, 0)),
        ],
        out_specs=[
            pl.BlockSpec((tm, tn), lambda i, j, perm: (i, perm[j])),
            pl.BlockSpec((tm, LANES), lambda i, j, perm: (i, 0)),
            pl.BlockSpec((1, tm // MOBA_BLOCK, kw), lambda i, j, perm: (i, 0, 0)),
        ],
        scratch_shapes=[pltpu.VMEM((tm, D_MODEL), BF16)],
    )
    return pl.pallas_call(
        functools.partial(_inproj_kernel, k_tile=IN_PROJ_K_TILE),
        grid_spec=grid_spec,
        out_shape=[
            jax.ShapeDtypeStruct((t, PROJ_WIDTH), BF16),
            jax.ShapeDtypeStruct((t, LANES), F32),
            jax.ShapeDtypeStruct((t // tm, tm // MOBA_BLOCK, kw), F32),
        ],
        compiler_params=_cparams(("arbitrary", "arbitrary")),
        name="in_proj",
    )(jnp.asarray(IN_PROJ_PERM, jnp.int32), x2, g, w_t, w_dt)


def _t5_bucket(dist):
    n = jnp.maximum(dist, 0)
    max_exact = REL_BUCKETS // 2
    nf = jnp.maximum(n, max_exact).astype(F32)
    large = max_exact + (jnp.log(nf * (1.0 / max_exact)) / math.log(REL_MAX_DIST / max_exact)
                         * (REL_BUCKETS - max_exact)).astype(jnp.int32)
    large = jnp.minimum(large, REL_BUCKETS - 1)
    return jnp.where(n < max_exact, n, large)


def _bucket_range(dist_lo, dist_hi):
    def bucket(n):
        n = max(n, 0)
        max_exact = REL_BUCKETS // 2
        if n < max_exact:
            return n
        return min(max_exact + int(math.log(n / max_exact) / math.log(REL_MAX_DIST / max_exact)
                                   * (REL_BUCKETS - max_exact)), REL_BUCKETS - 1)
    return max(bucket(dist_lo) - 1, 0), min(bucket(dist_hi) + 1, REL_BUCKETS - 1) + 1


def _moba_kernel(rel_ref, q_ref, k_ref, v_ref, km_ref, o_ref, kmean_ref, bias_ref, vt_ref, neg_ref, s_ref, cm_ref,
                 m_ref, acc_ref, *, nb):
    h = pl.program_id(0)
    b = pl.program_id(1)
    i = pl.program_id(2)
    blk = MOBA_BLOCK
    scale = ATTN_HEAD_DIM ** -0.5


    @pl.when((b == 0) & (i == 0))
    def _():
        key = lax.broadcasted_iota(jnp.int32, (blk, blk), 0)
        qry = lax.broadcasted_iota(jnp.int32, (blk, blk), 1)
        for d in range(MOBA_NEAR):
            bucket = _t5_bucket(d * blk + qry - key)
            tile = jnp.zeros((blk, blk), F32)
            for bk in range(*_bucket_range(d * blk - (blk - 1), d * blk + (blk - 1))):
                tile = jnp.where(bucket == bk, rel_ref[h, bk], tile)
            bias_ref[d] = tile * LOG2E

    @pl.when(i == 0)
    def _():
        kmean_ref[...] = jnp.zeros_like(kmean_ref)
        kmean_ref[0:nb, :] = km_ref[0]
        ones_rows = (lax.broadcasted_iota(jnp.int32, (MOBA_VT_ROWS - ATTN_HEAD_DIM, blk), 0) == 0).astype(BF16)
        for jb in range(nb):
            vt_ref[jb, 0:ATTN_HEAD_DIM, :] = v_ref[0, jb * blk:(jb + 1) * blk, :].astype(F32).T.astype(BF16)
            vt_ref[jb, ATTN_HEAD_DIM:MOBA_VT_ROWS, :] = ones_rows

    qw = MOBA_QBLOCKS * blk
    i0 = i * MOBA_QBLOCKS
    q_t = q_ref[0].astype(F32).T
    qs_t = (q_t * (scale * LOG2E)).astype(BF16)

    gate = jnp.dot(kmean_ref[...].astype(BF16), q_t.astype(BF16), preferred_element_type=F32)
    nbp = kmean_ref.shape[0]
    row = lax.broadcasted_iota(jnp.int32, (nbp, qw), 0).astype(F32)
    own = (i0 + lax.broadcasted_iota(jnp.int32, (1, qw), 1) // blk).astype(F32)
    g = jnp.where(row < own, gate, -jnp.inf)
    sel = jnp.zeros((nbp, qw), F32)
    for t in range(MOBA_TOPK):
        mx = jnp.max(g, axis=0, keepdims=True)
        idx = jnp.min(jnp.where(g == mx, row, float(nbp)), axis=0, keepdims=True)
        hit = row == idx
        sel = jnp.maximum(sel, jnp.where(hit & (own > t), 1.0, 0.0))
        g = jnp.where(hit, -jnp.inf, g)
    neg_ref[...] = jnp.where(sel > 0.5, 0.0, NEG_BIG)

    def far_scores(j0, slot):
        start = pl.multiple_of(j0 * blk, blk)
        s = jnp.dot(k_ref[0, pl.ds(start, MOBA_GROUP * blk), :], qs_t, preferred_element_type=F32)
        s_ref[slot] = s
        for u in range(MOBA_GROUP):
            cm_ref[slot, u:u + 1, :] = jnp.max(s[u * blk:(u + 1) * blk], axis=0, keepdims=True)

    far_scores(0, 0)

    def softmax_group(scores, offsets, v_idx, m_old, col_max=None):
        if col_max is None:
            col_max = [jnp.max(sj, axis=0, keepdims=True) for sj in scores]
        m_new = m_old
        for cm, off in zip(col_max, offsets):
            m_new = jnp.maximum(m_new, cm + off)
        pv = jnp.zeros((MOBA_VT_ROWS, m_old.shape[1]), F32)
        for sj, off, vj in zip(scores, offsets, v_idx):
            pj = jnp.exp2(sj + (off - m_new))
            pv = pv + jnp.dot(vt_ref[vj], pj.astype(BF16), preferred_element_type=F32)
        return m_new, pv

    far_bias = rel_ref[h, REL_BUCKETS - 1] * LOG2E

    key = lax.broadcasted_iota(jnp.int32, (blk, blk), 0)
    qry = lax.broadcasted_iota(jnp.int32, (blk, blk), 1)
    scores, offsets, v_idx, first_lane = [], [], [], []
    for e in range(MOBA_NEAR - 1 + MOBA_QBLOCKS):
        jn = i0 - (MOBA_NEAR - 1) + e
        jc = jnp.maximum(jn, 0)
        start = pl.multiple_of(jc * blk, blk)
        w_min = max(0, e - (MOBA_NEAR - 1))
        lo = w_min * blk
        se = jnp.dot(k_ref[0, pl.ds(start, blk), :], qs_t[:, lo:], preferred_element_type=F32)
        sel_row = neg_ref[pl.ds(jnp.where(jn >= 0, jn, nbp - 1), 1), :]
        s_parts, o_parts = [], []
        for w in range(w_min, MOBA_QBLOCKS):
            d = w + (MOBA_NEAR - 1) - e
            sw = se[:, (w - w_min) * blk:(w - w_min + 1) * blk]
            ow = sel_row[:, w * blk:(w + 1) * blk]
            if d == 0:
                sw = jnp.where(qry >= key, sw + bias_ref[0], NEG_BIG)
                ow = jnp.zeros((1, blk), F32)
            elif d < MOBA_NEAR:
                sw = sw + bias_ref[d]
            else:
                ow = ow + far_bias
            s_parts.append(sw)
            o_parts.append(ow)
        scores.append(jnp.concatenate(s_parts, axis=1))
        offsets.append(jnp.concatenate(o_parts, axis=1))
        v_idx.append(jc)
        first_lane.append(lo)
    m_floor = jnp.concatenate(
        [jnp.max(scores[MOBA_NEAR - 1 + w][:, 0:blk], axis=0, keepdims=True) for w in range(MOBA_QBLOCKS)], axis=1)
    parts = []
    for sj, off, vj, lo in zip(scores, offsets, v_idx, first_lane):
        mu, pvu = softmax_group([sj], [off], [vj], m_floor[:, lo:])
        if lo:
            mu = jnp.concatenate([m_floor[:, :lo], mu], axis=1)
            pvu = jnp.concatenate([jnp.zeros((MOBA_VT_ROWS, lo), F32), pvu], axis=1)
        parts.append((mu, pvu))
    m0 = parts[0][0]
    for mu, _ in parts[1:]:
        m0 = jnp.maximum(m0, mu)
    pv0 = jnp.zeros((MOBA_VT_ROWS, qw), F32)
    for mu, pvu in parts:
        pv0 = pv0 + jnp.exp2(mu - m0) * pvu
    m_ref[...] = m0
    acc_ref[...] = pv0

    n_far = jnp.maximum(i0 - (MOBA_NEAR - 1), 0)

    n_groups = (n_far + MOBA_GROUP - 1) // MOBA_GROUP

    def far_group(gi, slot, prefetch):
        j0 = gi * MOBA_GROUP
        if prefetch:
            far_scores(j0 + MOBA_GROUP, 1 - slot)
        scores, offsets, v_idx, col_max = [], [], [], []
        for u in range(MOBA_GROUP):
            ju = j0 + u
            scores.append(s_ref[slot, u * blk:(u + 1) * blk, :])
            col_max.append(cm_ref[slot, u:u + 1, :])
            offsets.append(neg_ref[pl.ds(jnp.where(ju < n_far, ju, nbp - 1), 1), :] + far_bias)
            v_idx.append(ju)
        m_old = m_ref[...]
        m_new, pv = softmax_group(scores, offsets, v_idx, m_old, col_max)
        acc_ref[...] = jnp.exp2(m_old - m_new) * acc_ref[...] + pv
        m_ref[...] = m_new

    def far_body(t, carry):
        far_group(2 * t, 0, True)
        far_group(2 * t + 1, 1, True)
        return carry

    n_pairs = jnp.maximum(n_groups - 1, 0) // 2
    lax.fori_loop(0, n_pairs, far_body, 0)
    g_rest = 2 * n_pairs
    n_rest = n_groups - g_rest

    @pl.when(n_rest == 1)
    def _():
        far_group(g_rest, 0, False)

    @pl.when(n_rest == 2)
    def _():
        far_group(g_rest, 0, True)
        far_group(g_rest + 1, 1, False)

    out_t = acc_ref[0:ATTN_HEAD_DIM, :] / acc_ref[ATTN_HEAD_DIM:ATTN_HEAD_DIM + 1, :]
    o_ref[0] = out_t.T.astype(o_ref.dtype)


def _moba(proj3, kmean3, rel_t):
    bsz, s_len, _ = proj3.shape
    blk = MOBA_BLOCK
    nb = s_len // blk
    qw = MOBA_QBLOCKS * blk
    assert nb * blk == s_len and nb % MOBA_QBLOCKS == 0 and nb >= MOBA_GROUP
    nbp = -(-(nb + 1) // BF16_SUBLANES) * BF16_SUBLANES
    cb = LANES
    return pl.pallas_call(
        functools.partial(_moba_kernel, nb=nb),
        grid=(ATTN_HEADS, bsz, nb // MOBA_QBLOCKS),
        in_specs=[
            pl.BlockSpec(memory_space=pltpu.SMEM),
            pl.BlockSpec((1, qw, cb), lambda h, b, i: (b, i, COL_Q // cb + h)),
            pl.BlockSpec((1, s_len, cb), lambda h, b, i: (b, 0, COL_K // cb + h)),
            pl.BlockSpec((1, s_len, cb), lambda h, b, i: (b, 0, COL_V // cb + h)),
            pl.BlockSpec((1, nb, cb), lambda h, b, i: (b, 0, h)),
        ],
        out_specs=pl.BlockSpec((1, qw, cb), lambda h, b, i: (b, i, h)),
        out_shape=jax.ShapeDtypeStruct((bsz, s_len, ATTN_HEADS * ATTN_HEAD_DIM), BF16),
        scratch_shapes=[
            pltpu.VMEM((nbp, ATTN_HEAD_DIM), F32),
            pltpu.VMEM((MOBA_NEAR, blk, blk), F32),
            pltpu.VMEM((nb, MOBA_VT_ROWS, blk), BF16),
            pltpu.VMEM((nbp, qw), F32),
            pltpu.VMEM((2, MOBA_GROUP * blk, qw), F32),
            pltpu.VMEM((2, SUBLANES, qw), F32),
            pltpu.VMEM((1, qw), F32),
            pltpu.VMEM((MOBA_VT_ROWS, qw), F32),
        ],
        compiler_params=_cparams(("arbitrary", "arbitrary", "arbitrary")),
        name="moba",
    )(rel_t, proj3, proj3, proj3, kmean3)


def _ssd_kernel(xbc_ref, z_ref, dt_ref, cw_ref, cbias_ref, dtb_ref, alog_ref, dskip_ref, ng_ref,
                tril_ref, e_ref, shift_ref, o_ref, tail_ref, state_ref, xs_ref, xcb_ref, ea_ref, wend_ref, bm_ref,
                cm_ref):
    L = SSM_CHUNK
    G = SSM_GROUPS
    N = SSM_STATE
    GW = SSM_INNER // G
    pad = SUBLANES
    nt = (((1,), (1,)), ((), ()))

    @pl.when(pl.program_id(1) == 0)
    def _():
        tail_ref[...] = jnp.zeros_like(tail_ref)
        state_ref[...] = jnp.zeros_like(state_ref)

    dtr = dt_ref[0] + dtb_ref[...]
    dt = jnp.maximum(dtr, 0.0) + jnp.log1p(jnp.exp(-jnp.abs(dtr)))
    a = dt * (-jnp.exp(alog_ref[...]))
    a_cat = jnp.concatenate(_split_bf16(a, 3), axis=0)
    acs = jnp.dot(tril_ref[...], a_cat, preferred_element_type=F32) * LOG2E
    acs_t = acs.T

    dt_cat = jnp.concatenate(_split_bf16(dt, 2), axis=1)
    acs_cat = jnp.concatenate(_split_bf16(acs, 2), axis=1)

    SW = 2 * LANES
    for c in range(SSM_CONV_DIM // SW):
        cols = slice(c * SW, (c + 1) * SW)
        xb = xbc_ref[0, :, cols]
        x = xb.astype(F32)
        conv = cbias_ref[:, cols] + x * cw_ref[SSM_CONV - 1:SSM_CONV, cols]
        head = jnp.zeros((pad, SW), F32)
        for k in range(SSM_CONV - 1):
            s = SSM_CONV - 1 - k
            w_k = cw_ref[k:k + 1, cols]
            conv = conv + jnp.dot(shift_ref[s - 1], xb, preferred_element_type=F32) * w_k
            head = head + tail_ref[pad - s:2 * pad - s, cols] * w_k
        conv = jnp.concatenate([conv[0:pad] + head, conv[pad:]], axis=0)
        tail_ref[0:pad, cols] = x[L - pad:L]
        xa = _silu(conv)
        if c * SW < SSM_INNER:
            dt_x = jnp.dot(dt_cat, e_ref[:, cols], preferred_element_type=F32)
            acs_x = jnp.dot(acs_cat, e_ref[:, cols], preferred_element_type=F32)
            xc = xa * dt_x
            xs_ref[:, cols] = xa
            xcb_ref[:, cols] = xc.astype(BF16)
            ea_ref[:, cols] = jnp.exp2(acs_x)
            wend_ref[:, cols] = (jnp.exp2(acs_x[L - 1:L, :] - acs_x) * xc).astype(BF16)
        elif c * SW < SSM_INNER + G * N:
            bm_ref[:, c * SW - SSM_INNER:(c + 1) * SW - SSM_INNER] = xa
        else:
            cm_ref[:, c * SW - SSM_INNER - G * N:(c + 1) * SW - SSM_INNER - G * N] = xa.astype(BF16)

    row = lax.broadcasted_iota(jnp.int32, (L, L), 0)
    col = lax.broadcasted_iota(jnp.int32, (L, L), 1)
    causal = row >= col
    lane = lax.broadcasted_iota(jnp.int32, (L, LANES), 1)
    low_half = lane < SSM_HEAD_DIM

    for g in range(G):
        gcols = slice(g * GW, (g + 1) * GW)
        bg = bm_ref[:, g * N:(g + 1) * N]
        cg = cm_ref[:, g * N:(g + 1) * N]
        cb = lax.dot_general(cg, bg.astype(BF16), nt, preferred_element_type=F32)
        st = state_ref[g]
        ea_g = ea_ref[:, gcols]
        y_off = jnp.dot(cg, st.astype(BF16), preferred_element_type=F32) * ea_g
        y_parts = []
        for pr in range(GW // LANES):
            c0 = g * GW + pr * LANES
            x_pair = xcb_ref[:, c0:c0 + LANES]
            zero = jnp.zeros_like(x_pair)
            y_pair = y_off[:, pr * LANES:(pr + 1) * LANES]
            for half in range(2):
                hd = c0 // SSM_HEAD_DIM + half
                seg = acs[:, hd:hd + 1] - acs_t[hd:hd + 1, :]
                decay = jnp.exp2(jnp.where(causal, seg, -jnp.inf))
                mmat = (cb * decay).astype(BF16)
                x_half = jnp.where(low_half, x_pair, zero) if half == 0 else jnp.where(low_half, zero, x_pair)
                y_pair = y_pair + jnp.dot(mmat, x_half, preferred_element_type=F32)
            y_parts.append(y_pair)
        bg_t = bg.T.astype(BF16)
        state_ref[g] = (st * ea_g[L - 1:L, :]
                        + jnp.dot(bg_t, wend_ref[:, gcols], preferred_element_type=F32))

        y = jnp.concatenate(y_parts, axis=1) + dskip_ref[:, gcols] * xs_ref[:, gcols]
        v = y * _silu(z_ref[0, :, gcols].astype(F32))
        ms = jnp.mean(v * v, axis=-1, keepdims=True)
        o_ref[0, :, gcols] = (v * lax.rsqrt(ms + NORM_EPS) * ng_ref[:, gcols]).astype(o_ref.dtype)


def _ssd(proj3, dt3, conv_w, conv_b, dt_bias, a_log, d_skip_x, norm_g, tril3, e2, shifts):
    bsz, s_len, _ = proj3.shape
    L = SSM_CHUNK
    nc = s_len // L
    assert nc * L == s_len
    const = lambda b, c: (0, 0)
    return pl.pallas_call(
        _ssd_kernel,
        grid=(bsz, nc),
        in_specs=[
            pl.BlockSpec((1, L, SSM_CONV_DIM), lambda b, c: (b, c, COL_XBC // SSM_CONV_DIM)),
            pl.BlockSpec((1, L, SSM_INNER), lambda b, c: (b, c, COL_Z // SSM_INNER)),
            pl.BlockSpec((1, L, LANES), lambda b, c: (b, c, 0)),
            pl.BlockSpec((SSM_CONV, SSM_CONV_DIM), const),
            pl.BlockSpec((1, SSM_CONV_DIM), const),
            pl.BlockSpec((1, LANES), const),
            pl.BlockSpec((1, LANES), const),
            pl.BlockSpec((1, SSM_INNER), const),
            pl.BlockSpec((1, SSM_INNER), const),
            pl.BlockSpec((L, 3 * L), const),
            pl.BlockSpec((2 * LANES, SSM_INNER), const),
            pl.BlockSpec((SSM_CONV - 1, L, L), lambda b, c: (0, 0, 0)),
        ],
        out_specs=pl.BlockSpec((1, L, SSM_INNER), lambda b, c: (b, c, 0)),
        out_shape=jax.ShapeDtypeStruct((bsz, s_len, SSM_INNER), BF16),
        scratch_shapes=[
            pltpu.VMEM((2 * SUBLANES, SSM_CONV_DIM), F32),
            pltpu.VMEM((SSM_GROUPS, SSM_STATE, SSM_INNER // SSM_GROUPS), F32),
            pltpu.VMEM((L, SSM_INNER), F32),
            pltpu.VMEM((L, SSM_INNER), BF16),
            pltpu.VMEM((L, SSM_INNER), F32),
            pltpu.VMEM((L, SSM_INNER), BF16),
            pltpu.VMEM((L, SSM_GROUPS * SSM_STATE), F32),
            pltpu.VMEM((L, SSM_GROUPS * SSM_STATE), BF16),
        ],
        compiler_params=_cparams(("arbitrary", "arbitrary")),
        name="ssd",
    )(proj3, proj3, dt3, conv_w, conv_b, dt_bias, a_log, d_skip_x, norm_g, tril3, e2, shifts)


def _memkv_kernel(mem_ref, g_ref, w_ref, kv_ref):
    x = mem_ref[...]
    ms = jnp.mean(x * x, axis=-1, keepdims=True)
    u = (x * lax.rsqrt(ms + NORM_EPS) * g_ref[...]).astype(BF16)
    kv_ref[...] = jnp.dot(u, w_ref[...], preferred_element_type=F32).astype(BF16)


def _mem_kv(mem2, g, w_kv):
    rows = mem2.shape[0]
    width = w_kv.shape[1]
    return pl.pallas_call(
        _memkv_kernel,
        grid=(1,),
        in_specs=[
            pl.BlockSpec((rows, D_MODEL), lambda i: (0, 0)),
            pl.BlockSpec((1, D_MODEL), lambda i: (0, 0)),
            pl.BlockSpec((D_MODEL, width), lambda i: (0, 0)),
        ],
        out_specs=pl.BlockSpec((rows, width), lambda i: (0, 0)),
        out_shape=jax.ShapeDtypeStruct((rows, width), BF16),
        compiler_params=_cparams(("arbitrary",)),
        name="mem_kv",
    )(mem2, g, w_kv)


def _merge_kernel(x_ref, oa_ref, os_ref, qm_ref, gl_ref, kv_ref, bg_ref, wa_ref, ws_ref, wm_ref, wo_ref, h_ref):
    nt = (((1,), (1,)), ((), ()))
    hd = MEM_HEAD_DIM
    width = MEM_HEADS * hd
    scale = hd ** -0.5
    qm = qm_ref[0]
    kv = kv_ref[0]
    outs = []
    for hh in range(MEM_HEADS):
        q = qm[:, hh * hd:(hh + 1) * hd]
        km = kv[:, hh * hd:(hh + 1) * hd]
        vm = kv[:, width + hh * hd:width + (hh + 1) * hd]
        s = lax.dot_general(q, km, nt, preferred_element_type=F32) * scale
        p = jnp.exp(s - jnp.max(s, axis=1, keepdims=True))
        o = jnp.dot(p.astype(BF16), vm, preferred_element_type=F32)
        outs.append(o / jnp.sum(p, axis=1, keepdims=True))
    o_mem = jnp.concatenate(outs, axis=1).astype(BF16)

    gates = _sigmoid(gl_ref[0].astype(F32) + bg_ref[...])
    merged = (gates[:, :D_MODEL] * jnp.dot(oa_ref[0], wa_ref[...], preferred_element_type=F32)
              + gates[:, D_MODEL:2 * D_MODEL] * jnp.dot(os_ref[0], ws_ref[...], preferred_element_type=F32)
              + gates[:, 2 * D_MODEL:] * jnp.dot(o_mem, wm_ref[...], preferred_element_type=F32))
    h_ref[0] = x_ref[0] + jnp.dot(merged.astype(BF16), wo_ref[...], preferred_element_type=F32)


def _resident(shape):
    return pl.BlockSpec(shape, lambda *_: (0,) * len(shape), pipeline_mode=pl.Buffered(1))


def _merge(x, o_attn, o_ssm, proj3, kv3, b_gate, wa, ws, wm, wo, tm):
    bsz, s_len, _ = x.shape
    mem_len = kv3.shape[1]
    return pl.pallas_call(
        _merge_kernel,
        grid=(bsz, s_len // tm),
        in_specs=[
            pl.BlockSpec((1, tm, D_MODEL), lambda b, i: (b, i, 0)),
            pl.BlockSpec((1, tm, D_MODEL), lambda b, i: (b, i, 0)),
            pl.BlockSpec((1, tm, SSM_INNER), lambda b, i: (b, i, 0)),
            pl.BlockSpec((1, tm, D_MODEL), lambda b, i: (b, i, COL_QM // D_MODEL)),
            pl.BlockSpec((1, tm, 3 * D_MODEL), lambda b, i: (b, i, COL_GATE // (3 * D_MODEL))),
            pl.BlockSpec((1, mem_len, 2 * D_MODEL), lambda b, i: (b, 0, 0)),
            _resident((1, 3 * D_MODEL)),
            _resident((D_MODEL, D_MODEL)),
            _resident((SSM_INNER, D_MODEL)),
            _resident((D_MODEL, D_MODEL)),
            _resident((D_MODEL, D_MODEL)),
        ],
        out_specs=pl.BlockSpec((1, tm, D_MODEL), lambda b, i: (b, i, 0)),
        out_shape=jax.ShapeDtypeStruct((bsz, s_len, D_MODEL), F32),
        compiler_params=_cparams(("arbitrary", "arbitrary")),
        name="merge",
    )(x, o_attn, o_ssm, proj3, proj3, kv3, b_gate, wa, ws, wm, wo)


def _ffn_kernel(h_ref, ng_ref, wup_ref, cw_ref, cb_ref, wdn_ref, fg_ref, o_ref, hid_ref, *, tm):
    pad = SUBLANES

    @pl.when(pl.program_id(1) == 0)
    def _():
        hid_ref[0:pad, :] = jnp.zeros((pad, 2 * FFN_HIDDEN), F32)

    h = h_ref[0]
    ms = jnp.mean(h * h, axis=-1, keepdims=True)
    u = (h * lax.rsqrt(ms + NORM_EPS) * ng_ref[...]).astype(BF16)
    hid_ref[pad:pad + tm, :] = jnp.dot(u, wup_ref[...], preferred_element_type=F32)
    conv = cb_ref[...]
    for k in range(FFN_CONV):
        off = pad - (FFN_CONV - 1) + k
        conv = conv + hid_ref[off:off + tm, :] * cw_ref[k:k + 1, :]
    hid_ref[0:pad, :] = hid_ref[tm:tm + pad, :]
    act = (_silu(conv[:, :FFN_HIDDEN]) * conv[:, FFN_HIDDEN:]).astype(BF16)
    y = h + jnp.dot(act, wdn_ref[...], preferred_element_type=F32)
    ms2 = jnp.mean(y * y, axis=-1, keepdims=True)
    o_ref[0] = y * lax.rsqrt(ms2 + NORM_EPS) * fg_ref[...]


def _ffn(h, norm_g, w_up, conv_w, conv_b, w_down, final_g, tm):
    bsz, s_len, _ = h.shape
    f2 = 2 * FFN_HIDDEN
    return pl.pallas_call(
        functools.partial(_ffn_kernel, tm=tm),
        grid=(bsz, s_len // tm),
        in_specs=[
            pl.BlockSpec((1, tm, D_MODEL), lambda b, i: (b, i, 0)),
            _resident((1, D_MODEL)),
            _resident((D_MODEL, f2)),
            _resident((FFN_CONV, f2)),
            _resident((1, f2)),
            _resident((FFN_HIDDEN, D_MODEL)),
            _resident((1, D_MODEL)),
        ],
        out_specs=pl.BlockSpec((1, tm, D_MODEL), lambda b, i: (b, i, 0)),
        out_shape=jax.ShapeDtypeStruct((bsz, s_len, D_MODEL), F32),
        scratch_shapes=[pltpu.VMEM((tm + 2 * SUBLANES, f2), F32)],
        compiler_params=_cparams(("arbitrary", "arbitrary")),
        name="ffn",
    )(h, norm_g, w_up, conv_w, conv_b, w_down, final_g)


def _ssd_constants():
    L = SSM_CHUNK
    tril = np.tril(np.ones((L, L), np.float32))
    tril3 = np.concatenate([tril, tril, tril], axis=1)
    e = np.zeros((LANES, SSM_INNER), np.float32)
    for hd in range(SSM_HEADS):
        e[hd, hd * SSM_HEAD_DIM:(hd + 1) * SSM_HEAD_DIM] = 1.0
    e2 = np.concatenate([e, e], axis=0)
    shifts = np.stack([np.eye(L, k=-s, dtype=np.float32) for s in range(1, SSM_CONV)])
    return jnp.asarray(tril3, BF16), jnp.asarray(e2, BF16), jnp.asarray(shifts, BF16)


def _pad_lanes(v):
    return jnp.pad(v, ((0, 0), (0, LANES - v.shape[1])))


def _layer(h, mem, rel_bias, mix_norm_g, w_in, b_gate, ssm_conv_w, ssm_conv_b, ssm_dt_bias, ssm_A_log, ssm_D,
           ssm_norm_g, mem_norm_g, w_mem_kv, w_br_attn, w_br_ssm, w_br_mem, w_out, ffn_norm_g, w_ffn_up,
           ffn_conv_w, ffn_conv_b, w_ffn_down, final_g):
    bsz, s_len, _ = h.shape
    w_t = w_in.T.astype(BF16)
    w_dt = jnp.pad(w_t[IN_PROJ_SPLIT:IN_PROJ_SPLIT + SSM_DT_COLS], ((0, LANES - SSM_DT_COLS), (0, 0)))

    proj, dt_raw, kmean = _in_proj(h.reshape(bsz * s_len, D_MODEL), mix_norm_g[None, :], w_t, w_dt,
                                   tm=IN_PROJ_TM, tn=IN_PROJ_TN)
    proj3 = proj.reshape(bsz, s_len, PROJ_WIDTH)
    dt3 = dt_raw.reshape(bsz, s_len, LANES)
    kmean3 = kmean.reshape(bsz, s_len // MOBA_BLOCK, ATTN_HEADS * ATTN_HEAD_DIM)

    o_attn = _moba(proj3, kmean3, rel_bias.T)

    tril3, e2, shifts = _ssd_constants()
    o_ssm = _ssd(proj3, dt3, ssm_conv_w, ssm_conv_b[None, :], _pad_lanes(ssm_dt_bias[None, :]),
                 _pad_lanes(ssm_A_log[None, :]), jnp.repeat(ssm_D, SSM_HEAD_DIM)[None, :], ssm_norm_g[None, :],
                 tril3, e2, shifts)

    mem_len = mem.shape[1]
    kv = _mem_kv(mem.reshape(bsz * mem_len, D_MODEL), mem_norm_g[None, :], w_mem_kv.astype(BF16))
    kv3 = kv.reshape(bsz, mem_len, 2 * D_MODEL)

    h1 = _merge(h, o_attn, o_ssm, proj3, kv3, b_gate[None, :], w_br_attn.astype(BF16), w_br_ssm.astype(BF16),
                w_br_mem.astype(BF16), w_out.astype(BF16), tm=MERGE_TM)

    return _ffn(h1, ffn_norm_g[None, :], w_ffn_up.astype(BF16), ffn_conv_w, ffn_conv_b[None, :],
                w_ffn_down.astype(BF16), final_g[None, :], tm=FFN_TM)


def kernel(x, mem, rel_bias, mix_norm_g, w_in, b_gate, ssm_conv_w, ssm_conv_b, ssm_dt_bias, ssm_A_log, ssm_D,
           ssm_norm_g, mem_norm_g, w_mem_kv, w_br_attn, w_br_ssm, w_br_mem, w_out, ffn_norm_g, w_ffn_up,
           ffn_conv_w, ffn_conv_b, w_ffn_down, final_norm_g):
    assert w_in.shape[0] == 1, "single-layer trunk"
    return _layer(x, mem, rel_bias, mix_norm_g[0], w_in[0], b_gate[0], ssm_conv_w[0], ssm_conv_b[0],
                  ssm_dt_bias[0], ssm_A_log[0], ssm_D[0], ssm_norm_g[0], mem_norm_g[0], w_mem_kv[0],
                  w_br_attn[0], w_br_ssm[0], w_br_mem[0], w_out[0], ffn_norm_g[0], w_ffn_up[0],
                  ffn_conv_w[0], ffn_conv_b[0], w_ffn_down[0], final_norm_g)
```

```python
import functools
import math

import numpy as np
import jax
import jax.numpy as jnp
from jax import lax
from jax.experimental import pallas as pl
from jax.experimental.pallas import tpu as pltpu

F32 = jnp.float32
BF16 = jnp.bfloat16

D_MODEL = 1024
ATTN_HEADS = 8
ATTN_HEAD_DIM = 128
MOBA_BLOCK = 256
MOBA_TOPK = 3
REL_BUCKETS = 32
REL_MAX_DIST = 1024
SSM_INNER = 2048
SSM_HEAD_DIM = 64
SSM_HEADS = 32
SSM_GROUPS = 4
SSM_STATE = 128
SSM_CONV = 4
SSM_CHUNK = 256
SSM_CONV_DIM = 3072
MEM_HEADS = 4
MEM_HEAD_DIM = 256
FFN_HIDDEN = 2816
FFN_CONV = 3
NORM_EPS = 1e-6

LANES = 128
SUBLANES = 8
BF16_SUBLANES = 16
VMEM_LIMIT = 56 * 1024 * 1024

IN_PROJ_TM = 2048
IN_PROJ_TN = ATTN_HEADS * ATTN_HEAD_DIM
MERGE_TM = 512
FFN_TM = 512

ATTN_WIDTH = ATTN_HEADS * ATTN_HEAD_DIM
COL_XBC = 0
COL_GATE = COL_XBC + SSM_CONV_DIM
COL_Z = COL_GATE + 3 * D_MODEL
COL_Q = COL_Z + SSM_INNER
COL_K = COL_Q + ATTN_WIDTH
COL_V = COL_K + ATTN_WIDTH
COL_QM = COL_V + ATTN_WIDTH
PROJ_WIDTH = COL_QM + MEM_HEADS * MEM_HEAD_DIM
IN_PROJ_PERM = (8, 9, 10, 6, 7, 0, 1, 2, 11, 3, 4, 5)
IN_PROJ_K_TILE = 1
IN_PROJ_SPLIT = 3 * ATTN_WIDTH + SSM_INNER + SSM_CONV_DIM
SSM_DT_COLS = SSM_HEADS

MOBA_NEAR = -(-(REL_MAX_DIST + MOBA_BLOCK - 1) // MOBA_BLOCK)
NEG_BIG = -1e30
LOG2E = math.log2(math.e)
MOBA_QBLOCKS = 4
MOBA_GROUP = 4
MOBA_VT_ROWS = ATTN_HEAD_DIM + BF16_SUBLANES
assert MOBA_GROUP <= MOBA_NEAR + 1
assert MOBA_GROUP <= SUBLANES


def _cparams(sem):
    return pltpu.CompilerParams(dimension_semantics=sem, vmem_limit_bytes=VMEM_LIMIT)


def _sigmoid(x):
    return 1.0 / (1.0 + jnp.exp(-x))


def _silu(x):
    h = 0.5 * x
    return h + h * jnp.tanh(h)


def _split_bf16(x, parts):
    out = []
    r = x
    for _ in range(parts):
        hi = r.astype(BF16)
        out.append(hi)
        r = r - hi.astype(F32)
    return out


def _inproj_kernel(perm_ref, x_ref, g_ref, w_ref, wdt_ref, proj_ref, dt_ref, kmean_ref, u_ref, *, k_tile):
    del perm_ref
    j = pl.program_id(1)
    nt = (((1,), (1,)), ((), ()))

    @pl.when(j == 0)
    def _():
        x = x_ref[...]
        ms = jnp.mean(x * x, axis=-1, keepdims=True)
        u = (x * lax.rsqrt(ms + NORM_EPS) * g_ref[...]).astype(BF16)
        u_ref[...] = u
        dt_ref[...] = lax.dot_general(u, wdt_ref[...], nt, preferred_element_type=F32)

    acc = lax.dot_general(u_ref[...], w_ref[...], nt, preferred_element_type=F32)
    proj_ref[...] = acc.astype(BF16)

    @pl.when(j == k_tile)
    def _():
        blk = MOBA_BLOCK
        for r in range(acc.shape[0] // blk):
            kmean_ref[0, r:r + 1, :] = jnp.sum(acc[r * blk:(r + 1) * blk], axis=0, keepdims=True) * (1.0 / blk)


def _in_proj(x2, g, w_t, w_dt, tm, tn):
    t = x2.shape[0]
    kw = ATTN_HEADS * ATTN_HEAD_DIM
    n_a = IN_PROJ_SPLIT // tn
    n_b = (w_t.shape[0] - IN_PROJ_SPLIT - SSM_DT_COLS) // tn
    assert tn == kw and tm % MOBA_BLOCK == 0 and (n_a + n_b) * tn == PROJ_WIDTH == len(IN_PROJ_PERM) * tn
    grid_spec = pltpu.PrefetchScalarGridSpec(
        num_scalar_prefetch=1,
        grid=(t // tm, n_a + n_b),
        in_specs=[
            pl.BlockSpec((tm, D_MODEL), lambda i, j, perm: (i, 0)),
            pl.BlockSpec((1, D_MODEL), lambda i, j, perm: (0, 0)),
            pl.BlockSpec((pl.Element(tn), pl.Element(D_MODEL)),
                         lambda i, j, perm: (pl.multiple_of(j * tn + jnp.where(j >= n_a, SSM_DT_COLS, 0), BF16_SUBLANES), 0)),
            pl.BlockSpec((LANES, D_MODEL), lambda i, j, perm: (0, 0)),
        ],
        out_specs=[
            pl.BlockSpec((tm, tn), lambda i, j, perm: (i, perm[j])),
            pl.BlockSpec((tm, LANES), lambda i, j, perm: (i, 0)),
            pl.BlockSpec((1, tm // MOBA_BLOCK, kw), lambda i, j, perm: (i, 0, 0)),
        ],
        scratch_shapes=[pltpu.VMEM((tm, D_MODEL), BF16)],
    )
    return pl.pallas_call(
        functools.partial(_inproj_kernel, k_tile=IN_PROJ_K_TILE),
        grid_spec=grid_spec,
        out_shape=[
            jax.ShapeDtypeStruct((t, PROJ_WIDTH), BF16),
            jax.ShapeDtypeStruct((t, LANES), F32),
            jax.ShapeDtypeStruct((t // tm, tm // MOBA_BLOCK, kw), F32),
        ],
        compiler_params=_cparams(("arbitrary", "arbitrary")),
        name="in_proj",
    )(jnp.asarray(IN_PROJ_PERM, jnp.int32), x2, g, w_t, w_dt)


def _t5_bucket(dist):
    n = jnp.maximum(dist, 0)
    max_exact = REL_BUCKETS // 2
    nf = jnp.maximum(n, max_exact).astype(F32)
    large = max_exact + (jnp.log(nf * (1.0 / max_exact)) / math.log(REL_MAX_DIST / max_exact)
                         * (REL_BUCKETS - max_exact)).astype(jnp.int32)
    large = jnp.minimum(large, REL_BUCKETS - 1)
    return jnp.where(n < max_exact, n, large)


def _bucket_range(dist_lo, dist_hi):
    def bucket(n):
        n = max(n, 0)
        max_exact = REL_BUCKETS // 2
        if n < max_exact:
            return n
        return min(max_exact + int(math.log(n / max_exact) / math.log(REL_MAX_DIST / max_exact)
                                   * (REL_BUCKETS - max_exact)), REL_BUCKETS - 1)
    return max(bucket(dist_lo) - 1, 0), min(bucket(dist_hi) + 1, REL_BUCKETS - 1) + 1


def _moba_kernel(rel_ref, q_ref, k_ref, v_ref, km_ref, o_ref, kmean_ref, bias_ref, vt_ref, neg_ref, s_ref, cm_ref,
                 m_ref, acc_ref, *, nb):
    h = pl.program_id(0)
    b = pl.program_id(1)
    i = pl.program_id(2)
    blk = MOBA_BLOCK
    scale = ATTN_HEAD_DIM ** -0.5


    @pl.when((b == 0) & (i == 0))
    def _():
        key = lax.broadcasted_iota(jnp.int32, (blk, blk), 0)
        qry = lax.broadcasted_iota(jnp.int32, (blk, blk), 1)
        for d in range(MOBA_NEAR):
            bucket = _t5_bucket(d * blk + qry - key)
            tile = jnp.zeros((blk, blk), F32)
            for bk in range(*_bucket_range(d * blk - (blk - 1), d * blk + (blk - 1))):
                tile = jnp.where(bucket == bk, rel_ref[h, bk], tile)
            bias_ref[d] = tile * LOG2E

    @pl.when(i == 0)
    def _():
        kmean_ref[...] = jnp.zeros_like(kmean_ref)
        kmean_ref[0:nb, :] = km_ref[0]
        ones_rows = (lax.broadcasted_iota(jnp.int32, (MOBA_VT_ROWS - ATTN_HEAD_DIM, blk), 0) == 0).astype(BF16)
        for jb in range(nb):
            vt_ref[jb, 0:ATTN_HEAD_DIM, :] = v_ref[0, jb * blk:(jb + 1) * blk, :].astype(F32).T.astype(BF16)
            vt_ref[jb, ATTN_HEAD_DIM:MOBA_VT_ROWS, :] = ones_rows

    qw = MOBA_QBLOCKS * blk
    i0 = i * MOBA_QBLOCKS
    q_t = q_ref[0].astype(F32).T
    qs_t = (q_t * (scale * LOG2E)).astype(BF16)

    gate = jnp.dot(kmean_ref[...].astype(BF16), q_t.astype(BF16), preferred_element_type=F32)
    nbp = kmean_ref.shape[0]
    row = lax.broadcasted_iota(jnp.int32, (nbp, qw), 0).astype(F32)
    own = (i0 + lax.broadcasted_iota(jnp.int32, (1, qw), 1) // blk).astype(F32)
    g = jnp.where(row < own, gate, -jnp.inf)
    sel = jnp.zeros((nbp, qw), F32)
    for t in range(MOBA_TOPK):
        mx = jnp.max(g, axis=0, keepdims=True)
        idx = jnp.min(jnp.where(g == mx, row, float(nbp)), axis=0, keepdims=True)
        hit = row == idx
        sel = jnp.maximum(sel, jnp.where(hit & (own > t), 1.0, 0.0))
        g = jnp.where(hit, -jnp.inf, g)
    neg_ref[...] = jnp.where(sel > 0.5, 0.0, NEG_BIG)

    def far_scores(j0, slot):
        start = pl.multiple_of(j0 * blk, blk)
        s = jnp.dot(k_ref[0, pl.ds(start, MOBA_GROUP * blk), :], qs_t, preferred_element_type=F32)
        s_ref[slot] = s
        for u in range(MOBA_GROUP):
            cm_ref[slot, u:u + 1, :] = jnp.max(s[u * blk:(u + 1) * blk], axis=0, keepdims=True)

    far_scores(0, 0)

    def softmax_group(scores, offsets, v_idx, m_old, col_max=None):
        if col_max is None:
            col_max = [jnp.max(sj, axis=0, keepdims=True) for sj in scores]
        m_new = m_old
        for cm, off in zip(col_max, offsets):
            m_new = jnp.maximum(m_new, cm + off)
        pv = jnp.zeros((MOBA_VT_ROWS, m_old.shape[1]), F32)
        for sj, off, vj in zip(scores, offsets, v_idx):
            pj = jnp.exp2(sj + (off - m_new))
            pv = pv + jnp.dot(vt_ref[vj], pj.astype(BF16), preferred_element_type=F32)
        return m_new, pv

    far_bias = rel_ref[h, REL_BUCKETS - 1] * LOG2E

    key = lax.broadcasted_iota(jnp.int32, (blk, blk), 0)
    qry = lax.broadcasted_iota(jnp.int32, (blk, blk), 1)
    scores, offsets, v_idx, first_lane = [], [], [], []
    for e in range(MOBA_NEAR - 1 + MOBA_QBLOCKS):
        jn = i0 - (MOBA_NEAR - 1) + e
        jc = jnp.maximum(jn, 0)
        start = pl.multiple_of(jc * blk, blk)
        w_min = max(0, e - (MOBA_NEAR - 1))
        lo = w_min * blk
        se = jnp.dot(k_ref[0, pl.ds(start, blk), :], qs_t[:, lo:], preferred_element_type=F32)
        sel_row = neg_ref[pl.ds(jnp.where(jn >= 0, jn, nbp - 1), 1), :]
        s_parts, o_parts = [], []
        for w in range(w_min, MOBA_QBLOCKS):
            d = w + (MOBA_NEAR - 1) - e
            sw = se[:, (w - w_min) * blk:(w - w_min + 1) * blk]
            ow = sel_row[:, w * blk:(w + 1) * blk]
            if d == 0:
                sw = jnp.where(qry >= key, sw + bias_ref[0], NEG_BIG)
                ow = jnp.zeros((1, blk), F32)
            elif d < MOBA_NEAR:
                sw = sw + bias_ref[d]
            else:
                ow = ow + far_bias
            s_parts.append(sw)
            o_parts.append(ow)
        scores.append(jnp.concatenate(s_parts, axis=1))
        offsets.append(jnp.concatenate(o_parts, axis=1))
        v_idx.append(jc)
        first_lane.append(lo)
    m_floor = jnp.concatenate(
        [jnp.max(scores[MOBA_NEAR - 1 + w][:, 0:blk], axis=0, keepdims=True) for w in range(MOBA_QBLOCKS)], axis=1)
    parts = []
    n_pair = 4
    n_full = sum(1 for lo in first_lane if lo == 0) // n_pair * n_pair
    groups = [list(range(g, g + n_pair)) for g in range(0, n_full, n_pair)] + [[e] for e in range(n_full, len(scores))]
    for grp in groups:
        lo = first_lane[grp[0]]
        mu, pvu = softmax_group([scores[e] for e in grp], [offsets[e] for e in grp], [v_idx[e] for e in grp],
                                m_floor[:, lo:])
        if lo:
            mu = jnp.concatenate([m_floor[:, :lo], mu], axis=1)
            pvu = jnp.concatenate([jnp.zeros((MOBA_VT_ROWS, lo), F32), pvu], axis=1)
        parts.append((mu, pvu))
    m0 = parts[0][0]
    for mu, _ in parts[1:]:
        m0 = jnp.maximum(m0, mu)
    pv0 = jnp.zeros((MOBA_VT_ROWS, qw), F32)
    for mu, pvu in parts:
        pv0 = pv0 + jnp.exp2(mu - m0) * pvu
    m_ref[...] = m0
    acc_ref[...] = pv0

    n_far = jnp.maximum(i0 - (MOBA_NEAR - 1), 0)

    n_groups = (n_far + MOBA_GROUP - 1) // MOBA_GROUP

    def far_group(gi, slot, prefetch):
        j0 = gi * MOBA_GROUP
        if prefetch:
            far_scores(j0 + MOBA_GROUP, 1 - slot)
        scores, offsets, v_idx, col_max = [], [], [], []
        for u in range(MOBA_GROUP):
            ju = j0 + u
            scores.append(s_ref[slot, u * blk:(u + 1) * blk, :])
            col_max.append(cm_ref[slot, u:u + 1, :])
            offsets.append(neg_ref[pl.ds(jnp.where(ju < n_far, ju, nbp - 1), 1), :] + far_bias)
            v_idx.append(ju)
        m_old = m_ref[...]
        m_new, pv = softmax_group(scores, offsets, v_idx, m_old, col_max)
        acc_ref[...] = jnp.exp2(m_old - m_new) * acc_ref[...] + pv
        m_ref[...] = m_new

    def far_body(t, carry):
        far_group(2 * t, 0, True)
        far_group(2 * t + 1, 1, True)
        return carry

    n_pairs = jnp.maximum(n_groups - 1, 0) // 2
    lax.fori_loop(0, n_pairs, far_body, 0)
    g_rest = 2 * n_pairs
    n_rest = n_groups - g_rest

    @pl.when(n_rest == 1)
    def _():
        far_group(g_rest, 0, False)

    @pl.when(n_rest == 2)
    def _():
        far_group(g_rest, 0, True)
        far_group(g_rest + 1, 1, False)

    out_t = acc_ref[0:ATTN_HEAD_DIM, :] / acc_ref[ATTN_HEAD_DIM:ATTN_HEAD_DIM + 1, :]
    o_ref[0] = out_t.T.astype(o_ref.dtype)


def _moba(proj3, kmean3, rel_t):
    bsz, s_len, _ = proj3.shape
    blk = MOBA_BLOCK
    nb = s_len // blk
    qw = MOBA_QBLOCKS * blk
    assert nb * blk == s_len and nb % MOBA_QBLOCKS == 0 and nb >= MOBA_GROUP
    nbp = -(-(nb + 1) // BF16_SUBLANES) * BF16_SUBLANES
    cb = LANES
    return pl.pallas_call(
        functools.partial(_moba_kernel, nb=nb),
        grid=(ATTN_HEADS, bsz, nb // MOBA_QBLOCKS),
        in_specs=[
            pl.BlockSpec(memory_space=pltpu.SMEM),
            pl.BlockSpec((1, qw, cb), lambda h, b, i: (b, i, COL_Q // cb + h)),
            pl.BlockSpec((1, s_len, cb), lambda h, b, i: (b, 0, COL_K // cb + h)),
            pl.BlockSpec((1, s_len, cb), lambda h, b, i: (b, 0, COL_V // cb + h)),
            pl.BlockSpec((1, nb, cb), lambda h, b, i: (b, 0, h)),
        ],
        out_specs=pl.BlockSpec((1, qw, cb), lambda h, b, i: (b, i, h)),
        out_shape=jax.ShapeDtypeStruct((bsz, s_len, ATTN_HEADS * ATTN_HEAD_DIM), BF16),
        scratch_shapes=[
            pltpu.VMEM((nbp, ATTN_HEAD_DIM), F32),
            pltpu.VMEM((MOBA_NEAR, blk, blk), F32),
            pltpu.VMEM((nb, MOBA_VT_ROWS, blk), BF16),
            pltpu.VMEM((nbp, qw), F32),
            pltpu.VMEM((2, MOBA_GROUP * blk, qw), F32),
            pltpu.VMEM((2, SUBLANES, qw), F32),
            pltpu.VMEM((1, qw), F32),
            pltpu.VMEM((MOBA_VT_ROWS, qw), F32),
        ],
        compiler_params=_cparams(("arbitrary", "arbitrary", "arbitrary")),
        name="moba",
    )(rel_t, proj3, proj3, proj3, kmean3)


def _ssd_kernel(xbc_ref, z_ref, dt_ref, cw_ref, cbias_ref, dtb_ref, alog_ref, dskip_ref, ng_ref,
                tril_ref, e_ref, shift_ref, o_ref, tail_ref, state_ref, xs_ref, xcb_ref, ea_ref, wend_ref, bm_ref,
                cm_ref):
    L = SSM_CHUNK
    G = SSM_GROUPS
    N = SSM_STATE
    GW = SSM_INNER // G
    pad = SUBLANES
    nt = (((1,), (1,)), ((), ()))

    @pl.when(pl.program_id(1) == 0)
    def _():
        tail_ref[...] = jnp.zeros_like(tail_ref)
        state_ref[...] = jnp.zeros_like(state_ref)

    dtr = dt_ref[0] + dtb_ref[...]
    dt = jnp.maximum(dtr, 0.0) + jnp.log1p(jnp.exp(-jnp.abs(dtr)))
    a = dt * (-jnp.exp(alog_ref[...]))
    a_cat = jnp.concatenate(_split_bf16(a, 3), axis=0)
    acs = jnp.dot(tril_ref[...], a_cat, preferred_element_type=F32) * LOG2E
    acs_t = acs.T

    dt_cat = jnp.concatenate(_split_bf16(dt, 2), axis=1)
    acs_cat = jnp.concatenate(_split_bf16(acs, 2), axis=1)

    SW = 2 * LANES
    for c in range(SSM_CONV_DIM // SW):
        cols = slice(c * SW, (c + 1) * SW)
        xb = xbc_ref[0, :, cols]
        x = xb.astype(F32)
        conv = cbias_ref[:, cols] + x * cw_ref[SSM_CONV - 1:SSM_CONV, cols]
        head = jnp.zeros((pad, SW), F32)
        for k in range(SSM_CONV - 1):
            s = SSM_CONV - 1 - k
            w_k = cw_ref[k:k + 1, cols]
            conv = conv + jnp.dot(shift_ref[s - 1], xb, preferred_element_type=F32) * w_k
            head = head + tail_ref[pad - s:2 * pad - s, cols] * w_k
        conv = jnp.concatenate([conv[0:pad] + head, conv[pad:]], axis=0)
        tail_ref[0:pad, cols] = x[L - pad:L]
        xa = _silu(conv)
        if c * SW < SSM_INNER:
            dt_x = jnp.dot(dt_cat, e_ref[:, cols], preferred_element_type=F32)
            acs_x = jnp.dot(acs_cat, e_ref[:, cols], preferred_element_type=F32)
            xc = xa * dt_x
            xs_ref[:, cols] = xa
            xcb_ref[:, cols] = xc.astype(BF16)
            ea_ref[:, cols] = jnp.exp2(acs_x)
            wend_ref[:, cols] = (jnp.exp2(acs_x[L - 1:L, :] - acs_x) * xc).astype(BF16)
        elif c * SW < SSM_INNER + G * N:
            bm_ref[:, c * SW - SSM_INNER:(c + 1) * SW - SSM_INNER] = xa
        else:
            cm_ref[:, c * SW - SSM_INNER - G * N:(c + 1) * SW - SSM_INNER - G * N] = xa.astype(BF16)

    row = lax.broadcasted_iota(jnp.int32, (L, L), 0)
    col = lax.broadcasted_iota(jnp.int32, (L, L), 1)
    causal = row >= col
    lane = lax.broadcasted_iota(jnp.int32, (L, LANES), 1)
    low_half = lane < SSM_HEAD_DIM

    for g in range(G):
        gcols = slice(g * GW, (g + 1) * GW)
        bg = bm_ref[:, g * N:(g + 1) * N]
        cg = cm_ref[:, g * N:(g + 1) * N]
        cb = lax.dot_general(cg, bg.astype(BF16), nt, preferred_element_type=F32)
        st = state_ref[g]
        ea_g = ea_ref[:, gcols]
        y_off = jnp.dot(cg, st.astype(BF16), preferred_element_type=F32) * ea_g
        y_parts = []
        for pr in range(GW // LANES):
            c0 = g * GW + pr * LANES
            x_pair = xcb_ref[:, c0:c0 + LANES]
            zero = jnp.zeros_like(x_pair)
            y_pair = y_off[:, pr * LANES:(pr + 1) * LANES]
            for half in range(2):
                hd = c0 // SSM_HEAD_DIM + half
                seg = acs[:, hd:hd + 1] - acs_t[hd:hd + 1, :]
                decay = jnp.exp2(jnp.where(causal, seg, -jnp.inf))
                mmat = (cb * decay).astype(BF16)
                x_half = jnp.where(low_half, x_pair, zero) if half == 0 else jnp.where(low_half, zero, x_pair)
                y_pair = y_pair + jnp.dot(mmat, x_half, preferred_element_type=F32)
            y_parts.append(y_pair)
        bg_t = bg.T.astype(BF16)
        state_ref[g] = (st * ea_g[L - 1:L, :]
                        + jnp.dot(bg_t, wend_ref[:, gcols], preferred_element_type=F32))

        y = jnp.concatenate(y_parts, axis=1) + dskip_ref[:, gcols] * xs_ref[:, gcols]
        v = y * _silu(z_ref[0, :, gcols].astype(F32))
        ms = jnp.mean(v * v, axis=-1, keepdims=True)
        o_ref[0, :, gcols] = (v * lax.rsqrt(ms + NORM_EPS) * ng_ref[:, gcols]).astype(o_ref.dtype)


def _ssd(proj3, dt3, conv_w, conv_b, dt_bias, a_log, d_skip_x, norm_g, tril3, e2, shifts):
    bsz, s_len, _ = proj3.shape
    L = SSM_CHUNK
    nc = s_len // L
    assert nc * L == s_len
    const = lambda b, c: (0, 0)
    return pl.pallas_call(
        _ssd_kernel,
        grid=(bsz, nc),
        in_specs=[
            pl.BlockSpec((1, L, SSM_CONV_DIM), lambda b, c: (b, c, COL_XBC // SSM_CONV_DIM)),
            pl.BlockSpec((1, L, SSM_INNER), lambda b, c: (b, c, COL_Z // SSM_INNER)),
            pl.BlockSpec((1, L, LANES), lambda b, c: (b, c, 0)),
            pl.BlockSpec((SSM_CONV, SSM_CONV_DIM), const),
            pl.BlockSpec((1, SSM_CONV_DIM), const),
            pl.BlockSpec((1, LANES), const),
            pl.BlockSpec((1, LANES), const),
            pl.BlockSpec((1, SSM_INNER), const),
            pl.BlockSpec((1, SSM_INNER), const),
            pl.BlockSpec((L, 3 * L), const),
            pl.BlockSpec((2 * LANES, SSM_INNER), const),
            pl.BlockSpec((SSM_CONV - 1, L, L), lambda b, c: (0, 0, 0)),
        ],
        out_specs=pl.BlockSpec((1, L, SSM_INNER), lambda b, c: (b, c, 0)),
        out_shape=jax.ShapeDtypeStruct((bsz, s_len, SSM_INNER), BF16),
        scratch_shapes=[
            pltpu.VMEM((2 * SUBLANES, SSM_CONV_DIM), F32),
            pltpu.VMEM((SSM_GROUPS, SSM_STATE, SSM_INNER // SSM_GROUPS), F32),
            pltpu.VMEM((L, SSM_INNER), F32),
            pltpu.VMEM((L, SSM_INNER), BF16),
            pltpu.VMEM((L, SSM_INNER), F32),
            pltpu.VMEM((L, SSM_INNER), BF16),
            pltpu.VMEM((L, SSM_GROUPS * SSM_STATE), F32),
            pltpu.VMEM((L, SSM_GROUPS * SSM_STATE), BF16),
        ],
        compiler_params=_cparams(("arbitrary", "arbitrary")),
        name="ssd",
    )(proj3, proj3, dt3, conv_w, conv_b, dt_bias, a_log, d_skip_x, norm_g, tril3, e2, shifts)


def _memkv_kernel(mem_ref, g_ref, w_ref, kv_ref):
    x = mem_ref[...]
    ms = jnp.mean(x * x, axis=-1, keepdims=True)
    u = (x * lax.rsqrt(ms + NORM_EPS) * g_ref[...]).astype(BF16)
    kv_ref[...] = jnp.dot(u, w_ref[...], preferred_element_type=F32).astype(BF16)


def _mem_kv(mem2, g, w_kv):
    rows = mem2.shape[0]
    width = w_kv.shape[1]
    return pl.pallas_call(
        _memkv_kernel,
        grid=(1,),
        in_specs=[
            pl.BlockSpec((rows, D_MODEL), lambda i: (0, 0)),
            pl.BlockSpec((1, D_MODEL), lambda i: (0, 0)),
            pl.BlockSpec((D_MODEL, width), lambda i: (0, 0)),
        ],
        out_specs=pl.BlockSpec((rows, width), lambda i: (0, 0)),
        out_shape=jax.ShapeDtypeStruct((rows, width), BF16),
        compiler_params=_cparams(("arbitrary",)),
        name="mem_kv",
    )(mem2, g, w_kv)


def _merge_kernel(x_ref, oa_ref, os_ref, qm_ref, gl_ref, kv_ref, bg_ref, wa_ref, ws_ref, wm_ref, wo_ref, h_ref):
    nt = (((1,), (1,)), ((), ()))
    hd = MEM_HEAD_DIM
    width = MEM_HEADS * hd
    scale = hd ** -0.5
    qm = qm_ref[0]
    kv = kv_ref[0]
    outs = []
    for hh in range(MEM_HEADS):
        q = qm[:, hh * hd:(hh + 1) * hd]
        km = kv[:, hh * hd:(hh + 1) * hd]
        vm = kv[:, width + hh * hd:width + (hh + 1) * hd]
        s = lax.dot_general(q, km, nt, preferred_element_type=F32) * scale
        p = jnp.exp(s - jnp.max(s, axis=1, keepdims=True))
        o = jnp.dot(p.astype(BF16), vm, preferred_element_type=F32)
        outs.append(o / jnp.sum(p, axis=1, keepdims=True))
    o_mem = jnp.concatenate(outs, axis=1).astype(BF16)

    gates = _sigmoid(gl_ref[0].astype(F32) + bg_ref[...])
    merged = (gates[:, :D_MODEL] * jnp.dot(oa_ref[0], wa_ref[...], preferred_element_type=F32)
              + gates[:, D_MODEL:2 * D_MODEL] * jnp.dot(os_ref[0], ws_ref[...], preferred_element_type=F32)
              + gates[:, 2 * D_MODEL:] * jnp.dot(o_mem, wm_ref[...], preferred_element_type=F32))
    h_ref[0] = x_ref[0] + jnp.dot(merged.astype(BF16), wo_ref[...], preferred_element_type=F32)


def _resident(shape):
    return pl.BlockSpec(shape, lambda *_: (0,) * len(shape), pipeline_mode=pl.Buffered(1))


def _merge(x, o_attn, o_ssm, proj3, kv3, b_gate, wa, ws, wm, wo, tm):
    bsz, s_len, _ = x.shape
    mem_len = kv3.shape[1]
    return pl.pallas_call(
        _merge_kernel,
        grid=(bsz, s_len // tm),
        in_specs=[
            pl.BlockSpec((1, tm, D_MODEL), lambda b, i: (b, i, 0)),
            pl.BlockSpec((1, tm, D_MODEL), lambda b, i: (b, i, 0)),
            pl.BlockSpec((1, tm, SSM_INNER), lambda b, i: (b, i, 0)),
            pl.BlockSpec((1, tm, D_MODEL), lambda b, i: (b, i, COL_QM // D_MODEL)),
            pl.BlockSpec((1, tm, 3 * D_MODEL), lambda b, i: (b, i, COL_GATE // (3 * D_MODEL))),
            pl.BlockSpec((1, mem_len, 2 * D_MODEL), lambda b, i: (b, 0, 0)),
            _resident((1, 3 * D_MODEL)),
            _resident((D_MODEL, D_MODEL)),
            _resident((SSM_INNER, D_MODEL)),
            _resident((D_MODEL, D_MODEL)),
            _resident((D_MODEL, D_MODEL)),
        ],
        out_specs=pl.BlockSpec((1, tm, D_MODEL), lambda b, i: (b, i, 0)),
        out_shape=jax.ShapeDtypeStruct((bsz, s_len, D_MODEL), F32),
        compiler_params=_cparams(("arbitrary", "arbitrary")),
        name="merge",
    )(x, o_attn, o_ssm, proj3, proj3, kv3, b_gate, wa, ws, wm, wo)


def _ffn_kernel(h_ref, ng_ref, wup_ref, cw_ref, cb_ref, wdn_ref, fg_ref, o_ref, hid_ref, *, tm):
    pad = SUBLANES

    @pl.when(pl.program_id(1) == 0)
    def _():
        hid_ref[0:pad, :] = jnp.zeros((pad, 2 * FFN_HIDDEN), F32)

    h = h_ref[0]
    ms = jnp.mean(h * h, axis=-1, keepdims=True)
    u = (h * lax.rsqrt(ms + NORM_EPS) * ng_ref[...]).astype(BF16)
    hid_ref[pad:pad + tm, :] = jnp.dot(u, wup_ref[...], preferred_element_type=F32)
    conv = cb_ref[...]
    for k in range(FFN_CONV):
        off = pad - (FFN_CONV - 1) + k
        conv = conv + hid_ref[off:off + tm, :] * cw_ref[k:k + 1, :]
    hid_ref[0:pad, :] = hid_ref[tm:tm + pad, :]
    act = (_silu(conv[:, :FFN_HIDDEN]) * conv[:, FFN_HIDDEN:]).astype(BF16)
    y = h + jnp.dot(act, wdn_ref[...], preferred_element_type=F32)
    ms2 = jnp.mean(y * y, axis=-1, keepdims=True)
    o_ref[0] = y * lax.rsqrt(ms2 + NORM_EPS) * fg_ref[...]


def _ffn(h, norm_g, w_up, conv_w, conv_b, w_down, final_g, tm):
    bsz, s_len, _ = h.shape
    f2 = 2 * FFN_HIDDEN
    return pl.pallas_call(
        functools.partial(_ffn_kernel, tm=tm),
        grid=(bsz, s_len // tm),
        in_specs=[
            pl.BlockSpec((1, tm, D_MODEL), lambda b, i: (b, i, 0)),
            _resident((1, D_MODEL)),
            _resident((D_MODEL, f2)),
            _resident((FFN_CONV, f2)),
            _resident((1, f2)),
            _resident((FFN_HIDDEN, D_MODEL)),
            _resident((1, D_MODEL)),
        ],
        out_specs=pl.BlockSpec((1, tm, D_MODEL), lambda b, i: (b, i, 0)),
        out_shape=jax.ShapeDtypeStruct((bsz, s_len, D_MODEL), F32),
        scratch_shapes=[pltpu.VMEM((tm + 2 * SUBLANES, f2), F32)],
        compiler_params=_cparams(("arbitrary", "arbitrary")),
        name="ffn",
    )(h, norm_g, w_up, conv_w, conv_b, w_down, final_g)


def _ssd_constants():
    L = SSM_CHUNK
    tril = np.tril(np.ones((L, L), np.float32))
    tril3 = np.concatenate([tril, tril, tril], axis=1)
    e = np.zeros((LANES, SSM_INNER), np.float32)
    for hd in range(SSM_HEADS):
        e[hd, hd * SSM_HEAD_DIM:(hd + 1) * SSM_HEAD_DIM] = 1.0
    e2 = np.concatenate([e, e], axis=0)
    shifts = np.stack([np.eye(L, k=-s, dtype=np.float32) for s in range(1, SSM_CONV)])
    return jnp.asarray(tril3, BF16), jnp.asarray(e2, BF16), jnp.asarray(shifts, BF16)


def _pad_lanes(v):
    return jnp.pad(v, ((0, 0), (0, LANES - v.shape[1])))


def _layer(h, mem, rel_bias, mix_norm_g, w_in, b_gate, ssm_conv_w, ssm_conv_b, ssm_dt_bias, ssm_A_log, ssm_D,
           ssm_norm_g, mem_norm_g, w_mem_kv, w_br_attn, w_br_ssm, w_br_mem, w_out, ffn_norm_g, w_ffn_up,
           ffn_conv_w, ffn_conv_b, w_ffn_down, final_g):
    bsz, s_len, _ = h.shape
    w_t = w_in.T.astype(BF16)
    w_dt = jnp.pad(w_t[IN_PROJ_SPLIT:IN_PROJ_SPLIT + SSM_DT_COLS], ((0, LANES - SSM_DT_COLS), (0, 0)))

    proj, dt_raw, kmean = _in_proj(h.reshape(bsz * s_len, D_MODEL), mix_norm_g[None, :], w_t, w_dt,
                                   tm=IN_PROJ_TM, tn=IN_PROJ_TN)
    proj3 = proj.reshape(bsz, s_len, PROJ_WIDTH)
    dt3 = dt_raw.reshape(bsz, s_len, LANES)
    kmean3 = kmean.reshape(bsz, s_len // MOBA_BLOCK, ATTN_HEADS * ATTN_HEAD_DIM)

    o_attn = _moba(proj3, kmean3, rel_bias.T)

    tril3, e2, shifts = _ssd_constants()
    o_ssm = _ssd(proj3, dt3, ssm_conv_w, ssm_conv_b[None, :], _pad_lanes(ssm_dt_bias[None, :]),
                 _pad_lanes(ssm_A_log[None, :]), jnp.repeat(ssm_D, SSM_HEAD_DIM)[None, :], ssm_norm_g[None, :],
                 tril3, e2, shifts)

    mem_len = mem.shape[1]
    kv = _mem_kv(mem.reshape(bsz * mem_len, D_MODEL), mem_norm_g[None, :], w_mem_kv.astype(BF16))
    kv3 = kv.reshape(bsz, mem_len, 2 * D_MODEL)

    h1 = _merge(h, o_attn, o_ssm, proj3, kv3, b_gate[None, :], w_br_attn.astype(BF16), w_br_ssm.astype(BF16),
                w_br_mem.astype(BF16), w_out.astype(BF16), tm=MERGE_TM)

    return _ffn(h1, ffn_norm_g[None, :], w_ffn_up.astype(BF16), ffn_conv_w, ffn_conv_b[None, :],
                w_ffn_down.astype(BF16), final_g[None, :], tm=FFN_TM)


def kernel(x, mem, rel_bias, mix_norm_g, w_in, b_gate, ssm_conv_w, ssm_conv_b, ssm_dt_bias, ssm_A_log, ssm_D,
           ssm_norm_g, mem_norm_g, w_mem_kv, w_br_attn, w_br_ssm, w_br_mem, w_out, ffn_norm_g, w_ffn_up,
           ffn_conv_w, ffn_conv_b, w_ffn_down, final_norm_g):
    assert w_in.shape[0] == 1, "single-layer trunk"
    return _layer(x, mem, rel_bias, mix_norm_g[0], w_in[0], b_gate[0], ssm_conv_w[0], ssm_conv_b[0],
                  ssm_dt_bias[0], ssm_A_log[0], ssm_D[0], ssm_norm_g[0], mem_norm_g[0], w_mem_kv[0],
                  w_br_attn[0], w_br_ssm[0], w_br_mem[0], w_out[0], ffn_norm_g[0], w_ffn_up[0],
                  ffn_conv_w[0], ffn_conv_b[0], w_ffn_down[0], final_norm_g)
```

```python
import functools
import math

import numpy as np
import jax
import jax.numpy as jnp
from jax import lax
from jax.experimental import pallas as pl
from jax.experimental.pallas import tpu as pltpu

F32 = jnp.float32
BF16 = jnp.bfloat16

D_MODEL = 1024
ATTN_HEADS = 8
ATTN_HEAD_DIM = 128
MOBA_BLOCK = 256
MOBA_TOPK = 3
REL_BUCKETS = 32
REL_MAX_DIST = 1024
SSM_INNER = 2048
SSM_HEAD_DIM = 64
SSM_HEADS = 32
SSM_GROUPS = 4
SSM_STATE = 128
SSM_CONV = 4
SSM_CHUNK = 256
SSM_CONV_DIM = 3072
MEM_HEADS = 4
MEM_HEAD_DIM = 256
FFN_HIDDEN = 2816
FFN_CONV = 3
NORM_EPS = 1e-6

LANES = 128
SUBLANES = 8
BF16_SUBLANES = 16
VMEM_LIMIT = 56 * 1024 * 1024

IN_PROJ_TM = 2048
IN_PROJ_TN = ATTN_HEADS * ATTN_HEAD_DIM
MERGE_TM = 512
FFN_TM = 512

ATTN_WIDTH = ATTN_HEADS * ATTN_HEAD_DIM
COL_XBC = 0
COL_GATE = COL_XBC + SSM_CONV_DIM
COL_Z = COL_GATE + 3 * D_MODEL
COL_Q = COL_Z + SSM_INNER
COL_K = COL_Q + ATTN_WIDTH
COL_V = COL_K + ATTN_WIDTH
COL_QM = COL_V + ATTN_WIDTH
PROJ_WIDTH = COL_QM + MEM_HEADS * MEM_HEAD_DIM
IN_PROJ_PERM = (8, 9, 10, 6, 7, 0, 1, 2, 11, 3, 4, 5)
IN_PROJ_K_TILE = 1
IN_PROJ_SPLIT = 3 * ATTN_WIDTH + SSM_INNER + SSM_CONV_DIM
SSM_DT_COLS = SSM_HEADS

MOBA_NEAR = -(-(REL_MAX_DIST + MOBA_BLOCK - 1) // MOBA_BLOCK)
NEG_BIG = -1e30
LOG2E = math.log2(math.e)
MOBA_QBLOCKS = 4
MOBA_NEAR_SHARE = 4
MOBA_GROUP = 4
MOBA_VT_ROWS = ATTN_HEAD_DIM + BF16_SUBLANES
assert MOBA_GROUP <= MOBA_NEAR + 1
assert MOBA_GROUP <= SUBLANES


def _cparams(sem):
    return pltpu.CompilerParams(dimension_semantics=sem, vmem_limit_bytes=VMEM_LIMIT)


def _sigmoid(x):
    return 1.0 / (1.0 + jnp.exp(-x))


def _silu(x):
    h = 0.5 * x
    return h + h * jnp.tanh(h)


def _split_bf16(x, parts):
    out = []
    r = x
    for _ in range(parts):
        hi = r.astype(BF16)
        out.append(hi)
        r = r - hi.astype(F32)
    return out


def _inproj_kernel(perm_ref, x_ref, g_ref, w_ref, wdt_ref, proj_ref, dt_ref, kmean_ref, u_ref, *, k_tile):
    del perm_ref
    j = pl.program_id(1)
    nt = (((1,), (1,)), ((), ()))

    @pl.when(j == 0)
    def _():
        x = x_ref[...]
        ms = jnp.mean(x * x, axis=-1, keepdims=True)
        u = (x * lax.rsqrt(ms + NORM_EPS) * g_ref[...]).astype(BF16)
        u_ref[...] = u
        dt_ref[...] = lax.dot_general(u, wdt_ref[...], nt, preferred_element_type=F32)

    acc = lax.dot_general(u_ref[...], w_ref[...], nt, preferred_element_type=F32)
    proj_ref[...] = acc.astype(BF16)

    @pl.when(j == k_tile)
    def _():
        blk = MOBA_BLOCK
        for r in range(acc.shape[0] // blk):
            kmean_ref[0, r:r + 1, :] = jnp.sum(acc[r * blk:(r + 1) * blk], axis=0, keepdims=True) * (1.0 / blk)


def _in_proj(x2, g, w_t, w_dt, tm, tn):
    t = x2.shape[0]
    kw = ATTN_HEADS * ATTN_HEAD_DIM
    n_a = IN_PROJ_SPLIT // tn
    n_b = (w_t.shape[0] - IN_PROJ_SPLIT - SSM_DT_COLS) // tn
    assert tn == kw and tm % MOBA_BLOCK == 0 and (n_a + n_b) * tn == PROJ_WIDTH == len(IN_PROJ_PERM) * tn
    grid_spec = pltpu.PrefetchScalarGridSpec(
        num_scalar_prefetch=1,
        grid=(t // tm, n_a + n_b),
        in_specs=[
            pl.BlockSpec((tm, D_MODEL), lambda i, j, perm: (i, 0)),
            pl.BlockSpec((1, D_MODEL), lambda i, j, perm: (0, 0)),
            pl.BlockSpec((pl.Element(tn), pl.Element(D_MODEL)),
                         lambda i, j, perm: (pl.multiple_of(j * tn + jnp.where(j >= n_a, SSM_DT_COLS, 0), BF16_SUBLANES), 0)),
            pl.BlockSpec((LANES, D_MODEL), lambda i, j, perm: (0, 0)),
        ],
        out_specs=[
            pl.BlockSpec((tm, tn), lambda i, j, perm: (i, perm[j])),
            pl.BlockSpec((tm, LANES), lambda i, j, perm: (i, 0)),
            pl.BlockSpec((1, tm // MOBA_BLOCK, kw), lambda i, j, perm: (i, 0, 0)),
        ],
        scratch_shapes=[pltpu.VMEM((tm, D_MODEL), BF16)],
    )
    return pl.pallas_call(
        functools.partial(_inproj_kernel, k_tile=IN_PROJ_K_TILE),
        grid_spec=grid_spec,
        out_shape=[
            jax.ShapeDtypeStruct((t, PROJ_WIDTH), BF16),
            jax.ShapeDtypeStruct((t, LANES), F32),
            jax.ShapeDtypeStruct((t // tm, tm // MOBA_BLOCK, kw), F32),
        ],
        compiler_params=_cparams(("arbitrary", "arbitrary")),
        name="in_proj",
    )(jnp.asarray(IN_PROJ_PERM, jnp.int32), x2, g, w_t, w_dt)


def _t5_bucket(dist):
    n = jnp.maximum(dist, 0)
    max_exact = REL_BUCKETS // 2
    nf = jnp.maximum(n, max_exact).astype(F32)
    large = max_exact + (jnp.log(nf * (1.0 / max_exact)) / math.log(REL_MAX_DIST / max_exact)
                         * (REL_BUCKETS - max_exact)).astype(jnp.int32)
    large = jnp.minimum(large, REL_BUCKETS - 1)
    return jnp.where(n < max_exact, n, large)


def _bucket_range(dist_lo, dist_hi):
    def bucket(n):
        n = max(n, 0)
        max_exact = REL_BUCKETS // 2
        if n < max_exact:
            return n
        return min(max_exact + int(math.log(n / max_exact) / math.log(REL_MAX_DIST / max_exact)
                                   * (REL_BUCKETS - max_exact)), REL_BUCKETS - 1)
    return max(bucket(dist_lo) - 1, 0), min(bucket(dist_hi) + 1, REL_BUCKETS - 1) + 1


def _moba_kernel(rel_ref, q_ref, k_ref, v_ref, km_ref, o_ref, kmean_ref, bias_ref, vt_ref, neg_ref, s_ref, cm_ref,
                 m_ref, acc_ref, *, nb):
    h = pl.program_id(0)
    b = pl.program_id(1)
    i = pl.program_id(2)
    blk = MOBA_BLOCK
    scale = ATTN_HEAD_DIM ** -0.5


    @pl.when((b == 0) & (i == 0))
    def _():
        key = lax.broadcasted_iota(jnp.int32, (blk, blk), 0)
        qry = lax.broadcasted_iota(jnp.int32, (blk, blk), 1)
        for d in range(MOBA_NEAR):
            bucket = _t5_bucket(d * blk + qry - key)
            tile = jnp.zeros((blk, blk), F32)
            for bk in range(*_bucket_range(d * blk - (blk - 1), d * blk + (blk - 1))):
                tile = jnp.where(bucket == bk, rel_ref[h, bk], tile)
            bias_ref[d] = tile * LOG2E

    @pl.when(i == 0)
    def _():
        kmean_ref[...] = jnp.zeros_like(kmean_ref)
        kmean_ref[0:nb, :] = km_ref[0]
        ones_rows = (lax.broadcasted_iota(jnp.int32, (MOBA_VT_ROWS - ATTN_HEAD_DIM, blk), 0) == 0).astype(BF16)
        for jb in range(nb):
            vt_ref[jb, 0:ATTN_HEAD_DIM, :] = v_ref[0, jb * blk:(jb + 1) * blk, :].astype(F32).T.astype(BF16)
            vt_ref[jb, ATTN_HEAD_DIM:MOBA_VT_ROWS, :] = ones_rows

    qw = MOBA_QBLOCKS * blk
    i0 = i * MOBA_QBLOCKS
    q_t = q_ref[0].astype(F32).T
    qs_t = (q_t * (scale * LOG2E)).astype(BF16)

    gate = jnp.dot(kmean_ref[...].astype(BF16), q_t.astype(BF16), preferred_element_type=F32)
    nbp = kmean_ref.shape[0]
    row = lax.broadcasted_iota(jnp.int32, (nbp, qw), 0).astype(F32)
    own = (i0 + lax.broadcasted_iota(jnp.int32, (1, qw), 1) // blk).astype(F32)
    g = jnp.where(row < own, gate, -jnp.inf)
    sel = jnp.zeros((nbp, qw), F32)
    for t in range(MOBA_TOPK):
        mx = jnp.max(g, axis=0, keepdims=True)
        idx = jnp.min(jnp.where(g == mx, row, float(nbp)), axis=0, keepdims=True)
        hit = row == idx
        sel = jnp.maximum(sel, jnp.where(hit & (own > t), 1.0, 0.0))
        g = jnp.where(hit, -jnp.inf, g)
    neg_ref[...] = jnp.where(sel > 0.5, 0.0, NEG_BIG)

    def far_scores(j0, slot):
        start = pl.multiple_of(j0 * blk, blk)
        s = jnp.dot(k_ref[0, pl.ds(start, MOBA_GROUP * blk), :], qs_t, preferred_element_type=F32)
        s_ref[slot] = s
        for u in range(MOBA_GROUP):
            cm_ref[slot, u:u + 1, :] = jnp.max(s[u * blk:(u + 1) * blk], axis=0, keepdims=True)

    far_scores(0, 0)

    def softmax_group(scores, offsets, v_idx, m_old, col_max=None):
        if col_max is None:
            col_max = [jnp.max(sj, axis=0, keepdims=True) for sj in scores]
        m_new = m_old
        for cm, off in zip(col_max, offsets):
            m_new = jnp.maximum(m_new, cm + off)
        pv = jnp.zeros((MOBA_VT_ROWS, m_old.shape[1]), F32)
        for sj, off, vj in zip(scores, offsets, v_idx):
            pj = jnp.exp2(sj + (off - m_new))
            pv = pv + jnp.dot(vt_ref[vj], pj.astype(BF16), preferred_element_type=F32)
        return m_new, pv

    far_bias = rel_ref[h, REL_BUCKETS - 1] * LOG2E

    key = lax.broadcasted_iota(jnp.int32, (blk, blk), 0)
    qry = lax.broadcasted_iota(jnp.int32, (blk, blk), 1)
    scores, offsets, v_idx, first_lane = [], [], [], []
    for e in range(MOBA_NEAR - 1 + MOBA_QBLOCKS):
        jn = i0 - (MOBA_NEAR - 1) + e
        jc = jnp.maximum(jn, 0)
        start = pl.multiple_of(jc * blk, blk)
        w_min = max(0, e - (MOBA_NEAR - 1))
        lo = w_min * blk
        se = jnp.dot(k_ref[0, pl.ds(start, blk), :], qs_t[:, lo:], preferred_element_type=F32)
        sel_row = neg_ref[pl.ds(jnp.where(jn >= 0, jn, nbp - 1), 1), :]
        s_parts, o_parts = [], []
        for w in range(w_min, MOBA_QBLOCKS):
            d = w + (MOBA_NEAR - 1) - e
            sw = se[:, (w - w_min) * blk:(w - w_min + 1) * blk]
            ow = sel_row[:, w * blk:(w + 1) * blk]
            if d == 0:
                sw = jnp.where(qry >= key, sw + bias_ref[0], NEG_BIG)
                ow = jnp.zeros((1, blk), F32)
            elif d < MOBA_NEAR:
                sw = sw + bias_ref[d]
            else:
                ow = ow + far_bias
            s_parts.append(sw)
            o_parts.append(ow)
        scores.append(jnp.concatenate(s_parts, axis=1))
        offsets.append(jnp.concatenate(o_parts, axis=1))
        v_idx.append(jc)
        first_lane.append(lo)
    m_floor = jnp.concatenate(
        [jnp.max(scores[MOBA_NEAR - 1 + w][:, 0:blk], axis=0, keepdims=True) for w in range(MOBA_QBLOCKS)], axis=1)
    parts = []
    n_full = sum(1 for lo in first_lane if lo == 0) // MOBA_NEAR_SHARE * MOBA_NEAR_SHARE
    groups = ([list(range(g, g + MOBA_NEAR_SHARE)) for g in range(0, n_full, MOBA_NEAR_SHARE)]
              + [[e] for e in range(n_full, len(scores))])
    for grp in groups:
        lo = first_lane[grp[0]]
        mu, pvu = softmax_group([scores[e] for e in grp], [offsets[e] for e in grp], [v_idx[e] for e in grp],
                                m_floor[:, lo:])
        if lo:
            mu = jnp.concatenate([m_floor[:, :lo], mu], axis=1)
            pvu = jnp.concatenate([jnp.zeros((MOBA_VT_ROWS, lo), F32), pvu], axis=1)
        parts.append((mu, pvu))
    m0 = parts[0][0]
    for mu, _ in parts[1:]:
        m0 = jnp.maximum(m0, mu)
    pv0 = jnp.zeros((MOBA_VT_ROWS, qw), F32)
    for mu, pvu in parts:
        pv0 = pv0 + jnp.exp2(mu - m0) * pvu
    m_ref[...] = m0
    acc_ref[...] = pv0

    n_far = jnp.maximum(i0 - (MOBA_NEAR - 1), 0)

    n_groups = (n_far + MOBA_GROUP - 1) // MOBA_GROUP

    def far_group(gi, slot, prefetch):
        j0 = gi * MOBA_GROUP
        if prefetch:
            far_scores(j0 + MOBA_GROUP, 1 - slot)
        scores, offsets, v_idx, col_max = [], [], [], []
        for u in range(MOBA_GROUP):
            ju = j0 + u
            scores.append(s_ref[slot, u * blk:(u + 1) * blk, :])
            col_max.append(cm_ref[slot, u:u + 1, :])
            offsets.append(neg_ref[pl.ds(jnp.where(ju < n_far, ju, nbp - 1), 1), :] + far_bias)
            v_idx.append(ju)
        m_old = m_ref[...]
        m_new, pv = softmax_group(scores, offsets, v_idx, m_old, col_max)
        acc_ref[...] = jnp.exp2(m_old - m_new) * acc_ref[...] + pv
        m_ref[...] = m_new

    def far_body(t, carry):
        far_group(2 * t, 0, True)
        far_group(2 * t + 1, 1, True)
        return carry

    n_pairs = jnp.maximum(n_groups - 1, 0) // 2
    lax.fori_loop(0, n_pairs, far_body, 0)
    g_rest = 2 * n_pairs
    n_rest = n_groups - g_rest

    @pl.when(n_rest == 1)
    def _():
        far_group(g_rest, 0, False)

    @pl.when(n_rest == 2)
    def _():
        far_group(g_rest, 0, True)
        far_group(g_rest + 1, 1, False)

    out_t = acc_ref[0:ATTN_HEAD_DIM, :] / acc_ref[ATTN_HEAD_DIM:ATTN_HEAD_DIM + 1, :]
    o_ref[0] = out_t.T.astype(o_ref.dtype)


def _moba(proj3, kmean3, rel_t):
    bsz, s_len, _ = proj3.shape
    blk = MOBA_BLOCK
    nb = s_len // blk
    qw = MOBA_QBLOCKS * blk
    assert nb * blk == s_len and nb % MOBA_QBLOCKS == 0 and nb >= MOBA_GROUP
    nbp = -(-(nb + 1) // BF16_SUBLANES) * BF16_SUBLANES
    cb = LANES
    return pl.pallas_call(
        functools.partial(_moba_kernel, nb=nb),
        grid=(ATTN_HEADS, bsz, nb // MOBA_QBLOCKS),
        in_specs=[
            pl.BlockSpec(memory_space=pltpu.SMEM),
            pl.BlockSpec((1, qw, cb), lambda h, b, i: (b, i, COL_Q // cb + h)),
            pl.BlockSpec((1, s_len, cb), lambda h, b, i: (b, 0, COL_K // cb + h)),
            pl.BlockSpec((1, s_len, cb), lambda h, b, i: (b, 0, COL_V // cb + h)),
            pl.BlockSpec((1, nb, cb), lambda h, b, i: (b, 0, h)),
        ],
        out_specs=pl.BlockSpec((1, qw, cb), lambda h, b, i: (b, i, h)),
        out_shape=jax.ShapeDtypeStruct((bsz, s_len, ATTN_HEADS * ATTN_HEAD_DIM), BF16),
        scratch_shapes=[
            pltpu.VMEM((nbp, ATTN_HEAD_DIM), F32),
            pltpu.VMEM((MOBA_NEAR, blk, blk), F32),
            pltpu.VMEM((nb, MOBA_VT_ROWS, blk), BF16),
            pltpu.VMEM((nbp, qw), F32),
            pltpu.VMEM((2, MOBA_GROUP * blk, qw), F32),
            pltpu.VMEM((2, SUBLANES, qw), F32),
            pltpu.VMEM((1, qw), F32),
            pltpu.VMEM((MOBA_VT_ROWS, qw), F32),
        ],
        compiler_params=_cparams(("arbitrary", "arbitrary", "arbitrary")),
        name="moba",
    )(rel_t, proj3, proj3, proj3, kmean3)


def _ssd_kernel(xbc_ref, z_ref, dt_ref, cw_ref, cbias_ref, dtb_ref, alog_ref, dskip_ref, ng_ref,
                tril_ref, e_ref, shift_ref, o_ref, tail_ref, state_ref, xs_ref, xcb_ref, ea_ref, wend_ref, bm_ref,
                cm_ref):
    L = SSM_CHUNK
    G = SSM_GROUPS
    N = SSM_STATE
    GW = SSM_INNER // G
    pad = SUBLANES
    nt = (((1,), (1,)), ((), ()))

    @pl.when(pl.program_id(1) == 0)
    def _():
        tail_ref[...] = jnp.zeros_like(tail_ref)
        state_ref[...] = jnp.zeros_like(state_ref)

    dtr = dt_ref[0] + dtb_ref[...]
    dt = jnp.maximum(dtr, 0.0) + jnp.log1p(jnp.exp(-jnp.abs(dtr)))
    a = dt * (-jnp.exp(alog_ref[...]))
    a_cat = jnp.concatenate(_split_bf16(a, 3), axis=0)
    acs = jnp.dot(tril_ref[...], a_cat, preferred_element_type=F32) * LOG2E
    acs_t = acs.T

    dt_cat = jnp.concatenate(_split_bf16(dt, 2), axis=1)
    acs_cat = jnp.concatenate(_split_bf16(acs, 2), axis=1)

    SW = 2 * LANES
    for c in range(SSM_CONV_DIM // SW):
        cols = slice(c * SW, (c + 1) * SW)
        xb = xbc_ref[0, :, cols]
        x = xb.astype(F32)
        conv = cbias_ref[:, cols] + x * cw_ref[SSM_CONV - 1:SSM_CONV, cols]
        head = jnp.zeros((pad, SW), F32)
        for k in range(SSM_CONV - 1):
            s = SSM_CONV - 1 - k
            w_k = cw_ref[k:k + 1, cols]
            conv = conv + jnp.dot(shift_ref[s - 1], xb, preferred_element_type=F32) * w_k
            head = head + tail_ref[pad - s:2 * pad - s, cols] * w_k
        conv = jnp.concatenate([conv[0:pad] + head, conv[pad:]], axis=0)
        tail_ref[0:pad, cols] = x[L - pad:L]
        xa = _silu(conv)
        if c * SW < SSM_INNER:
            dt_x = jnp.dot(dt_cat, e_ref[:, cols], preferred_element_type=F32)
            acs_x = jnp.dot(acs_cat, e_ref[:, cols], preferred_element_type=F32)
            xc = xa * dt_x
            xs_ref[:, cols] = xa
            xcb_ref[:, cols] = xc.astype(BF16)
            ea_ref[:, cols] = jnp.exp2(acs_x)
            wend_ref[:, cols] = (jnp.exp2(acs_x[L - 1:L, :] - acs_x) * xc).astype(BF16)
        elif c * SW < SSM_INNER + G * N:
            bm_ref[:, c * SW - SSM_INNER:(c + 1) * SW - SSM_INNER] = xa
        else:
            cm_ref[:, c * SW - SSM_INNER - G * N:(c + 1) * SW - SSM_INNER - G * N] = xa.astype(BF16)

    row = lax.broadcasted_iota(jnp.int32, (L, L), 0)
    col = lax.broadcasted_iota(jnp.int32, (L, L), 1)
    causal = row >= col
    lane = lax.broadcasted_iota(jnp.int32, (L, LANES), 1)
    low_half = lane < SSM_HEAD_DIM

    for g in range(G):
        gcols = slice(g * GW, (g + 1) * GW)
        bg = bm_ref[:, g * N:(g + 1) * N]
        cg = cm_ref[:, g * N:(g + 1) * N]
        cb = lax.dot_general(cg, bg.astype(BF16), nt, preferred_element_type=F32)
        st = state_ref[g]
        ea_g = ea_ref[:, gcols]
        y_off = jnp.dot(cg, st.astype(BF16), preferred_element_type=F32) * ea_g
        y_parts = []
        for pr in range(GW // LANES):
            c0 = g * GW + pr * LANES
            x_pair = xcb_ref[:, c0:c0 + LANES]
            zero = jnp.zeros_like(x_pair)
            y_pair = y_off[:, pr * LANES:(pr + 1) * LANES]
            for half in range(2):
                hd = c0 // SSM_HEAD_DIM + half
                seg = acs[:, hd:hd + 1] - acs_t[hd:hd + 1, :]
                decay = jnp.exp2(jnp.where(causal, seg, -jnp.inf))
                mmat = (cb * decay).astype(BF16)
                x_half = jnp.where(low_half, x_pair, zero) if half == 0 else jnp.where(low_half, zero, x_pair)
                y_pair = y_pair + jnp.dot(mmat, x_half, preferred_element_type=F32)
            y_parts.append(y_pair)
        bg_t = bg.T.astype(BF16)
        state_ref[g] = (st * ea_g[L - 1:L, :]
                        + jnp.dot(bg_t, wend_ref[:, gcols], preferred_element_type=F32))

        y = jnp.concatenate(y_parts, axis=1) + dskip_ref[:, gcols] * xs_ref[:, gcols]
        v = y * _silu(z_ref[0, :, gcols].astype(F32))
        ms = jnp.mean(v * v, axis=-1, keepdims=True)
        o_ref[0, :, gcols] = (v * lax.rsqrt(ms + NORM_EPS) * ng_ref[:, gcols]).astype(o_ref.dtype)


def _ssd(proj3, dt3, conv_w, conv_b, dt_bias, a_log, d_skip_x, norm_g, tril3, e2, shifts):
    bsz, s_len, _ = proj3.shape
    L = SSM_CHUNK
    nc = s_len // L
    assert nc * L == s_len
    const = lambda b, c: (0, 0)
    return pl.pallas_call(
        _ssd_kernel,
        grid=(bsz, nc),
        in_specs=[
            pl.BlockSpec((1, L, SSM_CONV_DIM), lambda b, c: (b, c, COL_XBC // SSM_CONV_DIM)),
            pl.BlockSpec((1, L, SSM_INNER), lambda b, c: (b, c, COL_Z // SSM_INNER)),
            pl.BlockSpec((1, L, LANES), lambda b, c: (b, c, 0)),
            pl.BlockSpec((SSM_CONV, SSM_CONV_DIM), const),
            pl.BlockSpec((1, SSM_CONV_DIM), const),
            pl.BlockSpec((1, LANES), const),
            pl.BlockSpec((1, LANES), const),
            pl.BlockSpec((1, SSM_INNER), const),
            pl.BlockSpec((1, SSM_INNER), const),
            pl.BlockSpec((L, 3 * L), const),
            pl.BlockSpec((2 * LANES, SSM_INNER), const),
            pl.BlockSpec((SSM_CONV - 1, L, L), lambda b, c: (0, 0, 0)),
        ],
        out_specs=pl.BlockSpec((1, L, SSM_INNER), lambda b, c: (b, c, 0)),
        out_shape=jax.ShapeDtypeStruct((bsz, s_len, SSM_INNER), BF16),
        scratch_shapes=[
            pltpu.VMEM((2 * SUBLANES, SSM_CONV_DIM), F32),
            pltpu.VMEM((SSM_GROUPS, SSM_STATE, SSM_INNER // SSM_GROUPS), F32),
            pltpu.VMEM((L, SSM_INNER), F32),
            pltpu.VMEM((L, SSM_INNER), BF16),
            pltpu.VMEM((L, SSM_INNER), F32),
            pltpu.VMEM((L, SSM_INNER), BF16),
            pltpu.VMEM((L, SSM_GROUPS * SSM_STATE), F32),
            pltpu.VMEM((L, SSM_GROUPS * SSM_STATE), BF16),
        ],
        compiler_params=_cparams(("arbitrary", "arbitrary")),
        name="ssd",
    )(proj3, proj3, dt3, conv_w, conv_b, dt_bias, a_log, d_skip_x, norm_g, tril3, e2, shifts)


def _memkv_kernel(mem_ref, g_ref, w_ref, kv_ref):
    x = mem_ref[...]
    ms = jnp.mean(x * x, axis=-1, keepdims=True)
    u = (x * lax.rsqrt(ms + NORM_EPS) * g_ref[...]).astype(BF16)
    kv_ref[...] = jnp.dot(u, w_ref[...], preferred_element_type=F32).astype(BF16)


def _mem_kv(mem2, g, w_kv):
    rows = mem2.shape[0]
    width = w_kv.shape[1]
    return pl.pallas_call(
        _memkv_kernel,
        grid=(1,),
        in_specs=[
            pl.BlockSpec((rows, D_MODEL), lambda i: (0, 0)),
            pl.BlockSpec((1, D_MODEL), lambda i: (0, 0)),
            pl.BlockSpec((D_MODEL, width), lambda i: (0, 0)),
        ],
        out_specs=pl.BlockSpec((rows, width), lambda i: (0, 0)),
        out_shape=jax.ShapeDtypeStruct((rows, width), BF16),
        compiler_params=_cparams(("arbitrary",)),
        name="mem_kv",
    )(mem2, g, w_kv)


def _merge_kernel(x_ref, oa_ref, os_ref, qm_ref, gl_ref, kv_ref, bg_ref, wa_ref, ws_ref, wm_ref, wo_ref, h_ref):
    nt = (((1,), (1,)), ((), ()))
    hd = MEM_HEAD_DIM
    width = MEM_HEADS * hd
    scale = hd ** -0.5
    qm = qm_ref[0]
    kv = kv_ref[0]
    outs = []
    for hh in range(MEM_HEADS):
        q = qm[:, hh * hd:(hh + 1) * hd]
        km = kv[:, hh * hd:(hh + 1) * hd]
        vm = kv[:, width + hh * hd:width + (hh + 1) * hd]
        s = lax.dot_general(q, km, nt, preferred_element_type=F32) * scale
        p = jnp.exp(s - jnp.max(s, axis=1, keepdims=True))
        o = jnp.dot(p.astype(BF16), vm, preferred_element_type=F32)
        outs.append(o / jnp.sum(p, axis=1, keepdims=True))
    o_mem = jnp.concatenate(outs, axis=1).astype(BF16)

    gates = _sigmoid(gl_ref[0].astype(F32) + bg_ref[...])
    merged = (gates[:, :D_MODEL] * jnp.dot(oa_ref[0], wa_ref[...], preferred_element_type=F32)
              + gates[:, D_MODEL:2 * D_MODEL] * jnp.dot(os_ref[0], ws_ref[...], preferred_element_type=F32)
              + gates[:, 2 * D_MODEL:] * jnp.dot(o_mem, wm_ref[...], preferred_element_type=F32))
    h_ref[0] = x_ref[0] + jnp.dot(merged.astype(BF16), wo_ref[...], preferred_element_type=F32)


def _resident(shape):
    return pl.BlockSpec(shape, lambda *_: (0,) * len(shape), pipeline_mode=pl.Buffered(1))


def _merge(x, o_attn, o_ssm, proj3, kv3, b_gate, wa, ws, wm, wo, tm):
    bsz, s_len, _ = x.shape
    mem_len = kv3.shape[1]
    return pl.pallas_call(
        _merge_kernel,
        grid=(bsz, s_len // tm),
        in_specs=[
            pl.BlockSpec((1, tm, D_MODEL), lambda b, i: (b, i, 0)),
            pl.BlockSpec((1, tm, D_MODEL), lambda b, i: (b, i, 0)),
            pl.BlockSpec((1, tm, SSM_INNER), lambda b, i: (b, i, 0)),
            pl.BlockSpec((1, tm, D_MODEL), lambda b, i: (b, i, COL_QM // D_MODEL)),
            pl.BlockSpec((1, tm, 3 * D_MODEL), lambda b, i: (b, i, COL_GATE // (3 * D_MODEL))),
            pl.BlockSpec((1, mem_len, 2 * D_MODEL), lambda b, i: (b, 0, 0)),
            _resident((1, 3 * D_MODEL)),
            _resident((D_MODEL, D_MODEL)),
            _resident((SSM_INNER, D_MODEL)),
            _resident((D_MODEL, D_MODEL)),
            _resident((D_MODEL, D_MODEL)),
        ],
        out_specs=pl.BlockSpec((1, tm, D_MODEL), lambda b, i: (b, i, 0)),
        out_shape=jax.ShapeDtypeStruct((bsz, s_len, D_MODEL), F32),
        compiler_params=_cparams(("arbitrary", "arbitrary")),
        name="merge",
    )(x, o_attn, o_ssm, proj3, proj3, kv3, b_gate, wa, ws, wm, wo)


def _ffn_kernel(h_ref, ng_ref, wup_ref, cw_ref, cb_ref, wdn_ref, fg_ref, o_ref, hid_ref, *, tm):
    pad = SUBLANES

    @pl.when(pl.program_id(1) == 0)
    def _():
        hid_ref[0:pad, :] = jnp.zeros((pad, 2 * FFN_HIDDEN), F32)

    h = h_ref[0]
    ms = jnp.mean(h * h, axis=-1, keepdims=True)
    u = (h * lax.rsqrt(ms + NORM_EPS) * ng_ref[...]).astype(BF16)
    hid_ref[pad:pad + tm, :] = jnp.dot(u, wup_ref[...], preferred_element_type=F32)
    conv = cb_ref[...]
    for k in range(FFN_CONV):
        off = pad - (FFN_CONV - 1) + k
        conv = conv + hid_ref[off:off + tm, :] * cw_ref[k:k + 1, :]
    hid_ref[0:pad, :] = hid_ref[tm:tm + pad, :]
    act = (_silu(conv[:, :FFN_HIDDEN]) * conv[:, FFN_HIDDEN:]).astype(BF16)
    y = h + jnp.dot(act, wdn_ref[...], preferred_element_type=F32)
    ms2 = jnp.mean(y * y, axis=-1, keepdims=True)
    o_ref[0] = y * lax.rsqrt(ms2 + NORM_EPS) * fg_ref[...]


def _ffn(h, norm_g, w_up, conv_w, conv_b, w_down, final_g, tm):
    bsz, s_len, _ = h.shape
    f2 = 2 * FFN_HIDDEN
    return pl.pallas_call(
        functools.partial(_ffn_kernel, tm=tm),
        grid=(bsz, s_len // tm),
        in_specs=[
            pl.BlockSpec((1, tm, D_MODEL), lambda b, i: (b, i, 0)),
            _resident((1, D_MODEL)),
            _resident((D_MODEL, f2)),
            _resident((FFN_CONV, f2)),
            _resident((1, f2)),
            _resident((FFN_HIDDEN, D_MODEL)),
            _resident((1, D_MODEL)),
        ],
        out_specs=pl.BlockSpec((1, tm, D_MODEL), lambda b, i: (b, i, 0)),
        out_shape=jax.ShapeDtypeStruct((bsz, s_len, D_MODEL), F32),
        scratch_shapes=[pltpu.VMEM((tm + 2 * SUBLANES, f2), F32)],
        compiler_params=_cparams(("arbitrary", "arbitrary")),
        name="ffn",
    )(h, norm_g, w_up, conv_w, conv_b, w_down, final_g)


def _ssd_constants():
    L = SSM_CHUNK
    tril = np.tril(np.ones((L, L), np.float32))
    tril3 = np.concatenate([tril, tril, tril], axis=1)
    e = np.zeros((LANES, SSM_INNER), np.float32)
    for hd in range(SSM_HEADS):
        e[hd, hd * SSM_HEAD_DIM:(hd + 1) * SSM_HEAD_DIM] = 1.0
    e2 = np.concatenate([e, e], axis=0)
    shifts = np.stack([np.eye(L, k=-s, dtype=np.float32) for s in range(1, SSM_CONV)])
    return jnp.asarray(tril3, BF16), jnp.asarray(e2, BF16), jnp.asarray(shifts, BF16)


def _pad_lanes(v):
    return jnp.pad(v, ((0, 0), (0, LANES - v.shape[1])))


def _layer(h, mem, rel_bias, mix_norm_g, w_in, b_gate, ssm_conv_w, ssm_conv_b, ssm_dt_bias, ssm_A_log, ssm_D,
           ssm_norm_g, mem_norm_g, w_mem_kv, w_br_attn, w_br_ssm, w_br_mem, w_out, ffn_norm_g, w_ffn_up,
           ffn_conv_w, ffn_conv_b, w_ffn_down, final_g):
    bsz, s_len, _ = h.shape
    w_t = w_in.T.astype(BF16)
    w_dt = jnp.pad(w_t[IN_PROJ_SPLIT:IN_PROJ_SPLIT + SSM_DT_COLS], ((0, LANES - SSM_DT_COLS), (0, 0)))

    proj, dt_raw, kmean = _in_proj(h.reshape(bsz * s_len, D_MODEL), mix_norm_g[None, :], w_t, w_dt,
                                   tm=IN_PROJ_TM, tn=IN_PROJ_TN)
    proj3 = proj.reshape(bsz, s_len, PROJ_WIDTH)
    dt3 = dt_raw.reshape(bsz, s_len, LANES)
    kmean3 = kmean.reshape(bsz, s_len // MOBA_BLOCK, ATTN_HEADS * ATTN_HEAD_DIM)

    o_attn = _moba(proj3, kmean3, rel_bias.T)

    tril3, e2, shifts = _ssd_constants()
    o_ssm = _ssd(proj3, dt3, ssm_conv_w, ssm_conv_b[None, :], _pad_lanes(ssm_dt_bias[None, :]),
                 _pad_lanes(ssm_A_log[None, :]), jnp.repeat(ssm_D, SSM_HEAD_DIM)[None, :], ssm_norm_g[None, :],
                 tril3, e2, shifts)

    mem_len = mem.shape[1]
    kv = _mem_kv(mem.reshape(bsz * mem_len, D_MODEL), mem_norm_g[None, :], w_mem_kv.astype(BF16))
    kv3 = kv.reshape(bsz, mem_len, 2 * D_MODEL)

    h1 = _merge(h, o_attn, o_ssm, proj3, kv3, b_gate[None, :], w_br_attn.astype(BF16), w_br_ssm.astype(BF16),
                w_br_mem.astype(BF16), w_out.astype(BF16), tm=MERGE_TM)

    return _ffn(h1, ffn_norm_g[None, :], w_ffn_up.astype(BF16), ffn_conv_w, ffn_conv_b[None, :],
                w_ffn_down.astype(BF16), final_g[None, :], tm=FFN_TM)


def kernel(x, mem, rel_bias, mix_norm_g, w_in, b_gate, ssm_conv_w, ssm_conv_b, ssm_dt_bias, ssm_A_log, ssm_D,
           ssm_norm_g, mem_norm_g, w_mem_kv, w_br_attn, w_br_ssm, w_br_mem, w_out, ffn_norm_g, w_ffn_up,
           ffn_conv_w, ffn_conv_b, w_ffn_down, final_norm_g):
    assert w_in.shape[0] == 1, "single-layer trunk"
    return _layer(x, mem, rel_bias, mix_norm_g[0], w_in[0], b_gate[0], ssm_conv_w[0], ssm_conv_b[0],
                  ssm_dt_bias[0], ssm_A_log[0], ssm_D[0], ssm_norm_g[0], mem_norm_g[0], w_mem_kv[0],
                  w_br_attn[0], w_br_ssm[0], w_br_mem[0], w_out[0], ffn_norm_g[0], w_ffn_up[0],
                  ffn_conv_w[0], ffn_conv_b[0], w_ffn_down[0], final_norm_g)
```
